```python
import math
import jax, jax.numpy as jnp
from jax import lax
import numpy as np


D_MODEL = 1024
BATCH = 32
SEQ = 2048
DEPTH = 1
DEC_BATCH = 8
DEC_SEQ = 64
PAST_LEN = 4096

CHUNK = 64
D_SSM = 512
SSM_GROUP = 16
N_GROUPS = D_SSM // SSM_GROUP
N_STATE = 64
N_HEADS = 8
N_KV_HEADS = 2
HEAD_DIM = 64
D_ATTN = N_HEADS * HEAD_DIM
KV_GROUP = N_HEADS // N_KV_HEADS
ROPE_DIM = HEAD_DIM // 4
ROPE_THETA = 500000.0
IDX_HEADS = 8
IDX_DIM = 64
TOPK_KEYS = 256
Q_BLOCK = 64
N_EXPERTS = 64
MOE_TOPK = 6
D_EXPERT = 256
D_SHARED = 256
ROUTED_SCALE = 2.5
MOE_BLOCK = 128
LN_EPS = 1e-5
ALPHA = (2 * DEPTH) ** 0.25
BETA = (8 * DEPTH) ** -0.25
IN_WIDTHS = (D_SSM, D_ATTN, N_KV_HEADS * HEAD_DIM, N_KV_HEADS * HEAD_DIM,
             IDX_HEADS * IDX_DIM, IDX_DIM, IDX_HEADS, 2 * D_MODEL)
IN_SPLITS = tuple(int(s) for s in np.cumsum(IN_WIDTHS)[:-1])
D_IN = sum(IN_WIDTHS)

kernel_name = "hybrid_s5_dsa_moe_stream_step"


def layer_norm(x, g, b):
    xf = x.astype(jnp.float32)
    mu = xf.mean(-1, keepdims=True)
    var = jnp.square(xf - mu).mean(-1, keepdims=True)
    return ((xf - mu) * lax.rsqrt(var + LN_EPS) * g.astype(jnp.float32) + b.astype(jnp.float32)).astype(x.dtype)


def rope(x, pos):
    half = ROPE_DIM // 2
    inv_freq = ROPE_THETA ** (-jnp.arange(half, dtype=jnp.float32) * 2.0 / ROPE_DIM)
    ang = pos.astype(jnp.float32)[:, None] * inv_freq
    cos = jnp.cos(ang)[:, None, :]
    sin = jnp.sin(ang)[:, None, :]
    xr = x[..., :ROPE_DIM].astype(jnp.float32)
    x1, x2 = xr[..., :half], xr[..., half:]
    rot = jnp.concatenate([x1 * cos - x2 * sin, x1 * sin + x2 * cos], axis=-1).astype(x.dtype)
    return jnp.concatenate([rot, x[..., ROPE_DIM:]], axis=-1)


def s5_branch(u, h0, a_re, a_im, log_dt, b_re, b_im, c_re, c_im, d_skip, w_glu):
    B, L, _ = u.shape
    f32 = jnp.float32
    uf = u.astype(f32).reshape(B, L, N_GROUPS, SSM_GROUP)
    ar, ai = a_re.astype(f32), a_im.astype(f32)
    dt = jnp.exp(log_dt.astype(f32))[:, None]
    mag = jnp.exp(dt * ar)
    abar_re, abar_im = mag * jnp.cos(dt * ai), mag * jnp.sin(dt * ai)
    den = ar * ar + ai * ai
    num_re, num_im = abar_re - 1.0, abar_im
    coef_re = (num_re * ar + num_im * ai) / den
    coef_im = (num_im * ar - num_re * ai) / den
    br, bi = b_re.astype(f32), b_im.astype(f32)
    bbar_re = coef_re[..., None] * br - coef_im[..., None] * bi
    bbar_im = coef_re[..., None] * bi + coef_im[..., None] * br
    bu_re = jnp.einsum("blgc,gnc->blgn", uf, bbar_re)
    bu_im = jnp.einsum("blgc,gnc->blgn", uf, bbar_im)
    if h0 is not None:
        hr0, hi0 = h0[0].astype(f32), h0[1].astype(f32)
        bu_re = bu_re.at[:, 0].add(abar_re * hr0 - abar_im * hi0)
        bu_im = bu_im.at[:, 0].add(abar_re * hi0 + abar_im * hr0)
    a_seq_re = jnp.broadcast_to(abar_re, (1, L, N_GROUPS, N_STATE))
    a_seq_im = jnp.broadcast_to(abar_im, (1, L, N_GROUPS, N_STATE))

    def combine(e1, e2):
        a1r, a1i, b1r, b1i = e1
        a2r, a2i, b2r, b2i = e2
        return (a2r * a1r - a2i * a1i, a2r * a1i + a2i * a1r,
                a2r * b1r - a2i * b1i + b2r, a2r * b1i + a2i * b1r + b2i)

    _, _, h_re, h_im = lax.associative_scan(combine, (a_seq_re, a_seq_im, bu_re, bu_im), axis=1)
    y = (jnp.einsum("blgn,gcn->blgc", h_re, c_re.astype(f32))
         - jnp.einsum("blgn,gcn->blgc", h_im, c_im.astype(f32))
         + d_skip.astype(f32).reshape(N_GROUPS, SSM_GROUP) * uf)
    y = jax.nn.gelu(y.reshape(B, L, D_SSM)).astype(u.dtype)
    val, gate = jnp.split(jnp.einsum("ble,ed->bld", y, w_glu), 2, axis=-1)
    out = val * jax.nn.sigmoid(gate)
    return out, h_re[:, -1].astype(u.dtype), h_im[:, -1].astype(u.dtype)


def dsa_attention(q, iq, iw, k, v, ik, q_pos, k_pos):
    B, Lq = q.shape[:2]
    Lk = k.shape[1]
    n_sel = min(TOPK_KEYS, Lk // 4)
    qb = min(Q_BLOCK, Lq)
    nb = Lq // qb
    f32 = jnp.float32

    def to_blocks(a):
        return jnp.moveaxis(a.reshape(B, nb, qb, *a.shape[2:]), 1, 0)

    ikf = ik.astype(f32)
    k_chunk = k_pos // CHUNK

    def block(args):
        qblk, iqblk, iwblk, pblk = args
        q_chunk = pblk // CHUNK
        rel = jax.nn.relu(jnp.einsum("bqhd,bkd->bqhk", iqblk.astype(f32), ikf) * IDX_DIM ** -0.5)
        score = jnp.einsum("bqhk,bqh->bqk", rel, iwblk.astype(f32) * IDX_HEADS ** -0.5)
        visible = k_chunk[None, :] <= q_chunk[:, None]
        score = jnp.where(visible[None], score, -jnp.inf)
        _, sel = lax.top_k(score, n_sel)
        kg = jax.vmap(lambda kk, ii: kk[ii])(k, sel)
        vg = jax.vmap(lambda vv, ii: vv[ii])(v, sel)
        ok = k_chunk[sel] <= q_chunk[None, :, None]
        qg = qblk.reshape(B, qb, N_KV_HEADS, KV_GROUP, HEAD_DIM)
        logits = jnp.einsum("bqgrd,bqngd->bqgrn", qg, kg).astype(f32) * HEAD_DIM ** -0.5
        logits = jnp.where(ok[:, :, None, None, :], logits, -jnp.inf)
        prob = jax.nn.softmax(logits, axis=-1).astype(v.dtype)
        o = jnp.einsum("bqgrn,bqngd->bqgrd", prob, vg)
        return o.reshape(B, qb, D_ATTN)

    out = lax.map(block, (to_blocks(q), to_blocks(iq), to_blocks(iw), q_pos.reshape(nb, qb)))
    return jnp.moveaxis(out, 0, 1).reshape(B, Lq, D_ATTN)


def swiglu(x, w_gu, w_down):
    a, b = jnp.split(jnp.einsum("...d,df->...f", x, w_gu), 2, axis=-1)
    return jnp.einsum("...f,fd->...d", jax.nn.silu(a) * b, w_down)


def routed_experts(xt, idx, gate, w_exp_gu, w_exp_down):
    T, D = xt.shape
    S = T * MOE_TOPK
    n_blocks = -(-S // MOE_BLOCK) + N_EXPERTS
    R = n_blocks * MOE_BLOCK
    e_flat = idx.reshape(S)
    tok_flat = jnp.repeat(jnp.arange(T, dtype=jnp.int32), MOE_TOPK)
    g_flat = gate.reshape(S)
    order = jnp.argsort(e_flat)
    e_sorted = e_flat[order]
    counts = jnp.zeros((N_EXPERTS,), jnp.int32).at[e_flat].add(1)
    padded = (counts + MOE_BLOCK - 1) // MOE_BLOCK * MOE_BLOCK
    start = jnp.cumsum(counts) - counts
    pend = jnp.cumsum(padded)
    pstart = pend - padded
    dest = pstart[e_sorted] + (jnp.arange(S, dtype=jnp.int32) - start[e_sorted])
    row_tok = jnp.full((R,), T, jnp.int32).at[dest].set(tok_flat[order])
    row_gate = jnp.zeros((R,), gate.dtype).at[dest].set(g_flat[order])
    block_start = jnp.arange(n_blocks, dtype=jnp.int32) * MOE_BLOCK
    block_expert = jnp.minimum(jnp.searchsorted(pend, block_start, side="right"), N_EXPERTS - 1)
    x_pad = jnp.concatenate([xt, jnp.zeros((1, D), xt.dtype)], axis=0)

    def run_block(args):
        tok, g, e = args
        return swiglu(x_pad[tok], w_exp_gu[e], w_exp_down[e]) * g[:, None]

    out = lax.map(run_block, (row_tok.reshape(n_blocks, MOE_BLOCK),
                              row_gate.reshape(n_blocks, MOE_BLOCK), block_expert))
    return jnp.zeros((T + 1, D), xt.dtype).at[row_tok].add(out.reshape(R, D))[:T]


def moe(x, w_router, router_bias, w_exp_gu, w_exp_down, w_sh_gu, w_sh_down):
    B, L, D = x.shape
    xt = x.reshape(B * L, D)
    scores = jax.nn.sigmoid(jnp.einsum("td,de->te", xt.astype(jnp.float32), w_router.astype(jnp.float32)))
    _, idx = lax.top_k(scores + router_bias.astype(jnp.float32), MOE_TOPK)
    gate = jnp.take_along_axis(scores, idx, axis=-1)
    gate = gate / gate.sum(-1, keepdims=True) * ROUTED_SCALE
    routed = routed_experts(xt, idx, gate.astype(x.dtype), w_exp_gu, w_exp_down)
    shared = swiglu(xt, w_sh_gu, w_sh_down)
    return (routed + shared).reshape(B, L, D)


def layer(x, pos, past, p):
    (w_in, ssm_a_re, ssm_a_im, ssm_log_dt, ssm_b_re, ssm_b_im, ssm_c_re, ssm_c_im, ssm_d,
     w_glu, w_attn_proj, w_out, ln1_g, ln1_b, w_router, router_bias,
     w_exp_gu, w_exp_down, w_sh_gu, w_sh_down, ln2_g, ln2_b) = p
    B, L, _ = x.shape
    z = jnp.einsum("bld,de->ble", x, w_in)
    u, q, k, v, iq, ik, iw, gates = jnp.split(z, IN_SPLITS, axis=-1)
    q = rope(q.reshape(B, L, N_HEADS, HEAD_DIM), pos)
    k = rope(k.reshape(B, L, N_KV_HEADS, HEAD_DIM), pos)
    v = v.reshape(B, L, N_KV_HEADS, HEAD_DIM)
    iq = rope(iq.reshape(B, L, IDX_HEADS, IDX_DIM), pos)
    ik = rope(ik[:, :, None, :], pos)[:, :, 0, :]
    g_ssm, g_attn = jnp.split(jax.nn.sigmoid(gates), 2, axis=-1)
    if past is None:
        k_all, v_all, ik_all, h0 = k, v, ik, None
    else:
        ck, cv, cik, h_re0, h_im0 = past
        k_all = jnp.concatenate([ck, k], axis=1)
        v_all = jnp.concatenate([cv, v], axis=1)
        ik_all = jnp.concatenate([cik, ik], axis=1)
        h0 = (h_re0, h_im0)
    k_pos = jnp.arange(k_all.shape[1], dtype=jnp.int32)
    ssm_out, h_re, h_im = s5_branch(u, h0, ssm_a_re, ssm_a_im, ssm_log_dt, ssm_b_re, ssm_b_im,
                                    ssm_c_re, ssm_c_im, ssm_d, w_glu)
    attn = dsa_attention(q, iq, iw, k_all, v_all, ik_all, pos, k_pos)
    attn_out = jnp.einsum("ble,ed->bld", attn, w_attn_proj)
    mixed = jnp.einsum("ble,ed->bld", g_ssm * ssm_out + g_attn * attn_out, w_out)
    x = layer_norm(ALPHA * x + mixed, ln1_g, ln1_b)
    x = layer_norm(ALPHA * x + moe(x, w_router, router_bias, w_exp_gu, w_exp_down, w_sh_gu, w_sh_down),
                   ln2_g, ln2_b)
    return x, (k, v, ik, h_re, h_im)


def setup_inputs(seed: int = 0) -> dict:
    key = jax.random.key(seed)
    ks = iter(jax.random.split(key, 40))
    f32 = jnp.float32

    def nrm(shape, scale):
        return jax.random.normal(next(ks), shape, f32) * scale

    Dd = DEPTH
    return {
        "x_prompt": nrm((BATCH, SEQ, D_MODEL), 1.0),
        "x_sample": nrm((DEC_BATCH, DEC_SEQ, D_MODEL), 1.0),
        "cache_k": nrm((Dd, DEC_BATCH, PAST_LEN, N_KV_HEADS, HEAD_DIM), 1.0),
        "cache_v": nrm((Dd, DEC_BATCH, PAST_LEN, N_KV_HEADS, HEAD_DIM), 1.0),
        "cache_idx_k": nrm((Dd, DEC_BATCH, PAST_LEN, IDX_DIM), 1.0),
        "state_ssm_re": nrm((Dd, DEC_BATCH, N_GROUPS, N_STATE), 0.1),
        "state_ssm_im": nrm((Dd, DEC_BATCH, N_GROUPS, N_STATE), 0.1),
        "w_in": nrm((Dd, D_MODEL, D_IN), D_MODEL ** -0.5),
        "ssm_a_re": -0.5 + nrm((Dd, N_GROUPS, N_STATE), 0.01),
        "ssm_a_im": jnp.pi * jnp.arange(N_STATE, dtype=f32) + nrm((Dd, N_GROUPS, N_STATE), 0.01),
        "ssm_log_dt": jax.random.uniform(next(ks), (Dd, N_GROUPS), f32,
                                         minval=math.log(1e-3), maxval=math.log(1e-1)),
        "ssm_b_re": nrm((Dd, N_GROUPS, N_STATE, SSM_GROUP), (2 * SSM_GROUP) ** -0.5),
        "ssm_b_im": nrm((Dd, N_GROUPS, N_STATE, SSM_GROUP), (2 * SSM_GROUP) ** -0.5),
        "ssm_c_re": nrm((Dd, N_GROUPS, SSM_GROUP, N_STATE), N_STATE ** -0.5),
        "ssm_c_im": nrm((Dd, N_GROUPS, SSM_GROUP, N_STATE), N_STATE ** -0.5),
        "ssm_d": nrm((Dd, D_SSM), 1.0),
        "w_glu": nrm((Dd, D_SSM, 2 * D_MODEL), D_SSM ** -0.5),
        "w_attn_proj": nrm((Dd, D_ATTN, D_MODEL), D_ATTN ** -0.5),
        "w_out": nrm((Dd, D_MODEL, D_MODEL), BETA * D_MODEL ** -0.5),
        "ln1_g": 1.0 + nrm((Dd, D_MODEL), 0.02),
        "ln1_b": nrm((Dd, D_MODEL), 0.02),
        "w_router": nrm((Dd, D_MODEL, N_EXPERTS), D_MODEL ** -0.5),
        "router_bias": nrm((Dd, N_EXPERTS), 0.01),
        "w_exp_gu": nrm((Dd, N_EXPERTS, D_MODEL, 2 * D_EXPERT), D_MODEL ** -0.5),
        "w_exp_down": nrm((Dd, N_EXPERTS, D_EXPERT, D_MODEL), BETA * D_EXPERT ** -0.5),
        "w_sh_gu": nrm((Dd, D_MODEL, 2 * D_SHARED), D_MODEL ** -0.5),
        "w_sh_down": nrm((Dd, D_SHARED, D_MODEL), BETA * D_SHARED ** -0.5),
        "ln2_g": 1.0 + nrm((Dd, D_MODEL), 0.02),
        "ln2_b": nrm((Dd, D_MODEL), 0.02),
    }


def reference(x_prompt, x_sample, cache_k, cache_v, cache_idx_k, state_ssm_re, state_ssm_im,
              w_in, ssm_a_re, ssm_a_im, ssm_log_dt, ssm_b_re, ssm_b_im, ssm_c_re, ssm_c_im, ssm_d,
              w_glu, w_attn_proj, w_out, ln1_g, ln1_b, w_router, router_bias,
              w_exp_gu, w_exp_down, w_sh_gu, w_sh_down, ln2_g, ln2_b):
    weights = (w_in, ssm_a_re, ssm_a_im, ssm_log_dt, ssm_b_re, ssm_b_im, ssm_c_re, ssm_c_im, ssm_d,
               w_glu, w_attn_proj, w_out, ln1_g, ln1_b, w_router, router_bias,
               w_exp_gu, w_exp_down, w_sh_gu, w_sh_down, ln2_g, ln2_b)
    past_len = cache_k.shape[2]
    pos_prompt = jnp.arange(x_prompt.shape[1], dtype=jnp.int32)
    pos_sample = past_len + jnp.arange(x_sample.shape[1], dtype=jnp.int32)
    y_p, y_s = x_prompt, x_sample
    new_p, new_s = [], []
    for l in range(DEPTH):
        p_l = tuple(w[l] for w in weights)
        y_p, st_p = layer(y_p, pos_prompt, None, p_l)
        y_s, st_s = layer(y_s, pos_sample,
                          (cache_k[l], cache_v[l], cache_idx_k[l], state_ssm_re[l], state_ssm_im[l]), p_l)
        new_p.append(st_p)
        new_s.append(st_s)
    k_p, v_p, ik_p, hre_p, him_p = [jnp.stack(a) for a in zip(*new_p)]
    k_s, v_s, ik_s, hre_s, him_s = [jnp.stack(a) for a in zip(*new_s)]
    return (y_p, y_s, k_p, v_p, ik_p, hre_p, him_p, k_s, v_s, ik_s, hre_s, him_s)
```

```python
import functools

import jax
import jax.numpy as jnp
from jax import lax
from jax.experimental import pallas as pl
from jax.experimental.pallas import tpu as pltpu

CHUNK = 64
D_SSM = 512
SSM_GROUP = 16
N_GROUPS = D_SSM // SSM_GROUP
N_STATE = 64
N_HEADS = 8
N_KV_HEADS = 2
HEAD_DIM = 64
D_ATTN = N_HEADS * HEAD_DIM
KV_GROUP = N_HEADS // N_KV_HEADS
ROPE_DIM = HEAD_DIM // 4
ROPE_THETA = 500000.0
IDX_HEADS = 8
IDX_DIM = 64
TOPK_KEYS = 256
Q_BLOCK = 64
N_EXPERTS = 64
MOE_TOPK = 6
D_EXPERT = 256
D_SHARED = 256
ROUTED_SCALE = 2.5
LN_EPS = 1e-5

LANES = 128
KEY_CHUNK = 256
EXPERT_ROWS = 256
SLOT_PAD = 8
NS = N_GROUPS * N_STATE
VMEM_LIMIT = 56 * 1024 * 1024
NEG_BIG = -1e30
INT_MIN = -2147483648

MXU_DTYPE = jnp.bfloat16


def _dot(a, b):
    return jnp.dot(a.astype(MXU_DTYPE), b.astype(MXU_DTYPE), preferred_element_type=jnp.float32)


def _dot_nt(a, b):
    return lax.dot_general(a.astype(MXU_DTYPE), b.astype(MXU_DTYPE), (((1,), (1,)), ((), ())),
                           preferred_element_type=jnp.float32)


def _params(*sem):
    return pltpu.CompilerParams(dimension_semantics=sem, vmem_limit_bytes=VMEM_LIMIT)


def _const_spec(shape):
    return pl.BlockSpec(shape, lambda *_: (0,) * len(shape), pipeline_mode=pl.Buffered(1))


_C_U, _C_Q, _C_IQ, _C_K, _C_V, _C_IK, _C_IW, _C_END = 0, 512, 1024, 1536, 1664, 1792, 1920, 2048


def _inproj_kernel(x_ref, w_ref, cos_ref, sin_ref,
                   u_ref, q_ref, iq_ref, k_ref, v_ref, ik2_ref, ik_ref, iw_ref):
    z = _dot(x_ref[...], w_ref[...])
    cos = cos_ref[...]
    sin = sin_ref[...]
    lane = lax.broadcasted_iota(jnp.int32, cos.shape, 1)
    first_half = (lane % HEAD_DIM) < (ROPE_DIM // 2)

    def rope(zc):
        partner = jnp.where(first_half, pltpu.roll(zc, LANES - ROPE_DIM // 2, 1), pltpu.roll(zc, ROPE_DIM // 2, 1))
        return zc * cos + partner * sin

    u_ref[...] = z[:, _C_U:_C_Q]
    for c in range(4):
        q_ref[:, c * LANES:(c + 1) * LANES] = rope(z[:, _C_Q + c * LANES:_C_Q + (c + 1) * LANES])
        iq_ref[:, c * LANES:(c + 1) * LANES] = rope(z[:, _C_IQ + c * LANES:_C_IQ + (c + 1) * LANES])
    k_ref[...] = rope(z[:, _C_K:_C_V])
    v_ref[...] = z[:, _C_V:_C_IK]
    ik2 = rope(z[:, _C_IK:_C_IW])
    ik2_ref[...] = ik2
    ik_ref[...] = ik2[:, :IDX_DIM]
    iw_ref[...] = z[:, _C_IW:_C_END]


def _rope_tables(pos):
    half = ROPE_DIM // 2
    inv_freq = ROPE_THETA ** (-jnp.arange(half, dtype=jnp.float32) * 2.0 / ROPE_DIM)
    ang = pos.astype(jnp.float32)[:, None] * inv_freq
    cos, sin = jnp.cos(ang), jnp.sin(ang)
    n = pos.shape[0]
    pad = HEAD_DIM - ROPE_DIM
    cos_h = jnp.concatenate([cos, cos, jnp.ones((n, pad), jnp.float32)], axis=-1)
    sin_h = jnp.concatenate([-sin, sin, jnp.zeros((n, pad), jnp.float32)], axis=-1)
    return jnp.tile(cos_h, (1, LANES // HEAD_DIM)), jnp.tile(sin_h, (1, LANES // HEAD_DIM))


def _pack_w_in(w_in):
    s = [0, 512, 1024, 1152, 1280, 1792, 1856, 1864]
    w_u, w_q, w_k, w_v = w_in[:, s[0]:s[1]], w_in[:, s[1]:s[2]], w_in[:, s[2]:s[3]], w_in[:, s[3]:s[4]]
    w_iq, w_ik, w_iw = w_in[:, s[4]:s[5]], w_in[:, s[5]:s[6]], w_in[:, s[6]:s[7]]
    d = w_in.shape[0]
    w_qh = w_q.reshape(d, N_KV_HEADS, KV_GROUP, HEAD_DIM)
    w_qp = jnp.transpose(w_qh, (0, 2, 1, 3)).reshape(d, D_ATTN)
    w_iwp = jnp.concatenate([w_iw, jnp.zeros((d, LANES - IDX_HEADS), w_in.dtype)], axis=1)
    return jnp.concatenate([w_u, w_qp, w_iq, w_k, w_v, w_ik, w_ik, w_iwp], axis=1).astype(MXU_DTYPE)


def _in_proj(x2, w_pack, pos, seq):
    t, d = x2.shape
    tm = min(512, t)
    cos, sin = _rope_tables(pos)
    if seq >= tm:
        per = seq // tm
        tab_map = lambda i: (i % per, 0)
    else:
        cos, sin = jnp.tile(cos, (tm // seq, 1)), jnp.tile(sin, (tm // seq, 1))
        tab_map = lambda i: (0, 0)
    row = lambda w: pl.BlockSpec((tm, w), lambda i: (i, 0))
    widths = (D_SSM, D_ATTN, IDX_HEADS * IDX_DIM, LANES, LANES, LANES, IDX_DIM, LANES)
    return pl.pallas_call(
        _inproj_kernel,
        grid=(t // tm,),
        in_specs=[row(d), _const_spec(w_pack.shape),
                  pl.BlockSpec((tm, LANES), tab_map), pl.BlockSpec((tm, LANES), tab_map)],
        out_specs=[row(w) for w in widths],
        out_shape=[jax.ShapeDtypeStruct((t, w), jnp.float32) for w in widths],
        compiler_params=_params("parallel"),
        name="in_proj",
    )(x2, w_pack, cos, sin)


S5_STEPS = 8
S5_COLS = 512


def _s5_kernel(u_ref, h0_ref, a_ref, bcat_ref, ccat_ref, d_ref, wglu_ref,
               out_ref, ht_ref, h_scr, bu_scr, hall_scr, *, nb):
    @pl.when(pl.program_id(0) == 0)
    def _():
        h_scr[...] = h0_ref[...]

    u = u_ref[...].reshape(S5_STEPS * nb, D_SSM)
    bu_scr[...] = _dot(u, bcat_ref[...])
    for cb in range(NS // S5_COLS):
        re = slice(cb * S5_COLS, (cb + 1) * S5_COLS)
        im = slice(NS + cb * S5_COLS, NS + (cb + 1) * S5_COLS)
        a_re, a_im = a_ref[0:1, re], a_ref[1:2, re]
        h_re, h_im = h_scr[:, re], h_scr[:, im]
        for t in range(S5_STEPS):
            rows = slice(t * nb, (t + 1) * nb)
            h_re, h_im = (a_re * h_re - a_im * h_im + bu_scr[rows, re],
                          a_re * h_im + a_im * h_re + bu_scr[rows, im])
            hall_scr[rows, re] = h_re
            hall_scr[rows, im] = h_im
        h_scr[:, re] = h_re
        h_scr[:, im] = h_im
    y = _dot(hall_scr[...], ccat_ref[...]) + d_ref[...] * u
    y = jax.nn.gelu(y)
    g = _dot(y, wglu_ref[...])
    dm = g.shape[1] // 2
    out_ref[...] = (g[:, :dm] * jax.nn.sigmoid(g[:, dm:])).reshape(S5_STEPS, nb, dm)
    ht_ref[...] = h_scr[...]


def _s5_discretise(a_re, a_im, log_dt, b_re, b_im, c_re, c_im):
    f32 = jnp.float32
    ar, ai = a_re.astype(f32), a_im.astype(f32)
    dt = jnp.exp(log_dt.astype(f32))[:, None]
    mag = jnp.exp(dt * ar)
    abar_re, abar_im = mag * jnp.cos(dt * ai), mag * jnp.sin(dt * ai)
    den = ar * ar + ai * ai
    num_re, num_im = abar_re - 1.0, abar_im
    coef_re = (num_re * ar + num_im * ai) / den
    coef_im = (num_im * ar - num_re * ai) / den
    br, bi = b_re.astype(f32), b_im.astype(f32)
    bbar_re = coef_re[..., None] * br - coef_im[..., None] * bi
    bbar_im = coef_re[..., None] * bi + coef_im[..., None] * br
    eye = jnp.eye(N_GROUPS, dtype=f32)
    bd = lambda m: jnp.einsum("gnc,gh->gchn", m, eye).reshape(D_SSM, NS)
    bcat = jnp.concatenate([bd(bbar_re), bd(bbar_im)], axis=1)
    cd = lambda m: jnp.einsum("gcn,gh->gnhc", m, eye).reshape(NS, D_SSM)
    ccat = jnp.concatenate([cd(c_re.astype(f32)), cd(-c_im.astype(f32))], axis=0)
    a_cat = jnp.stack([abar_re.reshape(NS), abar_im.reshape(NS)])
    return a_cat, bcat.astype(MXU_DTYPE), ccat.astype(MXU_DTYPE)


def _s5(u_t, h0, a_cat, bcat, ccat, d_skip, w_glu):
    seq, nb, _ = u_t.shape
    dm = w_glu.shape[1] // 2
    rows = S5_STEPS * nb
    return pl.pallas_call(
        functools.partial(_s5_kernel, nb=nb),
        grid=(seq // S5_STEPS,),
        in_specs=[pl.BlockSpec((S5_STEPS, nb, D_SSM), lambda i: (i, 0, 0)),
                  _const_spec(h0.shape), _const_spec(a_cat.shape), _const_spec(bcat.shape),
                  _const_spec(ccat.shape), _const_spec(d_skip.shape), _const_spec(w_glu.shape)],
        out_specs=[pl.BlockSpec((S5_STEPS, nb, dm), lambda i: (i, 0, 0)),
                   pl.BlockSpec(h0.shape, lambda i: (0, 0))],
        out_shape=[jax.ShapeDtypeStruct((seq, nb, dm), jnp.float32),
                   jax.ShapeDtypeStruct(h0.shape, jnp.float32)],
        scratch_shapes=[pltpu.VMEM(h0.shape, jnp.float32),
                        pltpu.VMEM((rows, 2 * NS), jnp.float32),
                        pltpu.VMEM((rows, 2 * NS), jnp.float32)],
        compiler_params=_params("arbitrary"),
        name="s5_scan_glu",
    )(u_t, h0, a_cat, bcat, ccat, d_skip, w_glu)


def _sortable(x):
    bits = pltpu.bitcast(x, jnp.int32)
    key = bits ^ ((bits >> 31) & jnp.int32(0x7FFFFFFF))
    return jnp.where(x == 0.0, jnp.int32(0), key)


def _half_mask(shape, upper):
    lane = lax.broadcasted_iota(jnp.int32, shape, 1)
    return (lane >= HEAD_DIM) if upper else (lane < HEAD_DIM)


def _dsa_kernel(q_ref, iq_ref, iw_ref, k_ref, v_ref, ik_ref, o_ref,
                kb, vb, ikb, key_scr, bias_scr, *, n_keys, n_sel, first_chunk, idx_bits):
    j = pl.program_id(1)

    @pl.when(j == 0)
    def _():
        kb[...] = k_ref[...].astype(MXU_DTYPE)
        vb[...] = v_ref[...].astype(MXU_DTYPE)
        ikb[...] = ik_ref[...].astype(MXU_DTYPE)

    kc = KEY_CHUNK
    n_vis = jnp.minimum((first_chunk + j + 1) * CHUNK, n_keys)
    n_ck = (n_vis + kc - 1) // kc
    lane_k = lax.broadcasted_iota(jnp.int32, (Q_BLOCK, kc), 1)

    iq = iq_ref[...]
    lhs = jnp.concatenate(
        [jnp.where(_half_mask((Q_BLOCK, LANES), h % 2 == 1), iq[:, (h // 2) * LANES:(h // 2 + 1) * LANES], 0.0)
         for h in range(IDX_HEADS)], axis=0).astype(MXU_DTYPE)
    iw = iw_ref[...]
    w_cols = [iw[:, h:h + 1] * IDX_HEADS ** -0.5 for h in range(IDX_HEADS)]

    def score_chunk(c, carry):
        rel = jnp.maximum(_dot_nt(lhs, ikb[c]) * IDX_DIM ** -0.5, 0.0)
        score = rel[0:Q_BLOCK] * w_cols[0]
        for h in range(1, IDX_HEADS):
            score = score + rel[h * Q_BLOCK:(h + 1) * Q_BLOCK] * w_cols[h]
        key_scr[c] = jnp.where(c * kc + lane_k < n_vis, _sortable(score), jnp.int32(INT_MIN))
        return carry

    lax.fori_loop(0, n_ck, score_chunk, 0)

    def count(pred):
        def body(c, acc):
            m = jnp.where(pred(key_scr[c], c * kc + lane_k), 1.0, 0.0)
            for s in range(kc // LANES):
                acc = acc + m[:, s * LANES:(s + 1) * LANES]
            return acc
        acc = lax.fori_loop(0, n_ck, body, jnp.zeros((Q_BLOCK, LANES), jnp.float32))
        return jnp.sum(acc, axis=-1, keepdims=True)

    def value_bit(i, t):
        cand = t + (jnp.int32(1) << (31 - i))
        return jnp.where(count(lambda key, idx: key >= cand) >= n_sel, cand, t)

    t = lax.fori_loop(0, 32, value_bit, jnp.full((Q_BLOCK, 1), INT_MIN, jnp.int32))
    need = n_sel - count(lambda key, idx: key > t)
    n_ge = count(lambda key, idx: key >= t)

    def last_tied_index():
        def index_bit(i, m):
            cand = m + (jnp.int32(1) << (idx_bits - 1 - i))
            below = count(lambda key, idx: (key == t) & (idx < cand))
            return jnp.where(below < need, cand, m)
        return lax.fori_loop(0, idx_bits, index_bit, jnp.zeros((Q_BLOCK, 1), jnp.int32))

    m_idx = lax.cond(jnp.max(n_ge) > n_sel, last_tied_index,
                     lambda: jnp.full((Q_BLOCK, 1), 2 ** 30, jnp.int32))

    def bias_chunk(c, carry):
        key = key_scr[c]
        idx = c * kc + lane_k
        sel = ((key > t) | ((key == t) & (idx <= m_idx))) & (idx < n_vis)
        bias_scr[c] = jnp.where(sel, 0.0, NEG_BIG)
        return carry

    lax.fori_loop(0, n_ck, bias_chunk, 0)

    q = q_ref[...]
    outs = []
    for g in range(N_KV_HEADS):
        lhs_g = jnp.concatenate(
            [jnp.where(_half_mask((Q_BLOCK, LANES), g == 1), q[:, r * LANES:(r + 1) * LANES], 0.0)
             for r in range(KV_GROUP)], axis=0).astype(MXU_DTYPE)
        rows = KV_GROUP * Q_BLOCK

        def att_chunk(c, carry):
            m, l, acc = carry
            s = _dot_nt(lhs_g, kb[c]) * HEAD_DIM ** -0.5
            s = (s.reshape(KV_GROUP, Q_BLOCK, kc) + bias_scr[c][None]).reshape(rows, kc)
            m_new = jnp.maximum(m, jnp.max(s, axis=-1, keepdims=True))
            alpha = jnp.exp(m - m_new)
            p = jnp.exp(s - m_new)
            l = alpha * l + jnp.sum(p, axis=-1, keepdims=True)
            acc = alpha * acc + _dot(p, vb[c])
            return m_new, l, acc

        init = (jnp.full((rows, 1), NEG_BIG, jnp.float32), jnp.zeros((rows, 1), jnp.float32),
                jnp.zeros((rows, LANES), jnp.float32))
        _, l, acc = lax.fori_loop(0, n_ck, att_chunk, init)
        outs.append(acc / l)
    lower = _half_mask((Q_BLOCK, LANES), False)
    for r in range(KV_GROUP):
        rs = slice(r * Q_BLOCK, (r + 1) * Q_BLOCK)
        o_ref[:, r * LANES:(r + 1) * LANES] = jnp.where(lower, outs[0][rs], outs[1][rs])


def _dsa(q, iq, iw, k_all, v_all, ik2_all, first_chunk):
    nb, seq, _ = q.shape
    n_keys = k_all.shape[1]
    n_sel = min(TOPK_KEYS, n_keys // 4)
    kc = KEY_CHUNK
    n_ch = -(-n_keys // kc)
    pad = n_ch * kc - n_keys
    chunked = lambda a: jnp.pad(a, ((0, 0), (0, pad), (0, 0))).reshape(nb, n_ch, kc, LANES)
    qspec = lambda w: pl.BlockSpec((None, Q_BLOCK, w), lambda b, j: (b, j, 0))
    kspec = pl.BlockSpec((None, n_ch, kc, LANES), lambda b, j: (b, 0, 0, 0))
    return pl.pallas_call(
        functools.partial(_dsa_kernel, n_keys=n_keys, n_sel=n_sel, first_chunk=first_chunk,
                          idx_bits=max(1, (n_ch * kc - 1).bit_length())),
        grid=(nb, seq // Q_BLOCK),
        in_specs=[qspec(D_ATTN), qspec(IDX_HEADS * IDX_DIM), qspec(LANES), kspec, kspec, kspec],
        out_specs=qspec(D_ATTN),
        out_shape=jax.ShapeDtypeStruct((nb, seq, D_ATTN), jnp.float32),
        scratch_shapes=[pltpu.VMEM((n_ch, kc, LANES), MXU_DTYPE)] * 3
        + [pltpu.VMEM((n_ch, Q_BLOCK, kc), jnp.int32), pltpu.VMEM((n_ch, Q_BLOCK, kc), jnp.float32)],
        compiler_params=_params("arbitrary", "arbitrary"),
        name="dsa_attention",
    )(q, iq, iw, chunked(k_all), chunked(v_all), chunked(ik2_all))


def _layer_norm(h, g, b):
    mu = jnp.mean(h, axis=-1, keepdims=True)
    var = jnp.mean(jnp.square(h - mu), axis=-1, keepdims=True)
    return (h - mu) * lax.rsqrt(var + LN_EPS) * g + b


def _mix_kernel(x_ref, attn_ref, ssm_ref, wg_ref, wap_ref, wout_ref, g1_ref, b1_ref, wr_ref, rb_ref,
                x1_ref, e_ref, gate_ref, rank_ref, cnt_ref, cnt_scr, *, alpha):
    i = pl.program_id(0)

    @pl.when(i == 0)
    def _():
        cnt_scr[...] = jnp.zeros_like(cnt_scr)

    x = x_ref[...]
    tm, dm = x.shape
    gates = jax.nn.sigmoid(_dot(x, wg_ref[...]))
    attn_out = _dot(attn_ref[...], wap_ref[...])
    mixed = _dot(gates[:, :dm] * ssm_ref[...] + gates[:, dm:] * attn_out, wout_ref[...])
    x1 = _layer_norm(alpha * x + mixed, g1_ref[...], b1_ref[...])
    x1_ref[...] = x1

    logits = jnp.dot(x1, wr_ref[...], preferred_element_type=jnp.float32, precision=lax.Precision.HIGHEST)
    scores = jax.nn.sigmoid(logits)
    lane = lax.broadcasted_iota(jnp.int32, scores.shape, 1).astype(jnp.float32)
    slot = lax.broadcasted_iota(jnp.int32, (tm, SLOT_PAD), 1)
    cur = scores + rb_ref[...]
    chosen = jnp.zeros_like(scores)
    picks = []
    for _ in range(MOE_TOPK):
        best = jnp.max(cur, axis=-1, keepdims=True)
        e_k = jnp.min(jnp.where(cur == best, lane, float(N_EXPERTS)), axis=-1, keepdims=True)
        hot = lane == e_k
        picks.append((e_k, hot, jnp.sum(jnp.where(hot, scores, 0.0), axis=-1, keepdims=True)))
        chosen = jnp.where(hot, 1.0, chosen)
        cur = jnp.where(hot, -jnp.inf, cur)
    total = picks[0][2]
    for _, _, s_k in picks[1:]:
        total = total + s_k

    row = lax.broadcasted_iota(jnp.int32, (tm, tm), 0)
    col = lax.broadcasted_iota(jnp.int32, (tm, tm), 1)
    before = _dot(jnp.where(col < row, 1.0, 0.0), chosen) + cnt_scr[...]
    e_out = jnp.zeros((tm, SLOT_PAD), jnp.int32)
    g_out = jnp.zeros((tm, SLOT_PAD), jnp.float32)
    r_out = jnp.zeros((tm, SLOT_PAD), jnp.int32)
    for k, (e_k, hot, s_k) in enumerate(picks):
        rank_k = jnp.sum(jnp.where(hot, before, 0.0), axis=-1, keepdims=True)
        e_out = jnp.where(slot == k, e_k.astype(jnp.int32), e_out)
        g_out = jnp.where(slot == k, s_k / total * ROUTED_SCALE, g_out)
        r_out = jnp.where(slot == k, rank_k.astype(jnp.int32), r_out)
    e_ref[...] = e_out
    gate_ref[...] = g_out
    rank_ref[...] = r_out
    cnt_scr[...] = cnt_scr[...] + jnp.sum(chosen, axis=0, keepdims=True)
    cnt_ref[...] = cnt_scr[...]


def _mix(x2, attn2, ssm2, wg, wap, wout, ln_g, ln_b, w_router, router_bias, alpha):
    t, dm = x2.shape
    tm = min(256, t)
    row = lambda w: pl.BlockSpec((tm, w), lambda i: (i, 0))
    consts = (wg, wap, wout, ln_g, ln_b, w_router, router_bias)
    return pl.pallas_call(
        functools.partial(_mix_kernel, alpha=alpha),
        grid=(t // tm,),
        in_specs=[row(dm), row(D_ATTN), row(dm)] + [_const_spec(c.shape) for c in consts],
        out_specs=[row(dm), row(SLOT_PAD), row(SLOT_PAD), row(SLOT_PAD),
                   pl.BlockSpec((1, N_EXPERTS), lambda i: (0, 0))],
        out_shape=[jax.ShapeDtypeStruct((t, dm), jnp.float32),
                   jax.ShapeDtypeStruct((t, SLOT_PAD), jnp.int32),
                   jax.ShapeDtypeStruct((t, SLOT_PAD), jnp.float32),
                   jax.ShapeDtypeStruct((t, SLOT_PAD), jnp.int32),
                   jax.ShapeDtypeStruct((1, N_EXPERTS), jnp.float32)],
        scratch_shapes=[pltpu.VMEM((1, N_EXPERTS), jnp.float32)],
        compiler_params=_params("arbitrary"),
        name="mix_ln_router",
    )(x2, attn2, ssm2, *consts)


def _dispatch_kernel(start_ref, zlo_ref, zhi_ref, e_ref, r_ref, x_ref, xs_ref, zero_scr, sem, *, tm):
    i = pl.program_id(0)

    def row_copy(src_ref, src_row, dst_row):
        return pltpu.make_async_copy(src_ref.at[pl.ds(src_row, 1)], xs_ref.at[pl.ds(dst_row, 1)], sem)

    def send(tok, carry):
        for k in range(MOE_TOPK):
            dst = start_ref[e_ref[tok * SLOT_PAD + k]] + r_ref[tok * SLOT_PAD + k]
            row_copy(x_ref, tok, dst).start()
        return carry

    lax.fori_loop(0, tm, send, 0)

    def drain(n, carry):
        row_copy(x_ref, 0, 0).wait()
        return carry

    lax.fori_loop(0, tm * MOE_TOPK, drain, 0)

    @pl.when(i == pl.num_programs(0) - 1)
    def _():
        zero_scr[...] = jnp.zeros_like(zero_scr)

        def segment(s, carry):
            lo, hi = zlo_ref[s], zhi_ref[s]

            def fill(r, c):
                row_copy(zero_scr, 0, r).start()
                return c

            def fill_done(r, c):
                row_copy(zero_scr, 0, r).wait()
                return c

            lax.fori_loop(lo, hi, fill, 0)
            lax.fori_loop(lo, hi, fill_done, 0)
            return carry

        lax.fori_loop(0, N_EXPERTS + 1, segment, 0)


def _dispatch(x1, e_flat, r_flat, start, zlo, zhi, n_rows):
    t, dm = x1.shape
    tm = min(512, t)
    flat = pl.BlockSpec((tm * SLOT_PAD,), lambda i, *_: (i,), memory_space=pltpu.SMEM)
    return pl.pallas_call(
        functools.partial(_dispatch_kernel, tm=tm),
        grid_spec=pltpu.PrefetchScalarGridSpec(
            num_scalar_prefetch=3,
            grid=(t // tm,),
            in_specs=[flat, flat, pl.BlockSpec((tm, dm), lambda i, *_: (i, 0))],
            out_specs=pl.BlockSpec(memory_space=pl.ANY),
            scratch_shapes=[pltpu.VMEM((8, dm), jnp.float32), pltpu.SemaphoreType.DMA(())],
        ),
        out_shape=jax.ShapeDtypeStruct((n_rows, dm), jnp.float32),
        compiler_params=_params("arbitrary"),
        name="moe_dispatch",
    )(start, zlo, zhi, e_flat, r_flat, x1)


def _expert_kernel(be_ref, xs_ref, wgu_ref, wdn_ref, ys_ref, wgu_scr, wdn_scr):
    i = pl.program_id(0)

    @pl.when((i == 0) | (be_ref[i] != be_ref[jnp.maximum(i - 1, 0)]))
    def _():
        wgu_scr[...] = wgu_ref[...].astype(MXU_DTYPE)
        wdn_scr[...] = wdn_ref[...].astype(MXU_DTYPE)

    h = _dot(xs_ref[...], wgu_scr[...])
    f = h.shape[1] // 2
    ys_ref[...] = _dot(jax.nn.silu(h[:, :f]) * h[:, f:], wdn_scr[...])


def _experts(xs, block_expert, w_gu, w_down):
    n_rows, dm = xs.shape
    bm = EXPERT_ROWS
    f2 = w_gu.shape[2]
    return pl.pallas_call(
        _expert_kernel,
        grid_spec=pltpu.PrefetchScalarGridSpec(
            num_scalar_prefetch=1,
            grid=(n_rows // bm,),
            in_specs=[pl.BlockSpec((bm, dm), lambda i, be: (i, 0)),
                      pl.BlockSpec((None, dm, f2), lambda i, be: (be[i], 0, 0)),
                      pl.BlockSpec((None, f2 // 2, dm), lambda i, be: (be[i], 0, 0))],
            out_specs=pl.BlockSpec((bm, dm), lambda i, be: (i, 0)),
            scratch_shapes=[pltpu.VMEM((dm, f2), MXU_DTYPE), pltpu.VMEM((f2 // 2, dm), MXU_DTYPE)],
        ),
        out_shape=jax.ShapeDtypeStruct((n_rows, dm), jnp.float32),
        compiler_params=_params("arbitrary"),
        name="moe_experts",
    )(block_expert, xs, w_gu, w_down)


def _combine_kernel(start_ref, e_ref, r_ref, x1_ref, gate_ref, wsgu_ref, wsdn_ref, g2_ref, b2_ref, ys_ref,
                    y_ref, buf, sem, *, tm, alpha):
    def row_copy(k, tok, src_row):
        return pltpu.make_async_copy(ys_ref.at[pl.ds(src_row, 1)], buf.at[k, pl.ds(tok, 1)], sem)

    def fetch(tok, carry):
        for k in range(MOE_TOPK):
            src = start_ref[e_ref[tok * SLOT_PAD + k]] + r_ref[tok * SLOT_PAD + k]
            row_copy(k, tok, src).start()
        return carry

    lax.fori_loop(0, tm, fetch, 0)

    x1 = x1_ref[...]
    h = _dot(x1, wsgu_ref[...])
    f = h.shape[1] // 2
    shared = _dot(jax.nn.silu(h[:, :f]) * h[:, f:], wsdn_ref[...])

    def drain(n, carry):
        row_copy(0, 0, 0).wait()
        return carry

    lax.fori_loop(0, tm * MOE_TOPK, drain, 0)

    gate = gate_ref[...]
    routed = buf[0] * gate[:, 0:1]
    for k in range(1, MOE_TOPK):
        routed = routed + buf[k] * gate[:, k:k + 1]
    y_ref[...] = _layer_norm(alpha * x1 + (routed + shared), g2_ref[...], b2_ref[...])


def _combine(x1, gate, e_flat, r_flat, start, ys, w_sh_gu, w_sh_down, ln_g, ln_b, alpha):
    t, dm = x1.shape
    tm = min(256, t)
    flat = pl.BlockSpec((tm * SLOT_PAD,), lambda i, *_: (i,), memory_space=pltpu.SMEM)
    const = lambda a: pl.BlockSpec(a.shape, lambda i, *_: (0,) * a.ndim, pipeline_mode=pl.Buffered(1))
    return pl.pallas_call(
        functools.partial(_combine_kernel, tm=tm, alpha=alpha),
        grid_spec=pltpu.PrefetchScalarGridSpec(
            num_scalar_prefetch=1,
            grid=(t // tm,),
            in_specs=[flat, flat, pl.BlockSpec((tm, dm), lambda i, *_: (i, 0)),
                      pl.BlockSpec((tm, SLOT_PAD), lambda i, *_: (i, 0)),
                      const(w_sh_gu), const(w_sh_down), const(ln_g), const(ln_b),
                      pl.BlockSpec(memory_space=pl.ANY)],
            out_specs=pl.BlockSpec((tm, dm), lambda i, *_: (i, 0)),
            scratch_shapes=[pltpu.VMEM((MOE_TOPK, tm, dm), jnp.float32), pltpu.SemaphoreType.DMA(())],
        ),
        out_shape=jax.ShapeDtypeStruct((t, dm), jnp.float32),
        compiler_params=_params("arbitrary"),
        name="moe_combine",
    )(start, e_flat, r_flat, x1, gate, w_sh_gu, w_sh_down, ln_g, ln_b, ys)


def _moe(x1, e_k, gate, rank, counts, w_exp_gu, w_exp_down, w_sh_gu, w_sh_down, ln_g, ln_b, alpha):
    t = x1.shape[0]
    bm = EXPERT_ROWS
    n_rows = (-(-(t * MOE_TOPK) // bm) + N_EXPERTS) * bm
    cnt = counts.reshape(N_EXPERTS).astype(jnp.int32)
    padded = (cnt + bm - 1) // bm * bm
    end = jnp.cumsum(padded)
    start = end - padded
    zlo = jnp.concatenate([start + cnt, end[-1:]])
    zhi = jnp.concatenate([end, jnp.full((1,), n_rows, jnp.int32)])
    block_start = jnp.arange(n_rows // bm, dtype=jnp.int32) * bm
    block_expert = jnp.minimum(jnp.searchsorted(end, block_start, side="right"), N_EXPERTS - 1).astype(jnp.int32)
    e_flat, r_flat = e_k.reshape(-1), rank.reshape(-1)
    xs = _dispatch(x1, e_flat, r_flat, start, zlo, zhi, n_rows)
    ys = _experts(xs, block_expert, w_exp_gu, w_exp_down)
    return _combine(x1, gate, e_flat, r_flat, start, ys, w_sh_gu, w_sh_down, ln_g, ln_b, alpha)


def _layer(x, pos0, past, p, alpha):
    (w_in, a_re, a_im, log_dt, b_re, b_im, c_re, c_im, ssm_d, w_glu, w_attn_proj, w_out, ln1_g, ln1_b,
     w_router, router_bias, w_exp_gu, w_exp_down, w_sh_gu, w_sh_down, ln2_g, ln2_b) = p
    nb, seq, dm = x.shape
    assert pos0 % CHUNK == 0 and seq % Q_BLOCK == 0 and Q_BLOCK == CHUNK and seq % S5_STEPS == 0
    assert nb * seq * MOE_TOPK < 2 ** 24
    t = nb * seq
    f32 = jnp.float32
    x2 = x.reshape(t, dm)
    pos = pos0 + jnp.arange(seq, dtype=jnp.int32)

    u, q, iq, k, v, ik2, ik, iw = _in_proj(x2, _pack_w_in(w_in), pos, seq)

    a_cat, bcat, ccat = _s5_discretise(a_re, a_im, log_dt, b_re, b_im, c_re, c_im)
    if past is None:
        h0 = jnp.zeros((nb, 2 * NS), f32)
    else:
        h0 = jnp.concatenate([past[3].reshape(nb, NS), past[4].reshape(nb, NS)], axis=1).astype(f32)
    u_t = jnp.transpose(u.reshape(nb, seq, D_SSM), (1, 0, 2))
    ssm_t, h_t = _s5(u_t, h0, a_cat, bcat, ccat, ssm_d.reshape(1, D_SSM).astype(f32), w_glu.astype(MXU_DTYPE))
    ssm2 = jnp.transpose(ssm_t, (1, 0, 2)).reshape(t, dm)
    h_re = h_t[:, :NS].reshape(nb, N_GROUPS, N_STATE)
    h_im = h_t[:, NS:].reshape(nb, N_GROUPS, N_STATE)

    k3, v3, ik23 = (a.reshape(nb, seq, LANES) for a in (k, v, ik2))
    if past is not None:
        ck, cv, cik = past[0], past[1], past[2]
        n_past = ck.shape[1]
        k3 = jnp.concatenate([ck.reshape(nb, n_past, LANES), k3], axis=1)
        v3 = jnp.concatenate([cv.reshape(nb, n_past, LANES), v3], axis=1)
        ik23 = jnp.concatenate([jnp.concatenate([cik, cik], axis=-1), ik23], axis=1)
    attn = _dsa(q.reshape(nb, seq, D_ATTN), iq.reshape(nb, seq, IDX_HEADS * IDX_DIM), iw.reshape(nb, seq, LANES),
                k3, v3, ik23, pos0 // CHUNK)

    wap = jnp.transpose(w_attn_proj.reshape(N_KV_HEADS, KV_GROUP, HEAD_DIM, dm), (1, 0, 2, 3)).reshape(D_ATTN, dm)
    wg = w_in[:, w_in.shape[1] - 2 * dm:]
    row = lambda a: a.reshape(1, -1).astype(f32)
    x1, e_k, gate, rank, counts = _mix(
        x2, attn.reshape(t, D_ATTN), ssm2, wg.astype(MXU_DTYPE), wap.astype(MXU_DTYPE), w_out.astype(MXU_DTYPE),
        row(ln1_g), row(ln1_b), w_router.astype(f32), row(router_bias), alpha)

    y = _moe(x1, e_k, gate, rank, counts, w_exp_gu, w_exp_down,
             w_sh_gu.astype(MXU_DTYPE), w_sh_down.astype(MXU_DTYPE), row(ln2_g), row(ln2_b), alpha)
    state = (k.reshape(nb, seq, N_KV_HEADS, HEAD_DIM), v.reshape(nb, seq, N_KV_HEADS, HEAD_DIM),
             ik.reshape(nb, seq, IDX_DIM), h_re, h_im)
    return y.reshape(nb, seq, dm), state


def kernel(x_prompt, x_sample, cache_k, cache_v, cache_idx_k, state_ssm_re, state_ssm_im, w_in, ssm_a_re, ssm_a_im, ssm_log_dt, ssm_b_re, ssm_b_im, ssm_c_re, ssm_c_im, ssm_d, w_glu, w_attn_proj, w_out, ln1_g, ln1_b, w_router, router_bias, w_exp_gu, w_exp_down, w_sh_gu, w_sh_down, ln2_g, ln2_b):
    weights = (w_in, ssm_a_re, ssm_a_im, ssm_log_dt, ssm_b_re, ssm_b_im, ssm_c_re, ssm_c_im, ssm_d,
               w_glu, w_attn_proj, w_out, ln1_g, ln1_b, w_router, router_bias,
               w_exp_gu, w_exp_down, w_sh_gu, w_sh_down, ln2_g, ln2_b)
    depth = w_in.shape[0]
    alpha = (2 * depth) ** 0.25
    past_len = cache_k.shape[2]
    y_p, y_s = x_prompt, x_sample
    new_p, new_s = [], []
    for l in range(depth):
        p_l = tuple(w[l] for w in weights)
        y_p, st_p = _layer(y_p, 0, None, p_l, alpha)
        y_s, st_s = _layer(y_s, past_len,
                           (cache_k[l], cache_v[l], cache_idx_k[l], state_ssm_re[l], state_ssm_im[l]), p_l, alpha)
        new_p.append(st_p)
        new_s.append(st_s)
    k_p, v_p, ik_p, hre_p, him_p = [jnp.stack(a) for a in zip(*new_p)]
    k_s, v_s, ik_s, hre_s, him_s = [jnp.stack(a) for a in zip(*new_s)]
    return (y_p, y_s, k_p, v_p, ik_p, hre_p, him_p, k_s, v_s, ik_s, hre_s, him_s)
```

```python
import functools

import jax
import jax.numpy as jnp
from jax import lax
from jax.experimental import pallas as pl
from jax.experimental.pallas import tpu as pltpu

CHUNK = 64
D_SSM = 512
SSM_GROUP = 16
N_GROUPS = D_SSM // SSM_GROUP
N_STATE = 64
N_HEADS = 8
N_KV_HEADS = 2
HEAD_DIM = 64
D_ATTN = N_HEADS * HEAD_DIM
KV_GROUP = N_HEADS // N_KV_HEADS
ROPE_DIM = HEAD_DIM // 4
ROPE_THETA = 500000.0
IDX_HEADS = 8
IDX_DIM = 64
TOPK_KEYS = 256
Q_BLOCK = 64
N_EXPERTS = 64
MOE_TOPK = 6
D_EXPERT = 256
D_SHARED = 256
ROUTED_SCALE = 2.5
LN_EPS = 1e-5

LANES = 128
KEY_CHUNK = 256
DSA_ROWS = 256
EXPERT_ROWS = 256
SLOT_PAD = 8
NS = N_GROUPS * N_STATE
VMEM_LIMIT = 56 * 1024 * 1024
NEG_BIG = -1e30
INT_MIN = -2147483648

MXU_DTYPE = jnp.bfloat16


def _dot(a, b):
    return jnp.dot(a.astype(MXU_DTYPE), b.astype(MXU_DTYPE), preferred_element_type=jnp.float32)


def _dot_nt(a, b):
    return lax.dot_general(a.astype(MXU_DTYPE), b.astype(MXU_DTYPE), (((1,), (1,)), ((), ())),
                           preferred_element_type=jnp.float32)


def _params(*sem):
    return pltpu.CompilerParams(dimension_semantics=sem, vmem_limit_bytes=VMEM_LIMIT)


def _const_spec(shape):
    return pl.BlockSpec(shape, lambda *_: (0,) * len(shape), pipeline_mode=pl.Buffered(1))


_C_U, _C_Q, _C_IQ, _C_K, _C_V, _C_IK, _C_IW, _C_END = 0, 512, 1024, 1536, 1664, 1792, 1920, 2048


def _inproj_kernel(x_ref, w_ref, cos_ref, sin_ref,
                   u_ref, q_ref, iq_ref, k_ref, v_ref, ik2_ref, ik_ref, iw_ref):
    z = _dot(x_ref[...], w_ref[...])
    cos = cos_ref[...]
    sin = sin_ref[...]
    lane = lax.broadcasted_iota(jnp.int32, cos.shape, 1)
    first_half = (lane % HEAD_DIM) < (ROPE_DIM // 2)

    def rope(zc):
        partner = jnp.where(first_half, pltpu.roll(zc, LANES - ROPE_DIM // 2, 1), pltpu.roll(zc, ROPE_DIM // 2, 1))
        return zc * cos + partner * sin

    u_ref[...] = z[:, _C_U:_C_Q]
    for c in range(4):
        q_ref[:, c * LANES:(c + 1) * LANES] = rope(z[:, _C_Q + c * LANES:_C_Q + (c + 1) * LANES])
        iq_ref[:, c * LANES:(c + 1) * LANES] = rope(z[:, _C_IQ + c * LANES:_C_IQ + (c + 1) * LANES])
    k_ref[...] = rope(z[:, _C_K:_C_V])
    v_ref[...] = z[:, _C_V:_C_IK]
    ik2 = rope(z[:, _C_IK:_C_IW])
    ik2_ref[...] = ik2
    ik_ref[...] = ik2[:, :IDX_DIM]
    iw_ref[...] = z[:, _C_IW:_C_END]


def _rope_tables(pos):
    half = ROPE_DIM // 2
    inv_freq = ROPE_THETA ** (-jnp.arange(half, dtype=jnp.float32) * 2.0 / ROPE_DIM)
    ang = pos.astype(jnp.float32)[:, None] * inv_freq
    cos, sin = jnp.cos(ang), jnp.sin(ang)
    n = pos.shape[0]
    pad = HEAD_DIM - ROPE_DIM
    cos_h = jnp.concatenate([cos, cos, jnp.ones((n, pad), jnp.float32)], axis=-1)
    sin_h = jnp.concatenate([-sin, sin, jnp.zeros((n, pad), jnp.float32)], axis=-1)
    return jnp.tile(cos_h, (1, LANES // HEAD_DIM)), jnp.tile(sin_h, (1, LANES // HEAD_DIM))


def _pack_w_in(w_in):
    s = [0, 512, 1024, 1152, 1280, 1792, 1856, 1864]
    w_u, w_q, w_k, w_v = w_in[:, s[0]:s[1]], w_in[:, s[1]:s[2]], w_in[:, s[2]:s[3]], w_in[:, s[3]:s[4]]
    w_iq, w_ik, w_iw = w_in[:, s[4]:s[5]], w_in[:, s[5]:s[6]], w_in[:, s[6]:s[7]]
    d = w_in.shape[0]
    w_qh = w_q.reshape(d, N_KV_HEADS, KV_GROUP, HEAD_DIM)
    w_qp = jnp.transpose(w_qh, (0, 2, 1, 3)).reshape(d, D_ATTN)
    w_iwp = jnp.concatenate([w_iw, jnp.zeros((d, LANES - IDX_HEADS), w_in.dtype)], axis=1)
    return jnp.concatenate([w_u, w_qp, w_iq, w_k, w_v, w_ik, w_ik, w_iwp], axis=1).astype(MXU_DTYPE)


def _in_proj(x2, w_pack, pos, seq):
    t, d = x2.shape
    tm = min(512, t)
    cos, sin = _rope_tables(pos)
    if seq >= tm:
        per = seq // tm
        tab_map = lambda i: (i % per, 0)
    else:
        cos, sin = jnp.tile(cos, (tm // seq, 1)), jnp.tile(sin, (tm // seq, 1))
        tab_map = lambda i: (0, 0)
    row = lambda w: pl.BlockSpec((tm, w), lambda i: (i, 0))
    widths = (D_SSM, D_ATTN, IDX_HEADS * IDX_DIM, LANES, LANES, LANES, IDX_DIM, LANES)
    return pl.pallas_call(
        _inproj_kernel,
        grid=(t // tm,),
        in_specs=[row(d), _const_spec(w_pack.shape),
                  pl.BlockSpec((tm, LANES), tab_map), pl.BlockSpec((tm, LANES), tab_map)],
        out_specs=[row(w) for w in widths],
        out_shape=[jax.ShapeDtypeStruct((t, w), jnp.float32) for w in widths],
        compiler_params=_params("parallel"),
        name="in_proj",
    )(x2, w_pack, cos, sin)


S5_STEPS = 8
S5_COLS = 512


def _s5_kernel(u_ref, h0_ref, a_ref, bcat_ref, ccat_ref, d_ref, wglu_ref,
               out_ref, ht_ref, h_scr, bu_scr, hall_scr, *, nb):
    @pl.when(pl.program_id(0) == 0)
    def _():
        h_scr[...] = h0_ref[...]

    u = u_ref[...].reshape(S5_STEPS * nb, D_SSM)
    bu_scr[...] = _dot(u, bcat_ref[...])
    for cb in range(NS // S5_COLS):
        re = slice(cb * S5_COLS, (cb + 1) * S5_COLS)
        im = slice(NS + cb * S5_COLS, NS + (cb + 1) * S5_COLS)
        a_re, a_im = a_ref[0:1, re], a_ref[1:2, re]
        h_re, h_im = h_scr[:, re], h_scr[:, im]
        for t in range(S5_STEPS):
            rows = slice(t * nb, (t + 1) * nb)
            h_re, h_im = (a_re * h_re - a_im * h_im + bu_scr[rows, re],
                          a_re * h_im + a_im * h_re + bu_scr[rows, im])
            hall_scr[rows, re] = h_re
            hall_scr[rows, im] = h_im
        h_scr[:, re] = h_re
        h_scr[:, im] = h_im
    y = _dot(hall_scr[...], ccat_ref[...]) + d_ref[...] * u
    y = jax.nn.gelu(y)
    g = _dot(y, wglu_ref[...])
    dm = g.shape[1] // 2
    out_ref[...] = (g[:, :dm] * jax.nn.sigmoid(g[:, dm:])).reshape(S5_STEPS, nb, dm)
    ht_ref[...] = h_scr[...]


def _s5_discretise(a_re, a_im, log_dt, b_re, b_im, c_re, c_im):
    f32 = jnp.float32
    ar, ai = a_re.astype(f32), a_im.astype(f32)
    dt = jnp.exp(log_dt.astype(f32))[:, None]
    mag = jnp.exp(dt * ar)
    abar_re, abar_im = mag * jnp.cos(dt * ai), mag * jnp.sin(dt * ai)
    den = ar * ar + ai * ai
    num_re, num_im = abar_re - 1.0, abar_im
    coef_re = (num_re * ar + num_im * ai) / den
    coef_im = (num_im * ar - num_re * ai) / den
    br, bi = b_re.astype(f32), b_im.astype(f32)
    bbar_re = coef_re[..., None] * br - coef_im[..., None] * bi
    bbar_im = coef_re[..., None] * bi + coef_im[..., None] * br
    eye = jnp.eye(N_GROUPS, dtype=f32)
    bd = lambda m: jnp.einsum("gnc,gh->gchn", m, eye).reshape(D_SSM, NS)
    bcat = jnp.concatenate([bd(bbar_re), bd(bbar_im)], axis=1)
    cd = lambda m: jnp.einsum("gcn,gh->gnhc", m, eye).reshape(NS, D_SSM)
    ccat = jnp.concatenate([cd(c_re.astype(f32)), cd(-c_im.astype(f32))], axis=0)
    a_cat = jnp.stack([abar_re.reshape(NS), abar_im.reshape(NS)])
    return a_cat, bcat.astype(MXU_DTYPE), ccat.astype(MXU_DTYPE)


def _s5(u_t, h0, a_cat, bcat, ccat, d_skip, w_glu):
    seq, nb, _ = u_t.shape
    dm = w_glu.shape[1] // 2
    rows = S5_STEPS * nb
    return pl.pallas_call(
        functools.partial(_s5_kernel, nb=nb),
        grid=(seq // S5_STEPS,),
        in_specs=[pl.BlockSpec((S5_STEPS, nb, D_SSM), lambda i: (i, 0, 0)),
                  _const_spec(h0.shape), _const_spec(a_cat.shape), _const_spec(bcat.shape),
                  _const_spec(ccat.shape), _const_spec(d_skip.shape), _const_spec(w_glu.shape)],
        out_specs=[pl.BlockSpec((S5_STEPS, nb, dm), lambda i: (i, 0, 0)),
                   pl.BlockSpec(h0.shape, lambda i: (0, 0))],
        out_shape=[jax.ShapeDtypeStruct((seq, nb, dm), jnp.float32),
                   jax.ShapeDtypeStruct(h0.shape, jnp.float32)],
        scratch_shapes=[pltpu.VMEM(h0.shape, jnp.float32),
                        pltpu.VMEM((rows, 2 * NS), jnp.float32),
                        pltpu.VMEM((rows, 2 * NS), jnp.float32)],
        compiler_params=_params("arbitrary"),
        name="s5_scan_glu",
    )(u_t, h0, a_cat, bcat, ccat, d_skip, w_glu)


def _sortable(x):
    bits = pltpu.bitcast(x, jnp.int32)
    key = bits ^ ((bits >> 31) & jnp.int32(0x7FFFFFFF))
    return jnp.where(x == 0.0, jnp.int32(0), key)


def _half_mask(shape, upper):
    lane = lax.broadcasted_iota(jnp.int32, shape, 1)
    return (lane >= HEAD_DIM) if upper else (lane < HEAD_DIM)


def _dsa_kernel(q_ref, iq_ref, iw_ref, k_ref, v_ref, ik_ref, o_ref,
                kb, vs0, vs1, ikb, key_scr, keyt_scr, bias_scr, wb_scr, iqlhs_scr, qlhs_scr, s_scr, m_scr, acc_scr,
                *, qt, n_keys, n_sel, first_chunk, idx_bits):
    j = pl.program_id(1)
    kc = KEY_CHUNK
    n_sub = qt // Q_BLOCK
    qtp = max(qt, LANES)

    @pl.when(j == 0)
    def _():
        v = v_ref[...]
        lower = lax.broadcasted_iota(jnp.int32, v.shape, v.ndim - 1) < HEAD_DIM
        kb[...] = k_ref[...].astype(MXU_DTYPE)
        vs0[...] = jnp.where(lower, v, 1.0).astype(MXU_DTYPE)
        vs1[...] = jnp.where(lower, 1.0, v).astype(MXU_DTYPE)
        ikb[...] = ik_ref[...].astype(MXU_DTYPE)

    row_blk = lax.broadcasted_iota(jnp.int32, (qt, 1), 0) // Q_BLOCK
    n_vis = jnp.minimum((first_chunk + j * n_sub + row_blk + 1) * CHUNK, n_keys)
    n_ck = (jnp.minimum((first_chunk + (j + 1) * n_sub) * CHUNK, n_keys) + kc - 1) // kc
    lane_k = lax.broadcasted_iota(jnp.int32, (qt, kc), 1)

    iq = iq_ref[...] * IDX_DIM ** -0.5
    iw = iw_ref[...] * IDX_HEADS ** -0.5
    for h in range(IDX_HEADS):
        iqlhs_scr[h] = jnp.where(_half_mask((qt, LANES), h % 2 == 1),
                                 iq[:, (h // 2) * LANES:(h // 2 + 1) * LANES], 0.0).astype(MXU_DTYPE)
        wb_scr[h] = jnp.broadcast_to(iw[:, h:h + 1], (qt, LANES))

    def score_chunk(c, carry):
        ikc = ikb[c]
        score = None
        for h in range(IDX_HEADS):
            rel = jnp.maximum(_dot_nt(iqlhs_scr[h], ikc), 0.0)
            w = wb_scr[h]
            term = jnp.concatenate([rel[:, s * LANES:(s + 1) * LANES] * w for s in range(kc // LANES)], axis=1)
            score = term if score is None else score + term
        key = jnp.where(c * kc + lane_k < n_vis, _sortable(score), jnp.int32(INT_MIN))
        key_scr[c] = key
        if qtp > qt:
            key = jnp.concatenate([key, jnp.full((qtp - qt, kc), INT_MIN, jnp.int32)], axis=0)
        keyt_scr[c] = key.T
        return carry

    lax.fori_loop(0, n_ck, score_chunk, 0)

    sub_k = lax.broadcasted_iota(jnp.int32, (kc // 8, 8, qtp), 0) * 8 + lax.broadcasted_iota(
        jnp.int32, (kc // 8, 8, qtp), 1)

    def count(pred):
        def body(c, acc):
            hit = pred(keyt_scr[c].reshape(kc // 8, 8, qtp), c * kc + sub_k)
            return acc + jnp.sum(jnp.where(hit, 1.0, 0.0), axis=0)
        acc = lax.fori_loop(0, n_ck, body, jnp.zeros((8, qtp), jnp.float32))
        for shift in (4, 2, 1):
            acc = acc + pltpu.roll(acc, shift, 0)
        return acc

    def value_bit(i, t):
        cand = t + (jnp.int32(1) << (31 - i))
        return jnp.where(count(lambda key, idx: key >= cand[None]) >= n_sel, cand, t)

    t = lax.fori_loop(0, 32, value_bit, jnp.full((8, qtp), INT_MIN, jnp.int32))
    need = n_sel - count(lambda key, idx: key > t[None])
    n_ge = count(lambda key, idx: key >= t[None])

    def last_tied_index():
        def index_bit(i, m):
            cand = m + (jnp.int32(1) << (idx_bits - 1 - i))
            below = count(lambda key, idx: (key == t[None]) & (idx < cand[None]))
            return jnp.where(below < need, cand, m)
        return lax.fori_loop(0, idx_bits, index_bit, jnp.zeros((8, qtp), jnp.int32))

    m_idx = lax.cond(jnp.max(n_ge) > n_sel, last_tied_index,
                     lambda: jnp.full((8, qtp), 2 ** 30, jnp.int32))
    t_rep = jnp.broadcast_to(t[0:1], (LANES, qtp)).T[:qt]
    m_rep = jnp.broadcast_to(m_idx[0:1], (LANES, qtp)).T[:qt]
    t_row = jnp.concatenate([t_rep] * (kc // LANES), axis=1)
    m_row = jnp.concatenate([m_rep] * (kc // LANES), axis=1)

    def bias_chunk(c, carry):
        key = key_scr[c]
        idx = c * kc + lane_k
        sel = ((key > t_row) | ((key == t_row) & (idx <= m_row))) & (idx < n_vis)
        bias_scr[c] = jnp.where(sel, 0.0, NEG_BIG)
        return carry

    lax.fori_loop(0, n_ck, bias_chunk, 0)

    q = q_ref[...] * HEAD_DIM ** -0.5
    rows = KV_GROUP * qt
    for g in range(N_KV_HEADS):
        for r in range(KV_GROUP):
            qlhs_scr[g, r * qt:(r + 1) * qt] = jnp.where(
                _half_mask((qt, LANES), g == 1), q[:, r * LANES:(r + 1) * LANES], 0.0).astype(MXU_DTYPE)
    m_scr[...] = jnp.full(m_scr.shape, NEG_BIG, jnp.float32)
    acc_scr[...] = jnp.zeros(acc_scr.shape, jnp.float32)

    def logits_chunk(c, carry):
        bias = bias_scr[c][None]
        for g in range(N_KV_HEADS):
            s = (_dot_nt(qlhs_scr[g], kb[c]).reshape(KV_GROUP, qt, kc) + bias).reshape(rows, kc)
            s_scr[g, c] = s
            m = m_scr[g]
            for u in range(kc // LANES):
                m = jnp.maximum(m, s[:, u * LANES:(u + 1) * LANES])
            m_scr[g] = m
        return carry

    lax.fori_loop(0, n_ck, logits_chunk, 0)
    for g in range(N_KV_HEADS):
        m_scr[g] = jnp.broadcast_to(jnp.max(m_scr[g], axis=-1, keepdims=True), (rows, LANES))

    def pv_chunk(c, carry):
        for g, vs in enumerate((vs0, vs1)):
            m = m_scr[g]
            s = s_scr[g, c]
            p = jnp.concatenate([jnp.exp(s[:, u * LANES:(u + 1) * LANES] - m) for u in range(kc // LANES)], axis=1)
            acc_scr[g] += _dot(p, vs[c])
        return carry

    lax.fori_loop(0, n_ck, pv_chunk, 0)
    outs = [acc_scr[g] / pltpu.roll(acc_scr[g], HEAD_DIM, 1) for g in range(N_KV_HEADS)]
    lower = _half_mask((qt, LANES), False)
    for r in range(KV_GROUP):
        rs = slice(r * qt, (r + 1) * qt)
        o_ref[:, r * LANES:(r + 1) * LANES] = jnp.where(lower, outs[0][rs], outs[1][rs])


def _dsa(q, iq, iw, k_all, v_all, ik2_all, first_chunk):
    nb, seq, _ = q.shape
    n_keys = k_all.shape[1]
    n_sel = min(TOPK_KEYS, n_keys // 4)
    kc = KEY_CHUNK
    qt = min(DSA_ROWS, seq)
    assert seq % qt == 0
    n_ch = -(-n_keys // kc)
    pad = n_ch * kc - n_keys
    chunked = lambda a: jnp.pad(a, ((0, 0), (0, pad), (0, 0))).reshape(nb, n_ch, kc, LANES)
    qspec = lambda w: pl.BlockSpec((None, qt, w), lambda b, j: (b, j, 0))
    kspec = pl.BlockSpec((None, n_ch, kc, LANES), lambda b, j: (b, 0, 0, 0))
    rows = KV_GROUP * qt
    return pl.pallas_call(
        functools.partial(_dsa_kernel, qt=qt, n_keys=n_keys, n_sel=n_sel, first_chunk=first_chunk,
                          idx_bits=max(1, (n_ch * kc - 1).bit_length())),
        grid=(nb, seq // qt),
        in_specs=[qspec(D_ATTN), qspec(IDX_HEADS * IDX_DIM), qspec(LANES), kspec, kspec, kspec],
        out_specs=qspec(D_ATTN),
        out_shape=jax.ShapeDtypeStruct((nb, seq, D_ATTN), jnp.float32),
        scratch_shapes=[pltpu.VMEM((n_ch, kc, LANES), MXU_DTYPE)] * 4
        + [pltpu.VMEM((n_ch, qt, kc), jnp.int32), pltpu.VMEM((n_ch, kc, max(qt, LANES)), jnp.int32),
           pltpu.VMEM((n_ch, qt, kc), jnp.float32),
           pltpu.VMEM((IDX_HEADS, qt, LANES), jnp.float32), pltpu.VMEM((IDX_HEADS, qt, LANES), MXU_DTYPE),
           pltpu.VMEM((N_KV_HEADS, rows, LANES), MXU_DTYPE),
           pltpu.VMEM((N_KV_HEADS, n_ch, rows, kc), jnp.float32),
           pltpu.VMEM((N_KV_HEADS, rows, LANES), jnp.float32), pltpu.VMEM((N_KV_HEADS, rows, LANES), jnp.float32)],
        compiler_params=_params("arbitrary", "arbitrary"),
        name="dsa_attention",
    )(q, iq, iw, chunked(k_all), chunked(v_all), chunked(ik2_all))


def _layer_norm(h, g, b):
    mu = jnp.mean(h, axis=-1, keepdims=True)
    var = jnp.mean(jnp.square(h - mu), axis=-1, keepdims=True)
    return (h - mu) * lax.rsqrt(var + LN_EPS) * g + b


def _mix_kernel(x_ref, attn_ref, ssm_ref, wg_ref, wap_ref, wout_ref, g1_ref, b1_ref, wr_ref, rb_ref,
                x1_ref, e_ref, gate_ref, rank_ref, cnt_ref, cnt_scr, *, alpha):
    i = pl.program_id(0)

    @pl.when(i == 0)
    def _():
        cnt_scr[...] = jnp.zeros_like(cnt_scr)

    x = x_ref[...]
    tm, dm = x.shape
    gates = jax.nn.sigmoid(_dot(x, wg_ref[...]))
    attn_out = _dot(attn_ref[...], wap_ref[...])
    mixed = _dot(gates[:, :dm] * ssm_ref[...] + gates[:, dm:] * attn_out, wout_ref[...])
    x1 = _layer_norm(alpha * x + mixed, g1_ref[...], b1_ref[...])
    x1_ref[...] = x1

    logits = jnp.dot(x1, wr_ref[...], preferred_element_type=jnp.float32, precision=lax.Precision.HIGHEST)
    scores = jax.nn.sigmoid(logits)
    lane = lax.broadcasted_iota(jnp.int32, scores.shape, 1).astype(jnp.float32)
    slot = lax.broadcasted_iota(jnp.int32, (tm, SLOT_PAD), 1)
    cur = scores + rb_ref[...]
    chosen = jnp.zeros_like(scores)
    picks = []
    for _ in range(MOE_TOPK):
        best = jnp.max(cur, axis=-1, keepdims=True)
        e_k = jnp.min(jnp.where(cur == best, lane, float(N_EXPERTS)), axis=-1, keepdims=True)
        hot = lane == e_k
        picks.append((e_k, hot, jnp.sum(jnp.where(hot, scores, 0.0), axis=-1, keepdims=True)))
        chosen = jnp.where(hot, 1.0, chosen)
        cur = jnp.where(hot, -jnp.inf, cur)
    total = picks[0][2]
    for _, _, s_k in picks[1:]:
        total = total + s_k

    row = lax.broadcasted_iota(jnp.int32, (tm, tm), 0)
    col = lax.broadcasted_iota(jnp.int32, (tm, tm), 1)
    before = _dot(jnp.where(col < row, 1.0, 0.0), chosen) + cnt_scr[...]
    e_out = jnp.zeros((tm, SLOT_PAD), jnp.int32)
    g_out = jnp.zeros((tm, SLOT_PAD), jnp.float32)
    r_out = jnp.zeros((tm, SLOT_PAD), jnp.int32)
    for k, (e_k, hot, s_k) in enumerate(picks):
        rank_k = jnp.sum(jnp.where(hot, before, 0.0), axis=-1, keepdims=True)
        e_out = jnp.where(slot == k, e_k.astype(jnp.int32), e_out)
        g_out = jnp.where(slot == k, s_k / total * ROUTED_SCALE, g_out)
        r_out = jnp.where(slot == k, rank_k.astype(jnp.int32), r_out)
    e_ref[...] = e_out
    gate_ref[...] = g_out
    rank_ref[...] = r_out
    cnt_scr[...] = cnt_scr[...] + jnp.sum(chosen, axis=0, keepdims=True)
    cnt_ref[...] = cnt_scr[...]


def _mix(x2, attn2, ssm2, wg, wap, wout, ln_g, ln_b, w_router, router_bias, alpha):
    t, dm = x2.shape
    tm = min(512, t)
    row = lambda w: pl.BlockSpec((tm, w), lambda i: (i, 0))
    consts = (wg, wap, wout, ln_g, ln_b, w_router, router_bias)
    return pl.pallas_call(
        functools.partial(_mix_kernel, alpha=alpha),
        grid=(t // tm,),
        in_specs=[row(dm), row(D_ATTN), row(dm)] + [_const_spec(c.shape) for c in consts],
        out_specs=[row(dm), row(SLOT_PAD), row(SLOT_PAD), row(SLOT_PAD),
                   pl.BlockSpec((1, N_EXPERTS), lambda i: (0, 0))],
        out_shape=[jax.ShapeDtypeStruct((t, dm), jnp.float32),
                   jax.ShapeDtypeStruct((t, SLOT_PAD), jnp.int32),
                   jax.ShapeDtypeStruct((t, SLOT_PAD), jnp.float32),
                   jax.ShapeDtypeStruct((t, SLOT_PAD), jnp.int32),
                   jax.ShapeDtypeStruct((1, N_EXPERTS), jnp.float32)],
        scratch_shapes=[pltpu.VMEM((1, N_EXPERTS), jnp.float32)],
        compiler_params=_params("arbitrary"),
        name="mix_ln_router",
    )(x2, attn2, ssm2, *consts)


def _dispatch_kernel(start_ref, zlo_ref, zhi_ref, e_ref, r_ref, x_ref, xs_ref, zero_scr, sem, *, tm):
    i = pl.program_id(0)

    def row_copy(src_ref, src_row, dst_row):
        return pltpu.make_async_copy(src_ref.at[pl.ds(src_row, 1)], xs_ref.at[pl.ds(dst_row, 1)], sem)

    def send(tok, carry):
        for k in range(MOE_TOPK):
            dst = start_ref[e_ref[tok * SLOT_PAD + k]] + r_ref[tok * SLOT_PAD + k]
            row_copy(x_ref, tok, dst).start()
        return carry

    lax.fori_loop(0, tm, send, 0)

    for _ in range(MOE_TOPK):
        pltpu.make_async_copy(x_ref, x_ref, sem).wait()

    @pl.when(i == pl.num_programs(0) - 1)
    def _():
        zero_scr[...] = jnp.zeros_like(zero_scr)

        def segment(s, carry):
            lo, hi = zlo_ref[s], zhi_ref[s]

            def fill(r, c):
                row_copy(zero_scr, 0, r).start()
                return c

            def fill_done(r, c):
                row_copy(zero_scr, 0, r).wait()
                return c

            lax.fori_loop(lo, hi, fill, 0)
            lax.fori_loop(lo, hi, fill_done, 0)
            return carry

        lax.fori_loop(0, N_EXPERTS + 1, segment, 0)


def _dispatch(x1, e_flat, r_flat, start, zlo, zhi, n_rows):
    t, dm = x1.shape
    tm = min(512, t)
    flat = pl.BlockSpec((tm * SLOT_PAD,), lambda i, *_: (i,), memory_space=pltpu.SMEM)
    return pl.pallas_call(
        functools.partial(_dispatch_kernel, tm=tm),
        grid_spec=pltpu.PrefetchScalarGridSpec(
            num_scalar_prefetch=3,
            grid=(t // tm,),
            in_specs=[flat, flat, pl.BlockSpec((tm, dm), lambda i, *_: (i, 0))],
            out_specs=pl.BlockSpec(memory_space=pl.ANY),
            scratch_shapes=[pltpu.VMEM((8, dm), jnp.float32), pltpu.SemaphoreType.DMA(())],
        ),
        out_shape=jax.ShapeDtypeStruct((n_rows, dm), jnp.float32),
        compiler_params=_params("arbitrary"),
        name="moe_dispatch",
    )(start, zlo, zhi, e_flat, r_flat, x1)


def _expert_kernel(be_ref, xs_ref, wgu_ref, wdn_ref, ys_ref, wgu_scr, wdn_scr):
    i = pl.program_id(0)

    @pl.when((i == 0) | (be_ref[i] != be_ref[jnp.maximum(i - 1, 0)]))
    def _():
        wgu_scr[...] = wgu_ref[...].astype(MXU_DTYPE)
        wdn_scr[...] = wdn_ref[...].astype(MXU_DTYPE)

    h = _dot(xs_ref[...], wgu_scr[...])
    f = h.shape[1] // 2
    ys_ref[...] = _dot(jax.nn.silu(h[:, :f]) * h[:, f:], wdn_scr[...])


def _experts(xs, block_expert, w_gu, w_down):
    n_rows, dm = xs.shape
    bm = EXPERT_ROWS
    f2 = w_gu.shape[2]
    return pl.pallas_call(
        _expert_kernel,
        grid_spec=pltpu.PrefetchScalarGridSpec(
            num_scalar_prefetch=1,
            grid=(n_rows // bm,),
            in_specs=[pl.BlockSpec((bm, dm), lambda i, be: (i, 0)),
                      pl.BlockSpec((None, dm, f2), lambda i, be: (be[i], 0, 0)),
                      pl.BlockSpec((None, f2 // 2, dm), lambda i, be: (be[i], 0, 0))],
            out_specs=pl.BlockSpec((bm, dm), lambda i, be: (i, 0)),
            scratch_shapes=[pltpu.VMEM((dm, f2), MXU_DTYPE), pltpu.VMEM((f2 // 2, dm), MXU_DTYPE)],
        ),
        out_shape=jax.ShapeDtypeStruct((n_rows, dm), jnp.float32),
        compiler_params=_params("arbitrary"),
        name="moe_experts",
    )(block_expert, xs, w_gu, w_down)


def _combine_kernel(start_ref, e_ref, r_ref, x1_ref, gate_ref, wsgu_ref, wsdn_ref, g2_ref, b2_ref, ys_ref,
                    y_ref, buf, sem, *, tm, alpha):
    def row_copy(k, tok, src_row):
        return pltpu.make_async_copy(ys_ref.at[pl.ds(src_row, 1)], buf.at[k, pl.ds(tok, 1)], sem)

    def fetch(tok, carry):
        for k in range(MOE_TOPK):
            src = start_ref[e_ref[tok * SLOT_PAD + k]] + r_ref[tok * SLOT_PAD + k]
            row_copy(k, tok, src).start()
        return carry

    lax.fori_loop(0, tm, fetch, 0)

    x1 = x1_ref[...]
    h = _dot(x1, wsgu_ref[...])
    f = h.shape[1] // 2
    shared = _dot(jax.nn.silu(h[:, :f]) * h[:, f:], wsdn_ref[...])

    for k in range(MOE_TOPK):
        pltpu.make_async_copy(buf.at[k], buf.at[k], sem).wait()

    gate = gate_ref[...]
    routed = buf[0] * gate[:, 0:1]
    for k in range(1, MOE_TOPK):
        routed = routed + buf[k] * gate[:, k:k + 1]
    y_ref[...] = _layer_norm(alpha * x1 + (routed + shared), g2_ref[...], b2_ref[...])


def _combine(x1, gate, e_flat, r_flat, start, ys, w_sh_gu, w_sh_down, ln_g, ln_b, alpha):
    t, dm = x1.shape
    tm = min(256, t)
    flat = pl.BlockSpec((tm * SLOT_PAD,), lambda i, *_: (i,), memory_space=pltpu.SMEM)
    const = lambda a: pl.BlockSpec(a.shape, lambda i, *_: (0,) * a.ndim, pipeline_mode=pl.Buffered(1))
    return pl.pallas_call(
        functools.partial(_combine_kernel, tm=tm, alpha=alpha),
        grid_spec=pltpu.PrefetchScalarGridSpec(
            num_scalar_prefetch=1,
            grid=(t // tm,),
            in_specs=[flat, flat, pl.BlockSpec((tm, dm), lambda i, *_: (i, 0)),
                      pl.BlockSpec((tm, SLOT_PAD), lambda i, *_: (i, 0)),
                      const(w_sh_gu), const(w_sh_down), const(ln_g), const(ln_b),
                      pl.BlockSpec(memory_space=pl.ANY)],
            out_specs=pl.BlockSpec((tm, dm), lambda i, *_: (i, 0)),
            scratch_shapes=[pltpu.VMEM((MOE_TOPK, tm, dm), jnp.float32), pltpu.SemaphoreType.DMA(())],
        ),
        out_shape=jax.ShapeDtypeStruct((t, dm), jnp.float32),
        compiler_params=_params("arbitrary"),
        name="moe_combine",
    )(start, e_flat, r_flat, x1, gate, w_sh_gu, w_sh_down, ln_g, ln_b, ys)


def _moe(x1, e_k, gate, rank, counts, w_exp_gu, w_exp_down, w_sh_gu, w_sh_down, ln_g, ln_b, alpha):
    t = x1.shape[0]
    bm = EXPERT_ROWS
    n_rows = (-(-(t * MOE_TOPK) // bm) + N_EXPERTS) * bm
    cnt = counts.reshape(N_EXPERTS).astype(jnp.int32)
    padded = (cnt + bm - 1) // bm * bm
    end = jnp.cumsum(padded)
    start = end - padded
    zlo = jnp.concatenate([start + cnt, end[-1:]])
    zhi = jnp.concatenate([end, jnp.full((1,), n_rows, jnp.int32)])
    block_start = jnp.arange(n_rows // bm, dtype=jnp.int32) * bm
    block_expert = jnp.minimum(jnp.sum((end[None, :] <= block_start[:, None]).astype(jnp.int32), axis=1),
                               N_EXPERTS - 1)
    e_flat, r_flat = e_k.reshape(-1), rank.reshape(-1)
    xs = _dispatch(x1, e_flat, r_flat, start, zlo, zhi, n_rows)
    ys = _experts(xs, block_expert, w_exp_gu, w_exp_down)
    return _combine(x1, gate, e_flat, r_flat, start, ys, w_sh_gu, w_sh_down, ln_g, ln_b, alpha)


def _layer(x, pos0, past, p, alpha):
    (w_in, a_re, a_im, log_dt, b_re, b_im, c_re, c_im, ssm_d, w_glu, w_attn_proj, w_out, ln1_g, ln1_b,
     w_router, router_bias, w_exp_gu, w_exp_down, w_sh_gu, w_sh_down, ln2_g, ln2_b) = p
    nb, seq, dm = x.shape
    assert pos0 % CHUNK == 0 and seq % Q_BLOCK == 0 and Q_BLOCK == CHUNK and seq % S5_STEPS == 0
    assert nb * seq * MOE_TOPK < 2 ** 24
    t = nb * seq
    f32 = jnp.float32
    x2 = x.reshape(t, dm)
    pos = pos0 + jnp.arange(seq, dtype=jnp.int32)

    u, q, iq, k, v, ik2, ik, iw = _in_proj(x2, _pack_w_in(w_in), pos, seq)

    a_cat, bcat, ccat = _s5_discretise(a_re, a_im, log_dt, b_re, b_im, c_re, c_im)
    if past is None:
        h0 = jnp.zeros((nb, 2 * NS), f32)
    else:
        h0 = jnp.concatenate([past[3].reshape(nb, NS), past[4].reshape(nb, NS)], axis=1).astype(f32)
    u_t = jnp.transpose(u.reshape(nb, seq, D_SSM), (1, 0, 2))
    ssm_t, h_t = _s5(u_t, h0, a_cat, bcat, ccat, ssm_d.reshape(1, D_SSM).astype(f32), w_glu.astype(MXU_DTYPE))
    ssm2 = jnp.transpose(ssm_t, (1, 0, 2)).reshape(t, dm)
    h_re = h_t[:, :NS].reshape(nb, N_GROUPS, N_STATE)
    h_im = h_t[:, NS:].reshape(nb, N_GROUPS, N_STATE)

    k3, v3, ik23 = (a.reshape(nb, seq, LANES) for a in (k, v, ik2))
    if past is not None:
        ck, cv, cik = past[0], past[1], past[2]
        n_past = ck.shape[1]
        k3 = jnp.concatenate([ck.reshape(nb, n_past, LANES), k3], axis=1)
        v3 = jnp.concatenate([cv.reshape(nb, n_past, LANES), v3], axis=1)
        ik23 = jnp.concatenate([jnp.concatenate([cik, cik], axis=-1), ik23], axis=1)
    attn = _dsa(q.reshape(nb, seq, D_ATTN), iq.reshape(nb, seq, IDX_HEADS * IDX_DIM), iw.reshape(nb, seq, LANES),
                k3, v3, ik23, pos0 // CHUNK)

    wap = jnp.transpose(w_attn_proj.reshape(N_KV_HEADS, KV_GROUP, HEAD_DIM, dm), (1, 0, 2, 3)).reshape(D_ATTN, dm)
    wg = w_in[:, w_in.shape[1] - 2 * dm:]
    row = lambda a: a.reshape(1, -1).astype(f32)
    x1, e_k, gate, rank, counts = _mix(
        x2, attn.reshape(t, D_ATTN), ssm2, wg.astype(MXU_DTYPE), wap.astype(MXU_DTYPE), w_out.astype(MXU_DTYPE),
        row(ln1_g), row(ln1_b), w_router.astype(f32), row(router_bias), alpha)

    y = _moe(x1, e_k, gate, rank, counts, w_exp_gu, w_exp_down,
             w_sh_gu.astype(MXU_DTYPE), w_sh_down.astype(MXU_DTYPE), row(ln2_g), row(ln2_b), alpha)
    state = (k.reshape(nb, seq, N_KV_HEADS, HEAD_DIM), v.reshape(nb, seq, N_KV_HEADS, HEAD_DIM),
             ik.reshape(nb, seq, IDX_DIM), h_re, h_im)
    return y.reshape(nb, seq, dm), state


def kernel(x_prompt, x_sample, cache_k, cache_v, cache_idx_k, state_ssm_re, state_ssm_im, w_in, ssm_a_re, ssm_a_im, ssm_log_dt, ssm_b_re, ssm_b_im, ssm_c_re, ssm_c_im, ssm_d, w_glu, w_attn_proj, w_out, ln1_g, ln1_b, w_router, router_bias, w_exp_gu, w_exp_down, w_sh_gu, w_sh_down, ln2_g, ln2_b):
    weights = (w_in, ssm_a_re, ssm_a_im, ssm_log_dt, ssm_b_re, ssm_b_im, ssm_c_re, ssm_c_im, ssm_d,
               w_glu, w_attn_proj, w_out, ln1_g, ln1_b, w_router, router_bias,
               w_exp_gu, w_exp_down, w_sh_gu, w_sh_down, ln2_g, ln2_b)
    depth = w_in.shape[0]
    alpha = (2 * depth) ** 0.25
    past_len = cache_k.shape[2]
    y_p, y_s = x_prompt, x_sample
    new_p, new_s = [], []
    for l in range(depth):
        p_l = tuple(w[l] for w in weights)
        y_p, st_p = _layer(y_p, 0, None, p_l, alpha)
        y_s, st_s = _layer(y_s, past_len,
                           (cache_k[l], cache_v[l], cache_idx_k[l], state_ssm_re[l], state_ssm_im[l]), p_l, alpha)
        new_p.append(st_p)
        new_s.append(st_s)
    k_p, v_p, ik_p, hre_p, him_p = [jnp.stack(a) for a in zip(*new_p)]
    k_s, v_s, ik_s, hre_s, him_s = [jnp.stack(a) for a in zip(*new_s)]
    return (y_p, y_s, k_p, v_p, ik_p, hre_p, him_p, k_s, v_s, ik_s, hre_s, him_s)
```

```python
import functools

import jax
import jax.numpy as jnp
from jax import lax
from jax.experimental import pallas as pl
from jax.experimental.pallas import tpu as pltpu

CHUNK = 64
D_SSM = 512
SSM_GROUP = 16
N_GROUPS = D_SSM // SSM_GROUP
N_STATE = 64
N_HEADS = 8
N_KV_HEADS = 2
HEAD_DIM = 64
D_ATTN = N_HEADS * HEAD_DIM
KV_GROUP = N_HEADS // N_KV_HEADS
ROPE_DIM = HEAD_DIM // 4
ROPE_THETA = 500000.0
IDX_HEADS = 8
IDX_DIM = 64
TOPK_KEYS = 256
Q_BLOCK = 64
N_EXPERTS = 64
MOE_TOPK = 6
D_EXPERT = 256
D_SHARED = 256
ROUTED_SCALE = 2.5
LN_EPS = 1e-5

LANES = 128
KEY_CHUNK = 256
DSA_ROWS = 256
EXPERT_ROWS = 256
MOE_TILE = 512
SEG_ALIGN = 8
SLOT_PAD = 8
NS = N_GROUPS * N_STATE
VMEM_LIMIT = 56 * 1024 * 1024
NEG_BIG = -1e30
INT_MIN = -2147483648

MXU_DTYPE = jnp.bfloat16


def _dot(a, b):
    return jnp.dot(a.astype(MXU_DTYPE), b.astype(MXU_DTYPE), preferred_element_type=jnp.float32)


def _dot_nt(a, b):
    return lax.dot_general(a.astype(MXU_DTYPE), b.astype(MXU_DTYPE), (((1,), (1,)), ((), ())),
                           preferred_element_type=jnp.float32)


def _params(*sem):
    return pltpu.CompilerParams(dimension_semantics=sem, vmem_limit_bytes=VMEM_LIMIT)


def _const_spec(shape):
    return pl.BlockSpec(shape, lambda *_: (0,) * len(shape), pipeline_mode=pl.Buffered(1))


_C_U, _C_Q, _C_IQ, _C_K, _C_V, _C_IK, _C_IW, _C_END = 0, 512, 1024, 1536, 1664, 1792, 1920, 2048


def _inproj_kernel(x_ref, w_ref, cos_ref, sin_ref,
                   u_ref, q_ref, iq_ref, k_ref, v_ref, ik2_ref, ik_ref, iw_ref):
    z = _dot(x_ref[...], w_ref[...])
    cos = cos_ref[...]
    sin = sin_ref[...]
    lane = lax.broadcasted_iota(jnp.int32, cos.shape, 1)
    first_half = (lane % HEAD_DIM) < (ROPE_DIM // 2)

    def rope(zc):
        partner = jnp.where(first_half, pltpu.roll(zc, LANES - ROPE_DIM // 2, 1), pltpu.roll(zc, ROPE_DIM // 2, 1))
        return zc * cos + partner * sin

    u_ref[...] = z[:, _C_U:_C_Q]
    for c in range(4):
        q_ref[:, c * LANES:(c + 1) * LANES] = rope(z[:, _C_Q + c * LANES:_C_Q + (c + 1) * LANES])
        iq_ref[:, c * LANES:(c + 1) * LANES] = rope(z[:, _C_IQ + c * LANES:_C_IQ + (c + 1) * LANES])
    k_ref[...] = rope(z[:, _C_K:_C_V])
    v_ref[...] = z[:, _C_V:_C_IK]
    ik2 = rope(z[:, _C_IK:_C_IW])
    ik2_ref[...] = ik2
    ik_ref[...] = ik2[:, :IDX_DIM]
    iw_ref[...] = z[:, _C_IW:_C_END]


def _rope_tables(pos):
    half = ROPE_DIM // 2
    inv_freq = ROPE_THETA ** (-jnp.arange(half, dtype=jnp.float32) * 2.0 / ROPE_DIM)
    ang = pos.astype(jnp.float32)[:, None] * inv_freq
    cos, sin = jnp.cos(ang), jnp.sin(ang)
    n = pos.shape[0]
    pad = HEAD_DIM - ROPE_DIM
    cos_h = jnp.concatenate([cos, cos, jnp.ones((n, pad), jnp.float32)], axis=-1)
    sin_h = jnp.concatenate([-sin, sin, jnp.zeros((n, pad), jnp.float32)], axis=-1)
    return jnp.tile(cos_h, (1, LANES // HEAD_DIM)), jnp.tile(sin_h, (1, LANES // HEAD_DIM))


def _pack_w_in(w_in):
    s = [0, 512, 1024, 1152, 1280, 1792, 1856, 1864]
    w_u, w_q, w_k, w_v = w_in[:, s[0]:s[1]], w_in[:, s[1]:s[2]], w_in[:, s[2]:s[3]], w_in[:, s[3]:s[4]]
    w_iq, w_ik, w_iw = w_in[:, s[4]:s[5]], w_in[:, s[5]:s[6]], w_in[:, s[6]:s[7]]
    d = w_in.shape[0]
    w_qh = w_q.reshape(d, N_KV_HEADS, KV_GROUP, HEAD_DIM)
    w_qp = jnp.transpose(w_qh, (0, 2, 1, 3)).reshape(d, D_ATTN)
    w_iwp = jnp.concatenate([w_iw, jnp.zeros((d, LANES - IDX_HEADS), w_in.dtype)], axis=1)
    return jnp.concatenate([w_u, w_qp, w_iq, w_k, w_v, w_ik, w_ik, w_iwp], axis=1).astype(MXU_DTYPE)


def _in_proj(x2, w_pack, pos, seq):
    t, d = x2.shape
    tm = min(512, t)
    cos, sin = _rope_tables(pos)
    if seq >= tm:
        per = seq // tm
        tab_map = lambda i: (i % per, 0)
    else:
        cos, sin = jnp.tile(cos, (tm // seq, 1)), jnp.tile(sin, (tm // seq, 1))
        tab_map = lambda i: (0, 0)
    row = lambda w: pl.BlockSpec((tm, w), lambda i: (i, 0))
    widths = (D_SSM, D_ATTN, IDX_HEADS * IDX_DIM, LANES, LANES, LANES, IDX_DIM, LANES)
    return pl.pallas_call(
        _inproj_kernel,
        grid=(t // tm,),
        in_specs=[row(d), _const_spec(w_pack.shape),
                  pl.BlockSpec((tm, LANES), tab_map), pl.BlockSpec((tm, LANES), tab_map)],
        out_specs=[row(w) for w in widths],
        out_shape=[jax.ShapeDtypeStruct((t, w), jnp.float32) for w in widths],
        compiler_params=_params("parallel"),
        name="in_proj",
    )(x2, w_pack, cos, sin)


S5_STEPS = 8
S5_COLS = 512


def _s5_kernel(u_ref, h0_ref, a_ref, bcat_ref, ccat_ref, d_ref, wglu_ref,
               out_ref, ht_ref, h_scr, bu_scr, hall_scr, *, nb):
    @pl.when(pl.program_id(0) == 0)
    def _():
        h_scr[...] = h0_ref[...]

    u = u_ref[...].reshape(S5_STEPS * nb, D_SSM)
    bu_scr[...] = _dot(u, bcat_ref[...])
    for cb in range(NS // S5_COLS):
        re = slice(cb * S5_COLS, (cb + 1) * S5_COLS)
        im = slice(NS + cb * S5_COLS, NS + (cb + 1) * S5_COLS)
        a_re, a_im = a_ref[0:1, re], a_ref[1:2, re]
        h_re, h_im = h_scr[:, re], h_scr[:, im]
        for t in range(S5_STEPS):
            rows = slice(t * nb, (t + 1) * nb)
            h_re, h_im = (a_re * h_re - a_im * h_im + bu_scr[rows, re],
                          a_re * h_im + a_im * h_re + bu_scr[rows, im])
            hall_scr[rows, re] = h_re
            hall_scr[rows, im] = h_im
        h_scr[:, re] = h_re
        h_scr[:, im] = h_im
    y = _dot(hall_scr[...], ccat_ref[...]) + d_ref[...] * u
    y = jax.nn.gelu(y)
    g = _dot(y, wglu_ref[...])
    dm = g.shape[1] // 2
    out_ref[...] = (g[:, :dm] * jax.nn.sigmoid(g[:, dm:])).reshape(S5_STEPS, nb, dm)
    ht_ref[...] = h_scr[...]


def _s5_discretise(a_re, a_im, log_dt, b_re, b_im, c_re, c_im):
    f32 = jnp.float32
    ar, ai = a_re.astype(f32), a_im.astype(f32)
    dt = jnp.exp(log_dt.astype(f32))[:, None]
    mag = jnp.exp(dt * ar)
    abar_re, abar_im = mag * jnp.cos(dt * ai), mag * jnp.sin(dt * ai)
    den = ar * ar + ai * ai
    num_re, num_im = abar_re - 1.0, abar_im
    coef_re = (num_re * ar + num_im * ai) / den
    coef_im = (num_im * ar - num_re * ai) / den
    br, bi = b_re.astype(f32), b_im.astype(f32)
    bbar_re = coef_re[..., None] * br - coef_im[..., None] * bi
    bbar_im = coef_re[..., None] * bi + coef_im[..., None] * br
    eye = jnp.eye(N_GROUPS, dtype=f32)
    bd = lambda m: jnp.einsum("gnc,gh->gchn", m, eye).reshape(D_SSM, NS)
    bcat = jnp.concatenate([bd(bbar_re), bd(bbar_im)], axis=1)
    cd = lambda m: jnp.einsum("gcn,gh->gnhc", m, eye).reshape(NS, D_SSM)
    ccat = jnp.concatenate([cd(c_re.astype(f32)), cd(-c_im.astype(f32))], axis=0)
    a_cat = jnp.stack([abar_re.reshape(NS), abar_im.reshape(NS)])
    return a_cat, bcat.astype(MXU_DTYPE), ccat.astype(MXU_DTYPE)


def _s5(u_t, h0, a_cat, bcat, ccat, d_skip, w_glu):
    seq, nb, _ = u_t.shape
    dm = w_glu.shape[1] // 2
    rows = S5_STEPS * nb
    return pl.pallas_call(
        functools.partial(_s5_kernel, nb=nb),
        grid=(seq // S5_STEPS,),
        in_specs=[pl.BlockSpec((S5_STEPS, nb, D_SSM), lambda i: (i, 0, 0)),
                  _const_spec(h0.shape), _const_spec(a_cat.shape), _const_spec(bcat.shape),
                  _const_spec(ccat.shape), _const_spec(d_skip.shape), _const_spec(w_glu.shape)],
        out_specs=[pl.BlockSpec((S5_STEPS, nb, dm), lambda i: (i, 0, 0)),
                   pl.BlockSpec(h0.shape, lambda i: (0, 0))],
        out_shape=[jax.ShapeDtypeStruct((seq, nb, dm), jnp.float32),
                   jax.ShapeDtypeStruct(h0.shape, jnp.float32)],
        scratch_shapes=[pltpu.VMEM(h0.shape, jnp.float32),
                        pltpu.VMEM((rows, 2 * NS), jnp.float32),
                        pltpu.VMEM((rows, 2 * NS), jnp.float32)],
        compiler_params=_params("arbitrary"),
        name="s5_scan_glu",
    )(u_t, h0, a_cat, bcat, ccat, d_skip, w_glu)


def _sortable(x):
    bits = pltpu.bitcast(x, jnp.int32)
    key = bits ^ ((bits >> 31) & jnp.int32(0x7FFFFFFF))
    return jnp.where(x == 0.0, jnp.int32(0), key)


def _half_mask(shape, upper):
    lane = lax.broadcasted_iota(jnp.int32, shape, 1)
    return (lane >= HEAD_DIM) if upper else (lane < HEAD_DIM)


def _dsa_kernel(q_ref, iq_ref, iw_ref, k_ref, v_ref, ik_ref, o_ref,
                kb, vs0, vs1, ikb, key_scr, keyt_scr, bias_scr, wb_scr, iqlhs_scr, qlhs_scr, s_scr, m_scr, acc_scr,
                *, qt, n_keys, n_sel, first_chunk, idx_bits):
    j = pl.program_id(1)
    kc = KEY_CHUNK
    n_sub = qt // Q_BLOCK
    qtp = max(qt, LANES)

    @pl.when(j == 0)
    def _():
        v = v_ref[...]
        lower = lax.broadcasted_iota(jnp.int32, v.shape, v.ndim - 1) < HEAD_DIM
        kb[...] = k_ref[...].astype(MXU_DTYPE)
        vs0[...] = jnp.where(lower, v, 1.0).astype(MXU_DTYPE)
        vs1[...] = jnp.where(lower, 1.0, v).astype(MXU_DTYPE)
        ikb[...] = ik_ref[...].astype(MXU_DTYPE)

    row_blk = lax.broadcasted_iota(jnp.int32, (qt, 1), 0) // Q_BLOCK
    n_vis = jnp.minimum((first_chunk + j * n_sub + row_blk + 1) * CHUNK, n_keys)
    n_ck = (jnp.minimum((first_chunk + (j + 1) * n_sub) * CHUNK, n_keys) + kc - 1) // kc
    lane_k = lax.broadcasted_iota(jnp.int32, (qt, kc), 1)

    iq = iq_ref[...] * IDX_DIM ** -0.5
    iw = iw_ref[...] * IDX_HEADS ** -0.5
    for h in range(IDX_HEADS):
        iqlhs_scr[h] = jnp.where(_half_mask((qt, LANES), h % 2 == 1),
                                 iq[:, (h // 2) * LANES:(h // 2 + 1) * LANES], 0.0).astype(MXU_DTYPE)
        wb_scr[h] = jnp.broadcast_to(iw[:, h:h + 1], (qt, LANES))

    def score_chunk(c, carry):
        ikc = ikb[c]
        score = None
        for h in range(IDX_HEADS):
            rel = jnp.maximum(_dot_nt(iqlhs_scr[h], ikc), 0.0)
            w = wb_scr[h]
            term = jnp.concatenate([rel[:, s * LANES:(s + 1) * LANES] * w for s in range(kc // LANES)], axis=1)
            score = term if score is None else score + term
        key = jnp.where(c * kc + lane_k < n_vis, _sortable(score), jnp.int32(INT_MIN))
        key_scr[c] = key
        if qtp > qt:
            key = jnp.concatenate([key, jnp.full((qtp - qt, kc), INT_MIN, jnp.int32)], axis=0)
        keyt_scr[c] = key.T
        return carry

    lax.fori_loop(0, n_ck, score_chunk, 0)

    sub_k = lax.broadcasted_iota(jnp.int32, (kc // 8, 8, qtp), 0) * 8 + lax.broadcasted_iota(
        jnp.int32, (kc // 8, 8, qtp), 1)

    def count(pred):
        def body(c, acc):
            hit = pred(keyt_scr[c].reshape(kc // 8, 8, qtp), c * kc + sub_k)
            return acc + jnp.sum(jnp.where(hit, 1.0, 0.0), axis=0)
        acc = lax.fori_loop(0, n_ck, body, jnp.zeros((8, qtp), jnp.float32))
        for shift in (4, 2, 1):
            acc = acc + pltpu.roll(acc, shift, 0)
        return acc

    def value_bit(i, t):
        cand = t + (jnp.int32(1) << (31 - i))
        return jnp.where(count(lambda key, idx: key >= cand[None]) >= n_sel, cand, t)

    t = lax.fori_loop(0, 32, value_bit, jnp.full((8, qtp), INT_MIN, jnp.int32))
    need = n_sel - count(lambda key, idx: key > t[None])
    n_ge = count(lambda key, idx: key >= t[None])

    def last_tied_index():
        def index_bit(i, m):
            cand = m + (jnp.int32(1) << (idx_bits - 1 - i))
            below = count(lambda key, idx: (key == t[None]) & (idx < cand[None]))
            return jnp.where(below < need, cand, m)
        return lax.fori_loop(0, idx_bits, index_bit, jnp.zeros((8, qtp), jnp.int32))

    m_idx = lax.cond(jnp.max(n_ge) > n_sel, last_tied_index,
                     lambda: jnp.full((8, qtp), 2 ** 30, jnp.int32))
    t_rep = jnp.broadcast_to(t[0:1], (LANES, qtp)).T[:qt]
    m_rep = jnp.broadcast_to(m_idx[0:1], (LANES, qtp)).T[:qt]
    t_row = jnp.concatenate([t_rep] * (kc // LANES), axis=1)
    m_row = jnp.concatenate([m_rep] * (kc // LANES), axis=1)

    def bias_chunk(c, carry):
        key = key_scr[c]
        idx = c * kc + lane_k
        sel = ((key > t_row) | ((key == t_row) & (idx <= m_row))) & (idx < n_vis)
        bias_scr[c] = jnp.where(sel, 0.0, NEG_BIG)
        return carry

    lax.fori_loop(0, n_ck, bias_chunk, 0)

    q = q_ref[...] * HEAD_DIM ** -0.5
    rows = KV_GROUP * qt
    for g in range(N_KV_HEADS):
        for r in range(KV_GROUP):
            qlhs_scr[g, r * qt:(r + 1) * qt] = jnp.where(
                _half_mask((qt, LANES), g == 1), q[:, r * LANES:(r + 1) * LANES], 0.0).astype(MXU_DTYPE)
    m_scr[...] = jnp.full(m_scr.shape, NEG_BIG, jnp.float32)
    acc_scr[...] = jnp.zeros(acc_scr.shape, jnp.float32)

    def logits_chunk(c, carry):
        bias = bias_scr[c][None]
        for g in range(N_KV_HEADS):
            s = (_dot_nt(qlhs_scr[g], kb[c]).reshape(KV_GROUP, qt, kc) + bias).reshape(rows, kc)
            s_scr[g, c] = s
            m = m_scr[g]
            for u in range(kc // LANES):
                m = jnp.maximum(m, s[:, u * LANES:(u + 1) * LANES])
            m_scr[g] = m
        return carry

    lax.fori_loop(0, n_ck, logits_chunk, 0)
    for g in range(N_KV_HEADS):
        m_scr[g] = jnp.broadcast_to(jnp.max(m_scr[g], axis=-1, keepdims=True), (rows, LANES))

    def pv_chunk(c, carry):
        for g, vs in enumerate((vs0, vs1)):
            m = m_scr[g]
            s = s_scr[g, c]
            p = jnp.concatenate([jnp.exp(s[:, u * LANES:(u + 1) * LANES] - m) for u in range(kc // LANES)], axis=1)
            acc_scr[g] += _dot(p, vs[c])
        return carry

    lax.fori_loop(0, n_ck, pv_chunk, 0)
    outs = [acc_scr[g] / pltpu.roll(acc_scr[g], HEAD_DIM, 1) for g in range(N_KV_HEADS)]
    lower = _half_mask((qt, LANES), False)
    for r in range(KV_GROUP):
        rs = slice(r * qt, (r + 1) * qt)
        o_ref[:, r * LANES:(r + 1) * LANES] = jnp.where(lower, outs[0][rs], outs[1][rs])


def _dsa(q, iq, iw, k_all, v_all, ik2_all, first_chunk):
    nb, seq, _ = q.shape
    n_keys = k_all.shape[1]
    n_sel = min(TOPK_KEYS, n_keys // 4)
    kc = KEY_CHUNK
    qt = min(DSA_ROWS, seq)
    assert seq % qt == 0
    n_ch = -(-n_keys // kc)
    pad = n_ch * kc - n_keys
    chunked = lambda a: jnp.pad(a, ((0, 0), (0, pad), (0, 0))).reshape(nb, n_ch, kc, LANES)
    qspec = lambda w: pl.BlockSpec((None, qt, w), lambda b, j: (b, j, 0))
    kspec = pl.BlockSpec((None, n_ch, kc, LANES), lambda b, j: (b, 0, 0, 0))
    rows = KV_GROUP * qt
    return pl.pallas_call(
        functools.partial(_dsa_kernel, qt=qt, n_keys=n_keys, n_sel=n_sel, first_chunk=first_chunk,
                          idx_bits=max(1, (n_ch * kc - 1).bit_length())),
        grid=(nb, seq // qt),
        in_specs=[qspec(D_ATTN), qspec(IDX_HEADS * IDX_DIM), qspec(LANES), kspec, kspec, kspec],
        out_specs=qspec(D_ATTN),
        out_shape=jax.ShapeDtypeStruct((nb, seq, D_ATTN), jnp.float32),
        scratch_shapes=[pltpu.VMEM((n_ch, kc, LANES), MXU_DTYPE)] * 4
        + [pltpu.VMEM((n_ch, qt, kc), jnp.int32), pltpu.VMEM((n_ch, kc, max(qt, LANES)), jnp.int32),
           pltpu.VMEM((n_ch, qt, kc), jnp.float32),
           pltpu.VMEM((IDX_HEADS, qt, LANES), jnp.float32), pltpu.VMEM((IDX_HEADS, qt, LANES), MXU_DTYPE),
           pltpu.VMEM((N_KV_HEADS, rows, LANES), MXU_DTYPE),
           pltpu.VMEM((N_KV_HEADS, n_ch, rows, kc), jnp.float32),
           pltpu.VMEM((N_KV_HEADS, rows, LANES), jnp.float32), pltpu.VMEM((N_KV_HEADS, rows, LANES), jnp.float32)],
        compiler_params=_params("arbitrary", "arbitrary"),
        name="dsa_attention",
    )(q, iq, iw, chunked(k_all), chunked(v_all), chunked(ik2_all))


def _layer_norm(h, g, b):
    mu = jnp.mean(h, axis=-1, keepdims=True)
    var = jnp.mean(jnp.square(h - mu), axis=-1, keepdims=True)
    return (h - mu) * lax.rsqrt(var + LN_EPS) * g + b


def _mix_kernel(x_ref, attn_ref, ssm_ref, wg_ref, wap_ref, wout_ref, g1_ref, b1_ref, wr_ref, rb_ref,
                x1_ref, gate_ref, pos_ref, stats_ref, cnt_ref, cnt_scr, *, alpha):
    i = pl.program_id(0)

    @pl.when(i == 0)
    def _():
        cnt_scr[...] = jnp.zeros_like(cnt_scr)

    x = x_ref[...]
    tm, dm = x.shape
    gates = jax.nn.sigmoid(_dot(x, wg_ref[...]))
    attn_out = _dot(attn_ref[...], wap_ref[...])
    mixed = _dot(gates[:, :dm] * ssm_ref[...] + gates[:, dm:] * attn_out, wout_ref[...])
    x1 = _layer_norm(alpha * x + mixed, g1_ref[...], b1_ref[...])
    x1_ref[...] = x1

    logits = jnp.dot(x1, wr_ref[...], preferred_element_type=jnp.float32, precision=lax.Precision.HIGHEST)
    scores = jax.nn.sigmoid(logits)
    lane = lax.broadcasted_iota(jnp.int32, scores.shape, 1).astype(jnp.float32)
    slot = lax.broadcasted_iota(jnp.int32, (tm, SLOT_PAD), 1)
    cur = scores + rb_ref[...]
    chosen = jnp.zeros_like(scores)
    picks = []
    for _ in range(MOE_TOPK):
        best = jnp.max(cur, axis=-1, keepdims=True)
        e_k = jnp.min(jnp.where(cur == best, lane, float(N_EXPERTS)), axis=-1, keepdims=True)
        hot = lane == e_k
        picks.append((e_k, hot, jnp.sum(jnp.where(hot, scores, 0.0), axis=-1, keepdims=True)))
        chosen = jnp.where(hot, 1.0, chosen)
        cur = jnp.where(hot, -jnp.inf, cur)
    total = picks[0][2]
    for _, _, s_k in picks[1:]:
        total = total + s_k

    row = lax.broadcasted_iota(jnp.int32, (tm, tm), 0)
    col = lax.broadcasted_iota(jnp.int32, (tm, tm), 1)
    before = _dot(jnp.where(col < row, 1.0, 0.0), chosen)
    tile_cnt = jnp.ceil(jnp.sum(chosen, axis=0, keepdims=True) / SEG_ALIGN) * SEG_ALIGN
    e_row = lax.broadcasted_iota(jnp.int32, (N_EXPERTS, N_EXPERTS), 0)
    e_col = lax.broadcasted_iota(jnp.int32, (N_EXPERTS, N_EXPERTS), 1)
    tile_off = jnp.dot(jnp.broadcast_to(tile_cnt, (8, N_EXPERTS)), jnp.where(e_row < e_col, 1.0, 0.0),
                       preferred_element_type=jnp.float32, precision=lax.Precision.HIGHEST)[0:1]
    local = before + tile_off
    g_out = jnp.zeros((tm, SLOT_PAD), jnp.float32)
    p_out = jnp.zeros((tm, SLOT_PAD), jnp.int32)
    for k, (e_k, hot, s_k) in enumerate(picks):
        pos_k = jnp.sum(jnp.where(hot, local, 0.0), axis=-1, keepdims=True)
        g_out = jnp.where(slot == k, s_k / total * ROUTED_SCALE, g_out)
        p_out = jnp.where(slot == k, pos_k.astype(jnp.int32), p_out)
    gate_ref[...] = g_out
    pos_ref[...] = p_out
    srow = lax.broadcasted_iota(jnp.int32, (8, N_EXPERTS), 0)
    stats_ref[...] = jnp.where(srow == 0, cnt_scr[...], jnp.where(srow == 1, tile_cnt, jnp.where(
        srow == 2, tile_off, 0.0)))
    cnt_scr[...] = cnt_scr[...] + tile_cnt
    cnt_ref[...] = cnt_scr[...]


def _mix(x2, attn2, ssm2, wg, wap, wout, ln_g, ln_b, w_router, router_bias, alpha):
    t, dm = x2.shape
    tm = min(MOE_TILE, t)
    row = lambda w: pl.BlockSpec((tm, w), lambda i: (i, 0))
    consts = (wg, wap, wout, ln_g, ln_b, w_router, router_bias)
    return pl.pallas_call(
        functools.partial(_mix_kernel, alpha=alpha),
        grid=(t // tm,),
        in_specs=[row(dm), row(D_ATTN), row(dm)] + [_const_spec(c.shape) for c in consts],
        out_specs=[row(dm), row(SLOT_PAD), row(SLOT_PAD),
                   pl.BlockSpec((None, 8, N_EXPERTS), lambda i: (i, 0, 0)),
                   pl.BlockSpec((1, N_EXPERTS), lambda i: (0, 0))],
        out_shape=[jax.ShapeDtypeStruct((t, dm), jnp.float32),
                   jax.ShapeDtypeStruct((t, SLOT_PAD), jnp.float32),
                   jax.ShapeDtypeStruct((t, SLOT_PAD), jnp.int32),
                   jax.ShapeDtypeStruct((t // tm, 8, N_EXPERTS), jnp.float32),
                   jax.ShapeDtypeStruct((1, N_EXPERTS), jnp.float32)],
        scratch_shapes=[pltpu.VMEM((1, N_EXPERTS), jnp.float32)],
        compiler_params=_params("arbitrary"),
        name="mix_ln_router",
    )(x2, attn2, ssm2, *consts)


def _local_rows(tm):
    assert (N_EXPERTS * SEG_ALIGN) % tm == 0
    return MOE_TOPK * tm + N_EXPERTS * SEG_ALIGN


def _segment_pieces(i, cnt_ref, off_ref, dst_ref, tm, visit):
    def segment(e, carry):
        n = cnt_ref[i * N_EXPERTS + e]
        a = off_ref[i * N_EXPERTS + e]
        b = dst_ref[i * N_EXPERTS + e]
        p = tm
        while p >= SEG_ALIGN:
            done = (n // (2 * p)) * (2 * p)

            @pl.when((n & p) != 0)
            def _(p=p, done=done):
                visit(pl.multiple_of(a + done, SEG_ALIGN), pl.multiple_of(b + done, SEG_ALIGN), p)

            p //= 2
        return carry

    lax.fori_loop(0, N_EXPERTS, segment, 0)


def _dispatch_kernel(cnt_ref, off_ref, dst_ref, zlo_ref, zhi_ref, post_ref, x_ref, xs_ref,
                     loc, zero_scr, sem, *, tm):
    i = pl.program_id(0)

    xb = x_ref[...].astype(MXU_DTYPE)
    post = post_ref[...]
    for sc in range(_local_rows(tm) // tm):
        slot = sc * tm + lax.broadcasted_iota(jnp.int32, (tm, tm), 0)
        onehot = jnp.zeros((tm, tm), jnp.float32)
        for k in range(MOE_TOPK):
            onehot = jnp.where(slot == post[k:k + 1, :], 1.0, onehot)
        loc[sc * tm:(sc + 1) * tm, :] = _dot(onehot, xb)

    def piece(a, b, p):
        return pltpu.make_async_copy(loc.at[pl.ds(a, p)], xs_ref.at[pl.ds(b, p)], sem)

    _segment_pieces(i, cnt_ref, off_ref, dst_ref, tm, lambda a, b, p: piece(a, b, p).start())
    _segment_pieces(i, cnt_ref, off_ref, dst_ref, tm, lambda a, b, p: piece(a, b, p).wait())

    @pl.when(i == pl.num_programs(0) - 1)
    def _():
        zero_scr[...] = jnp.zeros_like(zero_scr)

        def zero_copy(group):
            return pltpu.make_async_copy(
                zero_scr, xs_ref.at[pl.ds(pl.multiple_of(group * SEG_ALIGN, SEG_ALIGN), SEG_ALIGN)], sem)

        def segment(s, carry):
            lo, hi = zlo_ref[s], zhi_ref[s]

            def fill(g, c):
                zero_copy(g).start()
                return c

            def fill_done(g, c):
                zero_copy(g).wait()
                return c

            lax.fori_loop(lo, hi, fill, 0)
            lax.fori_loop(lo, hi, fill_done, 0)
            return carry

        lax.fori_loop(0, N_EXPERTS, segment, 0)


def _dispatch(x1, pos_t, seg_cnt, seg_off, seg_dst, zlo, zhi, n_rows):
    t, dm = x1.shape
    tm = min(MOE_TILE, t)
    return pl.pallas_call(
        functools.partial(_dispatch_kernel, tm=tm),
        grid_spec=pltpu.PrefetchScalarGridSpec(
            num_scalar_prefetch=5,
            grid=(t // tm,),
            in_specs=[pl.BlockSpec((None, SLOT_PAD, tm), lambda i, *_: (i, 0, 0)),
                      pl.BlockSpec((tm, dm), lambda i, *_: (i, 0))],
            out_specs=pl.BlockSpec(memory_space=pl.ANY),
            scratch_shapes=[pltpu.VMEM((_local_rows(tm), dm), jnp.float32), pltpu.VMEM((SEG_ALIGN, dm), jnp.float32),
                            pltpu.SemaphoreType.DMA(())],
        ),
        out_shape=jax.ShapeDtypeStruct((n_rows, dm), jnp.float32),
        compiler_params=_params("arbitrary"),
        name="moe_dispatch",
    )(seg_cnt, seg_off, seg_dst, zlo, zhi, pos_t, x1)


def _expert_kernel(be_ref, nu_ref, xs_ref, wgu_ref, wdn_ref, ys_ref, wgu_scr, wdn_scr):
    i = pl.program_id(0)

    @pl.when(i < nu_ref[0])
    def _():
        @pl.when((i == 0) | (be_ref[i] != be_ref[jnp.maximum(i - 1, 0)]))
        def _():
            wgu_scr[...] = wgu_ref[...].astype(MXU_DTYPE)
            wdn_scr[...] = wdn_ref[...].astype(MXU_DTYPE)

        h = _dot(xs_ref[...], wgu_scr[...])
        f = h.shape[1] // 2
        ys_ref[...] = _dot(jax.nn.silu(h[:, :f]) * h[:, f:], wdn_scr[...])


def _experts(xs, block_expert, n_used, w_gu, w_down):
    n_rows, dm = xs.shape
    bm = EXPERT_ROWS
    f2 = w_gu.shape[2]
    used = lambda i, nu: jnp.minimum(i, nu[0] - 1)
    return pl.pallas_call(
        _expert_kernel,
        grid_spec=pltpu.PrefetchScalarGridSpec(
            num_scalar_prefetch=2,
            grid=(n_rows // bm,),
            in_specs=[pl.BlockSpec((bm, dm), lambda i, be, nu: (used(i, nu), 0)),
                      pl.BlockSpec((None, dm, f2), lambda i, be, nu: (be[used(i, nu)], 0, 0)),
                      pl.BlockSpec((None, f2 // 2, dm), lambda i, be, nu: (be[used(i, nu)], 0, 0))],
            out_specs=pl.BlockSpec((bm, dm), lambda i, be, nu: (used(i, nu), 0)),
            scratch_shapes=[pltpu.VMEM((dm, f2), MXU_DTYPE), pltpu.VMEM((f2 // 2, dm), MXU_DTYPE)],
        ),
        out_shape=jax.ShapeDtypeStruct((n_rows, dm), jnp.float32),
        compiler_params=_params("arbitrary"),
        name="moe_experts",
    )(block_expert, n_used, xs, w_gu, w_down)


def _combine_kernel(cnt_ref, off_ref, dst_ref, x1_ref, gate_ref, pos_ref, wsgu_ref, wsdn_ref, g2_ref, b2_ref,
                    ys_ref, y_ref, loc, sem, *, tm, alpha):
    i = pl.program_id(0)

    @pl.when(i == 0)
    def _():
        loc[...] = jnp.zeros_like(loc)

    def piece(a, b, p):
        return pltpu.make_async_copy(ys_ref.at[pl.ds(b, p)], loc.at[pl.ds(a, p)], sem)

    _segment_pieces(i, cnt_ref, off_ref, dst_ref, tm, lambda a, b, p: piece(a, b, p).start())

    x1 = x1_ref[...]
    h = _dot(x1, wsgu_ref[...])
    f = h.shape[1] // 2
    shared = _dot(jax.nn.silu(h[:, :f]) * h[:, f:], wsdn_ref[...])

    _segment_pieces(i, cnt_ref, off_ref, dst_ref, tm, lambda a, b, p: piece(a, b, p).wait())

    gate = gate_ref[...]
    pos = pos_ref[...]
    routed = shared
    for sc in range(_local_rows(tm) // tm):
        slot = sc * tm + lax.broadcasted_iota(jnp.int32, (tm, tm), 1)
        w = jnp.zeros((tm, tm), jnp.float32)
        for k in range(MOE_TOPK):
            w = jnp.where(slot == pos[:, k:k + 1], gate[:, k:k + 1], w)
        routed = routed + _dot(w, loc[sc * tm:(sc + 1) * tm, :])
    y_ref[...] = _layer_norm(alpha * x1 + routed, g2_ref[...], b2_ref[...])


def _combine(x1, gate, pos, seg_cnt, seg_off, seg_dst, ys, w_sh_gu, w_sh_down, ln_g, ln_b, alpha):
    t, dm = x1.shape
    tm = min(MOE_TILE, t)
    const = lambda a: pl.BlockSpec(a.shape, lambda i, *_: (0,) * a.ndim, pipeline_mode=pl.Buffered(1))
    return pl.pallas_call(
        functools.partial(_combine_kernel, tm=tm, alpha=alpha),
        grid_spec=pltpu.PrefetchScalarGridSpec(
            num_scalar_prefetch=3,
            grid=(t // tm,),
            in_specs=[pl.BlockSpec((tm, dm), lambda i, *_: (i, 0)),
                      pl.BlockSpec((tm, SLOT_PAD), lambda i, *_: (i, 0)),
                      pl.BlockSpec((tm, SLOT_PAD), lambda i, *_: (i, 0)),
                      const(w_sh_gu), const(w_sh_down), const(ln_g), const(ln_b),
                      pl.BlockSpec(memory_space=pl.ANY)],
            out_specs=pl.BlockSpec((tm, dm), lambda i, *_: (i, 0)),
            scratch_shapes=[pltpu.VMEM((_local_rows(tm), dm), jnp.float32), pltpu.SemaphoreType.DMA(())],
        ),
        out_shape=jax.ShapeDtypeStruct((t, dm), jnp.float32),
        compiler_params=_params("arbitrary"),
        name="moe_combine",
    )(seg_cnt, seg_off, seg_dst, x1, gate, pos, w_sh_gu, w_sh_down, ln_g, ln_b, ys)


def _moe(x1, gate, pos, stats, counts, w_exp_gu, w_exp_down, w_sh_gu, w_sh_down, ln_g, ln_b, alpha):
    t = x1.shape[0]
    bm = EXPERT_ROWS
    n_tiles = stats.shape[0]
    n_rows = (-(-(t * MOE_TOPK + n_tiles * N_EXPERTS * (SEG_ALIGN - 1)) // bm) + N_EXPERTS) * bm
    cnt = counts.reshape(N_EXPERTS).astype(jnp.int32)
    padded = (cnt + bm - 1) // bm * bm
    end = jnp.cumsum(padded)
    start = end - padded
    zlo, zhi = (start + cnt) // SEG_ALIGN, end // SEG_ALIGN
    n_used = end[-1:] // bm
    block_start = jnp.arange(n_rows // bm, dtype=jnp.int32) * bm
    block_expert = jnp.minimum(jnp.sum((end[None, :] <= block_start[:, None]).astype(jnp.int32), axis=1),
                               N_EXPERTS - 1)
    seg = stats.astype(jnp.int32)
    seg_cnt, seg_off = seg[:, 1, :].reshape(-1), seg[:, 2, :].reshape(-1)
    seg_dst = (start[None, :] + seg[:, 0, :]).reshape(-1)
    pos_t = jnp.transpose(pos.reshape(n_tiles, t // n_tiles, SLOT_PAD), (0, 2, 1))
    xs = _dispatch(x1, pos_t, seg_cnt, seg_off, seg_dst, zlo, zhi, n_rows)
    ys = _experts(xs, block_expert, n_used, w_exp_gu, w_exp_down)
    return _combine(x1, gate, pos, seg_cnt, seg_off, seg_dst, ys, w_sh_gu, w_sh_down, ln_g, ln_b, alpha)


def _layer(x, pos0, past, p, alpha):
    (w_in, a_re, a_im, log_dt, b_re, b_im, c_re, c_im, ssm_d, w_glu, w_attn_proj, w_out, ln1_g, ln1_b,
     w_router, router_bias, w_exp_gu, w_exp_down, w_sh_gu, w_sh_down, ln2_g, ln2_b) = p
    nb, seq, dm = x.shape
    assert pos0 % CHUNK == 0 and seq % Q_BLOCK == 0 and Q_BLOCK == CHUNK and seq % S5_STEPS == 0
    assert nb * seq * MOE_TOPK < 2 ** 24
    t = nb * seq
    f32 = jnp.float32
    x2 = x.reshape(t, dm)
    pos = pos0 + jnp.arange(seq, dtype=jnp.int32)

    u, q, iq, k, v, ik2, ik, iw = _in_proj(x2, _pack_w_in(w_in), pos, seq)

    a_cat, bcat, ccat = _s5_discretise(a_re, a_im, log_dt, b_re, b_im, c_re, c_im)
    if past is None:
        h0 = jnp.zeros((nb, 2 * NS), f32)
    else:
        h0 = jnp.concatenate([past[3].reshape(nb, NS), past[4].reshape(nb, NS)], axis=1).astype(f32)
    u_t = jnp.transpose(u.reshape(nb, seq, D_SSM), (1, 0, 2))
    ssm_t, h_t = _s5(u_t, h0, a_cat, bcat, ccat, ssm_d.reshape(1, D_SSM).astype(f32), w_glu.astype(MXU_DTYPE))
    ssm2 = jnp.transpose(ssm_t, (1, 0, 2)).reshape(t, dm)
    h_re = h_t[:, :NS].reshape(nb, N_GROUPS, N_STATE)
    h_im = h_t[:, NS:].reshape(nb, N_GROUPS, N_STATE)

    k3, v3, ik23 = (a.reshape(nb, seq, LANES) for a in (k, v, ik2))
    if past is not None:
        ck, cv, cik = past[0], past[1], past[2]
        n_past = ck.shape[1]
        k3 = jnp.concatenate([ck.reshape(nb, n_past, LANES), k3], axis=1)
        v3 = jnp.concatenate([cv.reshape(nb, n_past, LANES), v3], axis=1)
        ik23 = jnp.concatenate([jnp.concatenate([cik, cik], axis=-1), ik23], axis=1)
    attn = _dsa(q.reshape(nb, seq, D_ATTN), iq.reshape(nb, seq, IDX_HEADS * IDX_DIM), iw.reshape(nb, seq, LANES),
                k3, v3, ik23, pos0 // CHUNK)

    wap = jnp.transpose(w_attn_proj.reshape(N_KV_HEADS, KV_GROUP, HEAD_DIM, dm), (1, 0, 2, 3)).reshape(D_ATTN, dm)
    wg = w_in[:, w_in.shape[1] - 2 * dm:]
    row = lambda a: a.reshape(1, -1).astype(f32)
    x1, gate, pos_slot, stats, counts = _mix(
        x2, attn.reshape(t, D_ATTN), ssm2, wg.astype(MXU_DTYPE), wap.astype(MXU_DTYPE), w_out.astype(MXU_DTYPE),
        row(ln1_g), row(ln1_b), w_router.astype(f32), row(router_bias), alpha)

    y = _moe(x1, gate, pos_slot, stats, counts, w_exp_gu, w_exp_down,
             w_sh_gu.astype(MXU_DTYPE), w_sh_down.astype(MXU_DTYPE), row(ln2_g), row(ln2_b), alpha)
    state = (k.reshape(nb, seq, N_KV_HEADS, HEAD_DIM), v.reshape(nb, seq, N_KV_HEADS, HEAD_DIM),
             ik.reshape(nb, seq, IDX_DIM), h_re, h_im)
    return y.reshape(nb, seq, dm), state


def kernel(x_prompt, x_sample, cache_k, cache_v, cache_idx_k, state_ssm_re, state_ssm_im, w_in, ssm_a_re, ssm_a_im, ssm_log_dt, ssm_b_re, ssm_b_im, ssm_c_re, ssm_c_im, ssm_d, w_glu, w_attn_proj, w_out, ln1_g, ln1_b, w_router, router_bias, w_exp_gu, w_exp_down, w_sh_gu, w_sh_down, ln2_g, ln2_b):
    weights = (w_in, ssm_a_re, ssm_a_im, ssm_log_dt, ssm_b_re, ssm_b_im, ssm_c_re, ssm_c_im, ssm_d,
               w_glu, w_attn_proj, w_out, ln1_g, ln1_b, w_router, router_bias,
               w_exp_gu, w_exp_down, w_sh_gu, w_sh_down, ln2_g, ln2_b)
    depth = w_in.shape[0]
    alpha = (2 * depth) ** 0.25
    past_len = cache_k.shape[2]
    y_p, y_s = x_prompt, x_sample
    new_p, new_s = [], []
    for l in range(depth):
        p_l = tuple(w[l] for w in weights)
        y_p, st_p = _layer(y_p, 0, None, p_l, alpha)
        y_s, st_s = _layer(y_s, past_len,
                           (cache_k[l], cache_v[l], cache_idx_k[l], state_ssm_re[l], state_ssm_im[l]), p_l, alpha)
        new_p.append(st_p)
        new_s.append(st_s)
    k_p, v_p, ik_p, hre_p, him_p = [jnp.stack(a) for a in zip(*new_p)]
    k_s, v_s, ik_s, hre_s, him_s = [jnp.stack(a) for a in zip(*new_s)]
    return (y_p, y_s, k_p, v_p, ik_p, hre_p, him_p, k_s, v_s, ik_s, hre_s, him_s)
```

```python
import functools

import jax
import jax.numpy as jnp
from jax import lax
from jax.experimental import pallas as pl
from jax.experimental.pallas import tpu as pltpu

CHUNK = 64
D_SSM = 512
SSM_GROUP = 16
N_GROUPS = D_SSM // SSM_GROUP
N_STATE = 64
N_HEADS = 8
N_KV_HEADS = 2
HEAD_DIM = 64
D_ATTN = N_HEADS * HEAD_DIM
KV_GROUP = N_HEADS // N_KV_HEADS
ROPE_DIM = HEAD_DIM // 4
ROPE_THETA = 500000.0
IDX_HEADS = 8
IDX_DIM = 64
TOPK_KEYS = 256
Q_BLOCK = 64
N_EXPERTS = 64
MOE_TOPK = 6
D_EXPERT = 256
D_SHARED = 256
ROUTED_SCALE = 2.5
LN_EPS = 1e-5

LANES = 128
KEY_CHUNK = 256
DSA_ROWS = 256
EXPERT_ROWS = 256
MOE_TILE = 512
SEG_ALIGN = 16
SLOT_PAD = 8
NS = N_GROUPS * N_STATE
VMEM_LIMIT = 56 * 1024 * 1024
NEG_BIG = -1e30
INT_MIN = -2147483648

MXU_DTYPE = jnp.bfloat16


def _dot(a, b):
    return jnp.dot(a.astype(MXU_DTYPE), b.astype(MXU_DTYPE), preferred_element_type=jnp.float32)


def _dot_nt(a, b):
    return lax.dot_general(a.astype(MXU_DTYPE), b.astype(MXU_DTYPE), (((1,), (1,)), ((), ())),
                           preferred_element_type=jnp.float32)


def _params(*sem):
    return pltpu.CompilerParams(dimension_semantics=sem, vmem_limit_bytes=VMEM_LIMIT)


def _const_spec(shape):
    return pl.BlockSpec(shape, lambda *_: (0,) * len(shape), pipeline_mode=pl.Buffered(1))


_C_U, _C_Q, _C_IQ, _C_K, _C_V, _C_IK, _C_IW, _C_END = 0, 512, 1024, 1536, 1664, 1792, 1920, 2048


def _inproj_kernel(x_ref, w_ref, cos_ref, sin_ref,
                   u_ref, q_ref, iq_ref, k_ref, v_ref, ik2_ref, ik_ref, iw_ref):
    z = _dot(x_ref[...], w_ref[...])
    cos = cos_ref[...]
    sin = sin_ref[...]
    lane = lax.broadcasted_iota(jnp.int32, cos.shape, 1)
    first_half = (lane % HEAD_DIM) < (ROPE_DIM // 2)

    def rope(zc):
        partner = jnp.where(first_half, pltpu.roll(zc, LANES - ROPE_DIM // 2, 1), pltpu.roll(zc, ROPE_DIM // 2, 1))
        return zc * cos + partner * sin

    u_ref[...] = z[:, _C_U:_C_Q]
    for c in range(4):
        q_ref[:, c * LANES:(c + 1) * LANES] = rope(z[:, _C_Q + c * LANES:_C_Q + (c + 1) * LANES])
        iq_ref[:, c * LANES:(c + 1) * LANES] = rope(z[:, _C_IQ + c * LANES:_C_IQ + (c + 1) * LANES])
    k_ref[...] = rope(z[:, _C_K:_C_V])
    v_ref[...] = z[:, _C_V:_C_IK]
    ik2 = rope(z[:, _C_IK:_C_IW])
    ik2_ref[...] = ik2
    ik_ref[...] = ik2[:, :IDX_DIM]
    iw_ref[...] = z[:, _C_IW:_C_END]


def _rope_tables(pos):
    half = ROPE_DIM // 2
    inv_freq = ROPE_THETA ** (-jnp.arange(half, dtype=jnp.float32) * 2.0 / ROPE_DIM)
    ang = pos.astype(jnp.float32)[:, None] * inv_freq
    cos, sin = jnp.cos(ang), jnp.sin(ang)
    n = pos.shape[0]
    pad = HEAD_DIM - ROPE_DIM
    cos_h = jnp.concatenate([cos, cos, jnp.ones((n, pad), jnp.float32)], axis=-1)
    sin_h = jnp.concatenate([-sin, sin, jnp.zeros((n, pad), jnp.float32)], axis=-1)
    return jnp.tile(cos_h, (1, LANES // HEAD_DIM)), jnp.tile(sin_h, (1, LANES // HEAD_DIM))


def _pack_w_in(w_in):
    s = [0, 512, 1024, 1152, 1280, 1792, 1856, 1864]
    w_u, w_q, w_k, w_v = w_in[:, s[0]:s[1]], w_in[:, s[1]:s[2]], w_in[:, s[2]:s[3]], w_in[:, s[3]:s[4]]
    w_iq, w_ik, w_iw = w_in[:, s[4]:s[5]], w_in[:, s[5]:s[6]], w_in[:, s[6]:s[7]]
    d = w_in.shape[0]
    w_qh = w_q.reshape(d, N_KV_HEADS, KV_GROUP, HEAD_DIM)
    w_qp = jnp.transpose(w_qh, (0, 2, 1, 3)).reshape(d, D_ATTN)
    w_iwp = jnp.concatenate([w_iw, jnp.zeros((d, LANES - IDX_HEADS), w_in.dtype)], axis=1)
    return jnp.concatenate([w_u, w_qp, w_iq, w_k, w_v, w_ik, w_ik, w_iwp], axis=1).astype(MXU_DTYPE)


def _in_proj(x2, w_pack, pos, seq):
    t, d = x2.shape
    tm = min(512, t)
    cos, sin = _rope_tables(pos)
    if seq >= tm:
        per = seq // tm
        tab_map = lambda i: (i % per, 0)
    else:
        cos, sin = jnp.tile(cos, (tm // seq, 1)), jnp.tile(sin, (tm // seq, 1))
        tab_map = lambda i: (0, 0)
    row = lambda w: pl.BlockSpec((tm, w), lambda i: (i, 0))
    widths = (D_SSM, D_ATTN, IDX_HEADS * IDX_DIM, LANES, LANES, LANES, IDX_DIM, LANES)
    return pl.pallas_call(
        _inproj_kernel,
        grid=(t // tm,),
        in_specs=[row(d), _const_spec(w_pack.shape),
                  pl.BlockSpec((tm, LANES), tab_map), pl.BlockSpec((tm, LANES), tab_map)],
        out_specs=[row(w) for w in widths],
        out_shape=[jax.ShapeDtypeStruct((t, w), jnp.float32) for w in widths],
        compiler_params=_params("parallel"),
        name="in_proj",
    )(x2, w_pack, cos, sin)


S5_STEPS = 8
S5_COLS = 512


def _s5_kernel(u_ref, h0_ref, a_ref, bcat_ref, ccat_ref, d_ref, wglu_ref,
               out_ref, ht_ref, h_scr, bu_scr, hall_scr, *, nb):
    @pl.when(pl.program_id(0) == 0)
    def _():
        h_scr[...] = h0_ref[...]

    u = u_ref[...].reshape(S5_STEPS * nb, D_SSM)
    bu_scr[...] = _dot(u, bcat_ref[...])
    for cb in range(NS // S5_COLS):
        re = slice(cb * S5_COLS, (cb + 1) * S5_COLS)
        im = slice(NS + cb * S5_COLS, NS + (cb + 1) * S5_COLS)
        a_re, a_im = a_ref[0:1, re], a_ref[1:2, re]
        h_re, h_im = h_scr[:, re], h_scr[:, im]
        for t in range(S5_STEPS):
            rows = slice(t * nb, (t + 1) * nb)
            h_re, h_im = (a_re * h_re - a_im * h_im + bu_scr[rows, re],
                          a_re * h_im + a_im * h_re + bu_scr[rows, im])
            hall_scr[rows, re] = h_re
            hall_scr[rows, im] = h_im
        h_scr[:, re] = h_re
        h_scr[:, im] = h_im
    y = _dot(hall_scr[...], ccat_ref[...]) + d_ref[...] * u
    y = jax.nn.gelu(y)
    g = _dot(y, wglu_ref[...])
    dm = g.shape[1] // 2
    out_ref[...] = (g[:, :dm] * jax.nn.sigmoid(g[:, dm:])).reshape(S5_STEPS, nb, dm)
    ht_ref[...] = h_scr[...]


def _s5_discretise(a_re, a_im, log_dt, b_re, b_im, c_re, c_im):
    f32 = jnp.float32
    ar, ai = a_re.astype(f32), a_im.astype(f32)
    dt = jnp.exp(log_dt.astype(f32))[:, None]
    mag = jnp.exp(dt * ar)
    abar_re, abar_im = mag * jnp.cos(dt * ai), mag * jnp.sin(dt * ai)
    den = ar * ar + ai * ai
    num_re, num_im = abar_re - 1.0, abar_im
    coef_re = (num_re * ar + num_im * ai) / den
    coef_im = (num_im * ar - num_re * ai) / den
    br, bi = b_re.astype(f32), b_im.astype(f32)
    bbar_re = coef_re[..., None] * br - coef_im[..., None] * bi
    bbar_im = coef_re[..., None] * bi + coef_im[..., None] * br
    eye = jnp.eye(N_GROUPS, dtype=f32)
    bd = lambda m: jnp.einsum("gnc,gh->gchn", m, eye).reshape(D_SSM, NS)
    bcat = jnp.concatenate([bd(bbar_re), bd(bbar_im)], axis=1)
    cd = lambda m: jnp.einsum("gcn,gh->gnhc", m, eye).reshape(NS, D_SSM)
    ccat = jnp.concatenate([cd(c_re.astype(f32)), cd(-c_im.astype(f32))], axis=0)
    a_cat = jnp.stack([abar_re.reshape(NS), abar_im.reshape(NS)])
    return a_cat, bcat.astype(MXU_DTYPE), ccat.astype(MXU_DTYPE)


def _s5(u_t, h0, a_cat, bcat, ccat, d_skip, w_glu):
    seq, nb, _ = u_t.shape
    dm = w_glu.shape[1] // 2
    rows = S5_STEPS * nb
    return pl.pallas_call(
        functools.partial(_s5_kernel, nb=nb),
        grid=(seq // S5_STEPS,),
        in_specs=[pl.BlockSpec((S5_STEPS, nb, D_SSM), lambda i: (i, 0, 0)),
                  _const_spec(h0.shape), _const_spec(a_cat.shape), _const_spec(bcat.shape),
                  _const_spec(ccat.shape), _const_spec(d_skip.shape), _const_spec(w_glu.shape)],
        out_specs=[pl.BlockSpec((S5_STEPS, nb, dm), lambda i: (i, 0, 0)),
                   pl.BlockSpec(h0.shape, lambda i: (0, 0))],
        out_shape=[jax.ShapeDtypeStruct((seq, nb, dm), jnp.float32),
                   jax.ShapeDtypeStruct(h0.shape, jnp.float32)],
        scratch_shapes=[pltpu.VMEM(h0.shape, jnp.float32),
                        pltpu.VMEM((rows, 2 * NS), jnp.float32),
                        pltpu.VMEM((rows, 2 * NS), jnp.float32)],
        compiler_params=_params("arbitrary"),
        name="s5_scan_glu",
    )(u_t, h0, a_cat, bcat, ccat, d_skip, w_glu)


def _sortable(x):
    bits = pltpu.bitcast(x, jnp.int32)
    key = bits ^ ((bits >> 31) & jnp.int32(0x7FFFFFFF))
    return jnp.where(x == 0.0, jnp.int32(0), key)


def _half_mask(shape, upper):
    lane = lax.broadcasted_iota(jnp.int32, shape, 1)
    return (lane >= HEAD_DIM) if upper else (lane < HEAD_DIM)


def _dsa_kernel(q_ref, iq_ref, iw_ref, k_ref, v_ref, ik_ref, o_ref,
                kb, vs0, vs1, ikb, key_scr, keyt_scr, bias_scr, wb_scr, iqlhs_scr, qlhs_scr, s_scr, m_scr, acc_scr,
                *, qt, n_keys, n_sel, first_chunk, idx_bits):
    j = pl.program_id(1)
    kc = KEY_CHUNK
    n_sub = qt // Q_BLOCK
    qtp = max(qt, LANES)

    @pl.when(j == 0)
    def _():
        v = v_ref[...]
        lower = lax.broadcasted_iota(jnp.int32, v.shape, v.ndim - 1) < HEAD_DIM
        kb[...] = k_ref[...].astype(MXU_DTYPE)
        vs0[...] = jnp.where(lower, v, 1.0).astype(MXU_DTYPE)
        vs1[...] = jnp.where(lower, 1.0, v).astype(MXU_DTYPE)
        ikb[...] = ik_ref[...].astype(MXU_DTYPE)

    row_blk = lax.broadcasted_iota(jnp.int32, (qt, 1), 0) // Q_BLOCK
    n_vis = jnp.minimum((first_chunk + j * n_sub + row_blk + 1) * CHUNK, n_keys)
    n_ck = (jnp.minimum((first_chunk + (j + 1) * n_sub) * CHUNK, n_keys) + kc - 1) // kc
    lane_k = lax.broadcasted_iota(jnp.int32, (qt, kc), 1)

    iq = iq_ref[...] * IDX_DIM ** -0.5
    iw = iw_ref[...] * IDX_HEADS ** -0.5
    for h in range(IDX_HEADS):
        iqlhs_scr[h] = jnp.where(_half_mask((qt, LANES), h % 2 == 1),
                                 iq[:, (h // 2) * LANES:(h // 2 + 1) * LANES], 0.0).astype(MXU_DTYPE)
        wb_scr[h] = jnp.broadcast_to(iw[:, h:h + 1], (qt, LANES))

    def score_chunk(c, carry):
        ikc = ikb[c]
        score = None
        for h in range(IDX_HEADS):
            rel = jnp.maximum(_dot_nt(iqlhs_scr[h], ikc), 0.0)
            w = wb_scr[h]
            term = jnp.concatenate([rel[:, s * LANES:(s + 1) * LANES] * w for s in range(kc // LANES)], axis=1)
            score = term if score is None else score + term
        key = jnp.where(c * kc + lane_k < n_vis, _sortable(score), jnp.int32(INT_MIN))
        key_scr[c] = key
        if qtp > qt:
            key = jnp.concatenate([key, jnp.full((qtp - qt, kc), INT_MIN, jnp.int32)], axis=0)
        keyt_scr[c] = key.T
        return carry

    lax.fori_loop(0, n_ck, score_chunk, 0)

    sub_k = lax.broadcasted_iota(jnp.int32, (kc // 8, 8, qtp), 0) * 8 + lax.broadcasted_iota(
        jnp.int32, (kc // 8, 8, qtp), 1)

    def count(pred):
        def body(c, acc):
            hit = pred(keyt_scr[c].reshape(kc // 8, 8, qtp), c * kc + sub_k)
            return acc + jnp.sum(jnp.where(hit, 1.0, 0.0), axis=0)
        acc = lax.fori_loop(0, n_ck, body, jnp.zeros((8, qtp), jnp.float32))
        for shift in (4, 2, 1):
            acc = acc + pltpu.roll(acc, shift, 0)
        return acc

    def value_bit(i, t):
        cand = t + (jnp.int32(1) << (31 - i))
        return jnp.where(count(lambda key, idx: key >= cand[None]) >= n_sel, cand, t)

    t = lax.fori_loop(0, 32, value_bit, jnp.full((8, qtp), INT_MIN, jnp.int32))
    need = n_sel - count(lambda key, idx: key > t[None])
    n_ge = count(lambda key, idx: key >= t[None])

    def last_tied_index():
        def index_bit(i, m):
            cand = m + (jnp.int32(1) << (idx_bits - 1 - i))
            below = count(lambda key, idx: (key == t[None]) & (idx < cand[None]))
            return jnp.where(below < need, cand, m)
        return lax.fori_loop(0, idx_bits, index_bit, jnp.zeros((8, qtp), jnp.int32))

    m_idx = lax.cond(jnp.max(n_ge) > n_sel, last_tied_index,
                     lambda: jnp.full((8, qtp), 2 ** 30, jnp.int32))
    t_rep = jnp.broadcast_to(t[0:1], (LANES, qtp)).T[:qt]
    m_rep = jnp.broadcast_to(m_idx[0:1], (LANES, qtp)).T[:qt]
    t_row = jnp.concatenate([t_rep] * (kc // LANES), axis=1)
    m_row = jnp.concatenate([m_rep] * (kc // LANES), axis=1)

    def bias_chunk(c, carry):
        key = key_scr[c]
        idx = c * kc + lane_k
        sel = ((key > t_row) | ((key == t_row) & (idx <= m_row))) & (idx < n_vis)
        bias_scr[c] = jnp.where(sel, 0.0, NEG_BIG)
        return carry

    lax.fori_loop(0, n_ck, bias_chunk, 0)

    q = q_ref[...] * HEAD_DIM ** -0.5
    rows = KV_GROUP * qt
    for g in range(N_KV_HEADS):
        for r in range(KV_GROUP):
            qlhs_scr[g, r * qt:(r + 1) * qt] = jnp.where(
                _half_mask((qt, LANES), g == 1), q[:, r * LANES:(r + 1) * LANES], 0.0).astype(MXU_DTYPE)
    m_scr[...] = jnp.full(m_scr.shape, NEG_BIG, jnp.float32)
    acc_scr[...] = jnp.zeros(acc_scr.shape, jnp.float32)

    def logits_chunk(c, carry):
        bias = bias_scr[c][None]
        for g in range(N_KV_HEADS):
            s = (_dot_nt(qlhs_scr[g], kb[c]).reshape(KV_GROUP, qt, kc) + bias).reshape(rows, kc)
            s_scr[g, c] = s
            m = m_scr[g]
            for u in range(kc // LANES):
                m = jnp.maximum(m, s[:, u * LANES:(u + 1) * LANES])
            m_scr[g] = m
        return carry

    lax.fori_loop(0, n_ck, logits_chunk, 0)
    for g in range(N_KV_HEADS):
        m_scr[g] = jnp.broadcast_to(jnp.max(m_scr[g], axis=-1, keepdims=True), (rows, LANES))

    def pv_chunk(c, carry):
        for g, vs in enumerate((vs0, vs1)):
            m = m_scr[g]
            s = s_scr[g, c]
            p = jnp.concatenate([jnp.exp(s[:, u * LANES:(u + 1) * LANES] - m) for u in range(kc // LANES)], axis=1)
            acc_scr[g] += _dot(p, vs[c])
        return carry

    lax.fori_loop(0, n_ck, pv_chunk, 0)
    outs = [acc_scr[g] / pltpu.roll(acc_scr[g], HEAD_DIM, 1) for g in range(N_KV_HEADS)]
    lower = _half_mask((qt, LANES), False)
    for r in range(KV_GROUP):
        rs = slice(r * qt, (r + 1) * qt)
        o_ref[:, r * LANES:(r + 1) * LANES] = jnp.where(lower, outs[0][rs], outs[1][rs])


def _dsa(q, iq, iw, k_all, v_all, ik2_all, first_chunk):
    nb, seq, _ = q.shape
    n_keys = k_all.shape[1]
    n_sel = min(TOPK_KEYS, n_keys // 4)
    kc = KEY_CHUNK
    qt = min(DSA_ROWS, seq)
    assert seq % qt == 0
    n_ch = -(-n_keys // kc)
    pad = n_ch * kc - n_keys
    chunked = lambda a: jnp.pad(a, ((0, 0), (0, pad), (0, 0))).reshape(nb, n_ch, kc, LANES)
    qspec = lambda w: pl.BlockSpec((None, qt, w), lambda b, j: (b, j, 0))
    kspec = pl.BlockSpec((None, n_ch, kc, LANES), lambda b, j: (b, 0, 0, 0))
    rows = KV_GROUP * qt
    return pl.pallas_call(
        functools.partial(_dsa_kernel, qt=qt, n_keys=n_keys, n_sel=n_sel, first_chunk=first_chunk,
                          idx_bits=max(1, (n_ch * kc - 1).bit_length())),
        grid=(nb, seq // qt),
        in_specs=[qspec(D_ATTN), qspec(IDX_HEADS * IDX_DIM), qspec(LANES), kspec, kspec, kspec],
        out_specs=qspec(D_ATTN),
        out_shape=jax.ShapeDtypeStruct((nb, seq, D_ATTN), jnp.float32),
        scratch_shapes=[pltpu.VMEM((n_ch, kc, LANES), MXU_DTYPE)] * 4
        + [pltpu.VMEM((n_ch, qt, kc), jnp.int32), pltpu.VMEM((n_ch, kc, max(qt, LANES)), jnp.int32),
           pltpu.VMEM((n_ch, qt, kc), jnp.float32),
           pltpu.VMEM((IDX_HEADS, qt, LANES), jnp.float32), pltpu.VMEM((IDX_HEADS, qt, LANES), MXU_DTYPE),
           pltpu.VMEM((N_KV_HEADS, rows, LANES), MXU_DTYPE),
           pltpu.VMEM((N_KV_HEADS, n_ch, rows, kc), jnp.float32),
           pltpu.VMEM((N_KV_HEADS, rows, LANES), jnp.float32), pltpu.VMEM((N_KV_HEADS, rows, LANES), jnp.float32)],
        compiler_params=_params("arbitrary", "arbitrary"),
        name="dsa_attention",
    )(q, iq, iw, chunked(k_all), chunked(v_all), chunked(ik2_all))


def _layer_norm(h, g, b):
    mu = jnp.mean(h, axis=-1, keepdims=True)
    var = jnp.mean(jnp.square(h - mu), axis=-1, keepdims=True)
    return (h - mu) * lax.rsqrt(var + LN_EPS) * g + b


def _mix_kernel(x_ref, attn_ref, ssm_ref, wg_ref, wap_ref, wout_ref, g1_ref, b1_ref, wr_ref, rb_ref,
                x1_ref, gate_ref, pos_ref, stats_ref, cnt_ref, cnt_scr, *, alpha):
    i = pl.program_id(0)

    @pl.when(i == 0)
    def _():
        cnt_scr[...] = jnp.zeros_like(cnt_scr)

    tm, dm = x_ref.shape
    n_split = 2 if tm % 16 == 0 else 1
    hm = tm // n_split
    lane = lax.broadcasted_iota(jnp.int32, (hm, N_EXPERTS), 1).astype(jnp.float32)
    slot = lax.broadcasted_iota(jnp.int32, (hm, SLOT_PAD), 1)

    def route_rows(rows):
        x = x_ref[rows, :]
        gates = jax.nn.sigmoid(_dot(x, wg_ref[...]))
        attn_out = _dot(attn_ref[rows, :], wap_ref[...])
        mixed = _dot(gates[:, :dm] * ssm_ref[rows, :] + gates[:, dm:] * attn_out, wout_ref[...])
        x1 = _layer_norm(alpha * x + mixed, g1_ref[...], b1_ref[...])
        x1_ref[rows, :] = x1
        logits = jnp.dot(x1, wr_ref[...], preferred_element_type=jnp.float32, precision=lax.Precision.HIGHEST)
        scores = jax.nn.sigmoid(logits)
        cur = scores + rb_ref[...]
        chosen = jnp.zeros_like(scores)
        picks = []
        for _ in range(MOE_TOPK):
            best = jnp.max(cur, axis=-1, keepdims=True)
            e_k = jnp.min(jnp.where(cur == best, lane, float(N_EXPERTS)), axis=-1, keepdims=True)
            hot = lane == e_k
            picks.append((hot, jnp.sum(jnp.where(hot, scores, 0.0), axis=-1, keepdims=True)))
            chosen = jnp.where(hot, 1.0, chosen)
            cur = jnp.where(hot, -jnp.inf, cur)
        total = picks[0][1]
        for _, s_k in picks[1:]:
            total = total + s_k
        return chosen, picks, total

    halves = [route_rows(slice(h * hm, (h + 1) * hm)) for h in range(n_split)]
    chosen = jnp.concatenate([h[0] for h in halves], axis=0)

    row = lax.broadcasted_iota(jnp.int32, (tm, tm), 0)
    col = lax.broadcasted_iota(jnp.int32, (tm, tm), 1)
    before = _dot(jnp.where(col < row, 1.0, 0.0), chosen)
    tile_cnt = jnp.ceil(jnp.sum(chosen, axis=0, keepdims=True) / SEG_ALIGN) * SEG_ALIGN
    e_row = lax.broadcasted_iota(jnp.int32, (N_EXPERTS, N_EXPERTS), 0)
    e_col = lax.broadcasted_iota(jnp.int32, (N_EXPERTS, N_EXPERTS), 1)
    tile_off = jnp.dot(jnp.broadcast_to(tile_cnt, (8, N_EXPERTS)), jnp.where(e_row < e_col, 1.0, 0.0),
                       preferred_element_type=jnp.float32, precision=lax.Precision.HIGHEST)[0:1]
    local = before + tile_off
    for h, (_, picks, total) in enumerate(halves):
        rows = slice(h * hm, (h + 1) * hm)
        g_out = jnp.zeros((hm, SLOT_PAD), jnp.float32)
        p_out = jnp.zeros((hm, SLOT_PAD), jnp.int32)
        for k, (hot, s_k) in enumerate(picks):
            pos_k = jnp.sum(jnp.where(hot, local[rows], 0.0), axis=-1, keepdims=True)
            g_out = jnp.where(slot == k, s_k / total * ROUTED_SCALE, g_out)
            p_out = jnp.where(slot == k, pos_k.astype(jnp.int32), p_out)
        gate_ref[rows, :] = g_out
        pos_ref[rows, :] = p_out
    srow = lax.broadcasted_iota(jnp.int32, (8, N_EXPERTS), 0)
    stats_ref[...] = jnp.where(srow == 0, cnt_scr[...], jnp.where(srow == 1, tile_cnt, jnp.where(
        srow == 2, tile_off, 0.0)))
    cnt_scr[...] = cnt_scr[...] + tile_cnt
    cnt_ref[...] = cnt_scr[...]


def _mix(x2, attn2, ssm2, wg, wap, wout, ln_g, ln_b, w_router, router_bias, alpha):
    t, dm = x2.shape
    tm = min(MOE_TILE, t)
    row = lambda w: pl.BlockSpec((tm, w), lambda i: (i, 0))
    consts = (wg, wap, wout, ln_g, ln_b, w_router, router_bias)
    return pl.pallas_call(
        functools.partial(_mix_kernel, alpha=alpha),
        grid=(t // tm,),
        in_specs=[row(dm), row(D_ATTN), row(dm)] + [_const_spec(c.shape) for c in consts],
        out_specs=[row(dm), row(SLOT_PAD), row(SLOT_PAD),
                   pl.BlockSpec((None, 8, N_EXPERTS), lambda i: (i, 0, 0)),
                   pl.BlockSpec((1, N_EXPERTS), lambda i: (0, 0))],
        out_shape=[jax.ShapeDtypeStruct((t, dm), jnp.float32),
                   jax.ShapeDtypeStruct((t, SLOT_PAD), jnp.float32),
                   jax.ShapeDtypeStruct((t, SLOT_PAD), jnp.int32),
                   jax.ShapeDtypeStruct((t // tm, 8, N_EXPERTS), jnp.float32),
                   jax.ShapeDtypeStruct((1, N_EXPERTS), jnp.float32)],
        scratch_shapes=[pltpu.VMEM((1, N_EXPERTS), jnp.float32)],
        compiler_params=_params("arbitrary"),
        name="mix_ln_router",
    )(x2, attn2, ssm2, *consts)


def _local_rows(tm):
    assert (N_EXPERTS * SEG_ALIGN) % tm == 0
    return MOE_TOPK * tm + N_EXPERTS * SEG_ALIGN


def _segment_pieces(i, cnt_ref, off_ref, dst_ref, tm, visit):
    def segment(e, carry):
        n = cnt_ref[i * N_EXPERTS + e]
        a = off_ref[i * N_EXPERTS + e]
        b = dst_ref[i * N_EXPERTS + e]
        p = tm
        while p >= SEG_ALIGN:
            done = (n // (2 * p)) * (2 * p)

            @pl.when((n & p) != 0)
            def _(p=p, done=done):
                visit(pl.multiple_of(a + done, SEG_ALIGN), pl.multiple_of(b + done, SEG_ALIGN), p)

            p //= 2
        return carry

    lax.fori_loop(0, N_EXPERTS, segment, 0)


def _dispatch_kernel(cnt_ref, off_ref, dst_ref, zlo_ref, zhi_ref, post_ref, x_ref, xs_ref,
                     loc, zero_scr, sem, *, tm):
    i = pl.program_id(0)

    xb = x_ref[...].astype(MXU_DTYPE)
    post = post_ref[...]
    for sc in range(_local_rows(tm) // tm):
        slot = sc * tm + lax.broadcasted_iota(jnp.int32, (tm, tm), 0)
        onehot = jnp.zeros((tm, tm), jnp.float32)
        for k in range(MOE_TOPK):
            onehot = jnp.where(slot == post[k:k + 1, :], 1.0, onehot)
        loc[sc * tm:(sc + 1) * tm, :] = _dot(onehot, xb).astype(loc.dtype)

    def piece(a, b, p):
        return pltpu.make_async_copy(loc.at[pl.ds(a, p)], xs_ref.at[pl.ds(b, p)], sem)

    _segment_pieces(i, cnt_ref, off_ref, dst_ref, tm, lambda a, b, p: piece(a, b, p).start())
    _segment_pieces(i, cnt_ref, off_ref, dst_ref, tm, lambda a, b, p: piece(a, b, p).wait())

    @pl.when(i == pl.num_programs(0) - 1)
    def _():
        zero_scr[...] = jnp.zeros_like(zero_scr)

        def zero_copy(group, rows):
            return pltpu.make_async_copy(
                zero_scr.at[pl.ds(0, rows)], xs_ref.at[pl.ds(pl.multiple_of(group * rows, rows), rows)], sem)

        def fill_groups(lo, hi, rows):
            def fill(g, c):
                zero_copy(g, rows).start()
                return c

            def fill_done(g, c):
                zero_copy(g, rows).wait()
                return c

            lax.fori_loop(lo, hi, fill, 0)
            lax.fori_loop(lo, hi, fill_done, 0)

        def segment(s, carry):
            fill_groups(zlo_ref[s], zhi_ref[s], SEG_ALIGN)
            return carry

        lax.fori_loop(0, N_EXPERTS, segment, 0)
        fill_groups(zlo_ref[N_EXPERTS], zhi_ref[N_EXPERTS], EXPERT_ROWS)


def _dispatch(x1, pos_t, seg_cnt, seg_off, seg_dst, zlo, zhi, n_rows):
    t, dm = x1.shape
    tm = min(MOE_TILE, t)
    return pl.pallas_call(
        functools.partial(_dispatch_kernel, tm=tm),
        grid_spec=pltpu.PrefetchScalarGridSpec(
            num_scalar_prefetch=5,
            grid=(t // tm,),
            in_specs=[pl.BlockSpec((None, SLOT_PAD, tm), lambda i, *_: (i, 0, 0)),
                      pl.BlockSpec((tm, dm), lambda i, *_: (i, 0))],
            out_specs=pl.BlockSpec(memory_space=pl.ANY),
            scratch_shapes=[pltpu.VMEM((_local_rows(tm), dm), MXU_DTYPE), pltpu.VMEM((EXPERT_ROWS, dm), MXU_DTYPE),
                            pltpu.SemaphoreType.DMA(())],
        ),
        out_shape=jax.ShapeDtypeStruct((n_rows, dm), MXU_DTYPE),
        compiler_params=_params("arbitrary"),
        name="moe_dispatch",
    )(seg_cnt, seg_off, seg_dst, zlo, zhi, pos_t, x1)


def _expert_kernel(be_ref, nu_ref, xs_ref, wgu_ref, wdn_ref, ys_ref, wgu_scr, wdn_scr):
    i = pl.program_id(0)

    @pl.when(i < nu_ref[0])
    def _():
        @pl.when((i == 0) | (be_ref[i] != be_ref[jnp.maximum(i - 1, 0)]))
        def _():
            wgu_scr[...] = wgu_ref[...].astype(MXU_DTYPE)
            wdn_scr[...] = wdn_ref[...].astype(MXU_DTYPE)

        h = _dot(xs_ref[...], wgu_scr[...])
        f = h.shape[1] // 2
        ys_ref[...] = _dot(jax.nn.silu(h[:, :f]) * h[:, f:], wdn_scr[...]).astype(ys_ref.dtype)

    @pl.when(i >= nu_ref[0])
    def _():
        ys_ref[...] = jnp.zeros_like(ys_ref)


def _experts(xs, block_expert, n_used, w_gu, w_down):
    n_rows, dm = xs.shape
    bm = EXPERT_ROWS
    f2 = w_gu.shape[2]
    used = lambda i, nu: jnp.minimum(i, nu[0] - 1)
    return pl.pallas_call(
        _expert_kernel,
        grid_spec=pltpu.PrefetchScalarGridSpec(
            num_scalar_prefetch=2,
            grid=(n_rows // bm,),
            in_specs=[pl.BlockSpec((bm, dm), lambda i, be, nu: (used(i, nu), 0)),
                      pl.BlockSpec((None, dm, f2), lambda i, be, nu: (be[used(i, nu)], 0, 0)),
                      pl.BlockSpec((None, f2 // 2, dm), lambda i, be, nu: (be[used(i, nu)], 0, 0))],
            out_specs=pl.BlockSpec((bm, dm), lambda i, be, nu: (i, 0)),
            scratch_shapes=[pltpu.VMEM((dm, f2), MXU_DTYPE), pltpu.VMEM((f2 // 2, dm), MXU_DTYPE)],
        ),
        out_shape=jax.ShapeDtypeStruct((n_rows, dm), MXU_DTYPE),
        compiler_params=_params("arbitrary"),
        name="moe_experts",
    )(block_expert, n_used, xs, w_gu, w_down)


def _combine_kernel(cnt_ref, off_ref, dst_ref, x1_ref, gate_ref, pos_ref, wsgu_ref, wsdn_ref, g2_ref, b2_ref,
                    ys_ref, y_ref, loc, sem, *, tm, alpha):
    i = pl.program_id(0)

    @pl.when(i == 0)
    def _():
        loc[...] = jnp.zeros_like(loc)

    def piece(a, b, p):
        return pltpu.make_async_copy(ys_ref.at[pl.ds(b, p)], loc.at[pl.ds(a, p)], sem)

    _segment_pieces(i, cnt_ref, off_ref, dst_ref, tm, lambda a, b, p: piece(a, b, p).start())

    x1 = x1_ref[...]
    h = _dot(x1, wsgu_ref[...])
    f = h.shape[1] // 2
    shared = _dot(jax.nn.silu(h[:, :f]) * h[:, f:], wsdn_ref[...])

    _segment_pieces(i, cnt_ref, off_ref, dst_ref, tm, lambda a, b, p: piece(a, b, p).wait())

    gate = gate_ref[...]
    pos = pos_ref[...]
    routed = shared
    for sc in range(_local_rows(tm) // tm):
        slot = sc * tm + lax.broadcasted_iota(jnp.int32, (tm, tm), 1)
        w = jnp.zeros((tm, tm), jnp.float32)
        for k in range(MOE_TOPK):
            w = jnp.where(slot == pos[:, k:k + 1], gate[:, k:k + 1], w)
        routed = routed + _dot(w, loc[sc * tm:(sc + 1) * tm, :])
    y_ref[...] = _layer_norm(alpha * x1 + routed, g2_ref[...], b2_ref[...])


def _combine(x1, gate, pos, seg_cnt, seg_off, seg_dst, ys, w_sh_gu, w_sh_down, ln_g, ln_b, alpha):
    t, dm = x1.shape
    tm = min(MOE_TILE, t)
    const = lambda a: pl.BlockSpec(a.shape, lambda i, *_: (0,) * a.ndim, pipeline_mode=pl.Buffered(1))
    return pl.pallas_call(
        functools.partial(_combine_kernel, tm=tm, alpha=alpha),
        grid_spec=pltpu.PrefetchScalarGridSpec(
            num_scalar_prefetch=3,
            grid=(t // tm,),
            in_specs=[pl.BlockSpec((tm, dm), lambda i, *_: (i, 0)),
                      pl.BlockSpec((tm, SLOT_PAD), lambda i, *_: (i, 0)),
                      pl.BlockSpec((tm, SLOT_PAD), lambda i, *_: (i, 0)),
                      const(w_sh_gu), const(w_sh_down), const(ln_g), const(ln_b),
                      pl.BlockSpec(memory_space=pl.ANY)],
            out_specs=pl.BlockSpec((tm, dm), lambda i, *_: (i, 0)),
            scratch_shapes=[pltpu.VMEM((_local_rows(tm), dm), MXU_DTYPE), pltpu.SemaphoreType.DMA(())],
        ),
        out_shape=jax.ShapeDtypeStruct((t, dm), jnp.float32),
        compiler_params=_params("arbitrary"),
        name="moe_combine",
    )(seg_cnt, seg_off, seg_dst, x1, gate, pos, w_sh_gu, w_sh_down, ln_g, ln_b, ys)


def _moe(x1, gate, pos, stats, counts, w_exp_gu, w_exp_down, w_sh_gu, w_sh_down, ln_g, ln_b, alpha):
    t = x1.shape[0]
    bm = EXPERT_ROWS
    n_tiles = stats.shape[0]
    n_rows = (-(-(t * MOE_TOPK + n_tiles * N_EXPERTS * (SEG_ALIGN - 1)) // bm) + N_EXPERTS) * bm
    cnt = counts.reshape(N_EXPERTS).astype(jnp.int32)
    padded = (cnt + bm - 1) // bm * bm
    end = jnp.cumsum(padded)
    start = end - padded
    n_used = end[-1:] // bm
    zlo = jnp.concatenate([(start + cnt) // SEG_ALIGN, n_used])
    zhi = jnp.concatenate([end // SEG_ALIGN, jnp.full((1,), n_rows // bm, jnp.int32)])
    block_start = jnp.arange(n_rows // bm, dtype=jnp.int32) * bm
    block_expert = jnp.minimum(jnp.sum((end[None, :] <= block_start[:, None]).astype(jnp.int32), axis=1),
                               N_EXPERTS - 1)
    seg = stats.astype(jnp.int32)
    seg_cnt, seg_off = seg[:, 1, :].reshape(-1), seg[:, 2, :].reshape(-1)
    seg_dst = (start[None, :] + seg[:, 0, :]).reshape(-1)
    pos_t = jnp.transpose(pos.reshape(n_tiles, t // n_tiles, SLOT_PAD), (0, 2, 1))
    xs = _dispatch(x1, pos_t, seg_cnt, seg_off, seg_dst, zlo, zhi, n_rows)
    ys = _experts(xs, block_expert, n_used, w_exp_gu, w_exp_down)
    return _combine(x1, gate, pos, seg_cnt, seg_off, seg_dst, ys, w_sh_gu, w_sh_down, ln_g, ln_b, alpha)


def _layer(x, pos0, past, p, alpha):
    (w_in, a_re, a_im, log_dt, b_re, b_im, c_re, c_im, ssm_d, w_glu, w_attn_proj, w_out, ln1_g, ln1_b,
     w_router, router_bias, w_exp_gu, w_exp_down, w_sh_gu, w_sh_down, ln2_g, ln2_b) = p
    nb, seq, dm = x.shape
    assert pos0 % CHUNK == 0 and seq % Q_BLOCK == 0 and Q_BLOCK == CHUNK and seq % S5_STEPS == 0
    assert nb * seq * MOE_TOPK < 2 ** 24
    t = nb * seq
    f32 = jnp.float32
    x2 = x.reshape(t, dm)
    pos = pos0 + jnp.arange(seq, dtype=jnp.int32)

    u, q, iq, k, v, ik2, ik, iw = _in_proj(x2, _pack_w_in(w_in), pos, seq)

    a_cat, bcat, ccat = _s5_discretise(a_re, a_im, log_dt, b_re, b_im, c_re, c_im)
    if past is None:
        h0 = jnp.zeros((nb, 2 * NS), f32)
    else:
        h0 = jnp.concatenate([past[3].reshape(nb, NS), past[4].reshape(nb, NS)], axis=1).astype(f32)
    u_t = jnp.transpose(u.reshape(nb, seq, D_SSM), (1, 0, 2))
    ssm_t, h_t = _s5(u_t, h0, a_cat, bcat, ccat, ssm_d.reshape(1, D_SSM).astype(f32), w_glu.astype(MXU_DTYPE))
    ssm2 = jnp.transpose(ssm_t, (1, 0, 2)).reshape(t, dm)
    h_re = h_t[:, :NS].reshape(nb, N_GROUPS, N_STATE)
    h_im = h_t[:, NS:].reshape(nb, N_GROUPS, N_STATE)

    k3, v3, ik23 = (a.reshape(nb, seq, LANES) for a in (k, v, ik2))
    if past is not None:
        ck, cv, cik = past[0], past[1], past[2]
        n_past = ck.shape[1]
        k3 = jnp.concatenate([ck.reshape(nb, n_past, LANES), k3], axis=1)
        v3 = jnp.concatenate([cv.reshape(nb, n_past, LANES), v3], axis=1)
        ik23 = jnp.concatenate([jnp.concatenate([cik, cik], axis=-1), ik23], axis=1)
    attn = _dsa(q.reshape(nb, seq, D_ATTN), iq.reshape(nb, seq, IDX_HEADS * IDX_DIM), iw.reshape(nb, seq, LANES),
                k3, v3, ik23, pos0 // CHUNK)

    wap = jnp.transpose(w_attn_proj.reshape(N_KV_HEADS, KV_GROUP, HEAD_DIM, dm), (1, 0, 2, 3)).reshape(D_ATTN, dm)
    wg = w_in[:, w_in.shape[1] - 2 * dm:]
    row = lambda a: a.reshape(1, -1).astype(f32)
    x1, gate, pos_slot, stats, counts = _mix(
        x2, attn.reshape(t, D_ATTN), ssm2, wg.astype(MXU_DTYPE), wap.astype(MXU_DTYPE), w_out.astype(MXU_DTYPE),
        row(ln1_g), row(ln1_b), w_router.astype(f32), row(router_bias), alpha)

    y = _moe(x1, gate, pos_slot, stats, counts, w_exp_gu, w_exp_down,
             w_sh_gu.astype(MXU_DTYPE), w_sh_down.astype(MXU_DTYPE), row(ln2_g), row(ln2_b), alpha)
    state = (k.reshape(nb, seq, N_KV_HEADS, HEAD_DIM), v.reshape(nb, seq, N_KV_HEADS, HEAD_DIM),
             ik.reshape(nb, seq, IDX_DIM), h_re, h_im)
    return y.reshape(nb, seq, dm), state


def kernel(x_prompt, x_sample, cache_k, cache_v, cache_idx_k, state_ssm_re, state_ssm_im, w_in, ssm_a_re, ssm_a_im, ssm_log_dt, ssm_b_re, ssm_b_im, ssm_c_re, ssm_c_im, ssm_d, w_glu, w_attn_proj, w_out, ln1_g, ln1_b, w_router, router_bias, w_exp_gu, w_exp_down, w_sh_gu, w_sh_down, ln2_g, ln2_b):
    weights = (w_in, ssm_a_re, ssm_a_im, ssm_log_dt, ssm_b_re, ssm_b_im, ssm_c_re, ssm_c_im, ssm_d,
               w_glu, w_attn_proj, w_out, ln1_g, ln1_b, w_router, router_bias,
               w_exp_gu, w_exp_down, w_sh_gu, w_sh_down, ln2_g, ln2_b)
    depth = w_in.shape[0]
    alpha = (2 * depth) ** 0.25
    past_len = cache_k.shape[2]
    y_p, y_s = x_prompt, x_sample
    new_p, new_s = [], []
    for l in range(depth):
        p_l = tuple(w[l] for w in weights)
        y_p, st_p = _layer(y_p, 0, None, p_l, alpha)
        y_s, st_s = _layer(y_s, past_len,
                           (cache_k[l], cache_v[l], cache_idx_k[l], state_ssm_re[l], state_ssm_im[l]), p_l, alpha)
        new_p.append(st_p)
        new_s.append(st_s)
    k_p, v_p, ik_p, hre_p, him_p = [jnp.stack(a) for a in zip(*new_p)]
    k_s, v_s, ik_s, hre_s, him_s = [jnp.stack(a) for a in zip(*new_s)]
    return (y_p, y_s, k_p, v_p, ik_p, hre_p, him_p, k_s, v_s, ik_s, hre_s, him_s)
```

```python
import functools

import jax
import jax.numpy as jnp
from jax import lax
from jax.experimental import pallas as pl
from jax.experimental.pallas import tpu as pltpu

CHUNK = 64
D_SSM = 512
SSM_GROUP = 16
N_GROUPS = D_SSM // SSM_GROUP
N_STATE = 64
N_HEADS = 8
N_KV_HEADS = 2
HEAD_DIM = 64
D_ATTN = N_HEADS * HEAD_DIM
KV_GROUP = N_HEADS // N_KV_HEADS
ROPE_DIM = HEAD_DIM // 4
ROPE_THETA = 500000.0
IDX_HEADS = 8
IDX_DIM = 64
TOPK_KEYS = 256
Q_BLOCK = 64
N_EXPERTS = 64
MOE_TOPK = 6
D_EXPERT = 256
D_SHARED = 256
ROUTED_SCALE = 2.5
LN_EPS = 1e-5

LANES = 128
KEY_CHUNK = 256
DSA_ROWS = 256
EXPERT_ROWS = 512
MOE_TILE = 512
SEG_ALIGN = 16
SLOT_PAD = 8
NS = N_GROUPS * N_STATE
VMEM_LIMIT = 56 * 1024 * 1024
NEG_BIG = -1e30
INT_MIN = -2147483648

MXU_DTYPE = jnp.bfloat16


def _dot(a, b):
    return jnp.dot(a.astype(MXU_DTYPE), b.astype(MXU_DTYPE), preferred_element_type=jnp.float32)


def _dot_nt(a, b):
    return lax.dot_general(a.astype(MXU_DTYPE), b.astype(MXU_DTYPE), (((1,), (1,)), ((), ())),
                           preferred_element_type=jnp.float32)


def _params(*sem):
    return pltpu.CompilerParams(dimension_semantics=sem, vmem_limit_bytes=VMEM_LIMIT)


def _const_spec(shape):
    return pl.BlockSpec(shape, lambda *_: (0,) * len(shape), pipeline_mode=pl.Buffered(1))


_C_U, _C_Q, _C_IQ, _C_K, _C_V, _C_IK, _C_IW, _C_END = 0, 512, 1024, 1536, 1664, 1792, 1920, 2048


def _inproj_kernel(x_ref, w_ref, cos_ref, sin_ref,
                   u_ref, q_ref, iq_ref, k_ref, v_ref, ik2_ref, ik_ref, iw_ref):
    z = _dot(x_ref[...], w_ref[...])
    cos = cos_ref[...]
    sin = sin_ref[...]
    lane = lax.broadcasted_iota(jnp.int32, cos.shape, 1)
    first_half = (lane % HEAD_DIM) < (ROPE_DIM // 2)

    def rope(zc):
        partner = jnp.where(first_half, pltpu.roll(zc, LANES - ROPE_DIM // 2, 1), pltpu.roll(zc, ROPE_DIM // 2, 1))
        return zc * cos + partner * sin

    u_ref[...] = z[:, _C_U:_C_Q].astype(u_ref.dtype)
    for c in range(4):
        q_ref[:, c * LANES:(c + 1) * LANES] = rope(z[:, _C_Q + c * LANES:_C_Q + (c + 1) * LANES])
        iq_ref[:, c * LANES:(c + 1) * LANES] = rope(z[:, _C_IQ + c * LANES:_C_IQ + (c + 1) * LANES])
    k_ref[...] = rope(z[:, _C_K:_C_V])
    v_ref[...] = z[:, _C_V:_C_IK]
    ik2 = rope(z[:, _C_IK:_C_IW])
    ik2_ref[...] = ik2
    ik_ref[...] = ik2[:, :IDX_DIM]
    iw_ref[...] = z[:, _C_IW:_C_END]


def _rope_tables(pos):
    half = ROPE_DIM // 2
    inv_freq = ROPE_THETA ** (-jnp.arange(half, dtype=jnp.float32) * 2.0 / ROPE_DIM)
    ang = pos.astype(jnp.float32)[:, None] * inv_freq
    cos, sin = jnp.cos(ang), jnp.sin(ang)
    n = pos.shape[0]
    pad = HEAD_DIM - ROPE_DIM
    cos_h = jnp.concatenate([cos, cos, jnp.ones((n, pad), jnp.float32)], axis=-1)
    sin_h = jnp.concatenate([-sin, sin, jnp.zeros((n, pad), jnp.float32)], axis=-1)
    return jnp.tile(cos_h, (1, LANES // HEAD_DIM)), jnp.tile(sin_h, (1, LANES // HEAD_DIM))


def _pack_w_in(w_in):
    s = [0, 512, 1024, 1152, 1280, 1792, 1856, 1864]
    w_u, w_q, w_k, w_v = w_in[:, s[0]:s[1]], w_in[:, s[1]:s[2]], w_in[:, s[2]:s[3]], w_in[:, s[3]:s[4]]
    w_iq, w_ik, w_iw = w_in[:, s[4]:s[5]], w_in[:, s[5]:s[6]], w_in[:, s[6]:s[7]]
    d = w_in.shape[0]
    w_qh = w_q.reshape(d, N_KV_HEADS, KV_GROUP, HEAD_DIM)
    w_qp = jnp.transpose(w_qh, (0, 2, 1, 3)).reshape(d, D_ATTN)
    w_iwp = jnp.concatenate([w_iw, jnp.zeros((d, LANES - IDX_HEADS), w_in.dtype)], axis=1)
    return jnp.concatenate([w_u, w_qp, w_iq, w_k, w_v, w_ik, w_ik, w_iwp], axis=1).astype(MXU_DTYPE)


def _in_proj(x2, w_pack, pos, seq, u_dtype):
    t, d = x2.shape
    tm = min(512, t)
    cos, sin = _rope_tables(pos)
    if seq >= tm:
        per = seq // tm
        tab_map = lambda i: (i % per, 0)
    else:
        cos, sin = jnp.tile(cos, (tm // seq, 1)), jnp.tile(sin, (tm // seq, 1))
        tab_map = lambda i: (0, 0)
    row = lambda w: pl.BlockSpec((tm, w), lambda i: (i, 0))
    widths = (D_SSM, D_ATTN, IDX_HEADS * IDX_DIM, LANES, LANES, LANES, IDX_DIM, LANES)
    return pl.pallas_call(
        _inproj_kernel,
        grid=(t // tm,),
        in_specs=[row(d), _const_spec(w_pack.shape),
                  pl.BlockSpec((tm, LANES), tab_map), pl.BlockSpec((tm, LANES), tab_map)],
        out_specs=[row(w) for w in widths],
        out_shape=[jax.ShapeDtypeStruct((t, widths[0]), u_dtype)]
        + [jax.ShapeDtypeStruct((t, w), jnp.float32) for w in widths[1:]],
        compiler_params=_params("parallel"),
        name="in_proj",
    )(x2, w_pack, cos, sin)


S5_STEPS = 16
S5_COLS = 512


def _s5_kernel(u_ref, h0_ref, a_ref, bcat_ref, ccat_ref, d_ref, wglu_ref,
               out_ref, ht_ref, h_scr, bu_scr, hall_scr, *, nb):
    @pl.when(pl.program_id(0) == 0)
    def _():
        h_scr[...] = h0_ref[...]

    u = u_ref[...].reshape(S5_STEPS * nb, D_SSM)
    bu_scr[...] = _dot(u, bcat_ref[...])
    for cb in range(NS // S5_COLS):
        re = slice(cb * S5_COLS, (cb + 1) * S5_COLS)
        im = slice(NS + cb * S5_COLS, NS + (cb + 1) * S5_COLS)
        a_re, a_im = a_ref[0:1, re], a_ref[1:2, re]
        h_re, h_im = h_scr[:, re], h_scr[:, im]
        for t in range(S5_STEPS):
            rows = slice(t * nb, (t + 1) * nb)
            h_re, h_im = (a_re * h_re - a_im * h_im + bu_scr[rows, re],
                          a_re * h_im + a_im * h_re + bu_scr[rows, im])
            hall_scr[rows, re] = h_re
            hall_scr[rows, im] = h_im
        h_scr[:, re] = h_re
        h_scr[:, im] = h_im
    y = _dot(hall_scr[...], ccat_ref[...]) + d_ref[...] * u
    y = jax.nn.gelu(y)
    g = _dot(y, wglu_ref[...])
    dm = g.shape[1] // 2
    out_ref[...] = (g[:, :dm] * jax.nn.sigmoid(g[:, dm:])).astype(out_ref.dtype).reshape(S5_STEPS, nb, dm)
    ht_ref[...] = h_scr[...]


def _s5_discretise(a_re, a_im, log_dt, b_re, b_im, c_re, c_im):
    f32 = jnp.float32
    ar, ai = a_re.astype(f32), a_im.astype(f32)
    dt = jnp.exp(log_dt.astype(f32))[:, None]
    mag = jnp.exp(dt * ar)
    abar_re, abar_im = mag * jnp.cos(dt * ai), mag * jnp.sin(dt * ai)
    den = ar * ar + ai * ai
    num_re, num_im = abar_re - 1.0, abar_im
    coef_re = (num_re * ar + num_im * ai) / den
    coef_im = (num_im * ar - num_re * ai) / den
    br, bi = b_re.astype(f32), b_im.astype(f32)
    bbar_re = coef_re[..., None] * br - coef_im[..., None] * bi
    bbar_im = coef_re[..., None] * bi + coef_im[..., None] * br
    eye = jnp.eye(N_GROUPS, dtype=f32)
    bd = lambda m: jnp.einsum("gnc,gh->gchn", m, eye).reshape(D_SSM, NS)
    bcat = jnp.concatenate([bd(bbar_re), bd(bbar_im)], axis=1)
    cd = lambda m: jnp.einsum("gcn,gh->gnhc", m, eye).reshape(NS, D_SSM)
    ccat = jnp.concatenate([cd(c_re.astype(f32)), cd(-c_im.astype(f32))], axis=0)
    a_cat = jnp.stack([abar_re.reshape(NS), abar_im.reshape(NS)])
    return a_cat, bcat.astype(MXU_DTYPE), ccat.astype(MXU_DTYPE)


def _s5(u_t, h0, a_cat, bcat, ccat, d_skip, w_glu):
    seq, nb, _ = u_t.shape
    dm = w_glu.shape[1] // 2
    rows = S5_STEPS * nb
    return pl.pallas_call(
        functools.partial(_s5_kernel, nb=nb),
        grid=(seq // S5_STEPS,),
        in_specs=[pl.BlockSpec((S5_STEPS, nb, D_SSM), lambda i: (i, 0, 0)),
                  _const_spec(h0.shape), _const_spec(a_cat.shape), _const_spec(bcat.shape),
                  _const_spec(ccat.shape), _const_spec(d_skip.shape), _const_spec(w_glu.shape)],
        out_specs=[pl.BlockSpec((S5_STEPS, nb, dm), lambda i: (i, 0, 0)),
                   pl.BlockSpec(h0.shape, lambda i: (0, 0))],
        out_shape=[jax.ShapeDtypeStruct((seq, nb, dm), u_t.dtype),
                   jax.ShapeDtypeStruct(h0.shape, jnp.float32)],
        scratch_shapes=[pltpu.VMEM(h0.shape, jnp.float32),
                        pltpu.VMEM((rows, 2 * NS), jnp.float32),
                        pltpu.VMEM((rows, 2 * NS), jnp.float32)],
        compiler_params=_params("arbitrary"),
        name="s5_scan_glu",
    )(u_t, h0, a_cat, bcat, ccat, d_skip, w_glu)


def _sortable(x):
    bits = pltpu.bitcast(x, jnp.int32)
    key = bits ^ ((bits >> 31) & jnp.int32(0x7FFFFFFF))
    return jnp.where(x == 0.0, jnp.int32(0), key)


def _half_mask(shape, upper):
    lane = lax.broadcasted_iota(jnp.int32, shape, 1)
    return (lane >= HEAD_DIM) if upper else (lane < HEAD_DIM)


def _dsa_kernel(q_ref, iq_ref, iw_ref, k_ref, v_ref, ik_ref, o_ref,
                kb, vs0, vs1, ikb, key_scr, keyt_scr, bias_scr, wb_scr, iqlhs_scr, qlhs_scr, s_scr, m_scr, acc_scr,
                *, qt, n_keys, n_sel, first_chunk, idx_bits):
    j = pl.program_id(1)
    kc = KEY_CHUNK
    n_sub = qt // Q_BLOCK
    qtp = max(qt, LANES)

    @pl.when(j == 0)
    def _():
        v = v_ref[...]
        lower = lax.broadcasted_iota(jnp.int32, v.shape, v.ndim - 1) < HEAD_DIM
        kb[...] = k_ref[...].astype(MXU_DTYPE)
        vs0[...] = jnp.where(lower, v, 1.0).astype(MXU_DTYPE)
        vs1[...] = jnp.where(lower, 1.0, v).astype(MXU_DTYPE)
        ikb[...] = ik_ref[...].astype(MXU_DTYPE)

    row_blk = lax.broadcasted_iota(jnp.int32, (qt, 1), 0) // Q_BLOCK
    n_vis = jnp.minimum((first_chunk + j * n_sub + row_blk + 1) * CHUNK, n_keys)
    n_ck = (jnp.minimum((first_chunk + (j + 1) * n_sub) * CHUNK, n_keys) + kc - 1) // kc
    lane_k = lax.broadcasted_iota(jnp.int32, (qt, kc), 1)

    iq = iq_ref[...] * IDX_DIM ** -0.5
    iw = iw_ref[...] * IDX_HEADS ** -0.5
    for h in range(IDX_HEADS):
        iqlhs_scr[h] = jnp.where(_half_mask((qt, LANES), h % 2 == 1),
                                 iq[:, (h // 2) * LANES:(h // 2 + 1) * LANES], 0.0).astype(MXU_DTYPE)
        wb_scr[h] = jnp.broadcast_to(iw[:, h:h + 1], (qt, LANES))

    def score_chunk(c, carry):
        ikc = ikb[c]
        score = None
        for h in range(IDX_HEADS):
            rel = jnp.maximum(_dot_nt(iqlhs_scr[h], ikc), 0.0)
            w = wb_scr[h]
            term = jnp.concatenate([rel[:, s * LANES:(s + 1) * LANES] * w for s in range(kc // LANES)], axis=1)
            score = term if score is None else score + term
        key = jnp.where(c * kc + lane_k < n_vis, _sortable(score), jnp.int32(INT_MIN))
        key_scr[c] = key
        if qtp > qt:
            key = jnp.concatenate([key, jnp.full((qtp - qt, kc), INT_MIN, jnp.int32)], axis=0)
        keyt_scr[c] = key.T
        return carry

    lax.fori_loop(0, n_ck, score_chunk, 0)

    sub_k = lax.broadcasted_iota(jnp.int32, (kc // 8, 8, qtp), 0) * 8 + lax.broadcasted_iota(
        jnp.int32, (kc // 8, 8, qtp), 1)

    def count(pred):
        def body(c, acc):
            hit = pred(keyt_scr[c].reshape(kc // 8, 8, qtp), c * kc + sub_k)
            return acc + jnp.sum(jnp.where(hit, 1.0, 0.0), axis=0)
        acc = lax.fori_loop(0, n_ck, body, jnp.zeros((8, qtp), jnp.float32))
        for shift in (4, 2, 1):
            acc = acc + pltpu.roll(acc, shift, 0)
        return acc

    def value_bit(i, t):
        cand = t + (jnp.int32(1) << (31 - i))
        return jnp.where(count(lambda key, idx: key >= cand[None]) >= n_sel, cand, t)

    t = lax.fori_loop(0, 32, value_bit, jnp.full((8, qtp), INT_MIN, jnp.int32))
    need = n_sel - count(lambda key, idx: key > t[None])
    n_ge = count(lambda key, idx: key >= t[None])

    def last_tied_index():
        def index_bit(i, m):
            cand = m + (jnp.int32(1) << (idx_bits - 1 - i))
            below = count(lambda key, idx: (key == t[None]) & (idx < cand[None]))
            return jnp.where(below < need, cand, m)
        return lax.fori_loop(0, idx_bits, index_bit, jnp.zeros((8, qtp), jnp.int32))

    m_idx = lax.cond(jnp.max(n_ge) > n_sel, last_tied_index,
                     lambda: jnp.full((8, qtp), 2 ** 30, jnp.int32))
    t_rep = jnp.broadcast_to(t[0:1], (LANES, qtp)).T[:qt]
    m_rep = jnp.broadcast_to(m_idx[0:1], (LANES, qtp)).T[:qt]
    t_row = jnp.concatenate([t_rep] * (kc // LANES), axis=1)
    m_row = jnp.concatenate([m_rep] * (kc // LANES), axis=1)

    def bias_chunk(c, carry):
        key = key_scr[c]
        idx = c * kc + lane_k
        sel = ((key > t_row) | ((key == t_row) & (idx <= m_row))) & (idx < n_vis)
        bias_scr[c] = jnp.where(sel, 0.0, NEG_BIG)
        return carry

    lax.fori_loop(0, n_ck, bias_chunk, 0)

    q = q_ref[...] * HEAD_DIM ** -0.5
    rows = KV_GROUP * qt
    for g in range(N_KV_HEADS):
        for r in range(KV_GROUP):
            qlhs_scr[g, r * qt:(r + 1) * qt] = jnp.where(
                _half_mask((qt, LANES), g == 1), q[:, r * LANES:(r + 1) * LANES], 0.0).astype(MXU_DTYPE)
    m_scr[...] = jnp.full(m_scr.shape, NEG_BIG, jnp.float32)
    acc_scr[...] = jnp.zeros(acc_scr.shape, jnp.float32)

    def logits_chunk(c, carry):
        bias = bias_scr[c][None]
        for g in range(N_KV_HEADS):
            s = (_dot_nt(qlhs_scr[g], kb[c]).reshape(KV_GROUP, qt, kc) + bias).reshape(rows, kc)
            s_scr[g, c] = s
            m = m_scr[g]
            for u in range(kc // LANES):
                m = jnp.maximum(m, s[:, u * LANES:(u + 1) * LANES])
            m_scr[g] = m
        return carry

    lax.fori_loop(0, n_ck, logits_chunk, 0)
    for g in range(N_KV_HEADS):
        m_scr[g] = jnp.broadcast_to(jnp.max(m_scr[g], axis=-1, keepdims=True), (rows, LANES))

    def pv_chunk(c, carry):
        for g, vs in enumerate((vs0, vs1)):
            m = m_scr[g]
            s = s_scr[g, c]
            p = jnp.concatenate([jnp.exp(s[:, u * LANES:(u + 1) * LANES] - m) for u in range(kc // LANES)], axis=1)
            acc_scr[g] += _dot(p, vs[c])
        return carry

    lax.fori_loop(0, n_ck, pv_chunk, 0)
    outs = [acc_scr[g] / pltpu.roll(acc_scr[g], HEAD_DIM, 1) for g in range(N_KV_HEADS)]
    lower = _half_mask((qt, LANES), False)
    for r in range(KV_GROUP):
        rs = slice(r * qt, (r + 1) * qt)
        o_ref[:, r * LANES:(r + 1) * LANES] = jnp.where(lower, outs[0][rs], outs[1][rs])


def _dsa(q, iq, iw, k_all, v_all, ik2_all, first_chunk):
    nb, seq, _ = q.shape
    n_keys = k_all.shape[1]
    n_sel = min(TOPK_KEYS, n_keys // 4)
    kc = KEY_CHUNK
    qt = min(DSA_ROWS, seq)
    assert seq % qt == 0
    n_ch = -(-n_keys // kc)
    pad = n_ch * kc - n_keys
    chunked = lambda a: jnp.pad(a, ((0, 0), (0, pad), (0, 0))).reshape(nb, n_ch, kc, LANES)
    qspec = lambda w: pl.BlockSpec((None, qt, w), lambda b, j: (b, j, 0))
    kspec = pl.BlockSpec((None, n_ch, kc, LANES), lambda b, j: (b, 0, 0, 0))
    rows = KV_GROUP * qt
    return pl.pallas_call(
        functools.partial(_dsa_kernel, qt=qt, n_keys=n_keys, n_sel=n_sel, first_chunk=first_chunk,
                          idx_bits=max(1, (n_ch * kc - 1).bit_length())),
        grid=(nb, seq // qt),
        in_specs=[qspec(D_ATTN), qspec(IDX_HEADS * IDX_DIM), qspec(LANES), kspec, kspec, kspec],
        out_specs=qspec(D_ATTN),
        out_shape=jax.ShapeDtypeStruct((nb, seq, D_ATTN), jnp.float32),
        scratch_shapes=[pltpu.VMEM((n_ch, kc, LANES), MXU_DTYPE)] * 4
        + [pltpu.VMEM((n_ch, qt, kc), jnp.int32), pltpu.VMEM((n_ch, kc, max(qt, LANES)), jnp.int32),
           pltpu.VMEM((n_ch, qt, kc), jnp.float32),
           pltpu.VMEM((IDX_HEADS, qt, LANES), jnp.float32), pltpu.VMEM((IDX_HEADS, qt, LANES), MXU_DTYPE),
           pltpu.VMEM((N_KV_HEADS, rows, LANES), MXU_DTYPE),
           pltpu.VMEM((N_KV_HEADS, n_ch, rows, kc), jnp.float32),
           pltpu.VMEM((N_KV_HEADS, rows, LANES), jnp.float32), pltpu.VMEM((N_KV_HEADS, rows, LANES), jnp.float32)],
        compiler_params=_params("arbitrary", "arbitrary"),
        name="dsa_attention",
    )(q, iq, iw, chunked(k_all), chunked(v_all), chunked(ik2_all))


def _layer_norm(h, g, b):
    mu = jnp.mean(h, axis=-1, keepdims=True)
    var = jnp.mean(jnp.square(h - mu), axis=-1, keepdims=True)
    return (h - mu) * lax.rsqrt(var + LN_EPS) * g + b


def _mix_kernel(x_ref, attn_ref, ssm_ref, wg_ref, wap_ref, wout_ref, g1_ref, b1_ref, wr_ref, rb_ref,
                x1_ref, gate_ref, pos_ref, stats_ref, cnt_ref, cnt_scr, *, alpha):
    i = pl.program_id(0)

    @pl.when(i == 0)
    def _():
        cnt_scr[...] = jnp.zeros_like(cnt_scr)

    tm, dm = x_ref.shape
    n_split = 2 if tm % 16 == 0 else 1
    hm = tm // n_split
    lane = lax.broadcasted_iota(jnp.int32, (hm, N_EXPERTS), 1).astype(jnp.float32)
    slot = lax.broadcasted_iota(jnp.int32, (hm, SLOT_PAD), 1)

    def route_rows(rows):
        x = x_ref[rows, :]
        gates = jax.nn.sigmoid(_dot(x, wg_ref[...]))
        attn_out = _dot(attn_ref[rows, :], wap_ref[...])
        mixed = _dot(gates[:, :dm] * ssm_ref[rows, :] + gates[:, dm:] * attn_out, wout_ref[...])
        x1 = _layer_norm(alpha * x + mixed, g1_ref[...], b1_ref[...])
        x1_ref[rows, :] = x1
        logits = jnp.dot(x1, wr_ref[...], preferred_element_type=jnp.float32, precision=lax.Precision.HIGHEST)
        scores = jax.nn.sigmoid(logits)
        cur = scores + rb_ref[...]
        chosen = jnp.zeros_like(scores)
        picks = []
        for _ in range(MOE_TOPK):
            best = jnp.max(cur, axis=-1, keepdims=True)
            e_k = jnp.min(jnp.where(cur == best, lane, float(N_EXPERTS)), axis=-1, keepdims=True)
            hot = lane == e_k
            picks.append((hot, jnp.sum(jnp.where(hot, scores, 0.0), axis=-1, keepdims=True)))
            chosen = jnp.where(hot, 1.0, chosen)
            cur = jnp.where(hot, -jnp.inf, cur)
        total = picks[0][1]
        for _, s_k in picks[1:]:
            total = total + s_k
        return chosen, picks, total

    halves = [route_rows(slice(h * hm, (h + 1) * hm)) for h in range(n_split)]
    chosen = jnp.concatenate([h[0] for h in halves], axis=0)

    row = lax.broadcasted_iota(jnp.int32, (tm, tm), 0)
    col = lax.broadcasted_iota(jnp.int32, (tm, tm), 1)
    before = _dot(jnp.where(col < row, 1.0, 0.0), chosen)
    tile_cnt = jnp.ceil(jnp.sum(chosen, axis=0, keepdims=True) / SEG_ALIGN) * SEG_ALIGN
    e_row = lax.broadcasted_iota(jnp.int32, (N_EXPERTS, N_EXPERTS), 0)
    e_col = lax.broadcasted_iota(jnp.int32, (N_EXPERTS, N_EXPERTS), 1)
    tile_off = jnp.dot(jnp.broadcast_to(tile_cnt, (8, N_EXPERTS)), jnp.where(e_row < e_col, 1.0, 0.0),
                       preferred_element_type=jnp.float32, precision=lax.Precision.HIGHEST)[0:1]
    local = before + tile_off
    for h, (_, picks, total) in enumerate(halves):
        rows = slice(h * hm, (h + 1) * hm)
        g_out = jnp.zeros((hm, SLOT_PAD), jnp.float32)
        p_out = jnp.zeros((hm, SLOT_PAD), jnp.int32)
        for k, (hot, s_k) in enumerate(picks):
            pos_k = jnp.sum(jnp.where(hot, local[rows], 0.0), axis=-1, keepdims=True)
            g_out = jnp.where(slot == k, s_k / total * ROUTED_SCALE, g_out)
            p_out = jnp.where(slot == k, pos_k.astype(jnp.int32), p_out)
        gate_ref[rows, :] = g_out
        pos_ref[rows, :] = p_out
    srow = lax.broadcasted_iota(jnp.int32, (8, N_EXPERTS), 0)
    stats_ref[...] = jnp.where(srow == 0, cnt_scr[...], jnp.where(srow == 1, tile_cnt, jnp.where(
        srow == 2, tile_off, 0.0)))
    cnt_scr[...] = cnt_scr[...] + tile_cnt
    cnt_ref[...] = cnt_scr[...]


def _mix(x2, attn2, ssm2, wg, wap, wout, ln_g, ln_b, w_router, router_bias, alpha):
    t, dm = x2.shape
    tm = min(MOE_TILE, t)
    row = lambda w: pl.BlockSpec((tm, w), lambda i: (i, 0))
    consts = (wg, wap, wout, ln_g, ln_b, w_router, router_bias)
    return pl.pallas_call(
        functools.partial(_mix_kernel, alpha=alpha),
        grid=(t // tm,),
        in_specs=[row(dm), row(D_ATTN), row(dm)] + [_const_spec(c.shape) for c in consts],
        out_specs=[row(dm), row(SLOT_PAD), row(SLOT_PAD),
                   pl.BlockSpec((None, 8, N_EXPERTS), lambda i: (i, 0, 0)),
                   pl.BlockSpec((1, N_EXPERTS), lambda i: (0, 0))],
        out_shape=[jax.ShapeDtypeStruct((t, dm), jnp.float32),
                   jax.ShapeDtypeStruct((t, SLOT_PAD), jnp.float32),
                   jax.ShapeDtypeStruct((t, SLOT_PAD), jnp.int32),
                   jax.ShapeDtypeStruct((t // tm, 8, N_EXPERTS), jnp.float32),
                   jax.ShapeDtypeStruct((1, N_EXPERTS), jnp.float32)],
        scratch_shapes=[pltpu.VMEM((1, N_EXPERTS), jnp.float32)],
        compiler_params=_params("arbitrary"),
        name="mix_ln_router",
    )(x2, attn2, ssm2, *consts)


def _local_rows(tm):
    assert (N_EXPERTS * SEG_ALIGN) % tm == 0
    return MOE_TOPK * tm + N_EXPERTS * SEG_ALIGN


def _segment_pieces(i, cnt_ref, off_ref, dst_ref, tm, visit):
    def segment(e, carry):
        n = cnt_ref[i * N_EXPERTS + e]
        a = off_ref[i * N_EXPERTS + e]
        b = dst_ref[i * N_EXPERTS + e]
        p = tm
        while p >= SEG_ALIGN:
            done = (n // (2 * p)) * (2 * p)

            @pl.when((n & p) != 0)
            def _(p=p, done=done):
                visit(pl.multiple_of(a + done, SEG_ALIGN), pl.multiple_of(b + done, SEG_ALIGN), p)

            p //= 2
        return carry

    lax.fori_loop(0, N_EXPERTS, segment, 0)


def _dispatch_kernel(cnt_ref, off_ref, dst_ref, zlo_ref, zhi_ref, post_ref, x_ref, xs_ref,
                     loc, zero_scr, sem, *, tm):
    i = pl.program_id(0)

    xb = x_ref[...].astype(MXU_DTYPE)
    post = post_ref[...]
    for sc in range(_local_rows(tm) // tm):
        slot = sc * tm + lax.broadcasted_iota(jnp.int32, (tm, tm), 0)
        onehot = jnp.zeros((tm, tm), jnp.float32)
        for k in range(MOE_TOPK):
            onehot = jnp.where(slot == post[k:k + 1, :], 1.0, onehot)
        loc[sc * tm:(sc + 1) * tm, :] = _dot(onehot, xb).astype(loc.dtype)

    def piece(a, b, p):
        return pltpu.make_async_copy(loc.at[pl.ds(a, p)], xs_ref.at[pl.ds(b, p)], sem)

    _segment_pieces(i, cnt_ref, off_ref, dst_ref, tm, lambda a, b, p: piece(a, b, p).start())
    _segment_pieces(i, cnt_ref, off_ref, dst_ref, tm, lambda a, b, p: piece(a, b, p).wait())

    @pl.when(i == pl.num_programs(0) - 1)
    def _():
        zero_scr[...] = jnp.zeros_like(zero_scr)

        def zero_copy(group, rows):
            return pltpu.make_async_copy(
                zero_scr.at[pl.ds(0, rows)], xs_ref.at[pl.ds(pl.multiple_of(group * rows, rows), rows)], sem)

        def fill_groups(lo, hi, rows):
            def fill(g, c):
                zero_copy(g, rows).start()
                return c

            def fill_done(g, c):
                zero_copy(g, rows).wait()
                return c

            lax.fori_loop(lo, hi, fill, 0)
            lax.fori_loop(lo, hi, fill_done, 0)

        def segment(s, carry):
            fill_groups(zlo_ref[s], zhi_ref[s], SEG_ALIGN)
            return carry

        lax.fori_loop(0, N_EXPERTS, segment, 0)
        fill_groups(zlo_ref[N_EXPERTS], zhi_ref[N_EXPERTS], EXPERT_ROWS)


def _dispatch(x1, pos_t, seg_cnt, seg_off, seg_dst, zlo, zhi, n_rows):
    t, dm = x1.shape
    tm = min(MOE_TILE, t)
    return pl.pallas_call(
        functools.partial(_dispatch_kernel, tm=tm),
        grid_spec=pltpu.PrefetchScalarGridSpec(
            num_scalar_prefetch=5,
            grid=(t // tm,),
            in_specs=[pl.BlockSpec((None, SLOT_PAD, tm), lambda i, *_: (i, 0, 0)),
                      pl.BlockSpec((tm, dm), lambda i, *_: (i, 0))],
            out_specs=pl.BlockSpec(memory_space=pl.ANY),
            scratch_shapes=[pltpu.VMEM((_local_rows(tm), dm), MXU_DTYPE), pltpu.VMEM((EXPERT_ROWS, dm), MXU_DTYPE),
                            pltpu.SemaphoreType.DMA(())],
        ),
        out_shape=jax.ShapeDtypeStruct((n_rows, dm), MXU_DTYPE),
        compiler_params=_params("arbitrary"),
        name="moe_dispatch",
    )(seg_cnt, seg_off, seg_dst, zlo, zhi, pos_t, x1)


def _expert_kernel(be_ref, nu_ref, xs_ref, wgu_ref, wdn_ref, ys_ref, wgu_scr, wdn_scr):
    i = pl.program_id(0)

    @pl.when(i < nu_ref[0])
    def _():
        @pl.when((i == 0) | (be_ref[i] != be_ref[jnp.maximum(i - 1, 0)]))
        def _():
            wgu_scr[...] = wgu_ref[...].astype(MXU_DTYPE)
            wdn_scr[...] = wdn_ref[...].astype(MXU_DTYPE)

        h = _dot(xs_ref[...], wgu_scr[...])
        f = h.shape[1] // 2
        ys_ref[...] = _dot(jax.nn.silu(h[:, :f]) * h[:, f:], wdn_scr[...]).astype(ys_ref.dtype)

    @pl.when(i >= nu_ref[0])
    def _():
        ys_ref[...] = jnp.zeros_like(ys_ref)


def _experts(xs, block_expert, n_used, w_gu, w_down):
    n_rows, dm = xs.shape
    bm = EXPERT_ROWS
    f2 = w_gu.shape[2]
    used = lambda i, nu: jnp.minimum(i, nu[0] - 1)
    return pl.pallas_call(
        _expert_kernel,
        grid_spec=pltpu.PrefetchScalarGridSpec(
            num_scalar_prefetch=2,
            grid=(n_rows // bm,),
            in_specs=[pl.BlockSpec((bm, dm), lambda i, be, nu: (used(i, nu), 0)),
                      pl.BlockSpec((None, dm, f2), lambda i, be, nu: (be[used(i, nu)], 0, 0)),
                      pl.BlockSpec((None, f2 // 2, dm), lambda i, be, nu: (be[used(i, nu)], 0, 0))],
            out_specs=pl.BlockSpec((bm, dm), lambda i, be, nu: (i, 0)),
            scratch_shapes=[pltpu.VMEM((dm, f2), MXU_DTYPE), pltpu.VMEM((f2 // 2, dm), MXU_DTYPE)],
        ),
        out_shape=jax.ShapeDtypeStruct((n_rows, dm), MXU_DTYPE),
        compiler_params=_params("arbitrary"),
        name="moe_experts",
    )(block_expert, n_used, xs, w_gu, w_down)


def _combine_kernel(cnt_ref, off_ref, dst_ref, x1_ref, gate_ref, pos_ref, wsgu_ref, wsdn_ref, g2_ref, b2_ref,
                    ys_ref, y_ref, loc, sem, *, tm, alpha):
    i = pl.program_id(0)

    @pl.when(i == 0)
    def _():
        loc[...] = jnp.zeros_like(loc)

    def piece(a, b, p):
        return pltpu.make_async_copy(ys_ref.at[pl.ds(b, p)], loc.at[pl.ds(a, p)], sem)

    _segment_pieces(i, cnt_ref, off_ref, dst_ref, tm, lambda a, b, p: piece(a, b, p).start())

    x1 = x1_ref[...]
    h = _dot(x1, wsgu_ref[...])
    f = h.shape[1] // 2
    shared = _dot(jax.nn.silu(h[:, :f]) * h[:, f:], wsdn_ref[...])

    _segment_pieces(i, cnt_ref, off_ref, dst_ref, tm, lambda a, b, p: piece(a, b, p).wait())

    gate = gate_ref[...]
    pos = pos_ref[...]
    routed = shared
    for sc in range(_local_rows(tm) // tm):
        slot = sc * tm + lax.broadcasted_iota(jnp.int32, (tm, tm), 1)
        w = jnp.zeros((tm, tm), jnp.float32)
        for k in range(MOE_TOPK):
            w = jnp.where(slot == pos[:, k:k + 1], gate[:, k:k + 1], w)
        routed = routed + _dot(w, loc[sc * tm:(sc + 1) * tm, :])
    y_ref[...] = _layer_norm(alpha * x1 + routed, g2_ref[...], b2_ref[...])


def _combine(x1, gate, pos, seg_cnt, seg_off, seg_dst, ys, w_sh_gu, w_sh_down, ln_g, ln_b, alpha):
    t, dm = x1.shape
    tm = min(MOE_TILE, t)
    const = lambda a: pl.BlockSpec(a.shape, lambda i, *_: (0,) * a.ndim, pipeline_mode=pl.Buffered(1))
    return pl.pallas_call(
        functools.partial(_combine_kernel, tm=tm, alpha=alpha),
        grid_spec=pltpu.PrefetchScalarGridSpec(
            num_scalar_prefetch=3,
            grid=(t // tm,),
            in_specs=[pl.BlockSpec((tm, dm), lambda i, *_: (i, 0)),
                      pl.BlockSpec((tm, SLOT_PAD), lambda i, *_: (i, 0)),
                      pl.BlockSpec((tm, SLOT_PAD), lambda i, *_: (i, 0)),
                      const(w_sh_gu), const(w_sh_down), const(ln_g), const(ln_b),
                      pl.BlockSpec(memory_space=pl.ANY)],
            out_specs=pl.BlockSpec((tm, dm), lambda i, *_: (i, 0)),
            scratch_shapes=[pltpu.VMEM((_local_rows(tm), dm), MXU_DTYPE), pltpu.SemaphoreType.DMA(())],
        ),
        out_shape=jax.ShapeDtypeStruct((t, dm), jnp.float32),
        compiler_params=_params("arbitrary"),
        name="moe_combine",
    )(seg_cnt, seg_off, seg_dst, x1, gate, pos, w_sh_gu, w_sh_down, ln_g, ln_b, ys)


def _moe(x1, gate, pos, stats, counts, w_exp_gu, w_exp_down, w_sh_gu, w_sh_down, ln_g, ln_b, alpha):
    t = x1.shape[0]
    bm = EXPERT_ROWS
    n_tiles = stats.shape[0]
    n_rows = (-(-(t * MOE_TOPK + n_tiles * N_EXPERTS * (SEG_ALIGN - 1)) // bm) + N_EXPERTS) * bm
    cnt = counts.reshape(N_EXPERTS).astype(jnp.int32)
    padded = (cnt + bm - 1) // bm * bm
    end = jnp.cumsum(padded)
    start = end - padded
    n_used = end[-1:] // bm
    zlo = jnp.concatenate([(start + cnt) // SEG_ALIGN, n_used])
    zhi = jnp.concatenate([end // SEG_ALIGN, jnp.full((1,), n_rows // bm, jnp.int32)])
    block_start = jnp.arange(n_rows // bm, dtype=jnp.int32) * bm
    block_expert = jnp.minimum(jnp.sum((end[None, :] <= block_start[:, None]).astype(jnp.int32), axis=1),
                               N_EXPERTS - 1)
    seg = stats.astype(jnp.int32)
    seg_cnt, seg_off = seg[:, 1, :].reshape(-1), seg[:, 2, :].reshape(-1)
    seg_dst = (start[None, :] + seg[:, 0, :]).reshape(-1)
    pos_t = jnp.transpose(pos.reshape(n_tiles, t // n_tiles, SLOT_PAD), (0, 2, 1))
    xs = _dispatch(x1, pos_t, seg_cnt, seg_off, seg_dst, zlo, zhi, n_rows)
    ys = _experts(xs, block_expert, n_used, w_exp_gu, w_exp_down)
    return _combine(x1, gate, pos, seg_cnt, seg_off, seg_dst, ys, w_sh_gu, w_sh_down, ln_g, ln_b, alpha)


def _layer(x, pos0, past, p, alpha):
    (w_in, a_re, a_im, log_dt, b_re, b_im, c_re, c_im, ssm_d, w_glu, w_attn_proj, w_out, ln1_g, ln1_b,
     w_router, router_bias, w_exp_gu, w_exp_down, w_sh_gu, w_sh_down, ln2_g, ln2_b) = p
    nb, seq, dm = x.shape
    assert pos0 % CHUNK == 0 and seq % Q_BLOCK == 0 and Q_BLOCK == CHUNK and seq % S5_STEPS == 0
    assert nb * seq * MOE_TOPK < 2 ** 24
    t = nb * seq
    f32 = jnp.float32
    x2 = x.reshape(t, dm)
    pos = pos0 + jnp.arange(seq, dtype=jnp.int32)

    act_dtype = MXU_DTYPE if nb % 16 == 0 else f32
    u, q, iq, k, v, ik2, ik, iw = _in_proj(x2, _pack_w_in(w_in), pos, seq, act_dtype)

    a_cat, bcat, ccat = _s5_discretise(a_re, a_im, log_dt, b_re, b_im, c_re, c_im)
    if past is None:
        h0 = jnp.zeros((nb, 2 * NS), f32)
    else:
        h0 = jnp.concatenate([past[3].reshape(nb, NS), past[4].reshape(nb, NS)], axis=1).astype(f32)
    u_t = jnp.transpose(u.reshape(nb, seq, D_SSM), (1, 0, 2))
    ssm_t, h_t = _s5(u_t, h0, a_cat, bcat, ccat, ssm_d.reshape(1, D_SSM).astype(f32), w_glu.astype(MXU_DTYPE))
    ssm2 = jnp.transpose(ssm_t, (1, 0, 2)).reshape(t, dm)
    h_re = h_t[:, :NS].reshape(nb, N_GROUPS, N_STATE)
    h_im = h_t[:, NS:].reshape(nb, N_GROUPS, N_STATE)

    k3, v3, ik23 = (a.reshape(nb, seq, LANES) for a in (k, v, ik2))
    if past is not None:
        ck, cv, cik = past[0], past[1], past[2]
        n_past = ck.shape[1]
        k3 = jnp.concatenate([ck.reshape(nb, n_past, LANES), k3], axis=1)
        v3 = jnp.concatenate([cv.reshape(nb, n_past, LANES), v3], axis=1)
        ik23 = jnp.concatenate([jnp.concatenate([cik, cik], axis=-1), ik23], axis=1)
    attn = _dsa(q.reshape(nb, seq, D_ATTN), iq.reshape(nb, seq, IDX_HEADS * IDX_DIM), iw.reshape(nb, seq, LANES),
                k3, v3, ik23, pos0 // CHUNK)

    wap = jnp.transpose(w_attn_proj.reshape(N_KV_HEADS, KV_GROUP, HEAD_DIM, dm), (1, 0, 2, 3)).reshape(D_ATTN, dm)
    wg = w_in[:, w_in.shape[1] - 2 * dm:]
    row = lambda a: a.reshape(1, -1).astype(f32)
    x1, gate, pos_slot, stats, counts = _mix(
        x2, attn.reshape(t, D_ATTN), ssm2, wg.astype(MXU_DTYPE), wap.astype(MXU_DTYPE), w_out.astype(MXU_DTYPE),
        row(ln1_g), row(ln1_b), w_router.astype(f32), row(router_bias), alpha)

    y = _moe(x1, gate, pos_slot, stats, counts, w_exp_gu, w_exp_down,
             w_sh_gu.astype(MXU_DTYPE), w_sh_down.astype(MXU_DTYPE), row(ln2_g), row(ln2_b), alpha)
    state = (k.reshape(nb, seq, N_KV_HEADS, HEAD_DIM), v.reshape(nb, seq, N_KV_HEADS, HEAD_DIM),
             ik.reshape(nb, seq, IDX_DIM), h_re, h_im)
    return y.reshape(nb, seq, dm), state


def kernel(x_prompt, x_sample, cache_k, cache_v, cache_idx_k, state_ssm_re, state_ssm_im, w_in, ssm_a_re, ssm_a_im, ssm_log_dt, ssm_b_re, ssm_b_im, ssm_c_re, ssm_c_im, ssm_d, w_glu, w_attn_proj, w_out, ln1_g, ln1_b, w_router, router_bias, w_exp_gu, w_exp_down, w_sh_gu, w_sh_down, ln2_g, ln2_b):
    weights = (w_in, ssm_a_re, ssm_a_im, ssm_log_dt, ssm_b_re, ssm_b_im, ssm_c_re, ssm_c_im, ssm_d,
               w_glu, w_attn_proj, w_out, ln1_g, ln1_b, w_router, router_bias,
               w_exp_gu, w_exp_down, w_sh_gu, w_sh_down, ln2_g, ln2_b)
    depth = w_in.shape[0]
    alpha = (2 * depth) ** 0.25
    past_len = cache_k.shape[2]
    y_p, y_s = x_prompt, x_sample
    new_p, new_s = [], []
    for l in range(depth):
        p_l = tuple(w[l] for w in weights)
        y_p, st_p = _layer(y_p, 0, None, p_l, alpha)
        y_s, st_s = _layer(y_s, past_len,
                           (cache_k[l], cache_v[l], cache_idx_k[l], state_ssm_re[l], state_ssm_im[l]), p_l, alpha)
        new_p.append(st_p)
        new_s.append(st_s)
    k_p, v_p, ik_p, hre_p, him_p = [jnp.stack(a) for a in zip(*new_p)]
    k_s, v_s, ik_s, hre_s, him_s = [jnp.stack(a) for a in zip(*new_s)]
    return (y_p, y_s, k_p, v_p, ik_p, hre_p, him_p, k_s, v_s, ik_s, hre_s, him_s)
```

```python
import functools

import jax
import jax.numpy as jnp
from jax import lax
from jax.experimental import pallas as pl
from jax.experimental.pallas import tpu as pltpu

CHUNK = 64
D_SSM = 512
SSM_GROUP = 16
N_GROUPS = D_SSM // SSM_GROUP
N_STATE = 64
N_HEADS = 8
N_KV_HEADS = 2
HEAD_DIM = 64
D_ATTN = N_HEADS * HEAD_DIM
KV_GROUP = N_HEADS // N_KV_HEADS
ROPE_DIM = HEAD_DIM // 4
ROPE_THETA = 500000.0
IDX_HEADS = 8
IDX_DIM = 64
TOPK_KEYS = 256
Q_BLOCK = 64
N_EXPERTS = 64
MOE_TOPK = 6
D_EXPERT = 256
D_SHARED = 256
ROUTED_SCALE = 2.5
LN_EPS = 1e-5

LANES = 128
MXU_DIM = 256
KEY_CHUNK = 256
DSA_ROWS = 256
EXPERT_ROWS = 512
MOE_TILE = 512
SEG_ALIGN = 16
SEG_TYPICAL = 64
SLOT_PAD = 8
NS = N_GROUPS * N_STATE
VMEM_LIMIT = 56 * 1024 * 1024
NEG_BIG = -1e30
INT_MIN = -2147483648

MXU_DTYPE = jnp.bfloat16


def _dot(a, b):
    return jnp.dot(a.astype(MXU_DTYPE), b.astype(MXU_DTYPE), preferred_element_type=jnp.float32)


def _dot_nt(a, b):
    return lax.dot_general(a.astype(MXU_DTYPE), b.astype(MXU_DTYPE), (((1,), (1,)), ((), ())),
                           preferred_element_type=jnp.float32)


def _params(*sem):
    return pltpu.CompilerParams(dimension_semantics=sem, vmem_limit_bytes=VMEM_LIMIT)


def _const_spec(shape):
    return pl.BlockSpec(shape, lambda *_: (0,) * len(shape), pipeline_mode=pl.Buffered(1))


_C_U, _C_Q, _C_IQ, _C_K, _C_V, _C_IK, _C_IW, _C_END = 0, 512, 1024, 1536, 1664, 1792, 1920, 2048


def _inproj_kernel(x_ref, w_ref, cos_ref, sin_ref,
                   u_ref, q_ref, iq_ref, k_ref, v_ref, ik2_ref, ik_ref, iw_ref):
    z = _dot(x_ref[...], w_ref[...])
    cos = cos_ref[...]
    sin = sin_ref[...]
    lane = lax.broadcasted_iota(jnp.int32, cos.shape, 1)
    first_half = (lane % HEAD_DIM) < (ROPE_DIM // 2)

    def rope(zc):
        partner = jnp.where(first_half, pltpu.roll(zc, LANES - ROPE_DIM // 2, 1), pltpu.roll(zc, ROPE_DIM // 2, 1))
        return zc * cos + partner * sin

    u_ref[...] = z[:, _C_U:_C_Q].astype(u_ref.dtype)
    for c in range(4):
        q_ref[:, c * LANES:(c + 1) * LANES] = rope(z[:, _C_Q + c * LANES:_C_Q + (c + 1) * LANES])
        iq_ref[:, c * LANES:(c + 1) * LANES] = rope(z[:, _C_IQ + c * LANES:_C_IQ + (c + 1) * LANES])
    k_ref[...] = rope(z[:, _C_K:_C_V])
    v_ref[...] = z[:, _C_V:_C_IK]
    ik2 = rope(z[:, _C_IK:_C_IW])
    ik2_ref[...] = ik2
    ik_ref[...] = ik2[:, :IDX_DIM]
    iw_ref[...] = z[:, _C_IW:_C_END]


def _rope_tables(pos):
    half = ROPE_DIM // 2
    inv_freq = ROPE_THETA ** (-jnp.arange(half, dtype=jnp.float32) * 2.0 / ROPE_DIM)
    ang = pos.astype(jnp.float32)[:, None] * inv_freq
    cos, sin = jnp.cos(ang), jnp.sin(ang)
    n = pos.shape[0]
    pad = HEAD_DIM - ROPE_DIM
    cos_h = jnp.concatenate([cos, cos, jnp.ones((n, pad), jnp.float32)], axis=-1)
    sin_h = jnp.concatenate([-sin, sin, jnp.zeros((n, pad), jnp.float32)], axis=-1)
    return jnp.tile(cos_h, (1, LANES // HEAD_DIM)), jnp.tile(sin_h, (1, LANES // HEAD_DIM))


def _pack_w_in(w_in):
    s = [0, 512, 1024, 1152, 1280, 1792, 1856, 1864]
    w_u, w_q, w_k, w_v = w_in[:, s[0]:s[1]], w_in[:, s[1]:s[2]], w_in[:, s[2]:s[3]], w_in[:, s[3]:s[4]]
    w_iq, w_ik, w_iw = w_in[:, s[4]:s[5]], w_in[:, s[5]:s[6]], w_in[:, s[6]:s[7]]
    d = w_in.shape[0]
    w_qh = w_q.reshape(d, N_KV_HEADS, KV_GROUP, HEAD_DIM)
    w_qp = jnp.transpose(w_qh, (0, 2, 1, 3)).reshape(d, D_ATTN)
    w_iwp = jnp.concatenate([w_iw, jnp.zeros((d, LANES - IDX_HEADS), w_in.dtype)], axis=1)
    return jnp.concatenate([w_u, w_qp, w_iq, w_k, w_v, w_ik, w_ik, w_iwp], axis=1).astype(MXU_DTYPE)


def _in_proj(x2, w_pack, pos, seq, u_dtype):
    t, d = x2.shape
    tm = min(512, t)
    cos, sin = _rope_tables(pos)
    if seq >= tm:
        per = seq // tm
        tab_map = lambda i: (i % per, 0)
    else:
        cos, sin = jnp.tile(cos, (tm // seq, 1)), jnp.tile(sin, (tm // seq, 1))
        tab_map = lambda i: (0, 0)
    row = lambda w: pl.BlockSpec((tm, w), lambda i: (i, 0))
    widths = (D_SSM, D_ATTN, IDX_HEADS * IDX_DIM, LANES, LANES, LANES, IDX_DIM, LANES)
    return pl.pallas_call(
        _inproj_kernel,
        grid=(t // tm,),
        in_specs=[row(d), _const_spec(w_pack.shape),
                  pl.BlockSpec((tm, LANES), tab_map), pl.BlockSpec((tm, LANES), tab_map)],
        out_specs=[row(w) for w in widths],
        out_shape=[jax.ShapeDtypeStruct((t, widths[0]), u_dtype)]
        + [jax.ShapeDtypeStruct((t, w), jnp.float32) for w in widths[1:]],
        compiler_params=_params("parallel"),
        name="in_proj",
    )(x2, w_pack, cos, sin)


S5_STEPS = 16
S5_COLS = 512


def _s5_kernel(u_ref, h0_ref, a_ref, bcat_ref, ccat_ref, d_ref, wglu_ref,
               out_ref, ht_ref, h_scr, bu_scr, hall_scr, *, nb):
    @pl.when(pl.program_id(0) == 0)
    def _():
        h_scr[...] = h0_ref[...]

    u = u_ref[...].reshape(S5_STEPS * nb, D_SSM)
    per_tile = MXU_DIM // N_STATE
    for j in range(2 * NS // MXU_DIM):
        kb = ((j % (NS // MXU_DIM)) * per_tile * SSM_GROUP) // MXU_DIM * MXU_DIM
        cols = slice(j * MXU_DIM, (j + 1) * MXU_DIM)
        bu_scr[:, cols] = _dot(u[:, kb:kb + MXU_DIM], bcat_ref[kb:kb + MXU_DIM, cols])
    for cb in range(NS // S5_COLS):
        re = slice(cb * S5_COLS, (cb + 1) * S5_COLS)
        im = slice(NS + cb * S5_COLS, NS + (cb + 1) * S5_COLS)
        a_re, a_im = a_ref[0:1, re], a_ref[1:2, re]
        h_re, h_im = h_scr[:, re], h_scr[:, im]
        for t in range(S5_STEPS):
            rows = slice(t * nb, (t + 1) * nb)
            h_re, h_im = (a_re * h_re - a_im * h_im + bu_scr[rows, re],
                          a_re * h_im + a_im * h_re + bu_scr[rows, im])
            hall_scr[rows, re] = h_re
            hall_scr[rows, im] = h_im
        h_scr[:, re] = h_re
        h_scr[:, im] = h_im
    y_tiles = []
    for n in range(D_SSM // MXU_DIM):
        cols = slice(n * MXU_DIM, (n + 1) * MXU_DIM)
        k0, kn = n * MXU_DIM // SSM_GROUP * N_STATE, MXU_DIM // SSM_GROUP * N_STATE
        y_tiles.append(_dot(hall_scr[:, k0:k0 + kn], ccat_ref[k0:k0 + kn, cols])
                       + _dot(hall_scr[:, NS + k0:NS + k0 + kn], ccat_ref[NS + k0:NS + k0 + kn, cols]))
    y = jnp.concatenate(y_tiles, axis=1) + d_ref[...] * u
    y = jax.nn.gelu(y)
    g = _dot(y, wglu_ref[...])
    dm = g.shape[1] // 2
    out_ref[...] = (g[:, :dm] * jax.nn.sigmoid(g[:, dm:])).astype(out_ref.dtype).reshape(S5_STEPS, nb, dm)
    ht_ref[...] = h_scr[...]


def _s5_discretise(a_re, a_im, log_dt, b_re, b_im, c_re, c_im):
    f32 = jnp.float32
    ar, ai = a_re.astype(f32), a_im.astype(f32)
    dt = jnp.exp(log_dt.astype(f32))[:, None]
    mag = jnp.exp(dt * ar)
    abar_re, abar_im = mag * jnp.cos(dt * ai), mag * jnp.sin(dt * ai)
    den = ar * ar + ai * ai
    num_re, num_im = abar_re - 1.0, abar_im
    coef_re = (num_re * ar + num_im * ai) / den
    coef_im = (num_im * ar - num_re * ai) / den
    br, bi = b_re.astype(f32), b_im.astype(f32)
    bbar_re = coef_re[..., None] * br - coef_im[..., None] * bi
    bbar_im = coef_re[..., None] * bi + coef_im[..., None] * br
    eye = jnp.eye(N_GROUPS, dtype=f32)
    bd = lambda m: jnp.einsum("gnc,gh->gchn", m, eye).reshape(D_SSM, NS)
    bcat = jnp.concatenate([bd(bbar_re), bd(bbar_im)], axis=1)
    cd = lambda m: jnp.einsum("gcn,gh->gnhc", m, eye).reshape(NS, D_SSM)
    ccat = jnp.concatenate([cd(c_re.astype(f32)), cd(-c_im.astype(f32))], axis=0)
    a_cat = jnp.stack([abar_re.reshape(NS), abar_im.reshape(NS)])
    return a_cat, bcat.astype(MXU_DTYPE), ccat.astype(MXU_DTYPE)


def _s5(u_t, h0, a_cat, bcat, ccat, d_skip, w_glu):
    seq, nb, _ = u_t.shape
    dm = w_glu.shape[1] // 2
    rows = S5_STEPS * nb
    return pl.pallas_call(
        functools.partial(_s5_kernel, nb=nb),
        grid=(seq // S5_STEPS,),
        in_specs=[pl.BlockSpec((S5_STEPS, nb, D_SSM), lambda i: (i, 0, 0)),
                  _const_spec(h0.shape), _const_spec(a_cat.shape), _const_spec(bcat.shape),
                  _const_spec(ccat.shape), _const_spec(d_skip.shape), _const_spec(w_glu.shape)],
        out_specs=[pl.BlockSpec((S5_STEPS, nb, dm), lambda i: (i, 0, 0)),
                   pl.BlockSpec(h0.shape, lambda i: (0, 0))],
        out_shape=[jax.ShapeDtypeStruct((seq, nb, dm), u_t.dtype),
                   jax.ShapeDtypeStruct(h0.shape, jnp.float32)],
        scratch_shapes=[pltpu.VMEM(h0.shape, jnp.float32),
                        pltpu.VMEM((rows, 2 * NS), jnp.float32),
                        pltpu.VMEM((rows, 2 * NS), jnp.float32)],
        compiler_params=_params("arbitrary"),
        name="s5_scan_glu",
    )(u_t, h0, a_cat, bcat, ccat, d_skip, w_glu)


def _sortable(x):
    bits = pltpu.bitcast(x, jnp.int32)
    key = bits ^ ((bits >> 31) & jnp.int32(0x7FFFFFFF))
    return jnp.where(x == 0.0, jnp.int32(0), key)


def _half_mask(shape, upper):
    lane = lax.broadcasted_iota(jnp.int32, shape, 1)
    return (lane >= HEAD_DIM) if upper else (lane < HEAD_DIM)


def _dsa_kernel(q_ref, iq_ref, iw_ref, k_ref, v_ref, ik_ref, o_ref,
                kb, vs0, vs1, ikb, key_scr, keyt_scr, bias_scr, wb_scr, iqlhs_scr, qlhs_scr, s_scr, m_scr, acc_scr,
                *, qt, n_keys, n_sel, first_chunk, idx_bits):
    j = pl.program_id(1)
    kc = KEY_CHUNK
    n_sub = qt // Q_BLOCK
    qtp = max(qt, LANES)

    @pl.when(j == 0)
    def _():
        v = v_ref[...]
        lower = lax.broadcasted_iota(jnp.int32, v.shape, v.ndim - 1) < HEAD_DIM
        kb[...] = k_ref[...].astype(MXU_DTYPE)
        vs0[...] = jnp.where(lower, v, 1.0).astype(MXU_DTYPE)
        vs1[...] = jnp.where(lower, 1.0, v).astype(MXU_DTYPE)
        ikb[...] = ik_ref[...].astype(MXU_DTYPE)

    row_blk = lax.broadcasted_iota(jnp.int32, (qt, 1), 0) // Q_BLOCK
    n_vis = jnp.minimum((first_chunk + j * n_sub + row_blk + 1) * CHUNK, n_keys)
    n_ck = (jnp.minimum((first_chunk + (j + 1) * n_sub) * CHUNK, n_keys) + kc - 1) // kc
    lane_k = lax.broadcasted_iota(jnp.int32, (qt, kc), 1)

    iq = iq_ref[...] * IDX_DIM ** -0.5
    iw = iw_ref[...] * IDX_HEADS ** -0.5
    for h in range(IDX_HEADS):
        iqlhs_scr[h] = jnp.where(_half_mask((qt, LANES), h % 2 == 1),
                                 iq[:, (h // 2) * LANES:(h // 2 + 1) * LANES], 0.0).astype(MXU_DTYPE)
        wb_scr[h] = jnp.broadcast_to(iw[:, h:h + 1], (qt, LANES))

    def score_chunk(c, carry):
        ikc = ikb[c]
        score = None
        for h in range(IDX_HEADS):
            rel = jnp.maximum(_dot_nt(iqlhs_scr[h], ikc), 0.0)
            w = wb_scr[h]
            term = jnp.concatenate([rel[:, s * LANES:(s + 1) * LANES] * w for s in range(kc // LANES)], axis=1)
            score = term if score is None else score + term
        key = jnp.where(c * kc + lane_k < n_vis, _sortable(score), jnp.int32(INT_MIN))
        key_scr[c] = key
        if qtp > qt:
            key = jnp.concatenate([key, jnp.full((qtp - qt, kc), INT_MIN, jnp.int32)], axis=0)
        keyt_scr[c] = key.T
        return carry

    lax.fori_loop(0, n_ck, score_chunk, 0)

    sub_k = lax.broadcasted_iota(jnp.int32, (kc // 8, 8, qtp), 0) * 8 + lax.broadcasted_iota(
        jnp.int32, (kc // 8, 8, qtp), 1)

    def count(pred):
        def body(c, acc):
            hit = pred(keyt_scr[c].reshape(kc // 8, 8, qtp), c * kc + sub_k)
            return acc + jnp.sum(jnp.where(hit, 1.0, 0.0), axis=0)
        acc = lax.fori_loop(0, n_ck, body, jnp.zeros((8, qtp), jnp.float32))
        for shift in (4, 2, 1):
            acc = acc + pltpu.roll(acc, shift, 0)
        return acc

    def value_bit(i, t):
        cand = t + (jnp.int32(1) << (31 - i))
        return jnp.where(count(lambda key, idx: key >= cand[None]) >= n_sel, cand, t)

    t = lax.fori_loop(0, 32, value_bit, jnp.full((8, qtp), INT_MIN, jnp.int32))
    need = n_sel - count(lambda key, idx: key > t[None])
    n_ge = count(lambda key, idx: key >= t[None])

    def last_tied_index():
        def index_bit(i, m):
            cand = m + (jnp.int32(1) << (idx_bits - 1 - i))
            below = count(lambda key, idx: (key == t[None]) & (idx < cand[None]))
            return jnp.where(below < need, cand, m)
        return lax.fori_loop(0, idx_bits, index_bit, jnp.zeros((8, qtp), jnp.int32))

    m_idx = lax.cond(jnp.max(n_ge) > n_sel, last_tied_index,
                     lambda: jnp.full((8, qtp), 2 ** 30, jnp.int32))
    t_rep = jnp.broadcast_to(t[0:1], (LANES, qtp)).T[:qt]
    m_rep = jnp.broadcast_to(m_idx[0:1], (LANES, qtp)).T[:qt]
    t_row = jnp.concatenate([t_rep] * (kc // LANES), axis=1)
    m_row = jnp.concatenate([m_rep] * (kc // LANES), axis=1)

    def bias_chunk(c, carry):
        key = key_scr[c]
        idx = c * kc + lane_k
        sel = ((key > t_row) | ((key == t_row) & (idx <= m_row))) & (idx < n_vis)
        bias_scr[c] = jnp.where(sel, 0.0, NEG_BIG)
        return carry

    lax.fori_loop(0, n_ck, bias_chunk, 0)

    q = q_ref[...] * HEAD_DIM ** -0.5
    rows = KV_GROUP * qt
    for g in range(N_KV_HEADS):
        for r in range(KV_GROUP):
            qlhs_scr[g, r * qt:(r + 1) * qt] = jnp.where(
                _half_mask((qt, LANES), g == 1), q[:, r * LANES:(r + 1) * LANES], 0.0).astype(MXU_DTYPE)
    m_scr[...] = jnp.full(m_scr.shape, NEG_BIG, jnp.float32)
    acc_scr[...] = jnp.zeros(acc_scr.shape, jnp.float32)

    def logits_chunk(c, carry):
        bias = bias_scr[c][None]
        for g in range(N_KV_HEADS):
            s = (_dot_nt(qlhs_scr[g], kb[c]).reshape(KV_GROUP, qt, kc) + bias).reshape(rows, kc)
            s_scr[g, c] = s
            m = m_scr[g]
            for u in range(kc // LANES):
                m = jnp.maximum(m, s[:, u * LANES:(u + 1) * LANES])
            m_scr[g] = m
        return carry

    lax.fori_loop(0, n_ck, logits_chunk, 0)
    for g in range(N_KV_HEADS):
        m_scr[g] = jnp.broadcast_to(jnp.max(m_scr[g], axis=-1, keepdims=True), (rows, LANES))

    def pv_chunk(c, carry):
        for g, vs in enumerate((vs0, vs1)):
            m = m_scr[g]
            s = s_scr[g, c]
            p = jnp.concatenate([jnp.exp(s[:, u * LANES:(u + 1) * LANES] - m) for u in range(kc // LANES)], axis=1)
            acc_scr[g] += _dot(p, vs[c])
        return carry

    lax.fori_loop(0, n_ck, pv_chunk, 0)
    outs = [acc_scr[g] / pltpu.roll(acc_scr[g], HEAD_DIM, 1) for g in range(N_KV_HEADS)]
    lower = _half_mask((qt, LANES), False)
    for r in range(KV_GROUP):
        rs = slice(r * qt, (r + 1) * qt)
        o_ref[:, r * LANES:(r + 1) * LANES] = jnp.where(lower, outs[0][rs], outs[1][rs])


def _dsa(q, iq, iw, k_all, v_all, ik2_all, first_chunk):
    nb, seq, _ = q.shape
    n_keys = k_all.shape[1]
    n_sel = min(TOPK_KEYS, n_keys // 4)
    kc = KEY_CHUNK
    qt = min(DSA_ROWS, seq)
    assert seq % qt == 0
    n_ch = -(-n_keys // kc)
    pad = n_ch * kc - n_keys
    chunked = lambda a: jnp.pad(a, ((0, 0), (0, pad), (0, 0))).reshape(nb, n_ch, kc, LANES)
    qspec = lambda w: pl.BlockSpec((None, qt, w), lambda b, j: (b, j, 0))
    kspec = pl.BlockSpec((None, n_ch, kc, LANES), lambda b, j: (b, 0, 0, 0))
    rows = KV_GROUP * qt
    return pl.pallas_call(
        functools.partial(_dsa_kernel, qt=qt, n_keys=n_keys, n_sel=n_sel, first_chunk=first_chunk,
                          idx_bits=max(1, (n_ch * kc - 1).bit_length())),
        grid=(nb, seq // qt),
        in_specs=[qspec(D_ATTN), qspec(IDX_HEADS * IDX_DIM), qspec(LANES), kspec, kspec, kspec],
        out_specs=qspec(D_ATTN),
        out_shape=jax.ShapeDtypeStruct((nb, seq, D_ATTN), jnp.float32),
        scratch_shapes=[pltpu.VMEM((n_ch, kc, LANES), MXU_DTYPE)] * 4
        + [pltpu.VMEM((n_ch, qt, kc), jnp.int32), pltpu.VMEM((n_ch, kc, max(qt, LANES)), jnp.int32),
           pltpu.VMEM((n_ch, qt, kc), jnp.float32),
           pltpu.VMEM((IDX_HEADS, qt, LANES), jnp.float32), pltpu.VMEM((IDX_HEADS, qt, LANES), MXU_DTYPE),
           pltpu.VMEM((N_KV_HEADS, rows, LANES), MXU_DTYPE),
           pltpu.VMEM((N_KV_HEADS, n_ch, rows, kc), jnp.float32),
           pltpu.VMEM((N_KV_HEADS, rows, LANES), jnp.float32), pltpu.VMEM((N_KV_HEADS, rows, LANES), jnp.float32)],
        compiler_params=_params("arbitrary", "arbitrary"),
        name="dsa_attention",
    )(q, iq, iw, chunked(k_all), chunked(v_all), chunked(ik2_all))


def _layer_norm(h, g, b):
    mu = jnp.mean(h, axis=-1, keepdims=True)
    var = jnp.mean(jnp.square(h - mu), axis=-1, keepdims=True)
    return (h - mu) * lax.rsqrt(var + LN_EPS) * g + b


def _mix_kernel(x_ref, attn_ref, ssm_ref, wg_ref, wap_ref, wout_ref, g1_ref, b1_ref, wr_ref, rb_ref,
                x1_ref, gate_ref, pos_ref, stats_ref, cnt_ref, cnt_scr, *, alpha):
    i = pl.program_id(0)

    @pl.when(i == 0)
    def _():
        cnt_scr[...] = jnp.zeros_like(cnt_scr)

    tm, dm = x_ref.shape
    n_split = 2 if tm % 16 == 0 else 1
    hm = tm // n_split
    lane = lax.broadcasted_iota(jnp.int32, (hm, N_EXPERTS), 1).astype(jnp.float32)
    slot = lax.broadcasted_iota(jnp.int32, (hm, SLOT_PAD), 1)

    def route_rows(rows):
        x = x_ref[rows, :]
        gates = jax.nn.sigmoid(_dot(x, wg_ref[...]))
        attn_out = _dot(attn_ref[rows, :], wap_ref[...])
        mixed = _dot(gates[:, :dm] * ssm_ref[rows, :] + gates[:, dm:] * attn_out, wout_ref[...])
        x1 = _layer_norm(alpha * x + mixed, g1_ref[...], b1_ref[...])
        x1_ref[rows, :] = x1
        logits = jnp.dot(x1, wr_ref[...], preferred_element_type=jnp.float32, precision=lax.Precision.HIGHEST)
        scores = jax.nn.sigmoid(logits)
        cur = scores + rb_ref[...]
        chosen = jnp.zeros_like(scores)
        picks = []
        for _ in range(MOE_TOPK):
            best = jnp.max(cur, axis=-1, keepdims=True)
            e_k = jnp.min(jnp.where(cur == best, lane, float(N_EXPERTS)), axis=-1, keepdims=True)
            hot = lane == e_k
            picks.append((hot, jnp.sum(jnp.where(hot, scores, 0.0), axis=-1, keepdims=True)))
            chosen = jnp.where(hot, 1.0, chosen)
            cur = jnp.where(hot, -jnp.inf, cur)
        total = picks[0][1]
        for _, s_k in picks[1:]:
            total = total + s_k
        return chosen, picks, total

    halves = [route_rows(slice(h * hm, (h + 1) * hm)) for h in range(n_split)]
    chosen = jnp.concatenate([h[0] for h in halves], axis=0)

    row = lax.broadcasted_iota(jnp.int32, (tm, tm), 0)
    col = lax.broadcasted_iota(jnp.int32, (tm, tm), 1)
    before = _dot(jnp.where(col < row, 1.0, 0.0), chosen)
    tile_cnt = jnp.ceil(jnp.sum(chosen, axis=0, keepdims=True) / SEG_ALIGN) * SEG_ALIGN
    e_row = lax.broadcasted_iota(jnp.int32, (N_EXPERTS, N_EXPERTS), 0)
    e_col = lax.broadcasted_iota(jnp.int32, (N_EXPERTS, N_EXPERTS), 1)
    tile_off = jnp.dot(jnp.broadcast_to(tile_cnt, (8, N_EXPERTS)), jnp.where(e_row < e_col, 1.0, 0.0),
                       preferred_element_type=jnp.float32, precision=lax.Precision.HIGHEST)[0:1]
    local = before + tile_off
    for h, (_, picks, total) in enumerate(halves):
        rows = slice(h * hm, (h + 1) * hm)
        g_out = jnp.zeros((hm, SLOT_PAD), jnp.float32)
        p_out = jnp.zeros((hm, SLOT_PAD), jnp.int32)
        for k, (hot, s_k) in enumerate(picks):
            pos_k = jnp.sum(jnp.where(hot, local[rows], 0.0), axis=-1, keepdims=True)
            g_out = jnp.where(slot == k, s_k / total * ROUTED_SCALE, g_out)
            p_out = jnp.where(slot == k, pos_k.astype(jnp.int32), p_out)
        gate_ref[rows, :] = g_out
        pos_ref[rows, :] = p_out
    srow = lax.broadcasted_iota(jnp.int32, (8, N_EXPERTS), 0)
    stats_ref[...] = jnp.where(srow == 0, cnt_scr[...], jnp.where(srow == 1, tile_cnt, jnp.where(
        srow == 2, tile_off, 0.0)))
    cnt_scr[...] = cnt_scr[...] + tile_cnt
    cnt_ref[...] = cnt_scr[...]


def _mix(x2, attn2, ssm2, wg, wap, wout, ln_g, ln_b, w_router, router_bias, alpha):
    t, dm = x2.shape
    tm = min(MOE_TILE, t)
    row = lambda w: pl.BlockSpec((tm, w), lambda i: (i, 0))
    consts = (wg, wap, wout, ln_g, ln_b, w_router, router_bias)
    return pl.pallas_call(
        functools.partial(_mix_kernel, alpha=alpha),
        grid=(t // tm,),
        in_specs=[row(dm), row(D_ATTN), row(dm)] + [_const_spec(c.shape) for c in consts],
        out_specs=[row(dm), row(SLOT_PAD), row(SLOT_PAD),
                   pl.BlockSpec((None, 8, N_EXPERTS), lambda i: (i, 0, 0)),
                   pl.BlockSpec((1, N_EXPERTS), lambda i: (0, 0))],
        out_shape=[jax.ShapeDtypeStruct((t, dm), jnp.float32),
                   jax.ShapeDtypeStruct((t, SLOT_PAD), jnp.float32),
                   jax.ShapeDtypeStruct((t, SLOT_PAD), jnp.int32),
                   jax.ShapeDtypeStruct((t // tm, 8, N_EXPERTS), jnp.float32),
                   jax.ShapeDtypeStruct((1, N_EXPERTS), jnp.float32)],
        scratch_shapes=[pltpu.VMEM((1, N_EXPERTS), jnp.float32)],
        compiler_params=_params("arbitrary"),
        name="mix_ln_router",
    )(x2, attn2, ssm2, *consts)


def _local_rows(tm):
    assert (N_EXPERTS * SEG_ALIGN) % tm == 0
    return MOE_TOPK * tm + N_EXPERTS * SEG_ALIGN


def _segment_pieces(i, cnt_ref, off_ref, dst_ref, tm, visit):
    def segment(e, carry):
        n = cnt_ref[i * N_EXPERTS + e]
        a = off_ref[i * N_EXPERTS + e]
        b = dst_ref[i * N_EXPERTS + e]
        def pieces(sizes):
            for p in sizes:
                done = (n // (2 * p)) * (2 * p)

                @pl.when((n & p) != 0)
                def _(p=p, done=done):
                    visit(pl.multiple_of(a + done, SEG_ALIGN), pl.multiple_of(b + done, SEG_ALIGN), p)

        sizes = [tm >> s for s in range(tm.bit_length()) if (tm >> s) >= SEG_ALIGN]
        rare = [p for p in sizes if p > SEG_TYPICAL]

        @pl.when(n > SEG_TYPICAL)
        def _():
            pieces(rare)

        pieces([p for p in sizes if p <= SEG_TYPICAL])
        return carry

    lax.fori_loop(0, N_EXPERTS, segment, 0)


def _dispatch_kernel(cnt_ref, off_ref, dst_ref, zlo_ref, zhi_ref, post_ref, x_ref, xs_ref,
                     loc, zero_scr, sem, *, tm):
    i = pl.program_id(0)

    xb = x_ref[...].astype(MXU_DTYPE)
    post = post_ref[...]
    last = i * N_EXPERTS + N_EXPERTS - 1
    used = off_ref[last] + cnt_ref[last]
    def sort_chunk(sc):
        slot = sc * tm + lax.broadcasted_iota(jnp.int32, (tm, tm), 0)
        onehot = jnp.zeros((tm, tm), jnp.float32)
        for k in range(MOE_TOPK):
            onehot = jnp.where(slot == post[k:k + 1, :], 1.0, onehot)
        loc[sc * tm:(sc + 1) * tm, :] = _dot(onehot, xb).astype(loc.dtype)

    n_chunks = _local_rows(tm) // tm
    for sc in range(n_chunks - 1):
        sort_chunk(sc)
    pl.when((n_chunks - 1) * tm < used)(lambda: sort_chunk(n_chunks - 1))

    def piece(a, b, p):
        return pltpu.make_async_copy(loc.at[pl.ds(a, p)], xs_ref.at[pl.ds(b, p)], sem)

    _segment_pieces(i, cnt_ref, off_ref, dst_ref, tm, lambda a, b, p: piece(a, b, p).start())
    _segment_pieces(i, cnt_ref, off_ref, dst_ref, tm, lambda a, b, p: piece(a, b, p).wait())

    @pl.when(i == pl.num_programs(0) - 1)
    def _():
        zero_scr[...] = jnp.zeros_like(zero_scr)

        def zero_copy(group, rows):
            return pltpu.make_async_copy(
                zero_scr.at[pl.ds(0, rows)], xs_ref.at[pl.ds(pl.multiple_of(group * rows, rows), rows)], sem)

        def fill_groups(lo, hi, rows):
            def fill(g, c):
                zero_copy(g, rows).start()
                return c

            def fill_done(g, c):
                zero_copy(g, rows).wait()
                return c

            lax.fori_loop(lo, hi, fill, 0)
            lax.fori_loop(lo, hi, fill_done, 0)

        def segment(s, carry):
            fill_groups(zlo_ref[s], zhi_ref[s], SEG_ALIGN)
            return carry

        lax.fori_loop(0, N_EXPERTS, segment, 0)
        fill_groups(zlo_ref[N_EXPERTS], zhi_ref[N_EXPERTS], EXPERT_ROWS)


def _dispatch(x1, pos_t, seg_cnt, seg_off, seg_dst, zlo, zhi, n_rows):
    t, dm = x1.shape
    tm = min(MOE_TILE, t)
    return pl.pallas_call(
        functools.partial(_dispatch_kernel, tm=tm),
        grid_spec=pltpu.PrefetchScalarGridSpec(
            num_scalar_prefetch=5,
            grid=(t // tm,),
            in_specs=[pl.BlockSpec((None, SLOT_PAD, tm), lambda i, *_: (i, 0, 0)),
                      pl.BlockSpec((tm, dm), lambda i, *_: (i, 0))],
            out_specs=pl.BlockSpec(memory_space=pl.ANY),
            scratch_shapes=[pltpu.VMEM((_local_rows(tm), dm), MXU_DTYPE), pltpu.VMEM((EXPERT_ROWS, dm), MXU_DTYPE),
                            pltpu.SemaphoreType.DMA(())],
        ),
        out_shape=jax.ShapeDtypeStruct((n_rows, dm), MXU_DTYPE),
        compiler_params=_params("arbitrary"),
        name="moe_dispatch",
    )(seg_cnt, seg_off, seg_dst, zlo, zhi, pos_t, x1)


def _expert_kernel(be_ref, nu_ref, xs_ref, wgu_ref, wdn_ref, ys_ref, wgu_scr, wdn_scr):
    i = pl.program_id(0)

    @pl.when(i < nu_ref[0])
    def _():
        @pl.when((i == 0) | (be_ref[i] != be_ref[jnp.maximum(i - 1, 0)]))
        def _():
            wgu_scr[...] = wgu_ref[...].astype(MXU_DTYPE)
            wdn_scr[...] = wdn_ref[...].astype(MXU_DTYPE)

        h = _dot(xs_ref[...], wgu_scr[...])
        f = h.shape[1] // 2
        ys_ref[...] = _dot(jax.nn.silu(h[:, :f]) * h[:, f:], wdn_scr[...]).astype(ys_ref.dtype)

    @pl.when(i >= nu_ref[0])
    def _():
        ys_ref[...] = jnp.zeros_like(ys_ref)


def _experts(xs, block_expert, n_used, w_gu, w_down):
    n_rows, dm = xs.shape
    bm = EXPERT_ROWS
    f2 = w_gu.shape[2]
    used = lambda i, nu: jnp.minimum(i, nu[0] - 1)
    return pl.pallas_call(
        _expert_kernel,
        grid_spec=pltpu.PrefetchScalarGridSpec(
            num_scalar_prefetch=2,
            grid=(n_rows // bm,),
            in_specs=[pl.BlockSpec((bm, dm), lambda i, be, nu: (used(i, nu), 0)),
                      pl.BlockSpec((None, dm, f2), lambda i, be, nu: (be[used(i, nu)], 0, 0)),
                      pl.BlockSpec((None, f2 // 2, dm), lambda i, be, nu: (be[used(i, nu)], 0, 0))],
            out_specs=pl.BlockSpec((bm, dm), lambda i, be, nu: (i, 0)),
            scratch_shapes=[pltpu.VMEM((dm, f2), MXU_DTYPE), pltpu.VMEM((f2 // 2, dm), MXU_DTYPE)],
        ),
        out_shape=jax.ShapeDtypeStruct((n_rows, dm), MXU_DTYPE),
        compiler_params=_params("arbitrary"),
        name="moe_experts",
    )(block_expert, n_used, xs, w_gu, w_down)


def _combine_kernel(cnt_ref, off_ref, dst_ref, x1_ref, gate_ref, pos_ref, wsgu_ref, wsdn_ref, g2_ref, b2_ref,
                    ys_ref, y_ref, loc, sem, *, tm, alpha):
    i = pl.program_id(0)

    @pl.when(i == 0)
    def _():
        loc[...] = jnp.zeros_like(loc)

    def piece(a, b, p):
        return pltpu.make_async_copy(ys_ref.at[pl.ds(b, p)], loc.at[pl.ds(a, p)], sem)

    _segment_pieces(i, cnt_ref, off_ref, dst_ref, tm, lambda a, b, p: piece(a, b, p).start())

    x1 = x1_ref[...]
    h = _dot(x1, wsgu_ref[...])
    f = h.shape[1] // 2
    shared = _dot(jax.nn.silu(h[:, :f]) * h[:, f:], wsdn_ref[...])

    _segment_pieces(i, cnt_ref, off_ref, dst_ref, tm, lambda a, b, p: piece(a, b, p).wait())

    gate = gate_ref[...]
    pos = pos_ref[...]
    last = i * N_EXPERTS + N_EXPERTS - 1
    used = off_ref[last] + cnt_ref[last]
    def gather_chunk(sc):
        slot = sc * tm + lax.broadcasted_iota(jnp.int32, (tm, tm), 1)
        w = jnp.zeros((tm, tm), jnp.float32)
        for k in range(MOE_TOPK):
            w = jnp.where(slot == pos[:, k:k + 1], gate[:, k:k + 1], w)
        return _dot(w, loc[sc * tm:(sc + 1) * tm, :])

    n_chunks = _local_rows(tm) // tm
    routed = shared
    for sc in range(n_chunks - 1):
        routed = routed + gather_chunk(sc)
    y_ref[...] = routed

    @pl.when((n_chunks - 1) * tm < used)
    def _():
        y_ref[...] += gather_chunk(n_chunks - 1)

    y_ref[...] = _layer_norm(alpha * x1 + y_ref[...], g2_ref[...], b2_ref[...])


def _combine(x1, gate, pos, seg_cnt, seg_off, seg_dst, ys, w_sh_gu, w_sh_down, ln_g, ln_b, alpha):
    t, dm = x1.shape
    tm = min(MOE_TILE, t)
    const = lambda a: pl.BlockSpec(a.shape, lambda i, *_: (0,) * a.ndim, pipeline_mode=pl.Buffered(1))
    return pl.pallas_call(
        functools.partial(_combine_kernel, tm=tm, alpha=alpha),
        grid_spec=pltpu.PrefetchScalarGridSpec(
            num_scalar_prefetch=3,
            grid=(t // tm,),
            in_specs=[pl.BlockSpec((tm, dm), lambda i, *_: (i, 0)),
                      pl.BlockSpec((tm, SLOT_PAD), lambda i, *_: (i, 0)),
                      pl.BlockSpec((tm, SLOT_PAD), lambda i, *_: (i, 0)),
                      const(w_sh_gu), const(w_sh_down), const(ln_g), const(ln_b),
                      pl.BlockSpec(memory_space=pl.ANY)],
            out_specs=pl.BlockSpec((tm, dm), lambda i, *_: (i, 0)),
            scratch_shapes=[pltpu.VMEM((_local_rows(tm), dm), MXU_DTYPE), pltpu.SemaphoreType.DMA(())],
        ),
        out_shape=jax.ShapeDtypeStruct((t, dm), jnp.float32),
        compiler_params=_params("arbitrary"),
        name="moe_combine",
    )(seg_cnt, seg_off, seg_dst, x1, gate, pos, w_sh_gu, w_sh_down, ln_g, ln_b, ys)


def _moe(x1, gate, pos, stats, counts, w_exp_gu, w_exp_down, w_sh_gu, w_sh_down, ln_g, ln_b, alpha):
    t = x1.shape[0]
    bm = EXPERT_ROWS
    n_tiles = stats.shape[0]
    n_rows = (-(-(t * MOE_TOPK + n_tiles * N_EXPERTS * (SEG_ALIGN - 1)) // bm) + N_EXPERTS) * bm
    cnt = counts.reshape(N_EXPERTS).astype(jnp.int32)
    padded = (cnt + bm - 1) // bm * bm
    end = jnp.cumsum(padded)
    start = end - padded
    n_used = end[-1:] // bm
    zlo = jnp.concatenate([(start + cnt) // SEG_ALIGN, n_used])
    zhi = jnp.concatenate([end // SEG_ALIGN, jnp.full((1,), n_rows // bm, jnp.int32)])
    block_start = jnp.arange(n_rows // bm, dtype=jnp.int32) * bm
    block_expert = jnp.minimum(jnp.sum((end[None, :] <= block_start[:, None]).astype(jnp.int32), axis=1),
                               N_EXPERTS - 1)
    seg = stats.astype(jnp.int32)
    seg_cnt, seg_off = seg[:, 1, :].reshape(-1), seg[:, 2, :].reshape(-1)
    seg_dst = (start[None, :] + seg[:, 0, :]).reshape(-1)
    pos_t = jnp.transpose(pos.reshape(n_tiles, t // n_tiles, SLOT_PAD), (0, 2, 1))
    xs = _dispatch(x1, pos_t, seg_cnt, seg_off, seg_dst, zlo, zhi, n_rows)
    ys = _experts(xs, block_expert, n_used, w_exp_gu, w_exp_down)
    return _combine(x1, gate, pos, seg_cnt, seg_off, seg_dst, ys, w_sh_gu, w_sh_down, ln_g, ln_b, alpha)


def _layer(x, pos0, past, p, alpha):
    (w_in, a_re, a_im, log_dt, b_re, b_im, c_re, c_im, ssm_d, w_glu, w_attn_proj, w_out, ln1_g, ln1_b,
     w_router, router_bias, w_exp_gu, w_exp_down, w_sh_gu, w_sh_down, ln2_g, ln2_b) = p
    nb, seq, dm = x.shape
    assert pos0 % CHUNK == 0 and seq % Q_BLOCK == 0 and Q_BLOCK == CHUNK and seq % S5_STEPS == 0
    assert nb * seq * MOE_TOPK < 2 ** 24
    t = nb * seq
    f32 = jnp.float32
    x2 = x.reshape(t, dm)
    pos = pos0 + jnp.arange(seq, dtype=jnp.int32)

    act_dtype = MXU_DTYPE if nb % 16 == 0 else f32
    u, q, iq, k, v, ik2, ik, iw = _in_proj(x2, _pack_w_in(w_in), pos, seq, act_dtype)

    a_cat, bcat, ccat = _s5_discretise(a_re, a_im, log_dt, b_re, b_im, c_re, c_im)
    if past is None:
        h0 = jnp.zeros((nb, 2 * NS), f32)
    else:
        h0 = jnp.concatenate([past[3].reshape(nb, NS), past[4].reshape(nb, NS)], axis=1).astype(f32)
    u_t = jnp.transpose(u.reshape(nb, seq, D_SSM), (1, 0, 2))
    ssm_t, h_t = _s5(u_t, h0, a_cat, bcat, ccat, ssm_d.reshape(1, D_SSM).astype(f32), w_glu.astype(MXU_DTYPE))
    ssm2 = jnp.transpose(ssm_t, (1, 0, 2)).reshape(t, dm)
    h_re = h_t[:, :NS].reshape(nb, N_GROUPS, N_STATE)
    h_im = h_t[:, NS:].reshape(nb, N_GROUPS, N_STATE)

    k3, v3, ik23 = (a.reshape(nb, seq, LANES) for a in (k, v, ik2))
    if past is not None:
        ck, cv, cik = past[0], past[1], past[2]
        n_past = ck.shape[1]
        k3 = jnp.concatenate([ck.reshape(nb, n_past, LANES), k3], axis=1)
        v3 = jnp.concatenate([cv.reshape(nb, n_past, LANES), v3], axis=1)
        ik23 = jnp.concatenate([jnp.concatenate([cik, cik], axis=-1), ik23], axis=1)
    attn = _dsa(q.reshape(nb, seq, D_ATTN), iq.reshape(nb, seq, IDX_HEADS * IDX_DIM), iw.reshape(nb, seq, LANES),
                k3, v3, ik23, pos0 // CHUNK)

    wap = jnp.transpose(w_attn_proj.reshape(N_KV_HEADS, KV_GROUP, HEAD_DIM, dm), (1, 0, 2, 3)).reshape(D_ATTN, dm)
    wg = w_in[:, w_in.shape[1] - 2 * dm:]
    row = lambda a: a.reshape(1, -1).astype(f32)
    x1, gate, pos_slot, stats, counts = _mix(
        x2, attn.reshape(t, D_ATTN), ssm2, wg.astype(MXU_DTYPE), wap.astype(MXU_DTYPE), w_out.astype(MXU_DTYPE),
        row(ln1_g), row(ln1_b), w_router.astype(f32), row(router_bias), alpha)

    y = _moe(x1, gate, pos_slot, stats, counts, w_exp_gu, w_exp_down,
             w_sh_gu.astype(MXU_DTYPE), w_sh_down.astype(MXU_DTYPE), row(ln2_g), row(ln2_b), alpha)
    state = (k.reshape(nb, seq, N_KV_HEADS, HEAD_DIM), v.reshape(nb, seq, N_KV_HEADS, HEAD_DIM),
             ik.reshape(nb, seq, IDX_DIM), h_re, h_im)
    return y.reshape(nb, seq, dm), state


def kernel(x_prompt, x_sample, cache_k, cache_v, cache_idx_k, state_ssm_re, state_ssm_im, w_in, ssm_a_re, ssm_a_im, ssm_log_dt, ssm_b_re, ssm_b_im, ssm_c_re, ssm_c_im, ssm_d, w_glu, w_attn_proj, w_out, ln1_g, ln1_b, w_router, router_bias, w_exp_gu, w_exp_down, w_sh_gu, w_sh_down, ln2_g, ln2_b):
    weights = (w_in, ssm_a_re, ssm_a_im, ssm_log_dt, ssm_b_re, ssm_b_im, ssm_c_re, ssm_c_im, ssm_d,
               w_glu, w_attn_proj, w_out, ln1_g, ln1_b, w_router, router_bias,
               w_exp_gu, w_exp_down, w_sh_gu, w_sh_down, ln2_g, ln2_b)
    depth = w_in.shape[0]
    alpha = (2 * depth) ** 0.25
    past_len = cache_k.shape[2]
    y_p, y_s = x_prompt, x_sample
    new_p, new_s = [], []
    for l in range(depth):
        p_l = tuple(w[l] for w in weights)
        y_p, st_p = _layer(y_p, 0, None, p_l, alpha)
        y_s, st_s = _layer(y_s, past_len,
                           (cache_k[l], cache_v[l], cache_idx_k[l], state_ssm_re[l], state_ssm_im[l]), p_l, alpha)
        new_p.append(st_p)
        new_s.append(st_s)
    k_p, v_p, ik_p, hre_p, him_p = [jnp.stack(a) for a in zip(*new_p)]
    k_s, v_s, ik_s, hre_s, him_s = [jnp.stack(a) for a in zip(*new_s)]
    return (y_p, y_s, k_p, v_p, ik_p, hre_p, him_p, k_s, v_s, ik_s, hre_s, him_s)
```

```python
import functools

import jax
import jax.numpy as jnp
from jax import lax
from jax.experimental import pallas as pl
from jax.experimental.pallas import tpu as pltpu

CHUNK = 64
D_SSM = 512
SSM_GROUP = 16
N_GROUPS = D_SSM // SSM_GROUP
N_STATE = 64
N_HEADS = 8
N_KV_HEADS = 2
HEAD_DIM = 64
D_ATTN = N_HEADS * HEAD_DIM
KV_GROUP = N_HEADS // N_KV_HEADS
ROPE_DIM = HEAD_DIM // 4
ROPE_THETA = 500000.0
IDX_HEADS = 8
IDX_DIM = 64
TOPK_KEYS = 256
Q_BLOCK = 64
N_EXPERTS = 64
MOE_TOPK = 6
D_EXPERT = 256
D_SHARED = 256
ROUTED_SCALE = 2.5
LN_EPS = 1e-5

LANES = 128
MXU_DIM = 256
KEY_CHUNK = 256
DSA_ROWS = 256
EXPERT_ROWS = 512
MOE_TILE = 512
SEG_ALIGN = 16
SEG_SLOT = 64
SLOT_ROWS = N_EXPERTS * SEG_SLOT
SLOT_PAD = 8
NS = N_GROUPS * N_STATE
VMEM_LIMIT = 56 * 1024 * 1024
NEG_BIG = -1e30
INT_MIN = -2147483648

MXU_DTYPE = jnp.bfloat16


def _dot(a, b):
    return jnp.dot(a.astype(MXU_DTYPE), b.astype(MXU_DTYPE), preferred_element_type=jnp.float32)


def _dot_nt(a, b):
    return lax.dot_general(a.astype(MXU_DTYPE), b.astype(MXU_DTYPE), (((1,), (1,)), ((), ())),
                           preferred_element_type=jnp.float32)


def _params(*sem):
    return pltpu.CompilerParams(dimension_semantics=sem, vmem_limit_bytes=VMEM_LIMIT)


def _const_spec(shape):
    return pl.BlockSpec(shape, lambda *_: (0,) * len(shape), pipeline_mode=pl.Buffered(1))


_C_U, _C_Q, _C_IQ, _C_K, _C_V, _C_IK, _C_IW, _C_END = 0, 512, 1024, 1536, 1664, 1792, 1920, 2048


def _inproj_kernel(x_ref, w_ref, cos_ref, sin_ref,
                   u_ref, q_ref, iq_ref, k_ref, v_ref, ik2_ref, ik_ref, iw_ref):
    z = _dot(x_ref[...], w_ref[...])
    cos = cos_ref[...]
    sin = sin_ref[...]
    lane = lax.broadcasted_iota(jnp.int32, cos.shape, 1)
    first_half = (lane % HEAD_DIM) < (ROPE_DIM // 2)

    def rope(zc):
        partner = jnp.where(first_half, pltpu.roll(zc, LANES - ROPE_DIM // 2, 1), pltpu.roll(zc, ROPE_DIM // 2, 1))
        return zc * cos + partner * sin

    u_ref[...] = z[:, _C_U:_C_Q].astype(u_ref.dtype)
    for c in range(4):
        q_ref[:, c * LANES:(c + 1) * LANES] = rope(z[:, _C_Q + c * LANES:_C_Q + (c + 1) * LANES])
        iq_ref[:, c * LANES:(c + 1) * LANES] = rope(z[:, _C_IQ + c * LANES:_C_IQ + (c + 1) * LANES])
    k_ref[...] = rope(z[:, _C_K:_C_V])
    v_ref[...] = z[:, _C_V:_C_IK]
    ik2 = rope(z[:, _C_IK:_C_IW])
    ik2_ref[...] = ik2
    ik_ref[...] = ik2[:, :IDX_DIM]
    iw_ref[...] = z[:, _C_IW:_C_END]


def _rope_tables(pos):
    half = ROPE_DIM // 2
    inv_freq = ROPE_THETA ** (-jnp.arange(half, dtype=jnp.float32) * 2.0 / ROPE_DIM)
    ang = pos.astype(jnp.float32)[:, None] * inv_freq
    cos, sin = jnp.cos(ang), jnp.sin(ang)
    n = pos.shape[0]
    pad = HEAD_DIM - ROPE_DIM
    cos_h = jnp.concatenate([cos, cos, jnp.ones((n, pad), jnp.float32)], axis=-1)
    sin_h = jnp.concatenate([-sin, sin, jnp.zeros((n, pad), jnp.float32)], axis=-1)
    return jnp.tile(cos_h, (1, LANES // HEAD_DIM)), jnp.tile(sin_h, (1, LANES // HEAD_DIM))


def _pack_w_in(w_in):
    s = [0, 512, 1024, 1152, 1280, 1792, 1856, 1864]
    w_u, w_q, w_k, w_v = w_in[:, s[0]:s[1]], w_in[:, s[1]:s[2]], w_in[:, s[2]:s[3]], w_in[:, s[3]:s[4]]
    w_iq, w_ik, w_iw = w_in[:, s[4]:s[5]], w_in[:, s[5]:s[6]], w_in[:, s[6]:s[7]]
    d = w_in.shape[0]
    w_qh = w_q.reshape(d, N_KV_HEADS, KV_GROUP, HEAD_DIM)
    w_qp = jnp.transpose(w_qh, (0, 2, 1, 3)).reshape(d, D_ATTN)
    w_iwp = jnp.concatenate([w_iw, jnp.zeros((d, LANES - IDX_HEADS), w_in.dtype)], axis=1)
    return jnp.concatenate([w_u, w_qp, w_iq, w_k, w_v, w_ik, w_ik, w_iwp], axis=1).astype(MXU_DTYPE)


def _in_proj(x2, w_pack, pos, seq, u_dtype):
    t, d = x2.shape
    tm = min(512, t)
    cos, sin = _rope_tables(pos)
    if seq >= tm:
        per = seq // tm
        tab_map = lambda i: (i % per, 0)
    else:
        cos, sin = jnp.tile(cos, (tm // seq, 1)), jnp.tile(sin, (tm // seq, 1))
        tab_map = lambda i: (0, 0)
    row = lambda w: pl.BlockSpec((tm, w), lambda i: (i, 0))
    widths = (D_SSM, D_ATTN, IDX_HEADS * IDX_DIM, LANES, LANES, LANES, IDX_DIM, LANES)
    return pl.pallas_call(
        _inproj_kernel,
        grid=(t // tm,),
        in_specs=[row(d), _const_spec(w_pack.shape),
                  pl.BlockSpec((tm, LANES), tab_map), pl.BlockSpec((tm, LANES), tab_map)],
        out_specs=[row(w) for w in widths],
        out_shape=[jax.ShapeDtypeStruct((t, widths[0]), u_dtype)]
        + [jax.ShapeDtypeStruct((t, w), jnp.float32) for w in widths[1:]],
        compiler_params=_params("parallel"),
        name="in_proj",
    )(x2, w_pack, cos, sin)


S5_STEPS = 16
S5_COLS = 512


def _s5_kernel(u_ref, h0_ref, a_ref, bcat_ref, ccat_ref, d_ref, wglu_ref,
               out_ref, ht_ref, h_scr, bu_scr, hall_scr, *, nb):
    @pl.when(pl.program_id(0) == 0)
    def _():
        h_scr[...] = h0_ref[...]

    u = u_ref[...].reshape(S5_STEPS * nb, D_SSM)
    per_tile = MXU_DIM // N_STATE
    for j in range(2 * NS // MXU_DIM):
        kb = ((j % (NS // MXU_DIM)) * per_tile * SSM_GROUP) // MXU_DIM * MXU_DIM
        cols = slice(j * MXU_DIM, (j + 1) * MXU_DIM)
        bu_scr[:, cols] = _dot(u[:, kb:kb + MXU_DIM], bcat_ref[kb:kb + MXU_DIM, cols])
    for cb in range(NS // S5_COLS):
        re = slice(cb * S5_COLS, (cb + 1) * S5_COLS)
        im = slice(NS + cb * S5_COLS, NS + (cb + 1) * S5_COLS)
        a_re, a_im = a_ref[0:1, re], a_ref[1:2, re]
        h_re, h_im = h_scr[:, re], h_scr[:, im]
        for t in range(S5_STEPS):
            rows = slice(t * nb, (t + 1) * nb)
            h_re, h_im = (a_re * h_re - a_im * h_im + bu_scr[rows, re],
                          a_re * h_im + a_im * h_re + bu_scr[rows, im])
            hall_scr[rows, re] = h_re
            hall_scr[rows, im] = h_im
        h_scr[:, re] = h_re
        h_scr[:, im] = h_im
    y_tiles = []
    for n in range(D_SSM // MXU_DIM):
        cols = slice(n * MXU_DIM, (n + 1) * MXU_DIM)
        k0, kn = n * MXU_DIM // SSM_GROUP * N_STATE, MXU_DIM // SSM_GROUP * N_STATE
        y_tiles.append(_dot(hall_scr[:, k0:k0 + kn], ccat_ref[k0:k0 + kn, cols])
                       + _dot(hall_scr[:, NS + k0:NS + k0 + kn], ccat_ref[NS + k0:NS + k0 + kn, cols]))
    y = jnp.concatenate(y_tiles, axis=1) + d_ref[...] * u
    y = jax.nn.gelu(y)
    g = _dot(y, wglu_ref[...])
    dm = g.shape[1] // 2
    out_ref[...] = (g[:, :dm] * jax.nn.sigmoid(g[:, dm:])).astype(out_ref.dtype).reshape(S5_STEPS, nb, dm)
    ht_ref[...] = h_scr[...]


def _s5_discretise(a_re, a_im, log_dt, b_re, b_im, c_re, c_im):
    f32 = jnp.float32
    ar, ai = a_re.astype(f32), a_im.astype(f32)
    dt = jnp.exp(log_dt.astype(f32))[:, None]
    mag = jnp.exp(dt * ar)
    abar_re, abar_im = mag * jnp.cos(dt * ai), mag * jnp.sin(dt * ai)
    den = ar * ar + ai * ai
    num_re, num_im = abar_re - 1.0, abar_im
    coef_re = (num_re * ar + num_im * ai) / den
    coef_im = (num_im * ar - num_re * ai) / den
    br, bi = b_re.astype(f32), b_im.astype(f32)
    bbar_re = coef_re[..., None] * br - coef_im[..., None] * bi
    bbar_im = coef_re[..., None] * bi + coef_im[..., None] * br
    eye = jnp.eye(N_GROUPS, dtype=f32)
    bd = lambda m: jnp.einsum("gnc,gh->gchn", m, eye).reshape(D_SSM, NS)
    bcat = jnp.concatenate([bd(bbar_re), bd(bbar_im)], axis=1)
    cd = lambda m: jnp.einsum("gcn,gh->gnhc", m, eye).reshape(NS, D_SSM)
    ccat = jnp.concatenate([cd(c_re.astype(f32)), cd(-c_im.astype(f32))], axis=0)
    a_cat = jnp.stack([abar_re.reshape(NS), abar_im.reshape(NS)])
    return a_cat, bcat.astype(MXU_DTYPE), ccat.astype(MXU_DTYPE)


def _s5(u_t, h0, a_cat, bcat, ccat, d_skip, w_glu):
    seq, nb, _ = u_t.shape
    dm = w_glu.shape[1] // 2
    rows = S5_STEPS * nb
    return pl.pallas_call(
        functools.partial(_s5_kernel, nb=nb),
        grid=(seq // S5_STEPS,),
        in_specs=[pl.BlockSpec((S5_STEPS, nb, D_SSM), lambda i: (i, 0, 0)),
                  _const_spec(h0.shape), _const_spec(a_cat.shape), _const_spec(bcat.shape),
                  _const_spec(ccat.shape), _const_spec(d_skip.shape), _const_spec(w_glu.shape)],
        out_specs=[pl.BlockSpec((S5_STEPS, nb, dm), lambda i: (i, 0, 0)),
                   pl.BlockSpec(h0.shape, lambda i: (0, 0))],
        out_shape=[jax.ShapeDtypeStruct((seq, nb, dm), u_t.dtype),
                   jax.ShapeDtypeStruct(h0.shape, jnp.float32)],
        scratch_shapes=[pltpu.VMEM(h0.shape, jnp.float32),
                        pltpu.VMEM((rows, 2 * NS), jnp.float32),
                        pltpu.VMEM((rows, 2 * NS), jnp.float32)],
        compiler_params=_params("arbitrary"),
        name="s5_scan_glu",
    )(u_t, h0, a_cat, bcat, ccat, d_skip, w_glu)


def _sortable(x):
    bits = pltpu.bitcast(x, jnp.int32)
    key = bits ^ ((bits >> 31) & jnp.int32(0x7FFFFFFF))
    return jnp.where(x == 0.0, jnp.int32(0), key)


def _half_mask(shape, upper):
    lane = lax.broadcasted_iota(jnp.int32, shape, 1)
    return (lane >= HEAD_DIM) if upper else (lane < HEAD_DIM)


def _dsa_kernel(q_ref, iq_ref, iw_ref, k_ref, v_ref, ik_ref, o_ref,
                kb, vs0, vs1, ikb, key_scr, keyt_scr, bias_scr, wb_scr, iqlhs_scr, qlhs_scr, s_scr, m_scr, acc_scr,
                *, qt, n_keys, n_sel, first_chunk, idx_bits):
    j = pl.program_id(1)
    kc = KEY_CHUNK
    n_sub = qt // Q_BLOCK
    qtp = max(qt, LANES)

    @pl.when(j == 0)
    def _():
        v = v_ref[...]
        lower = lax.broadcasted_iota(jnp.int32, v.shape, v.ndim - 1) < HEAD_DIM
        kb[...] = k_ref[...].astype(MXU_DTYPE)
        vs0[...] = jnp.where(lower, v, 1.0).astype(MXU_DTYPE)
        vs1[...] = jnp.where(lower, 1.0, v).astype(MXU_DTYPE)
        ikb[...] = ik_ref[...].astype(MXU_DTYPE)

    row_blk = lax.broadcasted_iota(jnp.int32, (qt, 1), 0) // Q_BLOCK
    n_vis = jnp.minimum((first_chunk + j * n_sub + row_blk + 1) * CHUNK, n_keys)
    n_ck = (jnp.minimum((first_chunk + (j + 1) * n_sub) * CHUNK, n_keys) + kc - 1) // kc
    lane_k = lax.broadcasted_iota(jnp.int32, (qt, kc), 1)

    iq = iq_ref[...] * IDX_DIM ** -0.5
    iw = iw_ref[...] * IDX_HEADS ** -0.5
    for h in range(IDX_HEADS):
        iqlhs_scr[h] = jnp.where(_half_mask((qt, LANES), h % 2 == 1),
                                 iq[:, (h // 2) * LANES:(h // 2 + 1) * LANES], 0.0).astype(MXU_DTYPE)
        wb_scr[h] = jnp.broadcast_to(iw[:, h:h + 1], (qt, LANES))

    def score_chunk(c, carry):
        ikc = ikb[c]
        score = None
        for h in range(IDX_HEADS):
            rel = jnp.maximum(_dot_nt(iqlhs_scr[h], ikc), 0.0)
            w = wb_scr[h]
            term = jnp.concatenate([rel[:, s * LANES:(s + 1) * LANES] * w for s in range(kc // LANES)], axis=1)
            score = term if score is None else score + term
        key = jnp.where(c * kc + lane_k < n_vis, _sortable(score), jnp.int32(INT_MIN))
        key_scr[c] = key
        if qtp > qt:
            key = jnp.concatenate([key, jnp.full((qtp - qt, kc), INT_MIN, jnp.int32)], axis=0)
        keyt_scr[c] = key.T
        return carry

    lax.fori_loop(0, n_ck, score_chunk, 0)

    sub_k = lax.broadcasted_iota(jnp.int32, (kc // 8, 8, qtp), 0) * 8 + lax.broadcasted_iota(
        jnp.int32, (kc // 8, 8, qtp), 1)

    def count(pred):
        def body(c, acc):
            hit = pred(keyt_scr[c].reshape(kc // 8, 8, qtp), c * kc + sub_k)
            return acc + jnp.sum(jnp.where(hit, 1.0, 0.0), axis=0)
        acc = lax.fori_loop(0, n_ck, body, jnp.zeros((8, qtp), jnp.float32))
        for shift in (4, 2, 1):
            acc = acc + pltpu.roll(acc, shift, 0)
        return acc

    def value_bit(i, t):
        cand = t + (jnp.int32(1) << (31 - i))
        return jnp.where(count(lambda key, idx: key >= cand[None]) >= n_sel, cand, t)

    t = lax.fori_loop(0, 32, value_bit, jnp.full((8, qtp), INT_MIN, jnp.int32))
    need = n_sel - count(lambda key, idx: key > t[None])
    n_ge = count(lambda key, idx: key >= t[None])

    def last_tied_index():
        def index_bit(i, m):
            cand = m + (jnp.int32(1) << (idx_bits - 1 - i))
            below = count(lambda key, idx: (key == t[None]) & (idx < cand[None]))
            return jnp.where(below < need, cand, m)
        return lax.fori_loop(0, idx_bits, index_bit, jnp.zeros((8, qtp), jnp.int32))

    m_idx = lax.cond(jnp.max(n_ge) > n_sel, last_tied_index,
                     lambda: jnp.full((8, qtp), 2 ** 30, jnp.int32))
    t_rep = jnp.broadcast_to(t[0:1], (LANES, qtp)).T[:qt]
    m_rep = jnp.broadcast_to(m_idx[0:1], (LANES, qtp)).T[:qt]
    t_row = jnp.concatenate([t_rep] * (kc // LANES), axis=1)
    m_row = jnp.concatenate([m_rep] * (kc // LANES), axis=1)

    def bias_chunk(c, carry):
        key = key_scr[c]
        idx = c * kc + lane_k
        sel = ((key > t_row) | ((key == t_row) & (idx <= m_row))) & (idx < n_vis)
        bias_scr[c] = jnp.where(sel, 0.0, NEG_BIG)
        return carry

    lax.fori_loop(0, n_ck, bias_chunk, 0)

    q = q_ref[...] * HEAD_DIM ** -0.5
    rows = KV_GROUP * qt
    for g in range(N_KV_HEADS):
        for r in range(KV_GROUP):
            qlhs_scr[g, r * qt:(r + 1) * qt] = jnp.where(
                _half_mask((qt, LANES), g == 1), q[:, r * LANES:(r + 1) * LANES], 0.0).astype(MXU_DTYPE)
    m_scr[...] = jnp.full(m_scr.shape, NEG_BIG, jnp.float32)
    acc_scr[...] = jnp.zeros(acc_scr.shape, jnp.float32)

    def logits_chunk(c, carry):
        bias = bias_scr[c][None]
        for g in range(N_KV_HEADS):
            s = (_dot_nt(qlhs_scr[g], kb[c]).reshape(KV_GROUP, qt, kc) + bias).reshape(rows, kc)
            s_scr[g, c] = s
            m = m_scr[g]
            for u in range(kc // LANES):
                m = jnp.maximum(m, s[:, u * LANES:(u + 1) * LANES])
            m_scr[g] = m
        return carry

    lax.fori_loop(0, n_ck, logits_chunk, 0)
    for g in range(N_KV_HEADS):
        m_scr[g] = jnp.broadcast_to(jnp.max(m_scr[g], axis=-1, keepdims=True), (rows, LANES))

    def pv_chunk(c, carry):
        for g, vs in enumerate((vs0, vs1)):
            m = m_scr[g]
            s = s_scr[g, c]
            p = jnp.concatenate([jnp.exp(s[:, u * LANES:(u + 1) * LANES] - m) for u in range(kc // LANES)], axis=1)
            acc_scr[g] += _dot(p, vs[c])
        return carry

    lax.fori_loop(0, n_ck, pv_chunk, 0)
    outs = [acc_scr[g] / pltpu.roll(acc_scr[g], HEAD_DIM, 1) for g in range(N_KV_HEADS)]
    lower = _half_mask((qt, LANES), False)
    for r in range(KV_GROUP):
        rs = slice(r * qt, (r + 1) * qt)
        o_ref[:, r * LANES:(r + 1) * LANES] = jnp.where(lower, outs[0][rs], outs[1][rs])


def _dsa(q, iq, iw, k_all, v_all, ik2_all, first_chunk):
    nb, seq, _ = q.shape
    n_keys = k_all.shape[1]
    n_sel = min(TOPK_KEYS, n_keys // 4)
    kc = KEY_CHUNK
    qt = min(DSA_ROWS, seq)
    assert seq % qt == 0
    n_ch = -(-n_keys // kc)
    pad = n_ch * kc - n_keys
    chunked = lambda a: jnp.pad(a, ((0, 0), (0, pad), (0, 0))).reshape(nb, n_ch, kc, LANES)
    qspec = lambda w: pl.BlockSpec((None, qt, w), lambda b, j: (b, j, 0))
    kspec = pl.BlockSpec((None, n_ch, kc, LANES), lambda b, j: (b, 0, 0, 0))
    rows = KV_GROUP * qt
    return pl.pallas_call(
        functools.partial(_dsa_kernel, qt=qt, n_keys=n_keys, n_sel=n_sel, first_chunk=first_chunk,
                          idx_bits=max(1, (n_ch * kc - 1).bit_length())),
        grid=(nb, seq // qt),
        in_specs=[qspec(D_ATTN), qspec(IDX_HEADS * IDX_DIM), qspec(LANES), kspec, kspec, kspec],
        out_specs=qspec(D_ATTN),
        out_shape=jax.ShapeDtypeStruct((nb, seq, D_ATTN), jnp.float32),
        scratch_shapes=[pltpu.VMEM((n_ch, kc, LANES), MXU_DTYPE)] * 4
        + [pltpu.VMEM((n_ch, qt, kc), jnp.int32), pltpu.VMEM((n_ch, kc, max(qt, LANES)), jnp.int32),
           pltpu.VMEM((n_ch, qt, kc), jnp.float32),
           pltpu.VMEM((IDX_HEADS, qt, LANES), jnp.float32), pltpu.VMEM((IDX_HEADS, qt, LANES), MXU_DTYPE),
           pltpu.VMEM((N_KV_HEADS, rows, LANES), MXU_DTYPE),
           pltpu.VMEM((N_KV_HEADS, n_ch, rows, kc), jnp.float32),
           pltpu.VMEM((N_KV_HEADS, rows, LANES), jnp.float32), pltpu.VMEM((N_KV_HEADS, rows, LANES), jnp.float32)],
        compiler_params=_params("arbitrary", "arbitrary"),
        name="dsa_attention",
    )(q, iq, iw, chunked(k_all), chunked(v_all), chunked(ik2_all))


def _layer_norm(h, g, b):
    mu = jnp.mean(h, axis=-1, keepdims=True)
    var = jnp.mean(jnp.square(h - mu), axis=-1, keepdims=True)
    return (h - mu) * lax.rsqrt(var + LN_EPS) * g + b


def _mix_kernel(x_ref, attn_ref, ssm_ref, wg_ref, wap_ref, wout_ref, g1_ref, b1_ref, wr_ref, rb_ref,
                x1_ref, gate_ref, pos_ref, stats_ref, cnt_ref, cnt_scr, *, alpha):
    i = pl.program_id(0)

    @pl.when(i == 0)
    def _():
        cnt_scr[...] = jnp.zeros_like(cnt_scr)

    tm, dm = x_ref.shape
    n_split = 2 if tm % 16 == 0 else 1
    hm = tm // n_split
    lane = lax.broadcasted_iota(jnp.int32, (hm, N_EXPERTS), 1).astype(jnp.float32)
    slot = lax.broadcasted_iota(jnp.int32, (hm, SLOT_PAD), 1)

    def route_rows(rows):
        x = x_ref[rows, :]
        gates = jax.nn.sigmoid(_dot(x, wg_ref[...]))
        attn_out = _dot(attn_ref[rows, :], wap_ref[...])
        mixed = _dot(gates[:, :dm] * ssm_ref[rows, :] + gates[:, dm:] * attn_out, wout_ref[...])
        x1 = _layer_norm(alpha * x + mixed, g1_ref[...], b1_ref[...])
        x1_ref[rows, :] = x1
        logits = jnp.dot(x1, wr_ref[...], preferred_element_type=jnp.float32, precision=lax.Precision.HIGHEST)
        scores = jax.nn.sigmoid(logits)
        cur = scores + rb_ref[...]
        chosen = jnp.zeros_like(scores)
        picks = []
        for _ in range(MOE_TOPK):
            best = jnp.max(cur, axis=-1, keepdims=True)
            e_k = jnp.min(jnp.where(cur == best, lane, float(N_EXPERTS)), axis=-1, keepdims=True)
            hot = lane == e_k
            picks.append((hot, jnp.sum(jnp.where(hot, scores, 0.0), axis=-1, keepdims=True)))
            chosen = jnp.where(hot, 1.0, chosen)
            cur = jnp.where(hot, -jnp.inf, cur)
        total = picks[0][1]
        for _, s_k in picks[1:]:
            total = total + s_k
        return chosen, picks, total

    halves = [route_rows(slice(h * hm, (h + 1) * hm)) for h in range(n_split)]
    chosen = jnp.concatenate([h[0] for h in halves], axis=0)

    row = lax.broadcasted_iota(jnp.int32, (tm, tm), 0)
    col = lax.broadcasted_iota(jnp.int32, (tm, tm), 1)
    before = _dot(jnp.where(col < row, 1.0, 0.0), chosen)
    tile_cnt = jnp.ceil(jnp.sum(chosen, axis=0, keepdims=True) / SEG_ALIGN) * SEG_ALIGN
    e_row = lax.broadcasted_iota(jnp.int32, (N_EXPERTS, N_EXPERTS), 0)
    e_col = lax.broadcasted_iota(jnp.int32, (N_EXPERTS, N_EXPERTS), 1)
    overflow = jnp.maximum(tile_cnt - SEG_SLOT, 0.0)
    tile_off = jnp.dot(jnp.broadcast_to(overflow, (8, N_EXPERTS)), jnp.where(e_row < e_col, 1.0, 0.0),
                       preferred_element_type=jnp.float32, precision=lax.Precision.HIGHEST)[0:1]
    slot_base = lax.broadcasted_iota(jnp.int32, (1, N_EXPERTS), 1).astype(jnp.float32) * SEG_SLOT
    local = jnp.where(before < SEG_SLOT, slot_base + before, (SLOT_ROWS - SEG_SLOT) + tile_off + before)
    for h, (_, picks, total) in enumerate(halves):
        rows = slice(h * hm, (h + 1) * hm)
        g_out = jnp.zeros((hm, SLOT_PAD), jnp.float32)
        p_out = jnp.zeros((hm, SLOT_PAD), jnp.int32)
        for k, (hot, s_k) in enumerate(picks):
            pos_k = jnp.sum(jnp.where(hot, local[rows], 0.0), axis=-1, keepdims=True)
            g_out = jnp.where(slot == k, s_k / total * ROUTED_SCALE, g_out)
            p_out = jnp.where(slot == k, pos_k.astype(jnp.int32), p_out)
        gate_ref[rows, :] = g_out
        pos_ref[rows, :] = p_out
    srow = lax.broadcasted_iota(jnp.int32, (8, N_EXPERTS), 0)
    stats_ref[...] = jnp.where(srow == 0, cnt_scr[...], jnp.where(srow == 1, tile_cnt, jnp.where(
        srow == 2, tile_off, 0.0)))
    cnt_scr[...] = cnt_scr[...] + tile_cnt
    cnt_ref[...] = cnt_scr[...]


def _mix(x2, attn2, ssm2, wg, wap, wout, ln_g, ln_b, w_router, router_bias, alpha):
    t, dm = x2.shape
    tm = min(MOE_TILE, t)
    row = lambda w: pl.BlockSpec((tm, w), lambda i: (i, 0))
    consts = (wg, wap, wout, ln_g, ln_b, w_router, router_bias)
    return pl.pallas_call(
        functools.partial(_mix_kernel, alpha=alpha),
        grid=(t // tm,),
        in_specs=[row(dm), row(D_ATTN), row(dm)] + [_const_spec(c.shape) for c in consts],
        out_specs=[row(dm), row(SLOT_PAD), row(SLOT_PAD),
                   pl.BlockSpec((None, 8, N_EXPERTS), lambda i: (i, 0, 0)),
                   pl.BlockSpec((1, N_EXPERTS), lambda i: (0, 0))],
        out_shape=[jax.ShapeDtypeStruct((t, dm), jnp.float32),
                   jax.ShapeDtypeStruct((t, SLOT_PAD), jnp.float32),
                   jax.ShapeDtypeStruct((t, SLOT_PAD), jnp.int32),
                   jax.ShapeDtypeStruct((t // tm, 8, N_EXPERTS), jnp.float32),
                   jax.ShapeDtypeStruct((1, N_EXPERTS), jnp.float32)],
        scratch_shapes=[pltpu.VMEM((1, N_EXPERTS), jnp.float32)],
        compiler_params=_params("arbitrary"),
        name="mix_ln_router",
    )(x2, attn2, ssm2, *consts)


def _local_rows(tm):
    assert SLOT_ROWS % tm == 0
    return SLOT_ROWS + MOE_TOPK * tm


def _overflow_rows(i, cnt_ref, off_ref):
    last = i * N_EXPERTS + N_EXPERTS - 1
    return off_ref[last] + jnp.maximum(cnt_ref[last] - SEG_SLOT, 0)


def _segment_copies(i, cnt_ref, off_ref, dst_ref, tm, make_copy):
    def segment(e, carry):
        n = cnt_ref[i * N_EXPERTS + e]
        b = pl.multiple_of(dst_ref[i * N_EXPERTS + e], SEG_ALIGN)
        make_copy(pl.multiple_of(e * SEG_SLOT, SEG_SLOT), b, SEG_SLOT).start()

        @pl.when(n > SEG_SLOT)
        def _():
            a = SLOT_ROWS + off_ref[i * N_EXPERTS + e]
            m = n - SEG_SLOT
            for p in [tm >> s for s in range(tm.bit_length()) if (tm >> s) >= SEG_ALIGN]:
                done = (m // (2 * p)) * (2 * p)

                @pl.when((m & p) != 0)
                def _(p=p, done=done):
                    piece = make_copy(pl.multiple_of(a + done, SEG_ALIGN),
                                      pl.multiple_of(b + SEG_SLOT + done, SEG_ALIGN), p)
                    piece.start()
                    piece.wait()

        return carry

    lax.fori_loop(0, N_EXPERTS, segment, 0)


def _dispatch_kernel(cnt_ref, off_ref, dst_ref, zlo_ref, zhi_ref, post_ref, x_ref, xs_ref,
                     loc, zero_scr, sem, *, tm):
    i = pl.program_id(0)

    xb = x_ref[...].astype(MXU_DTYPE)
    post = post_ref[...]
    def sort_chunk(sc):
        slot = sc * tm + lax.broadcasted_iota(jnp.int32, (tm, tm), 0)
        onehot = jnp.zeros((tm, tm), jnp.float32)
        for k in range(MOE_TOPK):
            onehot = jnp.where(slot == post[k:k + 1, :], 1.0, onehot)
        loc[sc * tm:(sc + 1) * tm, :] = _dot(onehot, xb).astype(loc.dtype)

    n_fixed = SLOT_ROWS // tm
    for sc in range(n_fixed):
        sort_chunk(sc)
    overflow = _overflow_rows(i, cnt_ref, off_ref)
    for sc in range(n_fixed, _local_rows(tm) // tm):
        pl.when((sc - n_fixed) * tm < overflow)(functools.partial(sort_chunk, sc))

    _segment_copies(i, cnt_ref, off_ref, dst_ref, tm, lambda a, b, p: pltpu.make_async_copy(
        loc.at[pl.ds(a, p)], xs_ref.at[pl.ds(b, p)], sem))
    for _ in range(n_fixed):
        pltpu.make_async_copy(loc.at[pl.ds(0, tm)], loc.at[pl.ds(0, tm)], sem).wait()

    @pl.when(i == pl.num_programs(0) - 1)
    def _():
        zero_scr[...] = jnp.zeros_like(zero_scr)

        def zero_copy(group, rows):
            return pltpu.make_async_copy(
                zero_scr.at[pl.ds(0, rows)], xs_ref.at[pl.ds(pl.multiple_of(group * rows, rows), rows)], sem)

        def fill_groups(lo, hi, rows):
            def fill(g, c):
                zero_copy(g, rows).start()
                return c

            def fill_done(g, c):
                zero_copy(g, rows).wait()
                return c

            lax.fori_loop(lo, hi, fill, 0)
            lax.fori_loop(lo, hi, fill_done, 0)

        def segment(s, carry):
            fill_groups(zlo_ref[s], zhi_ref[s], SEG_ALIGN)
            return carry

        lax.fori_loop(0, N_EXPERTS, segment, 0)
        fill_groups(zlo_ref[N_EXPERTS], zhi_ref[N_EXPERTS], EXPERT_ROWS)


def _dispatch(x1, pos_t, seg_cnt, seg_off, seg_dst, zlo, zhi, n_rows):
    t, dm = x1.shape
    tm = min(MOE_TILE, t)
    return pl.pallas_call(
        functools.partial(_dispatch_kernel, tm=tm),
        grid_spec=pltpu.PrefetchScalarGridSpec(
            num_scalar_prefetch=5,
            grid=(t // tm,),
            in_specs=[pl.BlockSpec((None, SLOT_PAD, tm), lambda i, *_: (i, 0, 0)),
                      pl.BlockSpec((tm, dm), lambda i, *_: (i, 0))],
            out_specs=pl.BlockSpec(memory_space=pl.ANY),
            scratch_shapes=[pltpu.VMEM((_local_rows(tm), dm), MXU_DTYPE), pltpu.VMEM((EXPERT_ROWS, dm), MXU_DTYPE),
                            pltpu.SemaphoreType.DMA(())],
        ),
        out_shape=jax.ShapeDtypeStruct((n_rows, dm), MXU_DTYPE),
        compiler_params=_params("arbitrary"),
        name="moe_dispatch",
    )(seg_cnt, seg_off, seg_dst, zlo, zhi, pos_t, x1)


def _expert_kernel(be_ref, nu_ref, xs_ref, wgu_ref, wdn_ref, ys_ref, wgu_scr, wdn_scr):
    i = pl.program_id(0)

    @pl.when(i < nu_ref[0])
    def _():
        @pl.when((i == 0) | (be_ref[i] != be_ref[jnp.maximum(i - 1, 0)]))
        def _():
            wgu_scr[...] = wgu_ref[...].astype(MXU_DTYPE)
            wdn_scr[...] = wdn_ref[...].astype(MXU_DTYPE)

        h = _dot(xs_ref[...], wgu_scr[...])
        f = h.shape[1] // 2
        ys_ref[...] = _dot(jax.nn.silu(h[:, :f]) * h[:, f:], wdn_scr[...]).astype(ys_ref.dtype)

    @pl.when(i >= nu_ref[0])
    def _():
        ys_ref[...] = jnp.zeros_like(ys_ref)


def _experts(xs, block_expert, n_used, w_gu, w_down):
    n_rows, dm = xs.shape
    bm = EXPERT_ROWS
    f2 = w_gu.shape[2]
    used = lambda i, nu: jnp.minimum(i, nu[0] - 1)
    return pl.pallas_call(
        _expert_kernel,
        grid_spec=pltpu.PrefetchScalarGridSpec(
            num_scalar_prefetch=2,
            grid=(n_rows // bm,),
            in_specs=[pl.BlockSpec((bm, dm), lambda i, be, nu: (used(i, nu), 0)),
                      pl.BlockSpec((None, dm, f2), lambda i, be, nu: (be[used(i, nu)], 0, 0)),
                      pl.BlockSpec((None, f2 // 2, dm), lambda i, be, nu: (be[used(i, nu)], 0, 0))],
            out_specs=pl.BlockSpec((bm, dm), lambda i, be, nu: (i, 0)),
            scratch_shapes=[pltpu.VMEM((dm, f2), MXU_DTYPE), pltpu.VMEM((f2 // 2, dm), MXU_DTYPE)],
        ),
        out_shape=jax.ShapeDtypeStruct((n_rows, dm), MXU_DTYPE),
        compiler_params=_params("arbitrary"),
        name="moe_experts",
    )(block_expert, n_used, xs, w_gu, w_down)


def _combine_kernel(cnt_ref, off_ref, dst_ref, x1_ref, gate_ref, pos_ref, wsgu_ref, wsdn_ref, g2_ref, b2_ref,
                    ys_ref, y_ref, loc, sem, *, tm, alpha):
    i = pl.program_id(0)

    @pl.when(i == 0)
    def _():
        loc[...] = jnp.zeros_like(loc)

    _segment_copies(i, cnt_ref, off_ref, dst_ref, tm, lambda a, b, p: pltpu.make_async_copy(
        ys_ref.at[pl.ds(b, p)], loc.at[pl.ds(a, p)], sem))

    x1 = x1_ref[...]
    h = _dot(x1, wsgu_ref[...])
    f = h.shape[1] // 2
    shared = _dot(jax.nn.silu(h[:, :f]) * h[:, f:], wsdn_ref[...])

    n_fixed = SLOT_ROWS // tm
    for _ in range(n_fixed):
        pltpu.make_async_copy(loc.at[pl.ds(0, tm)], loc.at[pl.ds(0, tm)], sem).wait()

    gate = gate_ref[...]
    pos = pos_ref[...]

    def gather_chunk(sc):
        slot = sc * tm + lax.broadcasted_iota(jnp.int32, (tm, tm), 1)
        w = jnp.zeros((tm, tm), jnp.float32)
        for k in range(MOE_TOPK):
            w = jnp.where(slot == pos[:, k:k + 1], gate[:, k:k + 1], w)
        return _dot(w, loc[sc * tm:(sc + 1) * tm, :])

    routed = shared
    for sc in range(n_fixed):
        routed = routed + gather_chunk(sc)
    y_ref[...] = routed
    overflow = _overflow_rows(i, cnt_ref, off_ref)
    for sc in range(n_fixed, _local_rows(tm) // tm):
        @pl.when((sc - n_fixed) * tm < overflow)
        def _(sc=sc):
            y_ref[...] += gather_chunk(sc)

    y_ref[...] = _layer_norm(alpha * x1 + y_ref[...], g2_ref[...], b2_ref[...])


def _combine(x1, gate, pos, seg_cnt, seg_off, seg_dst, ys, w_sh_gu, w_sh_down, ln_g, ln_b, alpha):
    t, dm = x1.shape
    tm = min(MOE_TILE, t)
    const = lambda a: pl.BlockSpec(a.shape, lambda i, *_: (0,) * a.ndim, pipeline_mode=pl.Buffered(1))
    return pl.pallas_call(
        functools.partial(_combine_kernel, tm=tm, alpha=alpha),
        grid_spec=pltpu.PrefetchScalarGridSpec(
            num_scalar_prefetch=3,
            grid=(t // tm,),
            in_specs=[pl.BlockSpec((tm, dm), lambda i, *_: (i, 0)),
                      pl.BlockSpec((tm, SLOT_PAD), lambda i, *_: (i, 0)),
                      pl.BlockSpec((tm, SLOT_PAD), lambda i, *_: (i, 0)),
                      const(w_sh_gu), const(w_sh_down), const(ln_g), const(ln_b),
                      pl.BlockSpec(memory_space=pl.ANY)],
            out_specs=pl.BlockSpec((tm, dm), lambda i, *_: (i, 0)),
            scratch_shapes=[pltpu.VMEM((_local_rows(tm), dm), MXU_DTYPE), pltpu.SemaphoreType.DMA(())],
        ),
        out_shape=jax.ShapeDtypeStruct((t, dm), jnp.float32),
        compiler_params=_params("arbitrary"),
        name="moe_combine",
    )(seg_cnt, seg_off, seg_dst, x1, gate, pos, w_sh_gu, w_sh_down, ln_g, ln_b, ys)


def _moe(x1, gate, pos, stats, counts, w_exp_gu, w_exp_down, w_sh_gu, w_sh_down, ln_g, ln_b, alpha):
    t = x1.shape[0]
    bm = EXPERT_ROWS
    n_tiles = stats.shape[0]
    n_rows = (-(-(t * MOE_TOPK + n_tiles * N_EXPERTS * (SEG_ALIGN - 1) + SLOT_ROWS) // bm) + N_EXPERTS) * bm
    cnt = counts.reshape(N_EXPERTS).astype(jnp.int32)
    padded = (cnt + SEG_SLOT + bm - 1) // bm * bm
    end = jnp.cumsum(padded)
    start = end - padded
    n_used = end[-1:] // bm
    zlo = jnp.concatenate([(start + cnt) // SEG_ALIGN, n_used])
    zhi = jnp.concatenate([end // SEG_ALIGN, jnp.full((1,), n_rows // bm, jnp.int32)])
    block_start = jnp.arange(n_rows // bm, dtype=jnp.int32) * bm
    block_expert = jnp.minimum(jnp.sum((end[None, :] <= block_start[:, None]).astype(jnp.int32), axis=1),
                               N_EXPERTS - 1)
    seg = stats.astype(jnp.int32)
    seg_cnt, seg_off = seg[:, 1, :].reshape(-1), seg[:, 2, :].reshape(-1)
    seg_dst = (start[None, :] + seg[:, 0, :]).reshape(-1)
    pos_t = jnp.transpose(pos.reshape(n_tiles, t // n_tiles, SLOT_PAD), (0, 2, 1))
    xs = _dispatch(x1, pos_t, seg_cnt, seg_off, seg_dst, zlo, zhi, n_rows)
    ys = _experts(xs, block_expert, n_used, w_exp_gu, w_exp_down)
    return _combine(x1, gate, pos, seg_cnt, seg_off, seg_dst, ys, w_sh_gu, w_sh_down, ln_g, ln_b, alpha)


def _layer(x, pos0, past, p, alpha):
    (w_in, a_re, a_im, log_dt, b_re, b_im, c_re, c_im, ssm_d, w_glu, w_attn_proj, w_out, ln1_g, ln1_b,
     w_router, router_bias, w_exp_gu, w_exp_down, w_sh_gu, w_sh_down, ln2_g, ln2_b) = p
    nb, seq, dm = x.shape
    assert pos0 % CHUNK == 0 and seq % Q_BLOCK == 0 and Q_BLOCK == CHUNK and seq % S5_STEPS == 0
    assert nb * seq * MOE_TOPK < 2 ** 24
    t = nb * seq
    f32 = jnp.float32
    x2 = x.reshape(t, dm)
    pos = pos0 + jnp.arange(seq, dtype=jnp.int32)

    act_dtype = MXU_DTYPE if nb % 16 == 0 else f32
    u, q, iq, k, v, ik2, ik, iw = _in_proj(x2, _pack_w_in(w_in), pos, seq, act_dtype)

    a_cat, bcat, ccat = _s5_discretise(a_re, a_im, log_dt, b_re, b_im, c_re, c_im)
    if past is None:
        h0 = jnp.zeros((nb, 2 * NS), f32)
    else:
        h0 = jnp.concatenate([past[3].reshape(nb, NS), past[4].reshape(nb, NS)], axis=1).astype(f32)
    u_t = jnp.transpose(u.reshape(nb, seq, D_SSM), (1, 0, 2))
    ssm_t, h_t = _s5(u_t, h0, a_cat, bcat, ccat, ssm_d.reshape(1, D_SSM).astype(f32), w_glu.astype(MXU_DTYPE))
    ssm2 = jnp.transpose(ssm_t, (1, 0, 2)).reshape(t, dm)
    h_re = h_t[:, :NS].reshape(nb, N_GROUPS, N_STATE)
    h_im = h_t[:, NS:].reshape(nb, N_GROUPS, N_STATE)

    k3, v3, ik23 = (a.reshape(nb, seq, LANES) for a in (k, v, ik2))
    if past is not None:
        ck, cv, cik = past[0], past[1], past[2]
        n_past = ck.shape[1]
        k3 = jnp.concatenate([ck.reshape(nb, n_past, LANES), k3], axis=1)
        v3 = jnp.concatenate([cv.reshape(nb, n_past, LANES), v3], axis=1)
        ik23 = jnp.concatenate([jnp.concatenate([cik, cik], axis=-1), ik23], axis=1)
    attn = _dsa(q.reshape(nb, seq, D_ATTN), iq.reshape(nb, seq, IDX_HEADS * IDX_DIM), iw.reshape(nb, seq, LANES),
                k3, v3, ik23, pos0 // CHUNK)

    wap = jnp.transpose(w_attn_proj.reshape(N_KV_HEADS, KV_GROUP, HEAD_DIM, dm), (1, 0, 2, 3)).reshape(D_ATTN, dm)
    wg = w_in[:, w_in.shape[1] - 2 * dm:]
    row = lambda a: a.reshape(1, -1).astype(f32)
    x1, gate, pos_slot, stats, counts = _mix(
        x2, attn.reshape(t, D_ATTN), ssm2, wg.astype(MXU_DTYPE), wap.astype(MXU_DTYPE), w_out.astype(MXU_DTYPE),
        row(ln1_g), row(ln1_b), w_router.astype(f32), row(router_bias), alpha)

    y = _moe(x1, gate, pos_slot, stats, counts, w_exp_gu, w_exp_down,
             w_sh_gu.astype(MXU_DTYPE), w_sh_down.astype(MXU_DTYPE), row(ln2_g), row(ln2_b), alpha)
    state = (k.reshape(nb, seq, N_KV_HEADS, HEAD_DIM), v.reshape(nb, seq, N_KV_HEADS, HEAD_DIM),
             ik.reshape(nb, seq, IDX_DIM), h_re, h_im)
    return y.reshape(nb, seq, dm), state


def kernel(x_prompt, x_sample, cache_k, cache_v, cache_idx_k, state_ssm_re, state_ssm_im, w_in, ssm_a_re, ssm_a_im, ssm_log_dt, ssm_b_re, ssm_b_im, ssm_c_re, ssm_c_im, ssm_d, w_glu, w_attn_proj, w_out, ln1_g, ln1_b, w_router, router_bias, w_exp_gu, w_exp_down, w_sh_gu, w_sh_down, ln2_g, ln2_b):
    weights = (w_in, ssm_a_re, ssm_a_im, ssm_log_dt, ssm_b_re, ssm_b_im, ssm_c_re, ssm_c_im, ssm_d,
               w_glu, w_attn_proj, w_out, ln1_g, ln1_b, w_router, router_bias,
               w_exp_gu, w_exp_down, w_sh_gu, w_sh_down, ln2_g, ln2_b)
    depth = w_in.shape[0]
    alpha = (2 * depth) ** 0.25
    past_len = cache_k.shape[2]
    y_p, y_s = x_prompt, x_sample
    new_p, new_s = [], []
    for l in range(depth):
        p_l = tuple(w[l] for w in weights)
        y_p, st_p = _layer(y_p, 0, None, p_l, alpha)
        y_s, st_s = _layer(y_s, past_len,
                           (cache_k[l], cache_v[l], cache_idx_k[l], state_ssm_re[l], state_ssm_im[l]), p_l, alpha)
        new_p.append(st_p)
        new_s.append(st_s)
    k_p, v_p, ik_p, hre_p, him_p = [jnp.stack(a) for a in zip(*new_p)]
    k_s, v_s, ik_s, hre_s, him_s = [jnp.stack(a) for a in zip(*new_s)]
    return (y_p, y_s, k_p, v_p, ik_p, hre_p, him_p, k_s, v_s, ik_s, hre_s, him_s)
```

```python
import functools

import jax
import jax.numpy as jnp
from jax import lax
from jax.experimental import pallas as pl
from jax.experimental.pallas import tpu as pltpu

CHUNK = 64
D_SSM = 512
SSM_GROUP = 16
N_GROUPS = D_SSM // SSM_GROUP
N_STATE = 64
N_HEADS = 8
N_KV_HEADS = 2
HEAD_DIM = 64
D_ATTN = N_HEADS * HEAD_DIM
KV_GROUP = N_HEADS // N_KV_HEADS
ROPE_DIM = HEAD_DIM // 4
ROPE_THETA = 500000.0
IDX_HEADS = 8
IDX_DIM = 64
TOPK_KEYS = 256
Q_BLOCK = 64
N_EXPERTS = 64
MOE_TOPK = 6
D_EXPERT = 256
D_SHARED = 256
ROUTED_SCALE = 2.5
LN_EPS = 1e-5

LANES = 128
MXU_DIM = 256
KEY_CHUNK = 256
DSA_ROWS = 256
EXPERT_ROWS = 512
MOE_TILE = 512
SEG_ALIGN = 16
SEG_SLOT = 64
SLOT_ROWS = N_EXPERTS * SEG_SLOT
SLOT_PAD = 8
NS = N_GROUPS * N_STATE
VMEM_LIMIT = 56 * 1024 * 1024
NEG_BIG = -1e30
INT_MIN = -2147483648

MXU_DTYPE = jnp.bfloat16


def _dot(a, b):
    return jnp.dot(a.astype(MXU_DTYPE), b.astype(MXU_DTYPE), preferred_element_type=jnp.float32)


def _dot_nt(a, b):
    return lax.dot_general(a.astype(MXU_DTYPE), b.astype(MXU_DTYPE), (((1,), (1,)), ((), ())),
                           preferred_element_type=jnp.float32)


def _params(*sem):
    return pltpu.CompilerParams(dimension_semantics=sem, vmem_limit_bytes=VMEM_LIMIT)


def _const_spec(shape):
    return pl.BlockSpec(shape, lambda *_: (0,) * len(shape), pipeline_mode=pl.Buffered(1))


_C_U, _C_Q, _C_IQ, _C_K, _C_V, _C_IK, _C_IW, _C_END = 0, 512, 1024, 1536, 1664, 1792, 1920, 2048


def _inproj_kernel(x_ref, w_ref, cos_ref, sin_ref,
                   u_ref, q_ref, iq_ref, k_ref, v_ref, ik2_ref, ik_ref, iw_ref):
    z = _dot(x_ref[...], w_ref[...])
    cos = cos_ref[...]
    sin = sin_ref[...]
    lane = lax.broadcasted_iota(jnp.int32, cos.shape, 1)
    first_half = (lane % HEAD_DIM) < (ROPE_DIM // 2)

    def rope(zc):
        partner = jnp.where(first_half, pltpu.roll(zc, LANES - ROPE_DIM // 2, 1), pltpu.roll(zc, ROPE_DIM // 2, 1))
        return zc * cos + partner * sin

    u_ref[...] = z[:, _C_U:_C_Q].astype(u_ref.dtype)
    for c in range(4):
        q_ref[:, c * LANES:(c + 1) * LANES] = rope(z[:, _C_Q + c * LANES:_C_Q + (c + 1) * LANES])
        iq_ref[:, c * LANES:(c + 1) * LANES] = rope(z[:, _C_IQ + c * LANES:_C_IQ + (c + 1) * LANES])
    k_ref[...] = rope(z[:, _C_K:_C_V])
    v_ref[...] = z[:, _C_V:_C_IK]
    ik2 = rope(z[:, _C_IK:_C_IW])
    ik2_ref[...] = ik2
    ik_ref[...] = ik2[:, :IDX_DIM]
    iw_ref[...] = z[:, _C_IW:_C_END]


def _rope_tables(pos):
    half = ROPE_DIM // 2
    inv_freq = ROPE_THETA ** (-jnp.arange(half, dtype=jnp.float32) * 2.0 / ROPE_DIM)
    ang = pos.astype(jnp.float32)[:, None] * inv_freq
    cos, sin = jnp.cos(ang), jnp.sin(ang)
    n = pos.shape[0]
    pad = HEAD_DIM - ROPE_DIM
    cos_h = jnp.concatenate([cos, cos, jnp.ones((n, pad), jnp.float32)], axis=-1)
    sin_h = jnp.concatenate([-sin, sin, jnp.zeros((n, pad), jnp.float32)], axis=-1)
    return jnp.tile(cos_h, (1, LANES // HEAD_DIM)), jnp.tile(sin_h, (1, LANES // HEAD_DIM))


def _pack_w_in(w_in):
    s = [0, 512, 1024, 1152, 1280, 1792, 1856, 1864]
    w_u, w_q, w_k, w_v = w_in[:, s[0]:s[1]], w_in[:, s[1]:s[2]], w_in[:, s[2]:s[3]], w_in[:, s[3]:s[4]]
    w_iq, w_ik, w_iw = w_in[:, s[4]:s[5]], w_in[:, s[5]:s[6]], w_in[:, s[6]:s[7]]
    d = w_in.shape[0]
    w_qh = w_q.reshape(d, N_KV_HEADS, KV_GROUP, HEAD_DIM)
    w_qp = jnp.transpose(w_qh, (0, 2, 1, 3)).reshape(d, D_ATTN)
    w_iwp = jnp.concatenate([w_iw, jnp.zeros((d, LANES - IDX_HEADS), w_in.dtype)], axis=1)
    return jnp.concatenate([w_u, w_qp, w_iq, w_k, w_v, w_ik, w_ik, w_iwp], axis=1).astype(MXU_DTYPE)


def _in_proj(x2, w_pack, pos, seq, u_dtype):
    t, d = x2.shape
    tm = min(512, t)
    cos, sin = _rope_tables(pos)
    if seq >= tm:
        per = seq // tm
        tab_map = lambda i: (i % per, 0)
    else:
        cos, sin = jnp.tile(cos, (tm // seq, 1)), jnp.tile(sin, (tm // seq, 1))
        tab_map = lambda i: (0, 0)
    row = lambda w: pl.BlockSpec((tm, w), lambda i: (i, 0))
    widths = (D_SSM, D_ATTN, IDX_HEADS * IDX_DIM, LANES, LANES, LANES, IDX_DIM, LANES)
    return pl.pallas_call(
        _inproj_kernel,
        grid=(t // tm,),
        in_specs=[row(d), _const_spec(w_pack.shape),
                  pl.BlockSpec((tm, LANES), tab_map), pl.BlockSpec((tm, LANES), tab_map)],
        out_specs=[row(w) for w in widths],
        out_shape=[jax.ShapeDtypeStruct((t, widths[0]), u_dtype)]
        + [jax.ShapeDtypeStruct((t, w), jnp.float32) for w in widths[1:]],
        compiler_params=_params("parallel"),
        name="in_proj",
    )(x2, w_pack, cos, sin)


S5_STEPS = 16
S5_COLS = 512


def _s5_kernel(u_ref, h0_ref, a_ref, bcat_ref, ccat_ref, d_ref, wglu_ref,
               out_ref, ht_ref, h_scr, bu_scr, hall_scr, *, nb):
    @pl.when(pl.program_id(0) == 0)
    def _():
        h_scr[...] = h0_ref[...]

    u = u_ref[...].reshape(S5_STEPS * nb, D_SSM)
    per_tile = MXU_DIM // N_STATE
    for j in range(2 * NS // MXU_DIM):
        kb = ((j % (NS // MXU_DIM)) * per_tile * SSM_GROUP) // MXU_DIM * MXU_DIM
        cols = slice(j * MXU_DIM, (j + 1) * MXU_DIM)
        bu_scr[:, cols] = _dot(u[:, kb:kb + MXU_DIM], bcat_ref[kb:kb + MXU_DIM, cols])
    for cb in range(NS // S5_COLS):
        re = slice(cb * S5_COLS, (cb + 1) * S5_COLS)
        im = slice(NS + cb * S5_COLS, NS + (cb + 1) * S5_COLS)
        a_re, a_im = a_ref[0:1, re], a_ref[1:2, re]
        h_re, h_im = h_scr[:, re], h_scr[:, im]
        for t in range(S5_STEPS):
            rows = slice(t * nb, (t + 1) * nb)
            h_re, h_im = (a_re * h_re - a_im * h_im + bu_scr[rows, re],
                          a_re * h_im + a_im * h_re + bu_scr[rows, im])
            hall_scr[rows, re] = h_re
            hall_scr[rows, im] = h_im
        h_scr[:, re] = h_re
        h_scr[:, im] = h_im
    y_tiles = []
    for n in range(D_SSM // MXU_DIM):
        cols = slice(n * MXU_DIM, (n + 1) * MXU_DIM)
        k0, kn = n * MXU_DIM // SSM_GROUP * N_STATE, MXU_DIM // SSM_GROUP * N_STATE
        y_tiles.append(_dot(hall_scr[:, k0:k0 + kn], ccat_ref[k0:k0 + kn, cols])
                       + _dot(hall_scr[:, NS + k0:NS + k0 + kn], ccat_ref[NS + k0:NS + k0 + kn, cols]))
    y = jnp.concatenate(y_tiles, axis=1) + d_ref[...] * u
    y = jax.nn.gelu(y)
    g = _dot(y, wglu_ref[...])
    dm = g.shape[1] // 2
    out_ref[...] = (g[:, :dm] * jax.nn.sigmoid(g[:, dm:])).astype(out_ref.dtype).reshape(S5_STEPS, nb, dm)
    ht_ref[...] = h_scr[...]


def _s5_discretise(a_re, a_im, log_dt, b_re, b_im, c_re, c_im):
    f32 = jnp.float32
    ar, ai = a_re.astype(f32), a_im.astype(f32)
    dt = jnp.exp(log_dt.astype(f32))[:, None]
    mag = jnp.exp(dt * ar)
    abar_re, abar_im = mag * jnp.cos(dt * ai), mag * jnp.sin(dt * ai)
    den = ar * ar + ai * ai
    num_re, num_im = abar_re - 1.0, abar_im
    coef_re = (num_re * ar + num_im * ai) / den
    coef_im = (num_im * ar - num_re * ai) / den
    br, bi = b_re.astype(f32), b_im.astype(f32)
    bbar_re = coef_re[..., None] * br - coef_im[..., None] * bi
    bbar_im = coef_re[..., None] * bi + coef_im[..., None] * br
    eye = jnp.eye(N_GROUPS, dtype=f32)
    bd = lambda m: jnp.einsum("gnc,gh->gchn", m, eye).reshape(D_SSM, NS)
    bcat = jnp.concatenate([bd(bbar_re), bd(bbar_im)], axis=1)
    cd = lambda m: jnp.einsum("gcn,gh->gnhc", m, eye).reshape(NS, D_SSM)
    ccat = jnp.concatenate([cd(c_re.astype(f32)), cd(-c_im.astype(f32))], axis=0)
    a_cat = jnp.stack([abar_re.reshape(NS), abar_im.reshape(NS)])
    return a_cat, bcat.astype(MXU_DTYPE), ccat.astype(MXU_DTYPE)


def _s5(u_t, h0, a_cat, bcat, ccat, d_skip, w_glu):
    seq, nb, _ = u_t.shape
    dm = w_glu.shape[1] // 2
    rows = S5_STEPS * nb
    return pl.pallas_call(
        functools.partial(_s5_kernel, nb=nb),
        grid=(seq // S5_STEPS,),
        in_specs=[pl.BlockSpec((S5_STEPS, nb, D_SSM), lambda i: (i, 0, 0)),
                  _const_spec(h0.shape), _const_spec(a_cat.shape), _const_spec(bcat.shape),
                  _const_spec(ccat.shape), _const_spec(d_skip.shape), _const_spec(w_glu.shape)],
        out_specs=[pl.BlockSpec((S5_STEPS, nb, dm), lambda i: (i, 0, 0)),
                   pl.BlockSpec(h0.shape, lambda i: (0, 0))],
        out_shape=[jax.ShapeDtypeStruct((seq, nb, dm), u_t.dtype),
                   jax.ShapeDtypeStruct(h0.shape, jnp.float32)],
        scratch_shapes=[pltpu.VMEM(h0.shape, jnp.float32),
                        pltpu.VMEM((rows, 2 * NS), jnp.float32),
                        pltpu.VMEM((rows, 2 * NS), jnp.float32)],
        compiler_params=_params("arbitrary"),
        name="s5_scan_glu",
    )(u_t, h0, a_cat, bcat, ccat, d_skip, w_glu)


def _sortable(x):
    bits = pltpu.bitcast(x, jnp.int32)
    key = bits ^ ((bits >> 31) & jnp.int32(0x7FFFFFFF))
    return jnp.where(x == 0.0, jnp.int32(0), key)


def _half_mask(shape, upper):
    lane = lax.broadcasted_iota(jnp.int32, shape, 1)
    return (lane >= HEAD_DIM) if upper else (lane < HEAD_DIM)


def _dsa_kernel(q_ref, iq_ref, iw_ref, k_ref, v_ref, ik_ref, o_ref,
                kb, vs0, vs1, ikb, key_scr, keyt_scr, bias_scr, wb_scr, iqlhs_scr, qlhs_scr, s_scr, m_scr, acc_scr,
                *, qt, n_keys, n_sel, first_chunk, idx_bits):
    j = pl.program_id(1)
    kc = KEY_CHUNK
    n_sub = qt // Q_BLOCK
    qtp = max(qt, LANES)

    @pl.when(j == 0)
    def _():
        v = v_ref[...]
        lower = lax.broadcasted_iota(jnp.int32, v.shape, v.ndim - 1) < HEAD_DIM
        kb[...] = k_ref[...].astype(MXU_DTYPE)
        vs0[...] = jnp.where(lower, v, 1.0).astype(MXU_DTYPE)
        vs1[...] = jnp.where(lower, 1.0, v).astype(MXU_DTYPE)
        ikb[...] = ik_ref[...].astype(MXU_DTYPE)

    row_blk = lax.broadcasted_iota(jnp.int32, (qt, 1), 0) // Q_BLOCK
    n_vis = jnp.minimum((first_chunk + j * n_sub + row_blk + 1) * CHUNK, n_keys)
    n_ck = (jnp.minimum((first_chunk + (j + 1) * n_sub) * CHUNK, n_keys) + kc - 1) // kc
    lane_k = lax.broadcasted_iota(jnp.int32, (qt, kc), 1)

    iq = iq_ref[...] * IDX_DIM ** -0.5
    iw = iw_ref[...] * IDX_HEADS ** -0.5
    for h in range(IDX_HEADS):
        iqlhs_scr[h] = jnp.where(_half_mask((qt, LANES), h % 2 == 1),
                                 iq[:, (h // 2) * LANES:(h // 2 + 1) * LANES], 0.0).astype(MXU_DTYPE)
        wb_scr[h] = jnp.broadcast_to(iw[:, h:h + 1], (qt, LANES))

    def score_chunk(c, carry):
        ikc = ikb[c]
        score = None
        for h in range(IDX_HEADS):
            rel = jnp.maximum(_dot_nt(iqlhs_scr[h], ikc), 0.0)
            w = wb_scr[h]
            term = jnp.concatenate([rel[:, s * LANES:(s + 1) * LANES] * w for s in range(kc // LANES)], axis=1)
            score = term if score is None else score + term
        key = jnp.where(c * kc + lane_k < n_vis, _sortable(score), jnp.int32(INT_MIN))
        key_scr[c] = key
        if qtp > qt:
            key = jnp.concatenate([key, jnp.full((qtp - qt, kc), INT_MIN, jnp.int32)], axis=0)
        keyt_scr[c] = key.T
        return carry

    lax.fori_loop(0, n_ck, score_chunk, 0)

    sub_k = lax.broadcasted_iota(jnp.int32, (kc // 8, 8, qtp), 0) * 8 + lax.broadcasted_iota(
        jnp.int32, (kc // 8, 8, qtp), 1)

    def count(pred):
        def body(c, acc):
            hit = pred(keyt_scr[c].reshape(kc // 8, 8, qtp), c * kc + sub_k)
            return acc + jnp.sum(jnp.where(hit, 1.0, 0.0), axis=0)
        acc = lax.fori_loop(0, n_ck, body, jnp.zeros((8, qtp), jnp.float32))
        for shift in (4, 2, 1):
            acc = acc + pltpu.roll(acc, shift, 0)
        return acc

    def value_bit(i, t):
        cand = t + (jnp.int32(1) << (31 - i))
        return jnp.where(count(lambda key, idx: key >= cand[None]) >= n_sel, cand, t)

    t = lax.fori_loop(0, 32, value_bit, jnp.full((8, qtp), INT_MIN, jnp.int32))
    need = n_sel - count(lambda key, idx: key > t[None])
    n_ge = count(lambda key, idx: key >= t[None])

    def last_tied_index():
        def index_bit(i, m):
            cand = m + (jnp.int32(1) << (idx_bits - 1 - i))
            below = count(lambda key, idx: (key == t[None]) & (idx < cand[None]))
            return jnp.where(below < need, cand, m)
        return lax.fori_loop(0, idx_bits, index_bit, jnp.zeros((8, qtp), jnp.int32))

    m_idx = lax.cond(jnp.max(n_ge) > n_sel, last_tied_index,
                     lambda: jnp.full((8, qtp), 2 ** 30, jnp.int32))
    t_rep = jnp.broadcast_to(t[0:1], (LANES, qtp)).T[:qt]
    m_rep = jnp.broadcast_to(m_idx[0:1], (LANES, qtp)).T[:qt]
    t_row = jnp.concatenate([t_rep] * (kc // LANES), axis=1)
    m_row = jnp.concatenate([m_rep] * (kc // LANES), axis=1)

    def bias_chunk(c, carry):
        key = key_scr[c]
        idx = c * kc + lane_k
        sel = ((key > t_row) | ((key == t_row) & (idx <= m_row))) & (idx < n_vis)
        bias_scr[c] = jnp.where(sel, 0.0, NEG_BIG)
        return carry

    lax.fori_loop(0, n_ck, bias_chunk, 0)

    q = q_ref[...] * HEAD_DIM ** -0.5
    rows = KV_GROUP * qt
    for g in range(N_KV_HEADS):
        for r in range(KV_GROUP):
            qlhs_scr[g, r * qt:(r + 1) * qt] = jnp.where(
                _half_mask((qt, LANES), g == 1), q[:, r * LANES:(r + 1) * LANES], 0.0).astype(MXU_DTYPE)
    m_scr[...] = jnp.full(m_scr.shape, NEG_BIG, jnp.float32)
    acc_scr[...] = jnp.zeros(acc_scr.shape, jnp.float32)

    def logits_chunk(c, carry):
        bias = bias_scr[c][None]
        for g in range(N_KV_HEADS):
            s = (_dot_nt(qlhs_scr[g], kb[c]).reshape(KV_GROUP, qt, kc) + bias).reshape(rows, kc)
            s_scr[g, c] = s
            m = m_scr[g]
            for u in range(kc // LANES):
                m = jnp.maximum(m, s[:, u * LANES:(u + 1) * LANES])
            m_scr[g] = m
        return carry

    lax.fori_loop(0, n_ck, logits_chunk, 0)
    for g in range(N_KV_HEADS):
        m_scr[g] = jnp.broadcast_to(jnp.max(m_scr[g], axis=-1, keepdims=True), (rows, LANES))

    def pv_chunk(c, carry):
        for g, vs in enumerate((vs0, vs1)):
            m = m_scr[g]
            s = s_scr[g, c]
            p = jnp.concatenate([jnp.exp(s[:, u * LANES:(u + 1) * LANES] - m) for u in range(kc // LANES)], axis=1)
            acc_scr[g] += _dot(p, vs[c])
        return carry

    lax.fori_loop(0, n_ck, pv_chunk, 0)
    outs = [acc_scr[g] / pltpu.roll(acc_scr[g], HEAD_DIM, 1) for g in range(N_KV_HEADS)]
    lower = _half_mask((qt, LANES), False)
    for r in range(KV_GROUP):
        rs = slice(r * qt, (r + 1) * qt)
        o_ref[:, r * LANES:(r + 1) * LANES] = jnp.where(lower, outs[0][rs], outs[1][rs])


def _dsa(q, iq, iw, k_all, v_all, ik2_all, first_chunk):
    nb, seq, _ = q.shape
    n_keys = k_all.shape[1]
    n_sel = min(TOPK_KEYS, n_keys // 4)
    kc = KEY_CHUNK
    qt = min(DSA_ROWS, seq)
    assert seq % qt == 0
    n_ch = -(-n_keys // kc)
    pad = n_ch * kc - n_keys
    chunked = lambda a: jnp.pad(a, ((0, 0), (0, pad), (0, 0))).reshape(nb, n_ch, kc, LANES)
    qspec = lambda w: pl.BlockSpec((None, qt, w), lambda b, j: (b, j, 0))
    kspec = pl.BlockSpec((None, n_ch, kc, LANES), lambda b, j: (b, 0, 0, 0))
    rows = KV_GROUP * qt
    return pl.pallas_call(
        functools.partial(_dsa_kernel, qt=qt, n_keys=n_keys, n_sel=n_sel, first_chunk=first_chunk,
                          idx_bits=max(1, (n_ch * kc - 1).bit_length())),
        grid=(nb, seq // qt),
        in_specs=[qspec(D_ATTN), qspec(IDX_HEADS * IDX_DIM), qspec(LANES), kspec, kspec, kspec],
        out_specs=qspec(D_ATTN),
        out_shape=jax.ShapeDtypeStruct((nb, seq, D_ATTN), jnp.float32),
        scratch_shapes=[pltpu.VMEM((n_ch, kc, LANES), MXU_DTYPE)] * 4
        + [pltpu.VMEM((n_ch, qt, kc), jnp.int32), pltpu.VMEM((n_ch, kc, max(qt, LANES)), jnp.int32),
           pltpu.VMEM((n_ch, qt, kc), jnp.float32),
           pltpu.VMEM((IDX_HEADS, qt, LANES), jnp.float32), pltpu.VMEM((IDX_HEADS, qt, LANES), MXU_DTYPE),
           pltpu.VMEM((N_KV_HEADS, rows, LANES), MXU_DTYPE),
           pltpu.VMEM((N_KV_HEADS, n_ch, rows, kc), jnp.float32),
           pltpu.VMEM((N_KV_HEADS, rows, LANES), jnp.float32), pltpu.VMEM((N_KV_HEADS, rows, LANES), jnp.float32)],
        compiler_params=_params("arbitrary", "arbitrary"),
        name="dsa_attention",
    )(q, iq, iw, chunked(k_all), chunked(v_all), chunked(ik2_all))


def _layer_norm(h, g, b):
    mu = jnp.mean(h, axis=-1, keepdims=True)
    var = jnp.mean(jnp.square(h - mu), axis=-1, keepdims=True)
    return (h - mu) * lax.rsqrt(var + LN_EPS) * g + b


def _mix_kernel(x_ref, attn_ref, ssm_ref, wg_ref, wap_ref, wout_ref, g1_ref, b1_ref, wr_ref, rb_ref,
                x1_ref, gate_ref, pos_ref, stats_ref, cnt_ref, cnt_scr, *, alpha):
    i = pl.program_id(0)

    @pl.when(i == 0)
    def _():
        cnt_scr[...] = jnp.zeros_like(cnt_scr)

    tm, dm = x_ref.shape
    n_split = 2 if tm % 16 == 0 else 1
    hm = tm // n_split
    lane = lax.broadcasted_iota(jnp.int32, (hm, N_EXPERTS), 1).astype(jnp.float32)
    slot = lax.broadcasted_iota(jnp.int32, (hm, SLOT_PAD), 1)

    def route_rows(rows):
        x = x_ref[rows, :]
        gates = jax.nn.sigmoid(_dot(x, wg_ref[...]))
        attn_out = _dot(attn_ref[rows, :], wap_ref[...])
        mixed = _dot(gates[:, :dm] * ssm_ref[rows, :] + gates[:, dm:] * attn_out, wout_ref[...])
        x1 = _layer_norm(alpha * x + mixed, g1_ref[...], b1_ref[...])
        x1_ref[rows, :] = x1
        logits = jnp.dot(x1, wr_ref[...], preferred_element_type=jnp.float32, precision=lax.Precision.HIGHEST)
        scores = jax.nn.sigmoid(logits)
        cur = scores + rb_ref[...]
        chosen = jnp.zeros_like(scores)
        picks = []
        for _ in range(MOE_TOPK):
            best = jnp.max(cur, axis=-1, keepdims=True)
            e_k = jnp.min(jnp.where(cur == best, lane, float(N_EXPERTS)), axis=-1, keepdims=True)
            hot = lane == e_k
            picks.append((hot, jnp.sum(jnp.where(hot, scores, 0.0), axis=-1, keepdims=True)))
            chosen = jnp.where(hot, 1.0, chosen)
            cur = jnp.where(hot, -jnp.inf, cur)
        total = picks[0][1]
        for _, s_k in picks[1:]:
            total = total + s_k
        return chosen, picks, total

    halves = [route_rows(slice(h * hm, (h + 1) * hm)) for h in range(n_split)]
    chosen = jnp.concatenate([h[0] for h in halves], axis=0)

    row = lax.broadcasted_iota(jnp.int32, (tm, tm), 0)
    col = lax.broadcasted_iota(jnp.int32, (tm, tm), 1)
    before = _dot(jnp.where(col < row, 1.0, 0.0), chosen)
    tile_cnt = jnp.ceil(jnp.sum(chosen, axis=0, keepdims=True) / SEG_ALIGN) * SEG_ALIGN
    e_row = lax.broadcasted_iota(jnp.int32, (N_EXPERTS, N_EXPERTS), 0)
    e_col = lax.broadcasted_iota(jnp.int32, (N_EXPERTS, N_EXPERTS), 1)
    overflow = jnp.maximum(tile_cnt - SEG_SLOT, 0.0)
    tile_off = jnp.dot(jnp.broadcast_to(overflow, (8, N_EXPERTS)), jnp.where(e_row < e_col, 1.0, 0.0),
                       preferred_element_type=jnp.float32, precision=lax.Precision.HIGHEST)[0:1]
    slot_base = lax.broadcasted_iota(jnp.int32, (1, N_EXPERTS), 1).astype(jnp.float32) * SEG_SLOT
    local = jnp.where(before < SEG_SLOT, slot_base + before, (SLOT_ROWS - SEG_SLOT) + tile_off + before)
    for h, (_, picks, total) in enumerate(halves):
        rows = slice(h * hm, (h + 1) * hm)
        g_out = jnp.zeros((hm, SLOT_PAD), jnp.float32)
        p_out = jnp.zeros((hm, SLOT_PAD), jnp.int32)
        for k, (hot, s_k) in enumerate(picks):
            pos_k = jnp.sum(jnp.where(hot, local[rows], 0.0), axis=-1, keepdims=True)
            g_out = jnp.where(slot == k, s_k / total * ROUTED_SCALE, g_out)
            p_out = jnp.where(slot == k, pos_k.astype(jnp.int32), p_out)
        gate_ref[rows, :] = g_out
        pos_ref[rows, :] = p_out
    srow = lax.broadcasted_iota(jnp.int32, (8, N_EXPERTS), 0)
    stats_ref[...] = jnp.where(srow == 0, cnt_scr[...], jnp.where(srow == 1, tile_cnt, jnp.where(
        srow == 2, tile_off, 0.0)))
    cnt_scr[...] = cnt_scr[...] + tile_cnt
    cnt_ref[...] = cnt_scr[...]


def _mix(x2, attn2, ssm2, wg, wap, wout, ln_g, ln_b, w_router, router_bias, alpha):
    t, dm = x2.shape
    tm = min(MOE_TILE, t)
    row = lambda w: pl.BlockSpec((tm, w), lambda i: (i, 0))
    consts = (wg, wap, wout, ln_g, ln_b, w_router, router_bias)
    return pl.pallas_call(
        functools.partial(_mix_kernel, alpha=alpha),
        grid=(t // tm,),
        in_specs=[row(dm), row(D_ATTN), row(dm)] + [_const_spec(c.shape) for c in consts],
        out_specs=[row(dm), row(SLOT_PAD), row(SLOT_PAD),
                   pl.BlockSpec((None, 8, N_EXPERTS), lambda i: (i, 0, 0)),
                   pl.BlockSpec((1, N_EXPERTS), lambda i: (0, 0))],
        out_shape=[jax.ShapeDtypeStruct((t, dm), jnp.float32),
                   jax.ShapeDtypeStruct((t, SLOT_PAD), jnp.float32),
                   jax.ShapeDtypeStruct((t, SLOT_PAD), jnp.int32),
                   jax.ShapeDtypeStruct((t // tm, 8, N_EXPERTS), jnp.float32),
                   jax.ShapeDtypeStruct((1, N_EXPERTS), jnp.float32)],
        scratch_shapes=[pltpu.VMEM((1, N_EXPERTS), jnp.float32)],
        compiler_params=_params("arbitrary"),
        name="mix_ln_router",
    )(x2, attn2, ssm2, *consts)


def _local_rows(tm):
    assert SLOT_ROWS % tm == 0
    return SLOT_ROWS + MOE_TOPK * tm


def _overflow_rows(i, cnt_ref, off_ref):
    last = i * N_EXPERTS + N_EXPERTS - 1
    return off_ref[last] + jnp.maximum(cnt_ref[last] - SEG_SLOT, 0)


def _segment_copies(i, cnt_ref, off_ref, dst_ref, tm, make_copy):
    def segment(e, carry):
        n = cnt_ref[i * N_EXPERTS + e]
        b = pl.multiple_of(dst_ref[i * N_EXPERTS + e], SEG_ALIGN)
        make_copy(pl.multiple_of(e * SEG_SLOT, SEG_SLOT), b, SEG_SLOT).start()

        @pl.when(n > SEG_SLOT)
        def _():
            a = SLOT_ROWS + off_ref[i * N_EXPERTS + e]
            m = n - SEG_SLOT
            for p in [tm >> s for s in range(tm.bit_length()) if (tm >> s) >= SEG_ALIGN]:
                done = (m // (2 * p)) * (2 * p)

                @pl.when((m & p) != 0)
                def _(p=p, done=done):
                    piece = make_copy(pl.multiple_of(a + done, SEG_ALIGN),
                                      pl.multiple_of(b + SEG_SLOT + done, SEG_ALIGN), p)
                    piece.start()
                    piece.wait()

        return carry

    lax.fori_loop(0, N_EXPERTS, segment, 0)


def _wait_slot_copies(buf, sem, tm):
    for _ in range(SLOT_ROWS // tm):
        pltpu.make_async_copy(buf.at[pl.ds(0, tm)], buf.at[pl.ds(0, tm)], sem).wait()


def _dispatch_kernel(cnt_ref, off_ref, dst_ref, zlo_ref, zhi_ref, post_ref, x_ref, xs_ref,
                     loc2, zero_scr, sems, *, tm, n_steps):
    i = pl.program_id(0)

    def sort_and_send(which):
        loc, sem = loc2.at[which], sems.at[which]

        xb = x_ref[...].astype(MXU_DTYPE)
        post = post_ref[...]

        def sort_chunk(sc):
            slot = sc * tm + lax.broadcasted_iota(jnp.int32, (tm, tm), 0)
            onehot = jnp.zeros((tm, tm), jnp.float32)
            for k in range(MOE_TOPK):
                onehot = jnp.where(slot == post[k:k + 1, :], 1.0, onehot)
            loc[sc * tm:(sc + 1) * tm, :] = _dot(onehot, xb).astype(loc.dtype)

        n_fixed = SLOT_ROWS // tm
        for sc in range(n_fixed):
            sort_chunk(sc)
        overflow = _overflow_rows(i, cnt_ref, off_ref)
        for sc in range(n_fixed, _local_rows(tm) // tm):
            pl.when((sc - n_fixed) * tm < overflow)(functools.partial(sort_chunk, sc))

        @pl.when(i >= 1)
        def _():
            _wait_slot_copies(loc2.at[1 - which], sems.at[1 - which], tm)

        _segment_copies(i, cnt_ref, off_ref, dst_ref, tm, lambda a, b, p: pltpu.make_async_copy(
            loc.at[pl.ds(a, p)], xs_ref.at[pl.ds(b, p)], sem))

        @pl.when(i == n_steps - 1)
        def _():
            _wait_slot_copies(loc, sem, tm)

    for which in range(2):
        pl.when(i % 2 == which)(functools.partial(sort_and_send, which))

    @pl.when(i == n_steps - 1)
    def _():
        sem = sems.at[0]
        zero_scr[...] = jnp.zeros_like(zero_scr)

        def zero_copy(group, rows):
            return pltpu.make_async_copy(
                zero_scr.at[pl.ds(0, rows)], xs_ref.at[pl.ds(pl.multiple_of(group * rows, rows), rows)], sem)

        def fill_groups(lo, hi, rows):
            def fill(g, c):
                zero_copy(g, rows).start()
                return c

            def fill_done(g, c):
                zero_copy(g, rows).wait()
                return c

            lax.fori_loop(lo, hi, fill, 0)
            lax.fori_loop(lo, hi, fill_done, 0)

        def segment(s, carry):
            fill_groups(zlo_ref[s], zhi_ref[s], SEG_ALIGN)
            return carry

        lax.fori_loop(0, N_EXPERTS, segment, 0)
        fill_groups(zlo_ref[N_EXPERTS], zhi_ref[N_EXPERTS], EXPERT_ROWS)


def _dispatch(x1, pos_t, seg_cnt, seg_off, seg_dst, zlo, zhi, n_rows):
    t, dm = x1.shape
    tm = min(MOE_TILE, t)
    return pl.pallas_call(
        functools.partial(_dispatch_kernel, tm=tm, n_steps=t // tm),
        grid_spec=pltpu.PrefetchScalarGridSpec(
            num_scalar_prefetch=5,
            grid=(t // tm,),
            in_specs=[pl.BlockSpec((None, SLOT_PAD, tm), lambda i, *_: (i, 0, 0)),
                      pl.BlockSpec((tm, dm), lambda i, *_: (i, 0))],
            out_specs=pl.BlockSpec(memory_space=pl.ANY),
            scratch_shapes=[pltpu.VMEM((2, _local_rows(tm), dm), MXU_DTYPE),
                            pltpu.VMEM((EXPERT_ROWS, dm), MXU_DTYPE), pltpu.SemaphoreType.DMA((2,))],
        ),
        out_shape=jax.ShapeDtypeStruct((n_rows, dm), MXU_DTYPE),
        compiler_params=_params("arbitrary"),
        name="moe_dispatch",
    )(seg_cnt, seg_off, seg_dst, zlo, zhi, pos_t, x1)


def _expert_kernel(be_ref, nu_ref, xs_ref, wgu_ref, wdn_ref, ys_ref, wgu_scr, wdn_scr):
    i = pl.program_id(0)

    @pl.when(i < nu_ref[0])
    def _():
        @pl.when((i == 0) | (be_ref[i] != be_ref[jnp.maximum(i - 1, 0)]))
        def _():
            wgu_scr[...] = wgu_ref[...].astype(MXU_DTYPE)
            wdn_scr[...] = wdn_ref[...].astype(MXU_DTYPE)

        h = _dot(xs_ref[...], wgu_scr[...])
        f = h.shape[1] // 2
        ys_ref[...] = _dot(jax.nn.silu(h[:, :f]) * h[:, f:], wdn_scr[...]).astype(ys_ref.dtype)

    @pl.when(i >= nu_ref[0])
    def _():
        ys_ref[...] = jnp.zeros_like(ys_ref)


def _experts(xs, block_expert, n_used, w_gu, w_down):
    n_rows, dm = xs.shape
    bm = EXPERT_ROWS
    f2 = w_gu.shape[2]
    used = lambda i, nu: jnp.minimum(i, nu[0] - 1)
    return pl.pallas_call(
        _expert_kernel,
        grid_spec=pltpu.PrefetchScalarGridSpec(
            num_scalar_prefetch=2,
            grid=(n_rows // bm,),
            in_specs=[pl.BlockSpec((bm, dm), lambda i, be, nu: (used(i, nu), 0)),
                      pl.BlockSpec((None, dm, f2), lambda i, be, nu: (be[used(i, nu)], 0, 0)),
                      pl.BlockSpec((None, f2 // 2, dm), lambda i, be, nu: (be[used(i, nu)], 0, 0))],
            out_specs=pl.BlockSpec((bm, dm), lambda i, be, nu: (i, 0)),
            scratch_shapes=[pltpu.VMEM((dm, f2), MXU_DTYPE), pltpu.VMEM((f2 // 2, dm), MXU_DTYPE)],
        ),
        out_shape=jax.ShapeDtypeStruct((n_rows, dm), MXU_DTYPE),
        compiler_params=_params("arbitrary"),
        name="moe_experts",
    )(block_expert, n_used, xs, w_gu, w_down)


def _combine_kernel(cnt_ref, off_ref, dst_ref, x1_ref, gate_ref, pos_ref, wsgu_ref, wsdn_ref, g2_ref, b2_ref,
                    ys_ref, y_ref, loc2, sems, *, tm, alpha, n_steps):
    i = pl.program_id(0)

    def fetch(tile, which):
        _segment_copies(tile, cnt_ref, off_ref, dst_ref, tm, lambda a, b, p: pltpu.make_async_copy(
            ys_ref.at[pl.ds(b, p)], loc2.at[which, pl.ds(a, p)], sems.at[which]))

    def combine_tile(which):
        loc = loc2.at[which]

        @pl.when(i == 0)
        def _():
            loc2[:, SLOT_ROWS:, :] = jnp.zeros((2, MOE_TOPK * tm, loc2.shape[2]), loc2.dtype)
            fetch(i, which)

        @pl.when(i + 1 < n_steps)
        def _():
            fetch(i + 1, 1 - which)

        x1 = x1_ref[...]
        h = _dot(x1, wsgu_ref[...])
        f = h.shape[1] // 2
        shared = _dot(jax.nn.silu(h[:, :f]) * h[:, f:], wsdn_ref[...])

        _wait_slot_copies(loc, sems.at[which], tm)
        n_fixed = SLOT_ROWS // tm

        gate = gate_ref[...]
        pos = pos_ref[...]

        def gather_chunk(sc):
            slot = sc * tm + lax.broadcasted_iota(jnp.int32, (tm, tm), 1)
            w = jnp.zeros((tm, tm), jnp.float32)
            for k in range(MOE_TOPK):
                w = jnp.where(slot == pos[:, k:k + 1], gate[:, k:k + 1], w)
            return _dot(w, loc[sc * tm:(sc + 1) * tm, :])

        routed = shared
        for sc in range(n_fixed):
            routed = routed + gather_chunk(sc)
        y_ref[...] = routed
        overflow = _overflow_rows(i, cnt_ref, off_ref)
        for sc in range(n_fixed, _local_rows(tm) // tm):
            @pl.when((sc - n_fixed) * tm < overflow)
            def _(sc=sc):
                y_ref[...] += gather_chunk(sc)

        y_ref[...] = _layer_norm(alpha * x1 + y_ref[...], g2_ref[...], b2_ref[...])

    for which in range(2):
        pl.when(i % 2 == which)(functools.partial(combine_tile, which))


def _combine(x1, gate, pos, seg_cnt, seg_off, seg_dst, ys, w_sh_gu, w_sh_down, ln_g, ln_b, alpha):
    t, dm = x1.shape
    tm = min(MOE_TILE, t)
    const = lambda a: pl.BlockSpec(a.shape, lambda i, *_: (0,) * a.ndim, pipeline_mode=pl.Buffered(1))
    return pl.pallas_call(
        functools.partial(_combine_kernel, tm=tm, alpha=alpha, n_steps=t // tm),
        grid_spec=pltpu.PrefetchScalarGridSpec(
            num_scalar_prefetch=3,
            grid=(t // tm,),
            in_specs=[pl.BlockSpec((tm, dm), lambda i, *_: (i, 0)),
                      pl.BlockSpec((tm, SLOT_PAD), lambda i, *_: (i, 0)),
                      pl.BlockSpec((tm, SLOT_PAD), lambda i, *_: (i, 0)),
                      const(w_sh_gu), const(w_sh_down), const(ln_g), const(ln_b),
                      pl.BlockSpec(memory_space=pl.ANY)],
            out_specs=pl.BlockSpec((tm, dm), lambda i, *_: (i, 0)),
            scratch_shapes=[pltpu.VMEM((2, _local_rows(tm), dm), MXU_DTYPE), pltpu.SemaphoreType.DMA((2,))],
        ),
        out_shape=jax.ShapeDtypeStruct((t, dm), jnp.float32),
        compiler_params=_params("arbitrary"),
        name="moe_combine",
    )(seg_cnt, seg_off, seg_dst, x1, gate, pos, w_sh_gu, w_sh_down, ln_g, ln_b, ys)


def _moe(x1, gate, pos, stats, counts, w_exp_gu, w_exp_down, w_sh_gu, w_sh_down, ln_g, ln_b, alpha):
    t = x1.shape[0]
    bm = EXPERT_ROWS
    n_tiles = stats.shape[0]
    n_rows = (-(-(t * MOE_TOPK + n_tiles * N_EXPERTS * (SEG_ALIGN - 1) + SLOT_ROWS) // bm) + N_EXPERTS) * bm
    cnt = counts.reshape(N_EXPERTS).astype(jnp.int32)
    padded = (cnt + SEG_SLOT + bm - 1) // bm * bm
    end = jnp.cumsum(padded)
    start = end - padded
    n_used = end[-1:] // bm
    zlo = jnp.concatenate([(start + cnt) // SEG_ALIGN, n_used])
    zhi = jnp.concatenate([end // SEG_ALIGN, jnp.full((1,), n_rows // bm, jnp.int32)])
    block_start = jnp.arange(n_rows // bm, dtype=jnp.int32) * bm
    block_expert = jnp.minimum(jnp.sum((end[None, :] <= block_start[:, None]).astype(jnp.int32), axis=1),
                               N_EXPERTS - 1)
    seg = stats.astype(jnp.int32)
    seg_cnt, seg_off = seg[:, 1, :].reshape(-1), seg[:, 2, :].reshape(-1)
    seg_dst = (start[None, :] + seg[:, 0, :]).reshape(-1)
    pos_t = jnp.transpose(pos.reshape(n_tiles, t // n_tiles, SLOT_PAD), (0, 2, 1))
    xs = _dispatch(x1, pos_t, seg_cnt, seg_off, seg_dst, zlo, zhi, n_rows)
    ys = _experts(xs, block_expert, n_used, w_exp_gu, w_exp_down)
    return _combine(x1, gate, pos, seg_cnt, seg_off, seg_dst, ys, w_sh_gu, w_sh_down, ln_g, ln_b, alpha)


def _layer(x, pos0, past, p, alpha):
    (w_in, a_re, a_im, log_dt, b_re, b_im, c_re, c_im, ssm_d, w_glu, w_attn_proj, w_out, ln1_g, ln1_b,
     w_router, router_bias, w_exp_gu, w_exp_down, w_sh_gu, w_sh_down, ln2_g, ln2_b) = p
    nb, seq, dm = x.shape
    assert pos0 % CHUNK == 0 and seq % Q_BLOCK == 0 and Q_BLOCK == CHUNK and seq % S5_STEPS == 0
    assert nb * seq * MOE_TOPK < 2 ** 24
    t = nb * seq
    f32 = jnp.float32
    x2 = x.reshape(t, dm)
    pos = pos0 + jnp.arange(seq, dtype=jnp.int32)

    act_dtype = MXU_DTYPE if nb % 16 == 0 else f32
    u, q, iq, k, v, ik2, ik, iw = _in_proj(x2, _pack_w_in(w_in), pos, seq, act_dtype)

    a_cat, bcat, ccat = _s5_discretise(a_re, a_im, log_dt, b_re, b_im, c_re, c_im)
    if past is None:
        h0 = jnp.zeros((nb, 2 * NS), f32)
    else:
        h0 = jnp.concatenate([past[3].reshape(nb, NS), past[4].reshape(nb, NS)], axis=1).astype(f32)
    u_t = jnp.transpose(u.reshape(nb, seq, D_SSM), (1, 0, 2))
    ssm_t, h_t = _s5(u_t, h0, a_cat, bcat, ccat, ssm_d.reshape(1, D_SSM).astype(f32), w_glu.astype(MXU_DTYPE))
    ssm2 = jnp.transpose(ssm_t, (1, 0, 2)).reshape(t, dm)
    h_re = h_t[:, :NS].reshape(nb, N_GROUPS, N_STATE)
    h_im = h_t[:, NS:].reshape(nb, N_GROUPS, N_STATE)

    k3, v3, ik23 = (a.reshape(nb, seq, LANES) for a in (k, v, ik2))
    if past is not None:
        ck, cv, cik = past[0], past[1], past[2]
        n_past = ck.shape[1]
        k3 = jnp.concatenate([ck.reshape(nb, n_past, LANES), k3], axis=1)
        v3 = jnp.concatenate([cv.reshape(nb, n_past, LANES), v3], axis=1)
        ik23 = jnp.concatenate([jnp.concatenate([cik, cik], axis=-1), ik23], axis=1)
    attn = _dsa(q.reshape(nb, seq, D_ATTN), iq.reshape(nb, seq, IDX_HEADS * IDX_DIM), iw.reshape(nb, seq, LANES),
                k3, v3, ik23, pos0 // CHUNK)

    wap = jnp.transpose(w_attn_proj.reshape(N_KV_HEADS, KV_GROUP, HEAD_DIM, dm), (1, 0, 2, 3)).reshape(D_ATTN, dm)
    wg = w_in[:, w_in.shape[1] - 2 * dm:]
    row = lambda a: a.reshape(1, -1).astype(f32)
    x1, gate, pos_slot, stats, counts = _mix(
        x2, attn.reshape(t, D_ATTN), ssm2, wg.astype(MXU_DTYPE), wap.astype(MXU_DTYPE), w_out.astype(MXU_DTYPE),
        row(ln1_g), row(ln1_b), w_router.astype(f32), row(router_bias), alpha)

    y = _moe(x1, gate, pos_slot, stats, counts, w_exp_gu, w_exp_down,
             w_sh_gu.astype(MXU_DTYPE), w_sh_down.astype(MXU_DTYPE), row(ln2_g), row(ln2_b), alpha)
    state = (k.reshape(nb, seq, N_KV_HEADS, HEAD_DIM), v.reshape(nb, seq, N_KV_HEADS, HEAD_DIM),
             ik.reshape(nb, seq, IDX_DIM), h_re, h_im)
    return y.reshape(nb, seq, dm), state


def kernel(x_prompt, x_sample, cache_k, cache_v, cache_idx_k, state_ssm_re, state_ssm_im, w_in, ssm_a_re, ssm_a_im, ssm_log_dt, ssm_b_re, ssm_b_im, ssm_c_re, ssm_c_im, ssm_d, w_glu, w_attn_proj, w_out, ln1_g, ln1_b, w_router, router_bias, w_exp_gu, w_exp_down, w_sh_gu, w_sh_down, ln2_g, ln2_b):
    weights = (w_in, ssm_a_re, ssm_a_im, ssm_log_dt, ssm_b_re, ssm_b_im, ssm_c_re, ssm_c_im, ssm_d,
               w_glu, w_attn_proj, w_out, ln1_g, ln1_b, w_router, router_bias,
               w_exp_gu, w_exp_down, w_sh_gu, w_sh_down, ln2_g, ln2_b)
    depth = w_in.shape[0]
    alpha = (2 * depth) ** 0.25
    past_len = cache_k.shape[2]
    y_p, y_s = x_prompt, x_sample
    new_p, new_s = [], []
    for l in range(depth):
        p_l = tuple(w[l] for w in weights)
        y_p, st_p = _layer(y_p, 0, None, p_l, alpha)
        y_s, st_s = _layer(y_s, past_len,
                           (cache_k[l], cache_v[l], cache_idx_k[l], state_ssm_re[l], state_ssm_im[l]), p_l, alpha)
        new_p.append(st_p)
        new_s.append(st_s)
    k_p, v_p, ik_p, hre_p, him_p = [jnp.stack(a) for a in zip(*new_p)]
    k_s, v_s, ik_s, hre_s, him_s = [jnp.stack(a) for a in zip(*new_s)]
    return (y_p, y_s, k_p, v_p, ik_p, hre_p, him_p, k_s, v_s, ik_s, hre_s, him_s)
```

```python
import functools

import jax
import jax.numpy as jnp
from jax import lax
from jax.experimental import pallas as pl
from jax.experimental.pallas import tpu as pltpu

CHUNK = 64
D_SSM = 512
SSM_GROUP = 16
N_GROUPS = D_SSM // SSM_GROUP
N_STATE = 64
N_HEADS = 8
N_KV_HEADS = 2
HEAD_DIM = 64
D_ATTN = N_HEADS * HEAD_DIM
KV_GROUP = N_HEADS // N_KV_HEADS
ROPE_DIM = HEAD_DIM // 4
ROPE_THETA = 500000.0
IDX_HEADS = 8
IDX_DIM = 64
TOPK_KEYS = 256
Q_BLOCK = 64
N_EXPERTS = 64
MOE_TOPK = 6
D_EXPERT = 256
D_SHARED = 256
ROUTED_SCALE = 2.5
LN_EPS = 1e-5

LANES = 128
MXU_DIM = 256
KEY_CHUNK = 256
DSA_ROWS = 256
EXPERT_ROWS = 512
MOE_TILE = 512
SEG_ALIGN = 16
SEG_SLOT = 64
SLOT_ROWS = N_EXPERTS * SEG_SLOT
SLOT_PAD = 8
NS = N_GROUPS * N_STATE
VMEM_LIMIT = 56 * 1024 * 1024
NEG_BIG = -1e30
INT_MIN = -2147483648

MXU_DTYPE = jnp.bfloat16


def _dot(a, b):
    return jnp.dot(a.astype(MXU_DTYPE), b.astype(MXU_DTYPE), preferred_element_type=jnp.float32)


def _dot_nt(a, b):
    return lax.dot_general(a.astype(MXU_DTYPE), b.astype(MXU_DTYPE), (((1,), (1,)), ((), ())),
                           preferred_element_type=jnp.float32)


def _split(a):
    high = a.astype(MXU_DTYPE)
    return high, (a - high.astype(jnp.float32)).astype(MXU_DTYPE)


def _params(*sem):
    return pltpu.CompilerParams(dimension_semantics=sem, vmem_limit_bytes=VMEM_LIMIT)


def _const_spec(shape):
    return pl.BlockSpec(shape, lambda *_: (0,) * len(shape), pipeline_mode=pl.Buffered(1))


_C_U, _C_Q, _C_IQ, _C_K, _C_V, _C_IK, _C_IW, _C_END = 0, 512, 1024, 1536, 1664, 1792, 1920, 2048


def _inproj_kernel(x_ref, w_ref, cos_ref, sin_ref,
                   u_ref, q_ref, iq_ref, k_ref, v_ref, ik2_ref, ik_ref, iw_ref):
    z = _dot(x_ref[...], w_ref[...])
    cos = cos_ref[...]
    sin = sin_ref[...]
    lane = lax.broadcasted_iota(jnp.int32, cos.shape, 1)
    first_half = (lane % HEAD_DIM) < (ROPE_DIM // 2)

    def rope(zc):
        partner = jnp.where(first_half, pltpu.roll(zc, LANES - ROPE_DIM // 2, 1), pltpu.roll(zc, ROPE_DIM // 2, 1))
        return zc * cos + partner * sin

    u_ref[...] = z[:, _C_U:_C_Q].astype(u_ref.dtype)
    for c in range(4):
        q_ref[:, c * LANES:(c + 1) * LANES] = rope(z[:, _C_Q + c * LANES:_C_Q + (c + 1) * LANES])
        iq_ref[:, c * LANES:(c + 1) * LANES] = rope(z[:, _C_IQ + c * LANES:_C_IQ + (c + 1) * LANES])
    k_ref[...] = rope(z[:, _C_K:_C_V])
    v_ref[...] = z[:, _C_V:_C_IK]
    ik2 = rope(z[:, _C_IK:_C_IW])
    ik2_ref[...] = ik2
    ik_ref[...] = ik2[:, :IDX_DIM]
    iw_ref[...] = z[:, _C_IW:_C_END]


def _rope_tables(pos):
    half = ROPE_DIM // 2
    inv_freq = ROPE_THETA ** (-jnp.arange(half, dtype=jnp.float32) * 2.0 / ROPE_DIM)
    ang = pos.astype(jnp.float32)[:, None] * inv_freq
    cos, sin = jnp.cos(ang), jnp.sin(ang)
    n = pos.shape[0]
    pad = HEAD_DIM - ROPE_DIM
    cos_h = jnp.concatenate([cos, cos, jnp.ones((n, pad), jnp.float32)], axis=-1)
    sin_h = jnp.concatenate([-sin, sin, jnp.zeros((n, pad), jnp.float32)], axis=-1)
    return jnp.tile(cos_h, (1, LANES // HEAD_DIM)), jnp.tile(sin_h, (1, LANES // HEAD_DIM))


def _pack_w_in(w_in):
    s = [0, 512, 1024, 1152, 1280, 1792, 1856, 1864]
    w_u, w_q, w_k, w_v = w_in[:, s[0]:s[1]], w_in[:, s[1]:s[2]], w_in[:, s[2]:s[3]], w_in[:, s[3]:s[4]]
    w_iq, w_ik, w_iw = w_in[:, s[4]:s[5]], w_in[:, s[5]:s[6]], w_in[:, s[6]:s[7]]
    d = w_in.shape[0]
    w_qh = w_q.reshape(d, N_KV_HEADS, KV_GROUP, HEAD_DIM)
    w_qp = jnp.transpose(w_qh, (0, 2, 1, 3)).reshape(d, D_ATTN)
    w_iwp = jnp.concatenate([w_iw, jnp.zeros((d, LANES - IDX_HEADS), w_in.dtype)], axis=1)
    return jnp.concatenate([w_u, w_qp, w_iq, w_k, w_v, w_ik, w_ik, w_iwp], axis=1).astype(MXU_DTYPE)


def _in_proj(x2, w_pack, pos, seq):
    t, d = x2.shape
    tm = min(512, t)
    cos, sin = _rope_tables(pos)
    if seq >= tm:
        per = seq // tm
        tab_map = lambda i: (i % per, 0)
    else:
        cos, sin = jnp.tile(cos, (tm // seq, 1)), jnp.tile(sin, (tm // seq, 1))
        tab_map = lambda i: (0, 0)
    row = lambda w: pl.BlockSpec((tm, w), lambda i: (i, 0))
    widths = (D_SSM, D_ATTN, IDX_HEADS * IDX_DIM, LANES, LANES, LANES, IDX_DIM, LANES)
    return pl.pallas_call(
        _inproj_kernel,
        grid=(t // tm,),
        in_specs=[row(d), _const_spec(w_pack.shape),
                  pl.BlockSpec((tm, LANES), tab_map), pl.BlockSpec((tm, LANES), tab_map)],
        out_specs=[row(w) for w in widths],
        out_shape=[jax.ShapeDtypeStruct((t, w), jnp.float32) for w in widths],
        compiler_params=_params("parallel"),
        name="in_proj",
    )(x2, w_pack, cos, sin)


S5_STEPS = 16
S5_COLS = 512


def _s5_kernel(u_ref, h0_ref, a_ref, bcat_ref, ccat_ref, d_ref, wglu_ref,
               out_ref, ht_ref, h_scr, bu_scr, hall_scr, *, nb):
    @pl.when(pl.program_id(0) == 0)
    def _():
        h_scr[...] = h0_ref[...]

    u = jnp.swapaxes(u_ref[...], 0, 1).reshape(S5_STEPS * nb, D_SSM)
    per_tile = MXU_DIM // N_STATE
    for j in range(2 * NS // MXU_DIM):
        kb = ((j % (NS // MXU_DIM)) * per_tile * SSM_GROUP) // MXU_DIM * MXU_DIM
        cols = slice(j * MXU_DIM, (j + 1) * MXU_DIM)
        bu_scr[:, cols] = _dot(u[:, kb:kb + MXU_DIM], bcat_ref[kb:kb + MXU_DIM, cols])
    for cb in range(NS // S5_COLS):
        re = slice(cb * S5_COLS, (cb + 1) * S5_COLS)
        im = slice(NS + cb * S5_COLS, NS + (cb + 1) * S5_COLS)
        a_re, a_im = a_ref[0:1, re], a_ref[1:2, re]
        h_re, h_im = h_scr[:, re], h_scr[:, im]
        for t in range(S5_STEPS):
            rows = slice(t * nb, (t + 1) * nb)
            h_re, h_im = (a_re * h_re - a_im * h_im + bu_scr[rows, re],
                          a_re * h_im + a_im * h_re + bu_scr[rows, im])
            hall_scr[rows, re] = h_re
            hall_scr[rows, im] = h_im
        h_scr[:, re] = h_re
        h_scr[:, im] = h_im
    y_tiles = []
    for n in range(D_SSM // MXU_DIM):
        cols = slice(n * MXU_DIM, (n + 1) * MXU_DIM)
        k0, kn = n * MXU_DIM // SSM_GROUP * N_STATE, MXU_DIM // SSM_GROUP * N_STATE
        y_tiles.append(_dot(hall_scr[:, k0:k0 + kn], ccat_ref[k0:k0 + kn, cols])
                       + _dot(hall_scr[:, NS + k0:NS + k0 + kn], ccat_ref[NS + k0:NS + k0 + kn, cols]))
    y = jnp.concatenate(y_tiles, axis=1) + d_ref[...] * u
    y = jax.nn.gelu(y)
    g = _dot(y, wglu_ref[...])
    dm = g.shape[1] // 2
    out_ref[...] = jnp.swapaxes((g[:, :dm] * jax.nn.sigmoid(g[:, dm:])).reshape(S5_STEPS, nb, dm), 0, 1)
    ht_ref[...] = h_scr[...]


def _s5_discretise(a_re, a_im, log_dt, b_re, b_im, c_re, c_im):
    f32 = jnp.float32
    ar, ai = a_re.astype(f32), a_im.astype(f32)
    dt = jnp.exp(log_dt.astype(f32))[:, None]
    mag = jnp.exp(dt * ar)
    abar_re, abar_im = mag * jnp.cos(dt * ai), mag * jnp.sin(dt * ai)
    den = ar * ar + ai * ai
    num_re, num_im = abar_re - 1.0, abar_im
    coef_re = (num_re * ar + num_im * ai) / den
    coef_im = (num_im * ar - num_re * ai) / den
    br, bi = b_re.astype(f32), b_im.astype(f32)
    bbar_re = coef_re[..., None] * br - coef_im[..., None] * bi
    bbar_im = coef_re[..., None] * bi + coef_im[..., None] * br
    eye = jnp.eye(N_GROUPS, dtype=f32)
    bd = lambda m: jnp.einsum("gnc,gh->gchn", m, eye).reshape(D_SSM, NS)
    bcat = jnp.concatenate([bd(bbar_re), bd(bbar_im)], axis=1)
    cd = lambda m: jnp.einsum("gcn,gh->gnhc", m, eye).reshape(NS, D_SSM)
    ccat = jnp.concatenate([cd(c_re.astype(f32)), cd(-c_im.astype(f32))], axis=0)
    a_cat = jnp.stack([abar_re.reshape(NS), abar_im.reshape(NS)])
    return a_cat, bcat.astype(MXU_DTYPE), ccat.astype(MXU_DTYPE)


def _s5(u, h0, a_cat, bcat, ccat, d_skip, w_glu):
    nb, seq, _ = u.shape
    dm = w_glu.shape[1] // 2
    rows = S5_STEPS * nb
    return pl.pallas_call(
        functools.partial(_s5_kernel, nb=nb),
        grid=(seq // S5_STEPS,),
        in_specs=[pl.BlockSpec((nb, S5_STEPS, D_SSM), lambda i: (0, i, 0)),
                  _const_spec(h0.shape), _const_spec(a_cat.shape), _const_spec(bcat.shape),
                  _const_spec(ccat.shape), _const_spec(d_skip.shape), _const_spec(w_glu.shape)],
        out_specs=[pl.BlockSpec((nb, S5_STEPS, dm), lambda i: (0, i, 0)),
                   pl.BlockSpec(h0.shape, lambda i: (0, 0))],
        out_shape=[jax.ShapeDtypeStruct((nb, seq, dm), jnp.float32),
                   jax.ShapeDtypeStruct(h0.shape, jnp.float32)],
        scratch_shapes=[pltpu.VMEM(h0.shape, jnp.float32),
                        pltpu.VMEM((rows, 2 * NS), jnp.float32),
                        pltpu.VMEM((rows, 2 * NS), jnp.float32)],
        compiler_params=_params("arbitrary"),
        name="s5_scan_glu",
    )(u, h0, a_cat, bcat, ccat, d_skip, w_glu)


def _sortable(x):
    bits = pltpu.bitcast(x, jnp.int32)
    key = bits ^ ((bits >> 31) & jnp.int32(0x7FFFFFFF))
    return jnp.where(x == 0.0, jnp.int32(0), key)


def _half_mask(shape, upper):
    lane = lax.broadcasted_iota(jnp.int32, shape, 1)
    return (lane >= HEAD_DIM) if upper else (lane < HEAD_DIM)


def _dsa_kernel(q_ref, iq_ref, iw_ref, k_ref, v_ref, ik_ref, o_ref,
                kb, vs0, vs1, ikb, key_scr, keyt_scr, bias_scr, wb_scr, iqlhs_scr, qlhs_scr, s_scr, m_scr, acc_scr,
                *, qt, n_keys, n_sel, first_chunk, idx_bits):
    j = pl.program_id(1)
    kc = KEY_CHUNK
    n_sub = qt // Q_BLOCK
    qtp = max(qt, LANES)

    @pl.when(j == 0)
    def _():
        v = v_ref[...]
        lower = lax.broadcasted_iota(jnp.int32, v.shape, v.ndim - 1) < HEAD_DIM
        kb[...] = k_ref[...].astype(MXU_DTYPE)
        vs0[...] = jnp.where(lower, v, 1.0).astype(MXU_DTYPE)
        vs1[...] = jnp.where(lower, 1.0, v).astype(MXU_DTYPE)
        ikb[...] = ik_ref[...].astype(MXU_DTYPE)

    row_blk = lax.broadcasted_iota(jnp.int32, (qt, 1), 0) // Q_BLOCK
    n_vis = jnp.minimum((first_chunk + j * n_sub + row_blk + 1) * CHUNK, n_keys)
    n_ck = (jnp.minimum((first_chunk + (j + 1) * n_sub) * CHUNK, n_keys) + kc - 1) // kc
    lane_k = lax.broadcasted_iota(jnp.int32, (qt, kc), 1)

    iq = iq_ref[...] * IDX_DIM ** -0.5
    iw = iw_ref[...] * IDX_HEADS ** -0.5
    for h in range(IDX_HEADS):
        iqlhs_scr[h] = jnp.where(_half_mask((qt, LANES), h % 2 == 1),
                                 iq[:, (h // 2) * LANES:(h // 2 + 1) * LANES], 0.0).astype(MXU_DTYPE)
        wb_scr[h] = jnp.broadcast_to(iw[:, h:h + 1], (qt, LANES))

    def score_chunk(c, carry):
        ikc = ikb[c]
        score = None
        for h in range(IDX_HEADS):
            rel = jnp.maximum(_dot_nt(iqlhs_scr[h], ikc), 0.0)
            w = wb_scr[h]
            term = jnp.concatenate([rel[:, s * LANES:(s + 1) * LANES] * w for s in range(kc // LANES)], axis=1)
            score = term if score is None else score + term
        key = jnp.where(c * kc + lane_k < n_vis, _sortable(score), jnp.int32(INT_MIN))
        key_scr[c] = key
        if qtp > qt:
            key = jnp.concatenate([key, jnp.full((qtp - qt, kc), INT_MIN, jnp.int32)], axis=0)
        keyt_scr[c] = key.T
        return carry

    lax.fori_loop(0, n_ck, score_chunk, 0)

    sub_k = lax.broadcasted_iota(jnp.int32, (kc // 8, 8, qtp), 0) * 8 + lax.broadcasted_iota(
        jnp.int32, (kc // 8, 8, qtp), 1)

    def count(pred):
        def body(c, acc):
            hit = pred(keyt_scr[c].reshape(kc // 8, 8, qtp), c * kc + sub_k)
            return acc + jnp.sum(jnp.where(hit, 1.0, 0.0), axis=0)
        acc = lax.fori_loop(0, n_ck, body, jnp.zeros((8, qtp), jnp.float32))
        for shift in (4, 2, 1):
            acc = acc + pltpu.roll(acc, shift, 0)
        return acc

    def value_bit(i, t):
        cand = t + (jnp.int32(1) << (31 - i))
        return jnp.where(count(lambda key, idx: key >= cand[None]) >= n_sel, cand, t)

    t = lax.fori_loop(0, 32, value_bit, jnp.full((8, qtp), INT_MIN, jnp.int32))
    need = n_sel - count(lambda key, idx: key > t[None])
    n_ge = count(lambda key, idx: key >= t[None])

    def last_tied_index():
        def index_bit(i, m):
            cand = m + (jnp.int32(1) << (idx_bits - 1 - i))
            below = count(lambda key, idx: (key == t[None]) & (idx < cand[None]))
            return jnp.where(below < need, cand, m)
        return lax.fori_loop(0, idx_bits, index_bit, jnp.zeros((8, qtp), jnp.int32))

    m_idx = lax.cond(jnp.max(n_ge) > n_sel, last_tied_index,
                     lambda: jnp.full((8, qtp), 2 ** 30, jnp.int32))
    t_rep = jnp.broadcast_to(t[0:1], (LANES, qtp)).T[:qt]
    m_rep = jnp.broadcast_to(m_idx[0:1], (LANES, qtp)).T[:qt]
    t_row = jnp.concatenate([t_rep] * (kc // LANES), axis=1)
    m_row = jnp.concatenate([m_rep] * (kc // LANES), axis=1)

    def bias_chunk(c, carry):
        key = key_scr[c]
        idx = c * kc + lane_k
        sel = ((key > t_row) | ((key == t_row) & (idx <= m_row))) & (idx < n_vis)
        bias_scr[c] = jnp.where(sel, 0.0, NEG_BIG)
        return carry

    lax.fori_loop(0, n_ck, bias_chunk, 0)

    q = q_ref[...] * HEAD_DIM ** -0.5
    rows = KV_GROUP * qt
    for g in range(N_KV_HEADS):
        for r in range(KV_GROUP):
            qlhs_scr[g, r * qt:(r + 1) * qt] = jnp.where(
                _half_mask((qt, LANES), g == 1), q[:, r * LANES:(r + 1) * LANES], 0.0).astype(MXU_DTYPE)
    m_scr[...] = jnp.full(m_scr.shape, NEG_BIG, jnp.float32)
    acc_scr[...] = jnp.zeros(acc_scr.shape, jnp.float32)

    def logits_chunk(c, carry):
        bias = bias_scr[c][None]
        for g in range(N_KV_HEADS):
            s = (_dot_nt(qlhs_scr[g], kb[c]).reshape(KV_GROUP, qt, kc) + bias).reshape(rows, kc)
            s_scr[g, c] = s
            m = m_scr[g]
            for u in range(kc // LANES):
                m = jnp.maximum(m, s[:, u * LANES:(u + 1) * LANES])
            m_scr[g] = m
        return carry

    lax.fori_loop(0, n_ck, logits_chunk, 0)
    for g in range(N_KV_HEADS):
        m_scr[g] = jnp.broadcast_to(jnp.max(m_scr[g], axis=-1, keepdims=True), (rows, LANES))

    def pv_chunk(c, carry):
        for g, vs in enumerate((vs0, vs1)):
            m = m_scr[g]
            s = s_scr[g, c]
            p = jnp.concatenate([jnp.exp(s[:, u * LANES:(u + 1) * LANES] - m) for u in range(kc // LANES)], axis=1)
            acc_scr[g] += _dot(p, vs[c])
        return carry

    lax.fori_loop(0, n_ck, pv_chunk, 0)
    outs = [acc_scr[g] / pltpu.roll(acc_scr[g], HEAD_DIM, 1) for g in range(N_KV_HEADS)]
    lower = _half_mask((qt, LANES), False)
    for r in range(KV_GROUP):
        rs = slice(r * qt, (r + 1) * qt)
        o_ref[:, r * LANES:(r + 1) * LANES] = jnp.where(lower, outs[0][rs], outs[1][rs])


def _dsa(q, iq, iw, k_all, v_all, ik2_all, first_chunk):
    nb, seq, _ = q.shape
    n_keys = k_all.shape[1]
    n_sel = min(TOPK_KEYS, n_keys // 4)
    kc = KEY_CHUNK
    qt = min(DSA_ROWS, seq)
    assert seq % qt == 0
    n_ch = -(-n_keys // kc)
    pad = n_ch * kc - n_keys
    chunked = lambda a: jnp.pad(a, ((0, 0), (0, pad), (0, 0))).reshape(nb, n_ch, kc, LANES)
    qspec = lambda w: pl.BlockSpec((None, qt, w), lambda b, j: (b, j, 0))
    kspec = pl.BlockSpec((None, n_ch, kc, LANES), lambda b, j: (b, 0, 0, 0))
    rows = KV_GROUP * qt
    return pl.pallas_call(
        functools.partial(_dsa_kernel, qt=qt, n_keys=n_keys, n_sel=n_sel, first_chunk=first_chunk,
                          idx_bits=max(1, (n_ch * kc - 1).bit_length())),
        grid=(nb, seq // qt),
        in_specs=[qspec(D_ATTN), qspec(IDX_HEADS * IDX_DIM), qspec(LANES), kspec, kspec, kspec],
        out_specs=qspec(D_ATTN),
        out_shape=jax.ShapeDtypeStruct((nb, seq, D_ATTN), jnp.float32),
        scratch_shapes=[pltpu.VMEM((n_ch, kc, LANES), MXU_DTYPE)] * 4
        + [pltpu.VMEM((n_ch, qt, kc), jnp.int32), pltpu.VMEM((n_ch, kc, max(qt, LANES)), jnp.int32),
           pltpu.VMEM((n_ch, qt, kc), jnp.float32),
           pltpu.VMEM((IDX_HEADS, qt, LANES), jnp.float32), pltpu.VMEM((IDX_HEADS, qt, LANES), MXU_DTYPE),
           pltpu.VMEM((N_KV_HEADS, rows, LANES), MXU_DTYPE),
           pltpu.VMEM((N_KV_HEADS, n_ch, rows, kc), jnp.float32),
           pltpu.VMEM((N_KV_HEADS, rows, LANES), jnp.float32), pltpu.VMEM((N_KV_HEADS, rows, LANES), jnp.float32)],
        compiler_params=_params("arbitrary", "arbitrary"),
        name="dsa_attention",
    )(q, iq, iw, chunked(k_all), chunked(v_all), chunked(ik2_all))


def _layer_norm(h, g, b):
    mu = jnp.mean(h, axis=-1, keepdims=True)
    var = jnp.mean(jnp.square(h - mu), axis=-1, keepdims=True)
    return (h - mu) * lax.rsqrt(var + LN_EPS) * g + b


def _mix_kernel(x_ref, attn_ref, ssm_ref, wg_ref, wap_ref, wout_ref, g1_ref, b1_ref, wrh_ref, wrl_ref, rb_ref,
                x1_ref, gate_ref, pos_ref, stats_ref, cnt_ref, cnt_scr, *, alpha):
    i = pl.program_id(0)

    @pl.when(i == 0)
    def _():
        cnt_scr[...] = jnp.zeros_like(cnt_scr)

    tm, dm = x_ref.shape
    n_split = 2 if tm % 16 == 0 else 1
    hm = tm // n_split
    lane = lax.broadcasted_iota(jnp.int32, (hm, N_EXPERTS), 1).astype(jnp.float32)
    slot = lax.broadcasted_iota(jnp.int32, (hm, SLOT_PAD), 1)

    def route_rows(rows):
        x = x_ref[rows, :]
        gates = jax.nn.sigmoid(_dot(x, wg_ref[...]))
        attn_out = _dot(attn_ref[rows, :], wap_ref[...])
        mixed = _dot(gates[:, :dm] * ssm_ref[rows, :] + gates[:, dm:] * attn_out, wout_ref[...])
        x1 = _layer_norm(alpha * x + mixed, g1_ref[...], b1_ref[...])
        x1_ref[rows, :] = x1
        x1_hi, x1_lo = _split(x1)
        logits = _dot(x1_hi, wrh_ref[...]) + (_dot(x1_hi, wrl_ref[...]) + _dot(x1_lo, wrh_ref[...]))
        scores = jax.nn.sigmoid(logits)
        cur = scores + rb_ref[...]
        chosen = jnp.zeros_like(scores)
        picks = []
        for _ in range(MOE_TOPK):
            best = jnp.max(cur, axis=-1, keepdims=True)
            e_k = jnp.min(jnp.where(cur == best, lane, float(N_EXPERTS)), axis=-1, keepdims=True)
            hot = lane == e_k
            picks.append((hot, jnp.sum(jnp.where(hot, scores, 0.0), axis=-1, keepdims=True)))
            chosen = jnp.where(hot, 1.0, chosen)
            cur = jnp.where(hot, -jnp.inf, cur)
        total = picks[0][1]
        for _, s_k in picks[1:]:
            total = total + s_k
        return chosen, picks, total

    halves = [route_rows(slice(h * hm, (h + 1) * hm)) for h in range(n_split)]
    chosen = jnp.concatenate([h[0] for h in halves], axis=0)

    row = lax.broadcasted_iota(jnp.int32, (tm, tm), 0)
    col = lax.broadcasted_iota(jnp.int32, (tm, tm), 1)
    before = _dot(jnp.where(col < row, 1.0, 0.0), chosen)
    tile_cnt = jnp.ceil(jnp.sum(chosen, axis=0, keepdims=True) / SEG_ALIGN) * SEG_ALIGN
    e_row = lax.broadcasted_iota(jnp.int32, (N_EXPERTS, N_EXPERTS), 0)
    e_col = lax.broadcasted_iota(jnp.int32, (N_EXPERTS, N_EXPERTS), 1)
    overflow = jnp.maximum(tile_cnt - SEG_SLOT, 0.0)
    tile_off = jnp.dot(jnp.broadcast_to(overflow, (8, N_EXPERTS)), jnp.where(e_row < e_col, 1.0, 0.0),
                       preferred_element_type=jnp.float32, precision=lax.Precision.HIGHEST)[0:1]
    slot_base = lax.broadcasted_iota(jnp.int32, (1, N_EXPERTS), 1).astype(jnp.float32) * SEG_SLOT
    local = jnp.where(before < SEG_SLOT, slot_base + before, (SLOT_ROWS - SEG_SLOT) + tile_off + before)
    for h, (_, picks, total) in enumerate(halves):
        rows = slice(h * hm, (h + 1) * hm)
        g_out = jnp.zeros((hm, SLOT_PAD), jnp.float32)
        p_out = jnp.zeros((hm, SLOT_PAD), jnp.int32)
        for k, (hot, s_k) in enumerate(picks):
            pos_k = jnp.sum(jnp.where(hot, local[rows], 0.0), axis=-1, keepdims=True)
            g_out = jnp.where(slot == k, s_k / total * ROUTED_SCALE, g_out)
            p_out = jnp.where(slot == k, pos_k.astype(jnp.int32), p_out)
        gate_ref[rows, :] = g_out
        pos_ref[rows, :] = p_out
    srow = lax.broadcasted_iota(jnp.int32, (8, N_EXPERTS), 0)
    stats_ref[...] = jnp.where(srow == 0, cnt_scr[...], jnp.where(srow == 1, tile_cnt, jnp.where(
        srow == 2, tile_off, 0.0)))
    cnt_scr[...] = cnt_scr[...] + tile_cnt
    cnt_ref[...] = cnt_scr[...]


def _mix(x2, attn2, ssm2, wg, wap, wout, ln_g, ln_b, w_router_hi, w_router_lo, router_bias, alpha):
    t, dm = x2.shape
    tm = min(MOE_TILE, t)
    row = lambda w: pl.BlockSpec((tm, w), lambda i: (i, 0))
    consts = (wg, wap, wout, ln_g, ln_b, w_router_hi, w_router_lo, router_bias)
    return pl.pallas_call(
        functools.partial(_mix_kernel, alpha=alpha),
        grid=(t // tm,),
        in_specs=[row(dm), row(D_ATTN), row(dm)] + [_const_spec(c.shape) for c in consts],
        out_specs=[row(dm), row(SLOT_PAD), row(SLOT_PAD),
                   pl.BlockSpec((None, 8, N_EXPERTS), lambda i: (i, 0, 0)),
                   pl.BlockSpec((1, N_EXPERTS), lambda i: (0, 0))],
        out_shape=[jax.ShapeDtypeStruct((t, dm), jnp.float32),
                   jax.ShapeDtypeStruct((t, SLOT_PAD), jnp.float32),
                   jax.ShapeDtypeStruct((t, SLOT_PAD), jnp.int32),
                   jax.ShapeDtypeStruct((t // tm, 8, N_EXPERTS), jnp.float32),
                   jax.ShapeDtypeStruct((1, N_EXPERTS), jnp.float32)],
        scratch_shapes=[pltpu.VMEM((1, N_EXPERTS), jnp.float32)],
        compiler_params=_params("arbitrary"),
        name="mix_ln_router",
    )(x2, attn2, ssm2, *consts)


def _local_rows(tm):
    assert SLOT_ROWS % tm == 0
    return SLOT_ROWS + MOE_TOPK * tm


def _overflow_rows(i, cnt_ref, off_ref):
    last = i * N_EXPERTS + N_EXPERTS - 1
    return off_ref[last] + jnp.maximum(cnt_ref[last] - SEG_SLOT, 0)


def _segment_copies(i, cnt_ref, off_ref, dst_ref, tm, make_copy):
    def segment(e, carry):
        n = cnt_ref[i * N_EXPERTS + e]
        b = pl.multiple_of(dst_ref[i * N_EXPERTS + e], SEG_ALIGN)
        make_copy(pl.multiple_of(e * SEG_SLOT, SEG_SLOT), b, SEG_SLOT).start()

        @pl.when(n > SEG_SLOT)
        def _():
            a = SLOT_ROWS + off_ref[i * N_EXPERTS + e]
            m = n - SEG_SLOT
            for p in [tm >> s for s in range(tm.bit_length()) if (tm >> s) >= SEG_ALIGN]:
                done = (m // (2 * p)) * (2 * p)

                @pl.when((m & p) != 0)
                def _(p=p, done=done):
                    piece = make_copy(pl.multiple_of(a + done, SEG_ALIGN),
                                      pl.multiple_of(b + SEG_SLOT + done, SEG_ALIGN), p)
                    piece.start()
                    piece.wait()

        return carry

    lax.fori_loop(0, N_EXPERTS, segment, 0)


def _wait_slot_copies(buf, sem, tm):
    for _ in range(SLOT_ROWS // tm):
        pltpu.make_async_copy(buf.at[pl.ds(0, tm)], buf.at[pl.ds(0, tm)], sem).wait()


def _dispatch_kernel(cnt_ref, off_ref, dst_ref, zlo_ref, zhi_ref, post_ref, x_ref, xs_ref,
                     loc2, zero_scr, sems, *, tm, n_steps):
    i = pl.program_id(0)

    def sort_and_send(which):
        loc, sem = loc2.at[which], sems.at[which]

        xb = x_ref[...].astype(MXU_DTYPE)
        post = post_ref[...]

        def sort_chunk(sc):
            slot = sc * tm + lax.broadcasted_iota(jnp.int32, (tm, tm), 0)
            onehot = jnp.zeros((tm, tm), jnp.float32)
            for k in range(MOE_TOPK):
                onehot = jnp.where(slot == post[k:k + 1, :], 1.0, onehot)
            loc[sc * tm:(sc + 1) * tm, :] = _dot(onehot, xb).astype(loc.dtype)

        n_fixed = SLOT_ROWS // tm
        for sc in range(n_fixed):
            sort_chunk(sc)
        overflow = _overflow_rows(i, cnt_ref, off_ref)
        for sc in range(n_fixed, _local_rows(tm) // tm):
            pl.when((sc - n_fixed) * tm < overflow)(functools.partial(sort_chunk, sc))

        @pl.when(i >= 1)
        def _():
            _wait_slot_copies(loc2.at[1 - which], sems.at[1 - which], tm)

        _segment_copies(i, cnt_ref, off_ref, dst_ref, tm, lambda a, b, p: pltpu.make_async_copy(
            loc.at[pl.ds(a, p)], xs_ref.at[pl.ds(b, p)], sem))

        @pl.when(i == n_steps - 1)
        def _():
            _wait_slot_copies(loc, sem, tm)

    for which in range(2):
        pl.when(i % 2 == which)(functools.partial(sort_and_send, which))

    @pl.when(i == n_steps - 1)
    def _():
        sem = sems.at[0]
        zero_scr[...] = jnp.zeros_like(zero_scr)

        def zero_copy(group, rows):
            return pltpu.make_async_copy(
                zero_scr.at[pl.ds(0, rows)], xs_ref.at[pl.ds(pl.multiple_of(group * rows, rows), rows)], sem)

        def fill_groups(lo, hi, rows):
            def fill(g, c):
                zero_copy(g, rows).start()
                return c

            def fill_done(g, c):
                zero_copy(g, rows).wait()
                return c

            lax.fori_loop(lo, hi, fill, 0)
            lax.fori_loop(lo, hi, fill_done, 0)

        def segment(s, carry):
            fill_groups(zlo_ref[s], zhi_ref[s], SEG_ALIGN)
            return carry

        lax.fori_loop(0, N_EXPERTS, segment, 0)
        fill_groups(zlo_ref[N_EXPERTS], zhi_ref[N_EXPERTS], EXPERT_ROWS)


def _dispatch(x1, pos_t, seg_cnt, seg_off, seg_dst, zlo, zhi, n_rows):
    t, dm = x1.shape
    tm = min(MOE_TILE, t)
    return pl.pallas_call(
        functools.partial(_dispatch_kernel, tm=tm, n_steps=t // tm),
        grid_spec=pltpu.PrefetchScalarGridSpec(
            num_scalar_prefetch=5,
            grid=(t // tm,),
            in_specs=[pl.BlockSpec((None, SLOT_PAD, tm), lambda i, *_: (i, 0, 0)),
                      pl.BlockSpec((tm, dm), lambda i, *_: (i, 0))],
            out_specs=pl.BlockSpec(memory_space=pl.ANY),
            scratch_shapes=[pltpu.VMEM((2, _local_rows(tm), dm), MXU_DTYPE),
                            pltpu.VMEM((EXPERT_ROWS, dm), MXU_DTYPE), pltpu.SemaphoreType.DMA((2,))],
        ),
        out_shape=jax.ShapeDtypeStruct((n_rows, dm), MXU_DTYPE),
        compiler_params=_params("arbitrary"),
        name="moe_dispatch",
    )(seg_cnt, seg_off, seg_dst, zlo, zhi, pos_t, x1)


def _expert_kernel(be_ref, nu_ref, xs_ref, wgu_ref, wdn_ref, ys_ref, wgu_scr, wdn_scr):
    i = pl.program_id(0)

    @pl.when(i < nu_ref[0])
    def _():
        @pl.when((i == 0) | (be_ref[i] != be_ref[jnp.maximum(i - 1, 0)]))
        def _():
            wgu_scr[...] = wgu_ref[...].astype(MXU_DTYPE)
            wdn_scr[...] = wdn_ref[...].astype(MXU_DTYPE)

        h = _dot(xs_ref[...], wgu_scr[...])
        f = h.shape[1] // 2
        ys_ref[...] = _dot(jax.nn.silu(h[:, :f]) * h[:, f:], wdn_scr[...]).astype(ys_ref.dtype)

    @pl.when(i >= nu_ref[0])
    def _():
        ys_ref[...] = jnp.zeros_like(ys_ref)


def _experts(xs, block_expert, n_used, w_gu, w_down):
    n_rows, dm = xs.shape
    bm = EXPERT_ROWS
    f2 = w_gu.shape[2]
    used = lambda i, nu: jnp.minimum(i, nu[0] - 1)
    return pl.pallas_call(
        _expert_kernel,
        grid_spec=pltpu.PrefetchScalarGridSpec(
            num_scalar_prefetch=2,
            grid=(n_rows // bm,),
            in_specs=[pl.BlockSpec((bm, dm), lambda i, be, nu: (used(i, nu), 0)),
                      pl.BlockSpec((None, dm, f2), lambda i, be, nu: (be[used(i, nu)], 0, 0)),
                      pl.BlockSpec((None, f2 // 2, dm), lambda i, be, nu: (be[used(i, nu)], 0, 0))],
            out_specs=pl.BlockSpec((bm, dm), lambda i, be, nu: (i, 0)),
            scratch_shapes=[pltpu.VMEM((dm, f2), MXU_DTYPE), pltpu.VMEM((f2 // 2, dm), MXU_DTYPE)],
        ),
        out_shape=jax.ShapeDtypeStruct((n_rows, dm), MXU_DTYPE),
        compiler_params=_params("arbitrary"),
        name="moe_experts",
    )(block_expert, n_used, xs, w_gu, w_down)


def _combine_kernel(cnt_ref, off_ref, dst_ref, x1_ref, gate_ref, pos_ref, wsgu_ref, wsdn_ref, g2_ref, b2_ref,
                    ys_ref, y_ref, loc2, sems, *, tm, alpha, n_steps):
    i = pl.program_id(0)

    def fetch(tile, which):
        _segment_copies(tile, cnt_ref, off_ref, dst_ref, tm, lambda a, b, p: pltpu.make_async_copy(
            ys_ref.at[pl.ds(b, p)], loc2.at[which, pl.ds(a, p)], sems.at[which]))

    def combine_tile(which):
        loc = loc2.at[which]

        @pl.when(i == 0)
        def _():
            loc2[:, SLOT_ROWS:, :] = jnp.zeros((2, MOE_TOPK * tm, loc2.shape[2]), loc2.dtype)
            fetch(i, which)

        @pl.when(i + 1 < n_steps)
        def _():
            fetch(i + 1, 1 - which)

        x1 = x1_ref[...]
        h = _dot(x1, wsgu_ref[...])
        f = h.shape[1] // 2
        shared = _dot(jax.nn.silu(h[:, :f]) * h[:, f:], wsdn_ref[...])

        _wait_slot_copies(loc, sems.at[which], tm)
        n_fixed = SLOT_ROWS // tm

        gate = gate_ref[...]
        pos = pos_ref[...]

        def gather_chunk(sc):
            slot = sc * tm + lax.broadcasted_iota(jnp.int32, (tm, tm), 1)
            w = jnp.zeros((tm, tm), jnp.float32)
            for k in range(MOE_TOPK):
                w = jnp.where(slot == pos[:, k:k + 1], gate[:, k:k + 1], w)
            return _dot(w, loc[sc * tm:(sc + 1) * tm, :])

        routed = shared
        for sc in range(n_fixed):
            routed = routed + gather_chunk(sc)
        y_ref[...] = routed
        overflow = _overflow_rows(i, cnt_ref, off_ref)
        for sc in range(n_fixed, _local_rows(tm) // tm):
            @pl.when((sc - n_fixed) * tm < overflow)
            def _(sc=sc):
                y_ref[...] += gather_chunk(sc)

        y_ref[...] = _layer_norm(alpha * x1 + y_ref[...], g2_ref[...], b2_ref[...])

    for which in range(2):
        pl.when(i % 2 == which)(functools.partial(combine_tile, which))


def _combine(x1, gate, pos, seg_cnt, seg_off, seg_dst, ys, w_sh_gu, w_sh_down, ln_g, ln_b, alpha):
    t, dm = x1.shape
    tm = min(MOE_TILE, t)
    const = lambda a: pl.BlockSpec(a.shape, lambda i, *_: (0,) * a.ndim, pipeline_mode=pl.Buffered(1))
    return pl.pallas_call(
        functools.partial(_combine_kernel, tm=tm, alpha=alpha, n_steps=t // tm),
        grid_spec=pltpu.PrefetchScalarGridSpec(
            num_scalar_prefetch=3,
            grid=(t // tm,),
            in_specs=[pl.BlockSpec((tm, dm), lambda i, *_: (i, 0)),
                      pl.BlockSpec((tm, SLOT_PAD), lambda i, *_: (i, 0)),
                      pl.BlockSpec((tm, SLOT_PAD), lambda i, *_: (i, 0)),
                      const(w_sh_gu), const(w_sh_down), const(ln_g), const(ln_b),
                      pl.BlockSpec(memory_space=pl.ANY)],
            out_specs=pl.BlockSpec((tm, dm), lambda i, *_: (i, 0)),
            scratch_shapes=[pltpu.VMEM((2, _local_rows(tm), dm), MXU_DTYPE), pltpu.SemaphoreType.DMA((2,))],
        ),
        out_shape=jax.ShapeDtypeStruct((t, dm), jnp.float32),
        compiler_params=_params("arbitrary"),
        name="moe_combine",
    )(seg_cnt, seg_off, seg_dst, x1, gate, pos, w_sh_gu, w_sh_down, ln_g, ln_b, ys)


def _moe(x1, gate, pos, stats, counts, w_exp_gu, w_exp_down, w_sh_gu, w_sh_down, ln_g, ln_b, alpha):
    t = x1.shape[0]
    bm = EXPERT_ROWS
    n_tiles = stats.shape[0]
    n_rows = (-(-(t * MOE_TOPK + n_tiles * N_EXPERTS * (SEG_ALIGN - 1) + SLOT_ROWS) // bm) + N_EXPERTS) * bm
    cnt = counts.reshape(N_EXPERTS).astype(jnp.int32)
    padded = (cnt + SEG_SLOT + bm - 1) // bm * bm
    end = jnp.cumsum(padded)
    start = end - padded
    n_used = end[-1:] // bm
    zlo = jnp.concatenate([(start + cnt) // SEG_ALIGN, n_used])
    zhi = jnp.concatenate([end // SEG_ALIGN, jnp.full((1,), n_rows // bm, jnp.int32)])
    block_start = jnp.arange(n_rows // bm, dtype=jnp.int32) * bm
    block_expert = jnp.minimum(jnp.sum((end[None, :] <= block_start[:, None]).astype(jnp.int32), axis=1),
                               N_EXPERTS - 1)
    seg = stats.astype(jnp.int32)
    seg_cnt, seg_off = seg[:, 1, :].reshape(-1), seg[:, 2, :].reshape(-1)
    seg_dst = (start[None, :] + seg[:, 0, :]).reshape(-1)
    pos_t = jnp.transpose(pos.reshape(n_tiles, t // n_tiles, SLOT_PAD), (0, 2, 1))
    xs = _dispatch(x1, pos_t, seg_cnt, seg_off, seg_dst, zlo, zhi, n_rows)
    ys = _experts(xs, block_expert, n_used, w_exp_gu, w_exp_down)
    return _combine(x1, gate, pos, seg_cnt, seg_off, seg_dst, ys, w_sh_gu, w_sh_down, ln_g, ln_b, alpha)


def _layer(x, pos0, past, p, alpha):
    (w_in, a_re, a_im, log_dt, b_re, b_im, c_re, c_im, ssm_d, w_glu, w_attn_proj, w_out, ln1_g, ln1_b,
     w_router, router_bias, w_exp_gu, w_exp_down, w_sh_gu, w_sh_down, ln2_g, ln2_b) = p
    nb, seq, dm = x.shape
    assert pos0 % CHUNK == 0 and seq % Q_BLOCK == 0 and Q_BLOCK == CHUNK and seq % S5_STEPS == 0
    assert nb * seq * MOE_TOPK < 2 ** 24
    t = nb * seq
    f32 = jnp.float32
    x2 = x.reshape(t, dm)
    pos = pos0 + jnp.arange(seq, dtype=jnp.int32)

    u, q, iq, k, v, ik2, ik, iw = _in_proj(x2, _pack_w_in(w_in), pos, seq)

    a_cat, bcat, ccat = _s5_discretise(a_re, a_im, log_dt, b_re, b_im, c_re, c_im)
    if past is None:
        h0 = jnp.zeros((nb, 2 * NS), f32)
    else:
        h0 = jnp.concatenate([past[3].reshape(nb, NS), past[4].reshape(nb, NS)], axis=1).astype(f32)
    ssm, h_t = _s5(u.reshape(nb, seq, D_SSM), h0, a_cat, bcat, ccat, ssm_d.reshape(1, D_SSM).astype(f32),
                   w_glu.astype(MXU_DTYPE))
    ssm2 = ssm.reshape(t, dm)
    h_re = h_t[:, :NS].reshape(nb, N_GROUPS, N_STATE)
    h_im = h_t[:, NS:].reshape(nb, N_GROUPS, N_STATE)

    k3, v3, ik23 = (a.reshape(nb, seq, LANES) for a in (k, v, ik2))
    if past is not None:
        ck, cv, cik = past[0], past[1], past[2]
        n_past = ck.shape[1]
        k3 = jnp.concatenate([ck.reshape(nb, n_past, LANES), k3], axis=1)
        v3 = jnp.concatenate([cv.reshape(nb, n_past, LANES), v3], axis=1)
        ik23 = jnp.concatenate([jnp.concatenate([cik, cik], axis=-1), ik23], axis=1)
    attn = _dsa(q.reshape(nb, seq, D_ATTN), iq.reshape(nb, seq, IDX_HEADS * IDX_DIM), iw.reshape(nb, seq, LANES),
                k3, v3, ik23, pos0 // CHUNK)

    wap = jnp.transpose(w_attn_proj.reshape(N_KV_HEADS, KV_GROUP, HEAD_DIM, dm), (1, 0, 2, 3)).reshape(D_ATTN, dm)
    wg = w_in[:, w_in.shape[1] - 2 * dm:]
    row = lambda a: a.reshape(1, -1).astype(f32)
    x1, gate, pos_slot, stats, counts = _mix(
        x2, attn.reshape(t, D_ATTN), ssm2, wg.astype(MXU_DTYPE), wap.astype(MXU_DTYPE), w_out.astype(MXU_DTYPE),
        row(ln1_g), row(ln1_b), *_split(w_router.astype(f32)), row(router_bias), alpha)

    y = _moe(x1, gate, pos_slot, stats, counts, w_exp_gu, w_exp_down,
             w_sh_gu.astype(MXU_DTYPE), w_sh_down.astype(MXU_DTYPE), row(ln2_g), row(ln2_b), alpha)
    state = (k.reshape(nb, seq, N_KV_HEADS, HEAD_DIM), v.reshape(nb, seq, N_KV_HEADS, HEAD_DIM),
             ik.reshape(nb, seq, IDX_DIM), h_re, h_im)
    return y.reshape(nb, seq, dm), state


def kernel(x_prompt, x_sample, cache_k, cache_v, cache_idx_k, state_ssm_re, state_ssm_im, w_in, ssm_a_re, ssm_a_im, ssm_log_dt, ssm_b_re, ssm_b_im, ssm_c_re, ssm_c_im, ssm_d, w_glu, w_attn_proj, w_out, ln1_g, ln1_b, w_router, router_bias, w_exp_gu, w_exp_down, w_sh_gu, w_sh_down, ln2_g, ln2_b):
    weights = (w_in, ssm_a_re, ssm_a_im, ssm_log_dt, ssm_b_re, ssm_b_im, ssm_c_re, ssm_c_im, ssm_d,
               w_glu, w_attn_proj, w_out, ln1_g, ln1_b, w_router, router_bias,
               w_exp_gu, w_exp_down, w_sh_gu, w_sh_down, ln2_g, ln2_b)
    depth = w_in.shape[0]
    alpha = (2 * depth) ** 0.25
    past_len = cache_k.shape[2]
    y_p, y_s = x_prompt, x_sample
    new_p, new_s = [], []
    for l in range(depth):
        p_l = tuple(w[l] for w in weights)
        y_p, st_p = _layer(y_p, 0, None, p_l, alpha)
        y_s, st_s = _layer(y_s, past_len,
                           (cache_k[l], cache_v[l], cache_idx_k[l], state_ssm_re[l], state_ssm_im[l]), p_l, alpha)
        new_p.append(st_p)
        new_s.append(st_s)
    k_p, v_p, ik_p, hre_p, him_p = [jnp.stack(a) for a in zip(*new_p)]
    k_s, v_s, ik_s, hre_s, him_s = [jnp.stack(a) for a in zip(*new_s)]
    return (y_p, y_s, k_p, v_p, ik_p, hre_p, him_p, k_s, v_s, ik_s, hre_s, him_s)
```

```python
import functools

import jax
import jax.numpy as jnp
from jax import lax
from jax.experimental import pallas as pl
from jax.experimental.pallas import tpu as pltpu

CHUNK = 64
D_SSM = 512
SSM_GROUP = 16
N_GROUPS = D_SSM // SSM_GROUP
N_STATE = 64
N_HEADS = 8
N_KV_HEADS = 2
HEAD_DIM = 64
D_ATTN = N_HEADS * HEAD_DIM
KV_GROUP = N_HEADS // N_KV_HEADS
ROPE_DIM = HEAD_DIM // 4
ROPE_THETA = 500000.0
IDX_HEADS = 8
IDX_DIM = 64
TOPK_KEYS = 256
Q_BLOCK = 64
N_EXPERTS = 64
MOE_TOPK = 6
D_EXPERT = 256
D_SHARED = 256
ROUTED_SCALE = 2.5
LN_EPS = 1e-5

LANES = 128
MXU_DIM = 256
KEY_CHUNK = 256
DSA_ROWS = 256
EXPERT_ROWS = 512
MOE_TILE = 512
SEG_ALIGN = 16
SEG_SLOT = 64
SLOT_ROWS = N_EXPERTS * SEG_SLOT
SLOT_PAD = 8
NS = N_GROUPS * N_STATE
VMEM_LIMIT = 56 * 1024 * 1024
NEG_BIG = -1e30
INT_MIN = -2147483648

MXU_DTYPE = jnp.bfloat16


def _dot(a, b):
    return jnp.dot(a.astype(MXU_DTYPE), b.astype(MXU_DTYPE), preferred_element_type=jnp.float32)


def _dot_nt(a, b):
    return lax.dot_general(a.astype(MXU_DTYPE), b.astype(MXU_DTYPE), (((1,), (1,)), ((), ())),
                           preferred_element_type=jnp.float32)


def _split(a):
    high = a.astype(MXU_DTYPE)
    return high, (a - high.astype(jnp.float32)).astype(MXU_DTYPE)


def _params(*sem):
    return pltpu.CompilerParams(dimension_semantics=sem, vmem_limit_bytes=VMEM_LIMIT)


def _const_spec(shape):
    return pl.BlockSpec(shape, lambda *_: (0,) * len(shape), pipeline_mode=pl.Buffered(1))


_C_U, _C_Q, _C_IQ, _C_K, _C_V, _C_IK, _C_IW, _C_END = 0, 512, 1024, 1536, 1664, 1792, 1920, 2048


def _inproj_kernel(x_ref, w_ref, cos_ref, sin_ref,
                   u_ref, q_ref, iq_ref, k_ref, v_ref, ik2_ref, ik_ref, iw_ref, kh_ref, vh_ref):
    z = _dot(x_ref[...], w_ref[...])
    cos = cos_ref[...]
    sin = sin_ref[...]
    lane = lax.broadcasted_iota(jnp.int32, cos.shape, 1)
    first_half = (lane % HEAD_DIM) < (ROPE_DIM // 2)

    def rope(zc):
        partner = jnp.where(first_half, pltpu.roll(zc, LANES - ROPE_DIM // 2, 1), pltpu.roll(zc, ROPE_DIM // 2, 1))
        return zc * cos + partner * sin

    u_ref[...] = z[:, _C_U:_C_Q].astype(u_ref.dtype)
    for c in range(4):
        q_ref[:, c * LANES:(c + 1) * LANES] = rope(z[:, _C_Q + c * LANES:_C_Q + (c + 1) * LANES])
        iq_ref[:, c * LANES:(c + 1) * LANES] = rope(z[:, _C_IQ + c * LANES:_C_IQ + (c + 1) * LANES])
    k = rope(z[:, _C_K:_C_V])
    v = z[:, _C_V:_C_IK]
    k_ref[...] = k
    v_ref[...] = v
    for g in range(N_KV_HEADS):
        kh_ref[:, g, :] = k[:, g * HEAD_DIM:(g + 1) * HEAD_DIM]
        vh_ref[:, g, :] = v[:, g * HEAD_DIM:(g + 1) * HEAD_DIM]
    ik2 = rope(z[:, _C_IK:_C_IW])
    ik2_ref[...] = ik2
    ik_ref[...] = ik2[:, :IDX_DIM]
    iw_ref[...] = z[:, _C_IW:_C_END]


def _rope_tables(pos):
    half = ROPE_DIM // 2
    inv_freq = ROPE_THETA ** (-jnp.arange(half, dtype=jnp.float32) * 2.0 / ROPE_DIM)
    ang = pos.astype(jnp.float32)[:, None] * inv_freq
    cos, sin = jnp.cos(ang), jnp.sin(ang)
    n = pos.shape[0]
    pad = HEAD_DIM - ROPE_DIM
    cos_h = jnp.concatenate([cos, cos, jnp.ones((n, pad), jnp.float32)], axis=-1)
    sin_h = jnp.concatenate([-sin, sin, jnp.zeros((n, pad), jnp.float32)], axis=-1)
    return jnp.tile(cos_h, (1, LANES // HEAD_DIM)), jnp.tile(sin_h, (1, LANES // HEAD_DIM))


def _pack_w_in(w_in):
    s = [0, 512, 1024, 1152, 1280, 1792, 1856, 1864]
    w_u, w_q, w_k, w_v = w_in[:, s[0]:s[1]], w_in[:, s[1]:s[2]], w_in[:, s[2]:s[3]], w_in[:, s[3]:s[4]]
    w_iq, w_ik, w_iw = w_in[:, s[4]:s[5]], w_in[:, s[5]:s[6]], w_in[:, s[6]:s[7]]
    d = w_in.shape[0]
    w_qh = w_q.reshape(d, N_KV_HEADS, KV_GROUP, HEAD_DIM)
    w_qp = jnp.transpose(w_qh, (0, 2, 1, 3)).reshape(d, D_ATTN)
    w_iwp = jnp.concatenate([w_iw, jnp.zeros((d, LANES - IDX_HEADS), w_in.dtype)], axis=1)
    return jnp.concatenate([w_u, w_qp, w_iq, w_k, w_v, w_ik, w_ik, w_iwp], axis=1).astype(MXU_DTYPE)


def _in_proj(x2, w_pack, pos, seq):
    t, d = x2.shape
    tm = min(512, t)
    cos, sin = _rope_tables(pos)
    if seq >= tm:
        per = seq // tm
        tab_map = lambda i: (i % per, 0)
    else:
        cos, sin = jnp.tile(cos, (tm // seq, 1)), jnp.tile(sin, (tm // seq, 1))
        tab_map = lambda i: (0, 0)
    row = lambda w: pl.BlockSpec((tm, w), lambda i: (i, 0))
    widths = (D_SSM, D_ATTN, IDX_HEADS * IDX_DIM, LANES, LANES, LANES, IDX_DIM, LANES)
    heads = pl.BlockSpec((tm, N_KV_HEADS, HEAD_DIM), lambda i: (i, 0, 0))
    return pl.pallas_call(
        _inproj_kernel,
        grid=(t // tm,),
        in_specs=[row(d), _const_spec(w_pack.shape),
                  pl.BlockSpec((tm, LANES), tab_map), pl.BlockSpec((tm, LANES), tab_map)],
        out_specs=[row(w) for w in widths] + [heads, heads],
        out_shape=[jax.ShapeDtypeStruct((t, w), jnp.float32) for w in widths]
        + [jax.ShapeDtypeStruct((t, N_KV_HEADS, HEAD_DIM), jnp.float32)] * 2,
        compiler_params=_params("parallel"),
        name="in_proj",
    )(x2, w_pack, cos, sin)


S5_STEPS = 16
S5_COLS = 512


def _s5_kernel(u_ref, h0_ref, a_ref, bcat_ref, ccat_ref, d_ref, wglu_ref,
               out_ref, ht_ref, h_scr, bu_scr, hall_scr, *, nb):
    @pl.when(pl.program_id(0) == 0)
    def _():
        h_scr[...] = h0_ref[...]

    u = jnp.swapaxes(u_ref[...], 0, 1).reshape(S5_STEPS * nb, D_SSM)
    per_tile = MXU_DIM // N_STATE
    for j in range(2 * NS // MXU_DIM):
        kb = ((j % (NS // MXU_DIM)) * per_tile * SSM_GROUP) // MXU_DIM * MXU_DIM
        cols = slice(j * MXU_DIM, (j + 1) * MXU_DIM)
        bu_scr[:, cols] = _dot(u[:, kb:kb + MXU_DIM], bcat_ref[kb:kb + MXU_DIM, cols])
    for cb in range(NS // S5_COLS):
        re = slice(cb * S5_COLS, (cb + 1) * S5_COLS)
        im = slice(NS + cb * S5_COLS, NS + (cb + 1) * S5_COLS)
        a_re, a_im = a_ref[0:1, re], a_ref[1:2, re]
        h_re, h_im = h_scr[:, re], h_scr[:, im]
        for t in range(S5_STEPS):
            rows = slice(t * nb, (t + 1) * nb)
            h_re, h_im = (a_re * h_re - a_im * h_im + bu_scr[rows, re],
                          a_re * h_im + a_im * h_re + bu_scr[rows, im])
            hall_scr[rows, re] = h_re
            hall_scr[rows, im] = h_im
        h_scr[:, re] = h_re
        h_scr[:, im] = h_im
    y_tiles = []
    for n in range(D_SSM // MXU_DIM):
        cols = slice(n * MXU_DIM, (n + 1) * MXU_DIM)
        k0, kn = n * MXU_DIM // SSM_GROUP * N_STATE, MXU_DIM // SSM_GROUP * N_STATE
        y_tiles.append(_dot(hall_scr[:, k0:k0 + kn], ccat_ref[k0:k0 + kn, cols])
                       + _dot(hall_scr[:, NS + k0:NS + k0 + kn], ccat_ref[NS + k0:NS + k0 + kn, cols]))
    y = jnp.concatenate(y_tiles, axis=1) + d_ref[...] * u
    y = jax.nn.gelu(y)
    g = _dot(y, wglu_ref[...])
    dm = g.shape[1] // 2
    out_ref[...] = jnp.swapaxes((g[:, :dm] * jax.nn.sigmoid(g[:, dm:])).reshape(S5_STEPS, nb, dm), 0, 1)
    ht_ref[...] = h_scr[...]


def _s5_discretise(a_re, a_im, log_dt, b_re, b_im, c_re, c_im):
    f32 = jnp.float32
    ar, ai = a_re.astype(f32), a_im.astype(f32)
    dt = jnp.exp(log_dt.astype(f32))[:, None]
    mag = jnp.exp(dt * ar)
    abar_re, abar_im = mag * jnp.cos(dt * ai), mag * jnp.sin(dt * ai)
    den = ar * ar + ai * ai
    num_re, num_im = abar_re - 1.0, abar_im
    coef_re = (num_re * ar + num_im * ai) / den
    coef_im = (num_im * ar - num_re * ai) / den
    br, bi = b_re.astype(f32), b_im.astype(f32)
    bbar_re = coef_re[..., None] * br - coef_im[..., None] * bi
    bbar_im = coef_re[..., None] * bi + coef_im[..., None] * br
    eye = jnp.eye(N_GROUPS, dtype=f32)
    bd = lambda m: jnp.einsum("gnc,gh->gchn", m, eye).reshape(D_SSM, NS)
    bcat = jnp.concatenate([bd(bbar_re), bd(bbar_im)], axis=1)
    cd = lambda m: jnp.einsum("gcn,gh->gnhc", m, eye).reshape(NS, D_SSM)
    ccat = jnp.concatenate([cd(c_re.astype(f32)), cd(-c_im.astype(f32))], axis=0)
    a_cat = jnp.stack([abar_re.reshape(NS), abar_im.reshape(NS)])
    return a_cat, bcat.astype(MXU_DTYPE), ccat.astype(MXU_DTYPE)


def _s5(u, h0, a_cat, bcat, ccat, d_skip, w_glu):
    nb, seq, _ = u.shape
    dm = w_glu.shape[1] // 2
    rows = S5_STEPS * nb
    return pl.pallas_call(
        functools.partial(_s5_kernel, nb=nb),
        grid=(seq // S5_STEPS,),
        in_specs=[pl.BlockSpec((nb, S5_STEPS, D_SSM), lambda i: (0, i, 0)),
                  _const_spec(h0.shape), _const_spec(a_cat.shape), _const_spec(bcat.shape),
                  _const_spec(ccat.shape), _const_spec(d_skip.shape), _const_spec(w_glu.shape)],
        out_specs=[pl.BlockSpec((nb, S5_STEPS, dm), lambda i: (0, i, 0)),
                   pl.BlockSpec(h0.shape, lambda i: (0, 0))],
        out_shape=[jax.ShapeDtypeStruct((nb, seq, dm), jnp.float32),
                   jax.ShapeDtypeStruct(h0.shape, jnp.float32)],
        scratch_shapes=[pltpu.VMEM(h0.shape, jnp.float32),
                        pltpu.VMEM((rows, 2 * NS), jnp.float32),
                        pltpu.VMEM((rows, 2 * NS), jnp.float32)],
        compiler_params=_params("arbitrary"),
        name="s5_scan_glu",
    )(u, h0, a_cat, bcat, ccat, d_skip, w_glu)


KEY_NEG_INF = -2139095041


def _ordered_f32(key):
    bits = jnp.where(key >= 0, key, key ^ jnp.int32(0x7FFFFFFF))
    return jnp.where(key <= KEY_NEG_INF, -jnp.inf, pltpu.bitcast(bits, jnp.float32))


def _half_mask(shape, upper):
    lane = lax.broadcasted_iota(jnp.int32, shape, 1)
    return (lane >= HEAD_DIM) if upper else (lane < HEAD_DIM)


def _dsa_kernel(q_ref, iq_ref, iw_ref, k_ref, v_ref, ik_ref, o_ref,
                kb, vs0, vs1, ikb, sc_scr, sct_scr, bias_scr, wb_scr, iqlhs_scr, qlhs_scr, s_scr, m_scr, acc_scr,
                *, qt, n_tiles, n_keys, n_sel, first_chunk, idx_bits):
    j = pl.program_id(1)
    kc = KEY_CHUNK
    n_sub = qt // Q_BLOCK
    qtp = max(qt, LANES)

    @pl.when(j == 0)
    def _():
        v = v_ref[...]
        lower = lax.broadcasted_iota(jnp.int32, v.shape, v.ndim - 1) < HEAD_DIM
        kb[...] = k_ref[...].astype(MXU_DTYPE)
        vs0[...] = jnp.where(lower, v, 1.0).astype(MXU_DTYPE)
        vs1[...] = jnp.where(lower, 1.0, v).astype(MXU_DTYPE)
        ikb[...] = ik_ref[...].astype(MXU_DTYPE)

    row_blk = lax.broadcasted_iota(jnp.int32, (qt, 1), 0) // Q_BLOCK
    n_vis = jnp.minimum((first_chunk + j * n_sub + row_blk + 1) * CHUNK, n_keys)
    if n_tiles == 1:
        n_ck = (min((first_chunk + n_sub) * CHUNK, n_keys) + kc - 1) // kc
        chunks = functools.partial(lax.fori_loop, 0, n_ck, unroll=2)
    else:
        n_ck = (jnp.minimum((first_chunk + (j + 1) * n_sub) * CHUNK, n_keys) + kc - 1) // kc
        chunks = functools.partial(lax.fori_loop, 0, n_ck)
    lane_k = lax.broadcasted_iota(jnp.int32, (qt, kc), 1)

    iq = iq_ref[...] * IDX_DIM ** -0.5
    iw = iw_ref[...] * IDX_HEADS ** -0.5
    for h in range(IDX_HEADS):
        iqlhs_scr[h] = jnp.where(_half_mask((qt, LANES), h % 2 == 1),
                                 iq[:, (h // 2) * LANES:(h // 2 + 1) * LANES], 0.0).astype(MXU_DTYPE)
        wb_scr[h] = jnp.broadcast_to(iw[:, h:h + 1], (qt, LANES))

    def score_chunk(c, carry):
        ikc = ikb[c]
        score = None
        for h in range(IDX_HEADS):
            rel = jnp.maximum(_dot_nt(iqlhs_scr[h], ikc), 0.0)
            w = wb_scr[h]
            term = jnp.concatenate([rel[:, s * LANES:(s + 1) * LANES] * w for s in range(kc // LANES)], axis=1)
            score = term if score is None else score + term
        score = jnp.where(c * kc + lane_k < n_vis, score, -jnp.inf)
        sc_scr[c] = score
        if qtp > qt:
            score = jnp.concatenate([score, jnp.full((qtp - qt, kc), -jnp.inf, jnp.float32)], axis=0)
        sct_scr[c] = score.T
        return carry

    chunks(score_chunk, 0)

    sub_k = lax.broadcasted_iota(jnp.int32, (kc // 8, 8, qtp), 0) * 8 + lax.broadcasted_iota(
        jnp.int32, (kc // 8, 8, qtp), 1)

    def count(pred):
        def body(c, acc):
            hit = pred(sct_scr[c].reshape(kc // 8, 8, qtp), c * kc + sub_k)
            return acc + jnp.sum(jnp.where(hit, 1.0, 0.0), axis=0)
        acc = chunks(body, jnp.zeros((8, qtp), jnp.float32))
        for shift in (4, 2, 1):
            acc = acc + pltpu.roll(acc, shift, 0)
        return acc

    def value_bit(i, t_pos):
        cand = t_pos + (jnp.int32(1) << (31 - i))
        cand_value = _ordered_f32(cand)
        return jnp.where(count(lambda sc, idx: sc >= cand_value[None]) >= n_sel, cand, t_pos)

    t = _ordered_f32(lax.fori_loop(0, 32, value_bit, jnp.full((8, qtp), INT_MIN, jnp.int32)))
    need = n_sel - count(lambda sc, idx: sc > t[None])
    n_ge = count(lambda sc, idx: sc >= t[None])

    def last_tied_index():
        def index_bit(i, m):
            cand = m + (jnp.int32(1) << (idx_bits - 1 - i))
            below = count(lambda sc, idx: (sc == t[None]) & (idx < cand[None]))
            return jnp.where(below < need, cand, m)
        return lax.fori_loop(0, idx_bits, index_bit, jnp.zeros((8, qtp), jnp.int32))

    m_idx = lax.cond(jnp.max(n_ge) > n_sel, last_tied_index,
                     lambda: jnp.full((8, qtp), 2 ** 30, jnp.int32))
    t_rep = jnp.broadcast_to(t[0:1], (LANES, qtp)).T[:qt]
    m_rep = jnp.broadcast_to(m_idx[0:1], (LANES, qtp)).T[:qt]
    t_row = jnp.concatenate([t_rep] * (kc // LANES), axis=1)
    m_row = jnp.concatenate([m_rep] * (kc // LANES), axis=1)

    def bias_chunk(c, carry):
        sc = sc_scr[c]
        idx = c * kc + lane_k
        sel = ((sc > t_row) | ((sc == t_row) & (idx <= m_row))) & (idx < n_vis)
        bias_scr[c] = jnp.where(sel, 0.0, NEG_BIG)
        return carry

    chunks(bias_chunk, 0)

    q = q_ref[...] * HEAD_DIM ** -0.5
    rows = KV_GROUP * qt
    for g in range(N_KV_HEADS):
        for r in range(KV_GROUP):
            qlhs_scr[g, r * qt:(r + 1) * qt] = jnp.where(
                _half_mask((qt, LANES), g == 1), q[:, r * LANES:(r + 1) * LANES], 0.0).astype(MXU_DTYPE)
    m_scr[...] = jnp.full(m_scr.shape, NEG_BIG, jnp.float32)
    acc_scr[...] = jnp.zeros(acc_scr.shape, jnp.float32)

    def logits_chunk(c, carry):
        bias = bias_scr[c][None]
        for g in range(N_KV_HEADS):
            s = (_dot_nt(qlhs_scr[g], kb[c]).reshape(KV_GROUP, qt, kc) + bias).reshape(rows, kc)
            s_scr[g, c] = s
            m = m_scr[g]
            for u in range(kc // LANES):
                m = jnp.maximum(m, s[:, u * LANES:(u + 1) * LANES])
            m_scr[g] = m
        return carry

    chunks(logits_chunk, 0)
    for g in range(N_KV_HEADS):
        m_scr[g] = jnp.broadcast_to(jnp.max(m_scr[g], axis=-1, keepdims=True), (rows, LANES))

    def pv_chunk(c, carry):
        for g, vs in enumerate((vs0, vs1)):
            m = m_scr[g]
            s = s_scr[g, c]
            p = jnp.concatenate([jnp.exp(s[:, u * LANES:(u + 1) * LANES] - m) for u in range(kc // LANES)], axis=1)
            acc_scr[g] += _dot(p, vs[c])
        return carry

    chunks(pv_chunk, 0)
    outs = [acc_scr[g] / pltpu.roll(acc_scr[g], HEAD_DIM, 1) for g in range(N_KV_HEADS)]
    lower = _half_mask((qt, LANES), False)
    for r in range(KV_GROUP):
        rs = slice(r * qt, (r + 1) * qt)
        o_ref[:, r * LANES:(r + 1) * LANES] = jnp.where(lower, outs[0][rs], outs[1][rs])


def _dsa(q, iq, iw, k_all, v_all, ik2_all, first_chunk):
    nb, seq, _ = q.shape
    n_keys = k_all.shape[1]
    n_sel = min(TOPK_KEYS, n_keys // 4)
    kc = KEY_CHUNK
    qt = min(DSA_ROWS, seq)
    assert seq % qt == 0
    n_ch = -(-n_keys // kc)
    pad = n_ch * kc - n_keys
    chunked = lambda a: jnp.pad(a, ((0, 0), (0, pad), (0, 0))).reshape(nb, n_ch, kc, LANES)
    qspec = lambda w: pl.BlockSpec((None, qt, w), lambda b, j: (b, j, 0))
    kspec = pl.BlockSpec((None, n_ch, kc, LANES), lambda b, j: (b, 0, 0, 0))
    rows = KV_GROUP * qt
    return pl.pallas_call(
        functools.partial(_dsa_kernel, qt=qt, n_tiles=seq // qt, n_keys=n_keys, n_sel=n_sel, first_chunk=first_chunk,
                          idx_bits=max(1, (n_ch * kc - 1).bit_length())),
        grid=(nb, seq // qt),
        in_specs=[qspec(D_ATTN), qspec(IDX_HEADS * IDX_DIM), qspec(LANES), kspec, kspec, kspec],
        out_specs=qspec(D_ATTN),
        out_shape=jax.ShapeDtypeStruct((nb, seq, D_ATTN), jnp.float32),
        scratch_shapes=[pltpu.VMEM((n_ch, kc, LANES), MXU_DTYPE)] * 4
        + [pltpu.VMEM((n_ch, qt, kc), jnp.float32), pltpu.VMEM((n_ch, kc, max(qt, LANES)), jnp.float32),
           pltpu.VMEM((n_ch, qt, kc), jnp.float32),
           pltpu.VMEM((IDX_HEADS, qt, LANES), jnp.float32), pltpu.VMEM((IDX_HEADS, qt, LANES), MXU_DTYPE),
           pltpu.VMEM((N_KV_HEADS, rows, LANES), MXU_DTYPE),
           pltpu.VMEM((N_KV_HEADS, n_ch, rows, kc), jnp.float32),
           pltpu.VMEM((N_KV_HEADS, rows, LANES), jnp.float32), pltpu.VMEM((N_KV_HEADS, rows, LANES), jnp.float32)],
        compiler_params=_params("arbitrary", "arbitrary"),
        name="dsa_attention",
    )(q, iq, iw, chunked(k_all), chunked(v_all), chunked(ik2_all))


def _layer_norm(h, g, b):
    mu = jnp.mean(h, axis=-1, keepdims=True)
    var = jnp.mean(jnp.square(h - mu), axis=-1, keepdims=True)
    return (h - mu) * lax.rsqrt(var + LN_EPS) * g + b


def _mix_kernel(x_ref, attn_ref, ssm_ref, wg_ref, wap_ref, wout_ref, g1_ref, b1_ref, wrh_ref, wrl_ref, rb_ref,
                x1_ref, gate_ref, pos_ref, stats_ref, cnt_ref, cnt_scr, *, alpha):
    i = pl.program_id(0)

    @pl.when(i == 0)
    def _():
        cnt_scr[...] = jnp.zeros_like(cnt_scr)

    tm, dm = x_ref.shape
    n_split = 2 if tm % 16 == 0 else 1
    hm = tm // n_split
    lane = lax.broadcasted_iota(jnp.int32, (hm, N_EXPERTS), 1).astype(jnp.float32)
    slot = lax.broadcasted_iota(jnp.int32, (hm, SLOT_PAD), 1)

    def route_rows(rows):
        x = x_ref[rows, :]
        gates = jax.nn.sigmoid(_dot(x, wg_ref[...]))
        attn_out = _dot(attn_ref[rows, :], wap_ref[...])
        mixed = _dot(gates[:, :dm] * ssm_ref[rows, :] + gates[:, dm:] * attn_out, wout_ref[...])
        x1 = _layer_norm(alpha * x + mixed, g1_ref[...], b1_ref[...])
        x1_ref[rows, :] = x1
        x1_hi, x1_lo = _split(x1)
        logits = _dot(x1_hi, wrh_ref[...]) + (_dot(x1_hi, wrl_ref[...]) + _dot(x1_lo, wrh_ref[...]))
        scores = jax.nn.sigmoid(logits)
        cur = scores + rb_ref[...]
        chosen = jnp.zeros_like(scores)
        picks = []
        for _ in range(MOE_TOPK):
            best = jnp.max(cur, axis=-1, keepdims=True)
            e_k = jnp.min(jnp.where(cur == best, lane, float(N_EXPERTS)), axis=-1, keepdims=True)
            hot = lane == e_k
            picks.append((hot, jnp.sum(jnp.where(hot, scores, 0.0), axis=-1, keepdims=True)))
            chosen = jnp.where(hot, 1.0, chosen)
            cur = jnp.where(hot, -jnp.inf, cur)
        total = picks[0][1]
        for _, s_k in picks[1:]:
            total = total + s_k
        return chosen, picks, total

    halves = [route_rows(slice(h * hm, (h + 1) * hm)) for h in range(n_split)]
    chosen = jnp.concatenate([h[0] for h in halves], axis=0)

    row = lax.broadcasted_iota(jnp.int32, (tm, tm), 0)
    col = lax.broadcasted_iota(jnp.int32, (tm, tm), 1)
    before = _dot(jnp.where(col < row, 1.0, 0.0), chosen)
    tile_cnt = jnp.ceil(jnp.sum(chosen, axis=0, keepdims=True) / SEG_ALIGN) * SEG_ALIGN
    e_row = lax.broadcasted_iota(jnp.int32, (N_EXPERTS, N_EXPERTS), 0)
    e_col = lax.broadcasted_iota(jnp.int32, (N_EXPERTS, N_EXPERTS), 1)
    overflow = jnp.maximum(tile_cnt - SEG_SLOT, 0.0)
    tile_off = jnp.dot(jnp.broadcast_to(overflow, (8, N_EXPERTS)), jnp.where(e_row < e_col, 1.0, 0.0),
                       preferred_element_type=jnp.float32, precision=lax.Precision.HIGHEST)[0:1]
    slot_base = lax.broadcasted_iota(jnp.int32, (1, N_EXPERTS), 1).astype(jnp.float32) * SEG_SLOT
    local = jnp.where(before < SEG_SLOT, slot_base + before, (SLOT_ROWS - SEG_SLOT) + tile_off + before)
    for h, (_, picks, total) in enumerate(halves):
        rows = slice(h * hm, (h + 1) * hm)
        g_out = jnp.zeros((hm, SLOT_PAD), jnp.float32)
        p_out = jnp.zeros((hm, SLOT_PAD), jnp.int32)
        for k, (hot, s_k) in enumerate(picks):
            pos_k = jnp.sum(jnp.where(hot, local[rows], 0.0), axis=-1, keepdims=True)
            g_out = jnp.where(slot == k, s_k / total * ROUTED_SCALE, g_out)
            p_out = jnp.where(slot == k, pos_k.astype(jnp.int32), p_out)
        gate_ref[rows, :] = g_out
        pos_ref[rows, :] = p_out
    srow = lax.broadcasted_iota(jnp.int32, (8, N_EXPERTS), 0)
    stats_ref[...] = jnp.where(srow == 0, cnt_scr[...], jnp.where(srow == 1, tile_cnt, jnp.where(
        srow == 2, tile_off, 0.0)))
    cnt_scr[...] = cnt_scr[...] + tile_cnt
    cnt_ref[...] = cnt_scr[...]


def _mix(x2, attn2, ssm2, wg, wap, wout, ln_g, ln_b, w_router_hi, w_router_lo, router_bias, alpha):
    t, dm = x2.shape
    tm = min(MOE_TILE, t)
    row = lambda w: pl.BlockSpec((tm, w), lambda i: (i, 0))
    consts = (wg, wap, wout, ln_g, ln_b, w_router_hi, w_router_lo, router_bias)
    return pl.pallas_call(
        functools.partial(_mix_kernel, alpha=alpha),
        grid=(t // tm,),
        in_specs=[row(dm), row(D_ATTN), row(dm)] + [_const_spec(c.shape) for c in consts],
        out_specs=[row(dm), row(SLOT_PAD), row(SLOT_PAD),
                   pl.BlockSpec((None, 8, N_EXPERTS), lambda i: (i, 0, 0)),
                   pl.BlockSpec((1, N_EXPERTS), lambda i: (0, 0))],
        out_shape=[jax.ShapeDtypeStruct((t, dm), jnp.float32),
                   jax.ShapeDtypeStruct((t, SLOT_PAD), jnp.float32),
                   jax.ShapeDtypeStruct((t, SLOT_PAD), jnp.int32),
                   jax.ShapeDtypeStruct((t // tm, 8, N_EXPERTS), jnp.float32),
                   jax.ShapeDtypeStruct((1, N_EXPERTS), jnp.float32)],
        scratch_shapes=[pltpu.VMEM((1, N_EXPERTS), jnp.float32)],
        compiler_params=_params("arbitrary"),
        name="mix_ln_router",
    )(x2, attn2, ssm2, *consts)


def _local_rows(tm):
    assert SLOT_ROWS % tm == 0
    return SLOT_ROWS + MOE_TOPK * tm


def _overflow_rows(i, cnt_ref, off_ref):
    last = i * N_EXPERTS + N_EXPERTS - 1
    return off_ref[last] + jnp.maximum(cnt_ref[last] - SEG_SLOT, 0)


def _segment_copies(i, cnt_ref, off_ref, dst_ref, tm, make_copy):
    def segment(e, carry):
        n = cnt_ref[i * N_EXPERTS + e]
        b = pl.multiple_of(dst_ref[i * N_EXPERTS + e], SEG_ALIGN)
        make_copy(pl.multiple_of(e * SEG_SLOT, SEG_SLOT), b, SEG_SLOT).start()

        @pl.when(n > SEG_SLOT)
        def _():
            a = SLOT_ROWS + off_ref[i * N_EXPERTS + e]
            m = n - SEG_SLOT
            for p in [tm >> s for s in range(tm.bit_length()) if (tm >> s) >= SEG_ALIGN]:
                done = (m // (2 * p)) * (2 * p)

                @pl.when((m & p) != 0)
                def _(p=p, done=done):
                    piece = make_copy(pl.multiple_of(a + done, SEG_ALIGN),
                                      pl.multiple_of(b + SEG_SLOT + done, SEG_ALIGN), p)
                    piece.start()
                    piece.wait()

        return carry

    lax.fori_loop(0, N_EXPERTS, segment, 0)


def _wait_slot_copies(buf, sem, tm):
    for _ in range(SLOT_ROWS // tm):
        pltpu.make_async_copy(buf.at[pl.ds(0, tm)], buf.at[pl.ds(0, tm)], sem).wait()


def _dispatch_kernel(cnt_ref, off_ref, dst_ref, zlo_ref, zhi_ref, post_ref, x_ref, xs_ref,
                     loc2, zero_scr, sems, *, tm, n_steps):
    i = pl.program_id(0)

    def sort_and_send(which):
        loc, sem = loc2.at[which], sems.at[which]

        xb = x_ref[...].astype(MXU_DTYPE)
        post = post_ref[...]

        def sort_chunk(sc):
            slot = sc * tm + lax.broadcasted_iota(jnp.int32, (tm, tm), 0)
            onehot = jnp.zeros((tm, tm), jnp.float32)
            for k in range(MOE_TOPK):
                onehot = jnp.where(slot == post[k:k + 1, :], 1.0, onehot)
            loc[sc * tm:(sc + 1) * tm, :] = _dot(onehot, xb).astype(loc.dtype)

        n_fixed = SLOT_ROWS // tm
        for sc in range(n_fixed):
            sort_chunk(sc)
        overflow = _overflow_rows(i, cnt_ref, off_ref)
        for sc in range(n_fixed, _local_rows(tm) // tm):
            pl.when((sc - n_fixed) * tm < overflow)(functools.partial(sort_chunk, sc))

        @pl.when(i >= 1)
        def _():
            _wait_slot_copies(loc2.at[1 - which], sems.at[1 - which], tm)

        _segment_copies(i, cnt_ref, off_ref, dst_ref, tm, lambda a, b, p: pltpu.make_async_copy(
            loc.at[pl.ds(a, p)], xs_ref.at[pl.ds(b, p)], sem))

        @pl.when(i == n_steps - 1)
        def _():
            _wait_slot_copies(loc, sem, tm)

    for which in range(2):
        pl.when(i % 2 == which)(functools.partial(sort_and_send, which))

    @pl.when(i == n_steps - 1)
    def _():
        sem = sems.at[0]
        zero_scr[...] = jnp.zeros_like(zero_scr)

        def zero_copy(group, rows):
            return pltpu.make_async_copy(
                zero_scr.at[pl.ds(0, rows)], xs_ref.at[pl.ds(pl.multiple_of(group * rows, rows), rows)], sem)

        def fill_groups(lo, hi, rows):
            def fill(g, c):
                zero_copy(g, rows).start()
                return c

            def fill_done(g, c):
                zero_copy(g, rows).wait()
                return c

            lax.fori_loop(lo, hi, fill, 0)
            lax.fori_loop(lo, hi, fill_done, 0)

        def segment(s, carry):
            fill_groups(zlo_ref[s], zhi_ref[s], SEG_ALIGN)
            return carry

        lax.fori_loop(0, N_EXPERTS, segment, 0)
        fill_groups(zlo_ref[N_EXPERTS], zhi_ref[N_EXPERTS], EXPERT_ROWS)


def _dispatch(x1, pos_t, seg_cnt, seg_off, seg_dst, zlo, zhi, n_rows):
    t, dm = x1.shape
    tm = min(MOE_TILE, t)
    return pl.pallas_call(
        functools.partial(_dispatch_kernel, tm=tm, n_steps=t // tm),
        grid_spec=pltpu.PrefetchScalarGridSpec(
            num_scalar_prefetch=5,
            grid=(t // tm,),
            in_specs=[pl.BlockSpec((None, SLOT_PAD, tm), lambda i, *_: (i, 0, 0)),
                      pl.BlockSpec((tm, dm), lambda i, *_: (i, 0))],
            out_specs=pl.BlockSpec(memory_space=pl.ANY),
            scratch_shapes=[pltpu.VMEM((2, _local_rows(tm), dm), MXU_DTYPE),
                            pltpu.VMEM((EXPERT_ROWS, dm), MXU_DTYPE), pltpu.SemaphoreType.DMA((2,))],
        ),
        out_shape=jax.ShapeDtypeStruct((n_rows, dm), MXU_DTYPE),
        compiler_params=_params("arbitrary"),
        name="moe_dispatch",
    )(seg_cnt, seg_off, seg_dst, zlo, zhi, pos_t, x1)


def _expert_kernel(be_ref, nu_ref, xs_ref, wgu_ref, wdn_ref, ys_ref, wgu_scr, wdn_scr):
    i = pl.program_id(0)

    @pl.when(i < nu_ref[0])
    def _():
        @pl.when((i == 0) | (be_ref[i] != be_ref[jnp.maximum(i - 1, 0)]))
        def _():
            wgu_scr[...] = wgu_ref[...].astype(MXU_DTYPE)
            wdn_scr[...] = wdn_ref[...].astype(MXU_DTYPE)

        h = _dot(xs_ref[...], wgu_scr[...])
        f = h.shape[1] // 2
        ys_ref[...] = _dot(jax.nn.silu(h[:, :f]) * h[:, f:], wdn_scr[...]).astype(ys_ref.dtype)

    @pl.when(i >= nu_ref[0])
    def _():
        ys_ref[...] = jnp.zeros_like(ys_ref)


def _experts(xs, block_expert, n_used, w_gu, w_down):
    n_rows, dm = xs.shape
    bm = EXPERT_ROWS
    f2 = w_gu.shape[2]
    used = lambda i, nu: jnp.minimum(i, nu[0] - 1)
    return pl.pallas_call(
        _expert_kernel,
        grid_spec=pltpu.PrefetchScalarGridSpec(
            num_scalar_prefetch=2,
            grid=(n_rows // bm,),
            in_specs=[pl.BlockSpec((bm, dm), lambda i, be, nu: (used(i, nu), 0)),
                      pl.BlockSpec((None, dm, f2), lambda i, be, nu: (be[used(i, nu)], 0, 0)),
                      pl.BlockSpec((None, f2 // 2, dm), lambda i, be, nu: (be[used(i, nu)], 0, 0))],
            out_specs=pl.BlockSpec((bm, dm), lambda i, be, nu: (i, 0)),
            scratch_shapes=[pltpu.VMEM((dm, f2), MXU_DTYPE), pltpu.VMEM((f2 // 2, dm), MXU_DTYPE)],
        ),
        out_shape=jax.ShapeDtypeStruct((n_rows, dm), MXU_DTYPE),
        compiler_params=_params("arbitrary"),
        name="moe_experts",
    )(block_expert, n_used, xs, w_gu, w_down)


def _combine_kernel(cnt_ref, off_ref, dst_ref, x1_ref, gate_ref, pos_ref, wsgu_ref, wsdn_ref, g2_ref, b2_ref,
                    ys_ref, y_ref, loc2, sems, *, tm, alpha, n_steps):
    i = pl.program_id(0)

    def fetch(tile, which):
        _segment_copies(tile, cnt_ref, off_ref, dst_ref, tm, lambda a, b, p: pltpu.make_async_copy(
            ys_ref.at[pl.ds(b, p)], loc2.at[which, pl.ds(a, p)], sems.at[which]))

    def combine_tile(which):
        loc = loc2.at[which]

        @pl.when(i == 0)
        def _():
            loc2[:, SLOT_ROWS:, :] = jnp.zeros((2, MOE_TOPK * tm, loc2.shape[2]), loc2.dtype)
            fetch(i, which)

        @pl.when(i + 1 < n_steps)
        def _():
            fetch(i + 1, 1 - which)

        x1 = x1_ref[...]
        h = _dot(x1, wsgu_ref[...])
        f = h.shape[1] // 2
        shared = _dot(jax.nn.silu(h[:, :f]) * h[:, f:], wsdn_ref[...])

        _wait_slot_copies(loc, sems.at[which], tm)
        n_fixed = SLOT_ROWS // tm

        gate = gate_ref[...]
        pos = pos_ref[...]

        def gather_chunk(sc):
            slot = sc * tm + lax.broadcasted_iota(jnp.int32, (tm, tm), 1)
            w = jnp.zeros((tm, tm), jnp.float32)
            for k in range(MOE_TOPK):
                w = jnp.where(slot == pos[:, k:k + 1], gate[:, k:k + 1], w)
            return _dot(w, loc[sc * tm:(sc + 1) * tm, :])

        routed = shared
        for sc in range(n_fixed):
            routed = routed + gather_chunk(sc)
        y_ref[...] = routed
        overflow = _overflow_rows(i, cnt_ref, off_ref)
        for sc in range(n_fixed, _local_rows(tm) // tm):
            @pl.when((sc - n_fixed) * tm < overflow)
            def _(sc=sc):
                y_ref[...] += gather_chunk(sc)

        y_ref[...] = _layer_norm(alpha * x1 + y_ref[...], g2_ref[...], b2_ref[...])

    for which in range(2):
        pl.when(i % 2 == which)(functools.partial(combine_tile, which))


def _combine(x1, gate, pos, seg_cnt, seg_off, seg_dst, ys, w_sh_gu, w_sh_down, ln_g, ln_b, alpha):
    t, dm = x1.shape
    tm = min(MOE_TILE, t)
    const = lambda a: pl.BlockSpec(a.shape, lambda i, *_: (0,) * a.ndim, pipeline_mode=pl.Buffered(1))
    return pl.pallas_call(
        functools.partial(_combine_kernel, tm=tm, alpha=alpha, n_steps=t // tm),
        grid_spec=pltpu.PrefetchScalarGridSpec(
            num_scalar_prefetch=3,
            grid=(t // tm,),
            in_specs=[pl.BlockSpec((tm, dm), lambda i, *_: (i, 0)),
                      pl.BlockSpec((tm, SLOT_PAD), lambda i, *_: (i, 0)),
                      pl.BlockSpec((tm, SLOT_PAD), lambda i, *_: (i, 0)),
                      const(w_sh_gu), const(w_sh_down), const(ln_g), const(ln_b),
                      pl.BlockSpec(memory_space=pl.ANY)],
            out_specs=pl.BlockSpec((tm, dm), lambda i, *_: (i, 0)),
            scratch_shapes=[pltpu.VMEM((2, _local_rows(tm), dm), MXU_DTYPE), pltpu.SemaphoreType.DMA((2,))],
        ),
        out_shape=jax.ShapeDtypeStruct((t, dm), jnp.float32),
        compiler_params=_params("arbitrary"),
        name="moe_combine",
    )(seg_cnt, seg_off, seg_dst, x1, gate, pos, w_sh_gu, w_sh_down, ln_g, ln_b, ys)


def _moe(x1, gate, pos, stats, counts, w_exp_gu, w_exp_down, w_sh_gu, w_sh_down, ln_g, ln_b, alpha):
    t = x1.shape[0]
    bm = EXPERT_ROWS
    n_tiles = stats.shape[0]
    n_rows = (-(-(t * MOE_TOPK + n_tiles * N_EXPERTS * (SEG_ALIGN - 1) + SLOT_ROWS) // bm) + N_EXPERTS) * bm
    cnt = counts.reshape(N_EXPERTS).astype(jnp.int32)
    padded = (cnt + SEG_SLOT + bm - 1) // bm * bm
    end = jnp.cumsum(padded)
    start = end - padded
    n_used = end[-1:] // bm
    zlo = jnp.concatenate([(start + cnt) // SEG_ALIGN, n_used])
    zhi = jnp.concatenate([end // SEG_ALIGN, jnp.full((1,), n_rows // bm, jnp.int32)])
    block_start = jnp.arange(n_rows // bm, dtype=jnp.int32) * bm
    block_expert = jnp.minimum(jnp.sum((end[None, :] <= block_start[:, None]).astype(jnp.int32), axis=1),
                               N_EXPERTS - 1)
    seg = stats.astype(jnp.int32)
    seg_cnt, seg_off = seg[:, 1, :].reshape(-1), seg[:, 2, :].reshape(-1)
    seg_dst = (start[None, :] + seg[:, 0, :]).reshape(-1)
    pos_t = jnp.transpose(pos.reshape(n_tiles, t // n_tiles, SLOT_PAD), (0, 2, 1))
    xs = _dispatch(x1, pos_t, seg_cnt, seg_off, seg_dst, zlo, zhi, n_rows)
    ys = _experts(xs, block_expert, n_used, w_exp_gu, w_exp_down)
    return _combine(x1, gate, pos, seg_cnt, seg_off, seg_dst, ys, w_sh_gu, w_sh_down, ln_g, ln_b, alpha)


def _layer(x, pos0, past, p, alpha):
    (w_in, a_re, a_im, log_dt, b_re, b_im, c_re, c_im, ssm_d, w_glu, w_attn_proj, w_out, ln1_g, ln1_b,
     w_router, router_bias, w_exp_gu, w_exp_down, w_sh_gu, w_sh_down, ln2_g, ln2_b) = p
    nb, seq, dm = x.shape
    assert pos0 % CHUNK == 0 and seq % Q_BLOCK == 0 and Q_BLOCK == CHUNK and seq % S5_STEPS == 0
    assert nb * seq * MOE_TOPK < 2 ** 24
    t = nb * seq
    f32 = jnp.float32
    x2 = x.reshape(t, dm)
    pos = pos0 + jnp.arange(seq, dtype=jnp.int32)

    u, q, iq, k, v, ik2, ik, iw, k_heads, v_heads = _in_proj(x2, _pack_w_in(w_in), pos, seq)

    a_cat, bcat, ccat = _s5_discretise(a_re, a_im, log_dt, b_re, b_im, c_re, c_im)
    if past is None:
        h0 = jnp.zeros((nb, 2 * NS), f32)
    else:
        h0 = jnp.concatenate([past[3].reshape(nb, NS), past[4].reshape(nb, NS)], axis=1).astype(f32)
    ssm, h_t = _s5(u.reshape(nb, seq, D_SSM), h0, a_cat, bcat, ccat, ssm_d.reshape(1, D_SSM).astype(f32),
                   w_glu.astype(MXU_DTYPE))
    ssm2 = ssm.reshape(t, dm)
    h_re = h_t[:, :NS].reshape(nb, N_GROUPS, N_STATE)
    h_im = h_t[:, NS:].reshape(nb, N_GROUPS, N_STATE)

    k3, v3, ik23 = (a.reshape(nb, seq, LANES) for a in (k, v, ik2))
    if past is not None:
        ck, cv, cik = past[0], past[1], past[2]
        n_past = ck.shape[1]
        k3 = jnp.concatenate([ck.reshape(nb, n_past, LANES), k3], axis=1)
        v3 = jnp.concatenate([cv.reshape(nb, n_past, LANES), v3], axis=1)
        ik23 = jnp.concatenate([jnp.concatenate([cik, cik], axis=-1), ik23], axis=1)
    attn = _dsa(q.reshape(nb, seq, D_ATTN), iq.reshape(nb, seq, IDX_HEADS * IDX_DIM), iw.reshape(nb, seq, LANES),
                k3, v3, ik23, pos0 // CHUNK)

    wap = jnp.transpose(w_attn_proj.reshape(N_KV_HEADS, KV_GROUP, HEAD_DIM, dm), (1, 0, 2, 3)).reshape(D_ATTN, dm)
    wg = w_in[:, w_in.shape[1] - 2 * dm:]
    row = lambda a: a.reshape(1, -1).astype(f32)
    x1, gate, pos_slot, stats, counts = _mix(
        x2, attn.reshape(t, D_ATTN), ssm2, wg.astype(MXU_DTYPE), wap.astype(MXU_DTYPE), w_out.astype(MXU_DTYPE),
        row(ln1_g), row(ln1_b), *_split(w_router.astype(f32)), row(router_bias), alpha)

    y = _moe(x1, gate, pos_slot, stats, counts, w_exp_gu, w_exp_down,
             w_sh_gu.astype(MXU_DTYPE), w_sh_down.astype(MXU_DTYPE), row(ln2_g), row(ln2_b), alpha)
    state = (k_heads.reshape(nb, seq, N_KV_HEADS, HEAD_DIM), v_heads.reshape(nb, seq, N_KV_HEADS, HEAD_DIM),
             ik.reshape(nb, seq, IDX_DIM), h_re, h_im)
    return y.reshape(nb, seq, dm), state


def kernel(x_prompt, x_sample, cache_k, cache_v, cache_idx_k, state_ssm_re, state_ssm_im, w_in, ssm_a_re, ssm_a_im, ssm_log_dt, ssm_b_re, ssm_b_im, ssm_c_re, ssm_c_im, ssm_d, w_glu, w_attn_proj, w_out, ln1_g, ln1_b, w_router, router_bias, w_exp_gu, w_exp_down, w_sh_gu, w_sh_down, ln2_g, ln2_b):
    weights = (w_in, ssm_a_re, ssm_a_im, ssm_log_dt, ssm_b_re, ssm_b_im, ssm_c_re, ssm_c_im, ssm_d,
               w_glu, w_attn_proj, w_out, ln1_g, ln1_b, w_router, router_bias,
               w_exp_gu, w_exp_down, w_sh_gu, w_sh_down, ln2_g, ln2_b)
    depth = w_in.shape[0]
    alpha = (2 * depth) ** 0.25
    past_len = cache_k.shape[2]
    y_p, y_s = x_prompt, x_sample
    new_p, new_s = [], []
    for l in range(depth):
        p_l = tuple(w[l] for w in weights)
        y_p, st_p = _layer(y_p, 0, None, p_l, alpha)
        y_s, st_s = _layer(y_s, past_len,
                           (cache_k[l], cache_v[l], cache_idx_k[l], state_ssm_re[l], state_ssm_im[l]), p_l, alpha)
        new_p.append(st_p)
        new_s.append(st_s)
    k_p, v_p, ik_p, hre_p, him_p = [jnp.stack(a) for a in zip(*new_p)]
    k_s, v_s, ik_s, hre_s, him_s = [jnp.stack(a) for a in zip(*new_s)]
    return (y_p, y_s, k_p, v_p, ik_p, hre_p, him_p, k_s, v_s, ik_s, hre_s, him_s)
```

```python
import functools

import jax
import jax.numpy as jnp
from jax import lax
from jax.experimental import pallas as pl
from jax.experimental.pallas import tpu as pltpu

CHUNK = 64
D_SSM = 512
SSM_GROUP = 16
N_GROUPS = D_SSM // SSM_GROUP
N_STATE = 64
N_HEADS = 8
N_KV_HEADS = 2
HEAD_DIM = 64
D_ATTN = N_HEADS * HEAD_DIM
KV_GROUP = N_HEADS // N_KV_HEADS
ROPE_DIM = HEAD_DIM // 4
ROPE_THETA = 500000.0
IDX_HEADS = 8
IDX_DIM = 64
TOPK_KEYS = 256
Q_BLOCK = 64
N_EXPERTS = 64
MOE_TOPK = 6
D_EXPERT = 256
D_SHARED = 256
ROUTED_SCALE = 2.5
LN_EPS = 1e-5

LANES = 128
MXU_DIM = 256
KEY_CHUNK = 256
DSA_ROWS = 256
EXPERT_ROWS = 512
MOE_TILE = 512
SEG_ALIGN = 16
SEG_SLOT = 64
SLOT_ROWS = N_EXPERTS * SEG_SLOT
SLOT_PAD = 8
NS = N_GROUPS * N_STATE
VMEM_LIMIT = 56 * 1024 * 1024
NEG_BIG = -1e30
INT_MIN = -2147483648

MXU_DTYPE = jnp.bfloat16


def _dot(a, b):
    return jnp.dot(a.astype(MXU_DTYPE), b.astype(MXU_DTYPE), preferred_element_type=jnp.float32)


def _dot_nt(a, b):
    return lax.dot_general(a.astype(MXU_DTYPE), b.astype(MXU_DTYPE), (((1,), (1,)), ((), ())),
                           preferred_element_type=jnp.float32)


def _split(a):
    high = a.astype(MXU_DTYPE)
    return high, (a - high.astype(jnp.float32)).astype(MXU_DTYPE)


def _params(*sem):
    return pltpu.CompilerParams(dimension_semantics=sem, vmem_limit_bytes=VMEM_LIMIT)


def _const_spec(shape):
    return pl.BlockSpec(shape, lambda *_: (0,) * len(shape), pipeline_mode=pl.Buffered(1))


_C_U, _C_Q, _C_IQ, _C_K, _C_V, _C_IK, _C_IW, _C_END = 0, 512, 1024, 1536, 1664, 1792, 1920, 2048


def _inproj_kernel(x_ref, w_ref, cos_ref, sin_ref,
                   u_ref, q_ref, iq_ref, k_ref, v_ref, ik2_ref, ik_ref, iw_ref, kh_ref, vh_ref):
    z = _dot(x_ref[...], w_ref[...])
    cos = cos_ref[...]
    sin = sin_ref[...]
    lane = lax.broadcasted_iota(jnp.int32, cos.shape, 1)
    first_half = (lane % HEAD_DIM) < (ROPE_DIM // 2)

    def rope(zc):
        partner = jnp.where(first_half, pltpu.roll(zc, LANES - ROPE_DIM // 2, 1), pltpu.roll(zc, ROPE_DIM // 2, 1))
        return zc * cos + partner * sin

    u_ref[...] = z[:, _C_U:_C_Q]
    for c in range(4):
        q_ref[:, c * LANES:(c + 1) * LANES] = (
            rope(z[:, _C_Q + c * LANES:_C_Q + (c + 1) * LANES]) * HEAD_DIM ** -0.5).astype(q_ref.dtype)
        iq_ref[:, c * LANES:(c + 1) * LANES] = (
            rope(z[:, _C_IQ + c * LANES:_C_IQ + (c + 1) * LANES]) * IDX_DIM ** -0.5).astype(iq_ref.dtype)
    k = rope(z[:, _C_K:_C_V])
    v = z[:, _C_V:_C_IK]
    k_ref[...] = k
    v_ref[...] = v
    for g in range(N_KV_HEADS):
        kh_ref[:, g, :] = k[:, g * HEAD_DIM:(g + 1) * HEAD_DIM]
        vh_ref[:, g, :] = v[:, g * HEAD_DIM:(g + 1) * HEAD_DIM]
    ik2 = rope(z[:, _C_IK:_C_IW])
    ik2_ref[...] = ik2
    ik_ref[...] = ik2[:, :IDX_DIM]
    iw_ref[...] = z[:, _C_IW:_C_END]


def _rope_tables(pos):
    half = ROPE_DIM // 2
    inv_freq = ROPE_THETA ** (-jnp.arange(half, dtype=jnp.float32) * 2.0 / ROPE_DIM)
    ang = pos.astype(jnp.float32)[:, None] * inv_freq
    cos, sin = jnp.cos(ang), jnp.sin(ang)
    n = pos.shape[0]
    pad = HEAD_DIM - ROPE_DIM
    cos_h = jnp.concatenate([cos, cos, jnp.ones((n, pad), jnp.float32)], axis=-1)
    sin_h = jnp.concatenate([-sin, sin, jnp.zeros((n, pad), jnp.float32)], axis=-1)
    return jnp.tile(cos_h, (1, LANES // HEAD_DIM)), jnp.tile(sin_h, (1, LANES // HEAD_DIM))


def _pack_w_in(w_in):
    s = [0, 512, 1024, 1152, 1280, 1792, 1856, 1864]
    w_u, w_q, w_k, w_v = w_in[:, s[0]:s[1]], w_in[:, s[1]:s[2]], w_in[:, s[2]:s[3]], w_in[:, s[3]:s[4]]
    w_iq, w_ik, w_iw = w_in[:, s[4]:s[5]], w_in[:, s[5]:s[6]], w_in[:, s[6]:s[7]]
    d = w_in.shape[0]
    w_qh = w_q.reshape(d, N_KV_HEADS, KV_GROUP, HEAD_DIM)
    w_qp = jnp.transpose(w_qh, (0, 2, 1, 3)).reshape(d, D_ATTN)
    w_iwp = jnp.concatenate([w_iw, jnp.zeros((d, LANES - IDX_HEADS), w_in.dtype)], axis=1)
    return jnp.concatenate([w_u, w_qp, w_iq, w_k, w_v, w_ik, w_ik, w_iwp], axis=1).astype(MXU_DTYPE)


def _in_proj(x2, w_pack, pos, seq):
    t, d = x2.shape
    tm = min(512, t)
    cos, sin = _rope_tables(pos)
    if seq >= tm:
        per = seq // tm
        tab_map = lambda i: (i % per, 0)
    else:
        cos, sin = jnp.tile(cos, (tm // seq, 1)), jnp.tile(sin, (tm // seq, 1))
        tab_map = lambda i: (0, 0)
    row = lambda w: pl.BlockSpec((tm, w), lambda i: (i, 0))
    widths = (D_SSM, D_ATTN, IDX_HEADS * IDX_DIM, LANES, LANES, LANES, IDX_DIM, LANES)
    heads = pl.BlockSpec((tm, N_KV_HEADS, HEAD_DIM), lambda i: (i, 0, 0))
    return pl.pallas_call(
        _inproj_kernel,
        grid=(t // tm,),
        in_specs=[row(d), _const_spec(w_pack.shape),
                  pl.BlockSpec((tm, LANES), tab_map), pl.BlockSpec((tm, LANES), tab_map)],
        out_specs=[row(w) for w in widths] + [heads, heads],
        out_shape=[jax.ShapeDtypeStruct((t, w), MXU_DTYPE if i in (1, 2) else jnp.float32)
                   for i, w in enumerate(widths)]
        + [jax.ShapeDtypeStruct((t, N_KV_HEADS, HEAD_DIM), jnp.float32)] * 2,
        compiler_params=_params("parallel"),
        name="in_proj",
    )(x2, w_pack, cos, sin)


S5_STEPS = 16
S5_COLS = 512


def _s5_kernel(u_ref, h0_ref, a_ref, bcat_ref, ccat_ref, d_ref, wglu_ref,
               out_ref, ht_ref, h_scr, bu_scr, hall_scr, *, nb):
    @pl.when(pl.program_id(0) == 0)
    def _():
        h_scr[...] = h0_ref[...]

    u = jnp.swapaxes(u_ref[...], 0, 1).reshape(S5_STEPS * nb, D_SSM)
    per_tile = MXU_DIM // N_STATE
    for j in range(2 * NS // MXU_DIM):
        kb = ((j % (NS // MXU_DIM)) * per_tile * SSM_GROUP) // MXU_DIM * MXU_DIM
        cols = slice(j * MXU_DIM, (j + 1) * MXU_DIM)
        bu_scr[:, cols] = _dot(u[:, kb:kb + MXU_DIM], bcat_ref[kb:kb + MXU_DIM, cols])
    for cb in range(NS // S5_COLS):
        re = slice(cb * S5_COLS, (cb + 1) * S5_COLS)
        im = slice(NS + cb * S5_COLS, NS + (cb + 1) * S5_COLS)
        a_re, a_im = a_ref[0:1, re], a_ref[1:2, re]
        h_re, h_im = h_scr[:, re], h_scr[:, im]
        for t in range(S5_STEPS):
            rows = slice(t * nb, (t + 1) * nb)
            h_re, h_im = (a_re * h_re - a_im * h_im + bu_scr[rows, re],
                          a_re * h_im + a_im * h_re + bu_scr[rows, im])
            hall_scr[rows, re] = h_re
            hall_scr[rows, im] = h_im
        h_scr[:, re] = h_re
        h_scr[:, im] = h_im
    y_tiles = []
    for n in range(D_SSM // MXU_DIM):
        cols = slice(n * MXU_DIM, (n + 1) * MXU_DIM)
        k0, kn = n * MXU_DIM // SSM_GROUP * N_STATE, MXU_DIM // SSM_GROUP * N_STATE
        y_tiles.append(_dot(hall_scr[:, k0:k0 + kn], ccat_ref[k0:k0 + kn, cols])
                       + _dot(hall_scr[:, NS + k0:NS + k0 + kn], ccat_ref[NS + k0:NS + k0 + kn, cols]))
    y = jnp.concatenate(y_tiles, axis=1) + d_ref[...] * u
    y = jax.nn.gelu(y)
    g = _dot(y, wglu_ref[...])
    dm = g.shape[1] // 2
    out_ref[...] = jnp.swapaxes((g[:, :dm] * jax.nn.sigmoid(g[:, dm:])).reshape(S5_STEPS, nb, dm), 0, 1)
    ht_ref[...] = h_scr[...]


def _s5_discretise(a_re, a_im, log_dt, b_re, b_im, c_re, c_im):
    f32 = jnp.float32
    ar, ai = a_re.astype(f32), a_im.astype(f32)
    dt = jnp.exp(log_dt.astype(f32))[:, None]
    mag = jnp.exp(dt * ar)
    abar_re, abar_im = mag * jnp.cos(dt * ai), mag * jnp.sin(dt * ai)
    den = ar * ar + ai * ai
    num_re, num_im = abar_re - 1.0, abar_im
    coef_re = (num_re * ar + num_im * ai) / den
    coef_im = (num_im * ar - num_re * ai) / den
    br, bi = b_re.astype(f32), b_im.astype(f32)
    bbar_re = coef_re[..., None] * br - coef_im[..., None] * bi
    bbar_im = coef_re[..., None] * bi + coef_im[..., None] * br
    eye = jnp.eye(N_GROUPS, dtype=f32)
    bd = lambda m: jnp.einsum("gnc,gh->gchn", m, eye).reshape(D_SSM, NS)
    bcat = jnp.concatenate([bd(bbar_re), bd(bbar_im)], axis=1)
    cd = lambda m: jnp.einsum("gcn,gh->gnhc", m, eye).reshape(NS, D_SSM)
    ccat = jnp.concatenate([cd(c_re.astype(f32)), cd(-c_im.astype(f32))], axis=0)
    a_cat = jnp.stack([abar_re.reshape(NS), abar_im.reshape(NS)])
    return a_cat, bcat.astype(MXU_DTYPE), ccat.astype(MXU_DTYPE)


def _s5(u, h0, a_cat, bcat, ccat, d_skip, w_glu):
    nb, seq, _ = u.shape
    dm = w_glu.shape[1] // 2
    rows = S5_STEPS * nb
    return pl.pallas_call(
        functools.partial(_s5_kernel, nb=nb),
        grid=(seq // S5_STEPS,),
        in_specs=[pl.BlockSpec((nb, S5_STEPS, D_SSM), lambda i: (0, i, 0)),
                  _const_spec(h0.shape), _const_spec(a_cat.shape), _const_spec(bcat.shape),
                  _const_spec(ccat.shape), _const_spec(d_skip.shape), _const_spec(w_glu.shape)],
        out_specs=[pl.BlockSpec((nb, S5_STEPS, dm), lambda i: (0, i, 0)),
                   pl.BlockSpec(h0.shape, lambda i: (0, 0))],
        out_shape=[jax.ShapeDtypeStruct((nb, seq, dm), jnp.float32),
                   jax.ShapeDtypeStruct(h0.shape, jnp.float32)],
        scratch_shapes=[pltpu.VMEM(h0.shape, jnp.float32),
                        pltpu.VMEM((rows, 2 * NS), jnp.float32),
                        pltpu.VMEM((rows, 2 * NS), jnp.float32)],
        compiler_params=_params("arbitrary"),
        name="s5_scan_glu",
    )(u, h0, a_cat, bcat, ccat, d_skip, w_glu)


KEY_NEG_INF = -2139095041


def _ordered_f32(key):
    bits = jnp.where(key >= 0, key, key ^ jnp.int32(0x7FFFFFFF))
    return jnp.where(key <= KEY_NEG_INF, -jnp.inf, pltpu.bitcast(bits, jnp.float32))


def _half_mask(shape, upper):
    lane = lax.broadcasted_iota(jnp.int32, shape, 1)
    return (lane >= HEAD_DIM) if upper else (lane < HEAD_DIM)


def _dsa_kernel(q_ref, iq_ref, iw_ref, k_ref, v_ref, ik_ref, o_ref,
                kb, vs0, vs1, ikb, sc_scr, sct_scr, bias_scr, wb_scr, iqlhs_scr, qlhs_scr, s_scr, m_scr, acc_scr,
                *, qt, n_tiles, n_keys, n_sel, first_chunk, idx_bits):
    j = pl.program_id(1)
    kc = KEY_CHUNK
    n_sub = qt // Q_BLOCK
    qtp = max(qt, LANES)

    @pl.when(j == 0)
    def _():
        v = v_ref[...]
        lower = lax.broadcasted_iota(jnp.int32, v.shape, v.ndim - 1) < HEAD_DIM
        kb[...] = k_ref[...].astype(MXU_DTYPE)
        vs0[...] = jnp.where(lower, v, 1.0).astype(MXU_DTYPE)
        vs1[...] = jnp.where(lower, 1.0, v).astype(MXU_DTYPE)
        ikb[...] = ik_ref[...].astype(MXU_DTYPE)

    row_blk = lax.broadcasted_iota(jnp.int32, (qt, 1), 0) // Q_BLOCK
    n_vis = jnp.minimum((first_chunk + j * n_sub + row_blk + 1) * CHUNK, n_keys)
    if n_tiles == 1:
        n_ck = (min((first_chunk + n_sub) * CHUNK, n_keys) + kc - 1) // kc
        chunks = functools.partial(lax.fori_loop, 0, n_ck, unroll=2)
    else:
        n_ck = (jnp.minimum((first_chunk + (j + 1) * n_sub) * CHUNK, n_keys) + kc - 1) // kc
        chunks = functools.partial(lax.fori_loop, 0, n_ck)
    lane_k = lax.broadcasted_iota(jnp.int32, (qt, kc), 1)

    iq = iq_ref[...]
    iw = iw_ref[...] * IDX_HEADS ** -0.5
    for h in range(IDX_HEADS):
        iqlhs_scr[h] = jnp.where(_half_mask((qt, LANES), h % 2 == 1),
                                 iq[:, (h // 2) * LANES:(h // 2 + 1) * LANES], 0.0)
        wb_scr[h] = jnp.broadcast_to(iw[:, h:h + 1], (qt, LANES))

    def score_chunk(c, carry):
        ikc = ikb[c]
        score = None
        for h in range(IDX_HEADS):
            rel = jnp.maximum(_dot_nt(iqlhs_scr[h], ikc), 0.0)
            w = wb_scr[h]
            term = jnp.concatenate([rel[:, s * LANES:(s + 1) * LANES] * w for s in range(kc // LANES)], axis=1)
            score = term if score is None else score + term
        score = jnp.where(c * kc + lane_k < n_vis, score, -jnp.inf)
        sc_scr[c] = score
        if qtp > qt:
            score = jnp.concatenate([score, jnp.full((qtp - qt, kc), -jnp.inf, jnp.float32)], axis=0)
        sct_scr[c] = score.T
        return carry

    chunks(score_chunk, 0)

    sub_k = lax.broadcasted_iota(jnp.int32, (kc // 8, 8, qtp), 0) * 8 + lax.broadcasted_iota(
        jnp.int32, (kc // 8, 8, qtp), 1)

    def count(pred):
        def body(c, acc):
            hit = pred(sct_scr[c].reshape(kc // 8, 8, qtp), c * kc + sub_k)
            return acc + jnp.sum(jnp.where(hit, 1.0, 0.0), axis=0)
        acc = chunks(body, jnp.zeros((8, qtp), jnp.float32))
        for shift in (4, 2, 1):
            acc = acc + pltpu.roll(acc, shift, 0)
        return acc

    def value_bit(i, t_pos):
        cand = t_pos + (jnp.int32(1) << (31 - i))
        cand_value = _ordered_f32(cand)
        return jnp.where(count(lambda sc, idx: sc >= cand_value[None]) >= n_sel, cand, t_pos)

    t = _ordered_f32(lax.fori_loop(0, 32, value_bit, jnp.full((8, qtp), INT_MIN, jnp.int32)))
    need = n_sel - count(lambda sc, idx: sc > t[None])
    n_ge = count(lambda sc, idx: sc >= t[None])

    def last_tied_index():
        def index_bit(i, m):
            cand = m + (jnp.int32(1) << (idx_bits - 1 - i))
            below = count(lambda sc, idx: (sc == t[None]) & (idx < cand[None]))
            return jnp.where(below < need, cand, m)
        return lax.fori_loop(0, idx_bits, index_bit, jnp.zeros((8, qtp), jnp.int32))

    m_idx = lax.cond(jnp.max(n_ge) > n_sel, last_tied_index,
                     lambda: jnp.full((8, qtp), 2 ** 30, jnp.int32))
    t_rep = jnp.broadcast_to(t[0:1], (LANES, qtp)).T[:qt]
    m_rep = jnp.broadcast_to(m_idx[0:1], (LANES, qtp)).T[:qt]
    t_row = jnp.concatenate([t_rep] * (kc // LANES), axis=1)
    m_row = jnp.concatenate([m_rep] * (kc // LANES), axis=1)

    def bias_chunk(c, carry):
        sc = sc_scr[c]
        idx = c * kc + lane_k
        sel = ((sc > t_row) | ((sc == t_row) & (idx <= m_row))) & (idx < n_vis)
        bias_scr[c] = jnp.where(sel, 0.0, NEG_BIG)
        return carry

    chunks(bias_chunk, 0)

    q = q_ref[...]
    rows = KV_GROUP * qt
    for g in range(N_KV_HEADS):
        for r in range(KV_GROUP):
            qlhs_scr[g, r * qt:(r + 1) * qt] = jnp.where(
                _half_mask((qt, LANES), g == 1), q[:, r * LANES:(r + 1) * LANES], 0.0)
    m_scr[...] = jnp.full(m_scr.shape, NEG_BIG, jnp.float32)
    acc_scr[...] = jnp.zeros(acc_scr.shape, jnp.float32)

    def logits_chunk(c, carry):
        bias = bias_scr[c][None]
        for g in range(N_KV_HEADS):
            s = (_dot_nt(qlhs_scr[g], kb[c]).reshape(KV_GROUP, qt, kc) + bias).reshape(rows, kc)
            s_scr[g, c] = s
            m = m_scr[g]
            for u in range(kc // LANES):
                m = jnp.maximum(m, s[:, u * LANES:(u + 1) * LANES])
            m_scr[g] = m
        return carry

    chunks(logits_chunk, 0)
    for g in range(N_KV_HEADS):
        m_scr[g] = jnp.broadcast_to(jnp.max(m_scr[g], axis=-1, keepdims=True), (rows, LANES))

    def pv_chunk(c, carry):
        for g, vs in enumerate((vs0, vs1)):
            m = m_scr[g]
            s = s_scr[g, c]
            p = jnp.concatenate([jnp.exp(s[:, u * LANES:(u + 1) * LANES] - m) for u in range(kc // LANES)], axis=1)
            acc_scr[g] += _dot(p, vs[c])
        return carry

    chunks(pv_chunk, 0)
    outs = [acc_scr[g] / pltpu.roll(acc_scr[g], HEAD_DIM, 1) for g in range(N_KV_HEADS)]
    lower = _half_mask((qt, LANES), False)
    for r in range(KV_GROUP):
        rs = slice(r * qt, (r + 1) * qt)
        o_ref[:, r * LANES:(r + 1) * LANES] = jnp.where(lower, outs[0][rs], outs[1][rs])


def _dsa(q, iq, iw, k_all, v_all, ik2_all, first_chunk):
    nb, seq, _ = q.shape
    n_keys = k_all.shape[1]
    n_sel = min(TOPK_KEYS, n_keys // 4)
    kc = KEY_CHUNK
    qt = min(DSA_ROWS, seq)
    assert seq % qt == 0
    n_ch = -(-n_keys // kc)
    pad = n_ch * kc - n_keys
    chunked = lambda a: jnp.pad(a, ((0, 0), (0, pad), (0, 0))).reshape(nb, n_ch, kc, LANES)
    qspec = lambda w: pl.BlockSpec((None, qt, w), lambda b, j: (b, j, 0))
    kspec = pl.BlockSpec((None, n_ch, kc, LANES), lambda b, j: (b, 0, 0, 0))
    rows = KV_GROUP * qt
    return pl.pallas_call(
        functools.partial(_dsa_kernel, qt=qt, n_tiles=seq // qt, n_keys=n_keys, n_sel=n_sel, first_chunk=first_chunk,
                          idx_bits=max(1, (n_ch * kc - 1).bit_length())),
        grid=(nb, seq // qt),
        in_specs=[qspec(D_ATTN), qspec(IDX_HEADS * IDX_DIM), qspec(LANES), kspec, kspec, kspec],
        out_specs=qspec(D_ATTN),
        out_shape=jax.ShapeDtypeStruct((nb, seq, D_ATTN), jnp.float32),
        scratch_shapes=[pltpu.VMEM((n_ch, kc, LANES), MXU_DTYPE)] * 4
        + [pltpu.VMEM((n_ch, qt, kc), jnp.float32), pltpu.VMEM((n_ch, kc, max(qt, LANES)), jnp.float32),
           pltpu.VMEM((n_ch, qt, kc), jnp.float32),
           pltpu.VMEM((IDX_HEADS, qt, LANES), jnp.float32), pltpu.VMEM((IDX_HEADS, qt, LANES), MXU_DTYPE),
           pltpu.VMEM((N_KV_HEADS, rows, LANES), MXU_DTYPE),
           pltpu.VMEM((N_KV_HEADS, n_ch, rows, kc), jnp.float32),
           pltpu.VMEM((N_KV_HEADS, rows, LANES), jnp.float32), pltpu.VMEM((N_KV_HEADS, rows, LANES), jnp.float32)],
        compiler_params=_params("arbitrary", "arbitrary"),
        name="dsa_attention",
    )(q, iq, iw, chunked(k_all), chunked(v_all), chunked(ik2_all))


def _layer_norm(h, g, b):
    mu = jnp.mean(h, axis=-1, keepdims=True)
    var = jnp.mean(jnp.square(h - mu), axis=-1, keepdims=True)
    return (h - mu) * lax.rsqrt(var + LN_EPS) * g + b


def _mix_kernel(x_ref, attn_ref, ssm_ref, wg_ref, wap_ref, wout_ref, g1_ref, b1_ref, wrh_ref, wrl_ref, rb_ref,
                x1_ref, gate_ref, pos_ref, stats_ref, cnt_ref, cnt_scr, *, alpha):
    i = pl.program_id(0)

    @pl.when(i == 0)
    def _():
        cnt_scr[...] = jnp.zeros_like(cnt_scr)

    tm, dm = x_ref.shape
    n_split = 2 if tm % 16 == 0 else 1
    hm = tm // n_split
    lane = lax.broadcasted_iota(jnp.int32, (hm, N_EXPERTS), 1).astype(jnp.float32)
    slot = lax.broadcasted_iota(jnp.int32, (hm, SLOT_PAD), 1)

    def route_rows(rows):
        x = x_ref[rows, :]
        gates = jax.nn.sigmoid(_dot(x, wg_ref[...]))
        attn_out = _dot(attn_ref[rows, :], wap_ref[...])
        mixed = _dot(gates[:, :dm] * ssm_ref[rows, :] + gates[:, dm:] * attn_out, wout_ref[...])
        x1 = _layer_norm(alpha * x + mixed, g1_ref[...], b1_ref[...])
        x1_ref[rows, :] = x1
        x1_hi, x1_lo = _split(x1)
        logits = _dot(x1_hi, wrh_ref[...]) + (_dot(x1_hi, wrl_ref[...]) + _dot(x1_lo, wrh_ref[...]))
        scores = jax.nn.sigmoid(logits)
        cur = scores + rb_ref[...]
        chosen = jnp.zeros_like(scores)
        picks = []
        for _ in range(MOE_TOPK):
            best = jnp.max(cur, axis=-1, keepdims=True)
            e_k = jnp.min(jnp.where(cur == best, lane, float(N_EXPERTS)), axis=-1, keepdims=True)
            hot = lane == e_k
            picks.append((hot, jnp.sum(jnp.where(hot, scores, 0.0), axis=-1, keepdims=True)))
            chosen = jnp.where(hot, 1.0, chosen)
            cur = jnp.where(hot, -jnp.inf, cur)
        total = picks[0][1]
        for _, s_k in picks[1:]:
            total = total + s_k
        return chosen, picks, total

    halves = [route_rows(slice(h * hm, (h + 1) * hm)) for h in range(n_split)]
    chosen = jnp.concatenate([h[0] for h in halves], axis=0)

    row = lax.broadcasted_iota(jnp.int32, (tm, tm), 0)
    col = lax.broadcasted_iota(jnp.int32, (tm, tm), 1)
    before = _dot(jnp.where(col < row, 1.0, 0.0), chosen)
    tile_cnt = jnp.ceil(jnp.sum(chosen, axis=0, keepdims=True) / SEG_ALIGN) * SEG_ALIGN
    e_row = lax.broadcasted_iota(jnp.int32, (N_EXPERTS, N_EXPERTS), 0)
    e_col = lax.broadcasted_iota(jnp.int32, (N_EXPERTS, N_EXPERTS), 1)
    overflow = jnp.maximum(tile_cnt - SEG_SLOT, 0.0)
    tile_off = jnp.dot(jnp.broadcast_to(overflow, (8, N_EXPERTS)), jnp.where(e_row < e_col, 1.0, 0.0),
                       preferred_element_type=jnp.float32, precision=lax.Precision.HIGHEST)[0:1]
    slot_base = lax.broadcasted_iota(jnp.int32, (1, N_EXPERTS), 1).astype(jnp.float32) * SEG_SLOT
    local = jnp.where(before < SEG_SLOT, slot_base + before, (SLOT_ROWS - SEG_SLOT) + tile_off + before)
    for h, (_, picks, total) in enumerate(halves):
        rows = slice(h * hm, (h + 1) * hm)
        g_out = jnp.zeros((hm, SLOT_PAD), jnp.float32)
        p_out = jnp.zeros((hm, SLOT_PAD), jnp.int32)
        for k, (hot, s_k) in enumerate(picks):
            pos_k = jnp.sum(jnp.where(hot, local[rows], 0.0), axis=-1, keepdims=True)
            g_out = jnp.where(slot == k, s_k / total * ROUTED_SCALE, g_out)
            p_out = jnp.where(slot == k, pos_k.astype(jnp.int32), p_out)
        gate_ref[rows, :] = g_out
        pos_ref[rows, :] = p_out
    srow = lax.broadcasted_iota(jnp.int32, (8, N_EXPERTS), 0)
    stats_ref[...] = jnp.where(srow == 0, cnt_scr[...], jnp.where(srow == 1, tile_cnt, jnp.where(
        srow == 2, tile_off, 0.0)))
    cnt_scr[...] = cnt_scr[...] + tile_cnt
    cnt_ref[...] = cnt_scr[...]


def _mix(x2, attn2, ssm2, wg, wap, wout, ln_g, ln_b, w_router_hi, w_router_lo, router_bias, alpha):
    t, dm = x2.shape
    tm = min(MOE_TILE, t)
    row = lambda w: pl.BlockSpec((tm, w), lambda i: (i, 0))
    consts = (wg, wap, wout, ln_g, ln_b, w_router_hi, w_router_lo, router_bias)
    return pl.pallas_call(
        functools.partial(_mix_kernel, alpha=alpha),
        grid=(t // tm,),
        in_specs=[row(dm), row(D_ATTN), row(dm)] + [_const_spec(c.shape) for c in consts],
        out_specs=[row(dm), row(SLOT_PAD), row(SLOT_PAD),
                   pl.BlockSpec((None, 8, N_EXPERTS), lambda i: (i, 0, 0)),
                   pl.BlockSpec((1, N_EXPERTS), lambda i: (0, 0))],
        out_shape=[jax.ShapeDtypeStruct((t, dm), jnp.float32),
                   jax.ShapeDtypeStruct((t, SLOT_PAD), jnp.float32),
                   jax.ShapeDtypeStruct((t, SLOT_PAD), jnp.int32),
                   jax.ShapeDtypeStruct((t // tm, 8, N_EXPERTS), jnp.float32),
                   jax.ShapeDtypeStruct((1, N_EXPERTS), jnp.float32)],
        scratch_shapes=[pltpu.VMEM((1, N_EXPERTS), jnp.float32)],
        compiler_params=_params("arbitrary"),
        name="mix_ln_router",
    )(x2, attn2, ssm2, *consts)


def _local_rows(tm):
    assert SLOT_ROWS % tm == 0
    return SLOT_ROWS + MOE_TOPK * tm


def _overflow_rows(i, cnt_ref, off_ref):
    last = i * N_EXPERTS + N_EXPERTS - 1
    return off_ref[last] + jnp.maximum(cnt_ref[last] - SEG_SLOT, 0)


def _segment_copies(i, cnt_ref, off_ref, dst_ref, tm, make_copy):
    def segment(e, carry):
        n = cnt_ref[i * N_EXPERTS + e]
        b = pl.multiple_of(dst_ref[i * N_EXPERTS + e], SEG_ALIGN)
        make_copy(pl.multiple_of(e * SEG_SLOT, SEG_SLOT), b, SEG_SLOT).start()

        @pl.when(n > SEG_SLOT)
        def _():
            a = SLOT_ROWS + off_ref[i * N_EXPERTS + e]
            m = n - SEG_SLOT
            for p in [tm >> s for s in range(tm.bit_length()) if (tm >> s) >= SEG_ALIGN]:
                done = (m // (2 * p)) * (2 * p)

                @pl.when((m & p) != 0)
                def _(p=p, done=done):
                    piece = make_copy(pl.multiple_of(a + done, SEG_ALIGN),
                                      pl.multiple_of(b + SEG_SLOT + done, SEG_ALIGN), p)
                    piece.start()
                    piece.wait()

        return carry

    lax.fori_loop(0, N_EXPERTS, segment, 0)


def _wait_slot_copies(buf, sem, tm):
    for _ in range(SLOT_ROWS // tm):
        pltpu.make_async_copy(buf.at[pl.ds(0, tm)], buf.at[pl.ds(0, tm)], sem).wait()


def _dispatch_kernel(cnt_ref, off_ref, dst_ref, zlo_ref, zhi_ref, post_ref, x_ref, xs_ref,
                     loc2, zero_scr, sems, *, tm, n_steps):
    i = pl.program_id(0)

    def sort_and_send(which):
        loc, sem = loc2.at[which], sems.at[which]

        xb = x_ref[...].astype(MXU_DTYPE)
        post = post_ref[...]

        def sort_chunk(sc):
            slot = sc * tm + lax.broadcasted_iota(jnp.int32, (tm, tm), 0)
            onehot = jnp.zeros((tm, tm), jnp.float32)
            for k in range(MOE_TOPK):
                onehot = jnp.where(slot == post[k:k + 1, :], 1.0, onehot)
            loc[sc * tm:(sc + 1) * tm, :] = _dot(onehot, xb).astype(loc.dtype)

        n_fixed = SLOT_ROWS // tm
        for sc in range(n_fixed):
            sort_chunk(sc)
        overflow = _overflow_rows(i, cnt_ref, off_ref)
        for sc in range(n_fixed, _local_rows(tm) // tm):
            pl.when((sc - n_fixed) * tm < overflow)(functools.partial(sort_chunk, sc))

        @pl.when(i >= 1)
        def _():
            _wait_slot_copies(loc2.at[1 - which], sems.at[1 - which], tm)

        _segment_copies(i, cnt_ref, off_ref, dst_ref, tm, lambda a, b, p: pltpu.make_async_copy(
            loc.at[pl.ds(a, p)], xs_ref.at[pl.ds(b, p)], sem))

        @pl.when(i == n_steps - 1)
        def _():
            _wait_slot_copies(loc, sem, tm)

    for which in range(2):
        pl.when(i % 2 == which)(functools.partial(sort_and_send, which))

    @pl.when(i == n_steps - 1)
    def _():
        sem = sems.at[0]
        zero_scr[...] = jnp.zeros_like(zero_scr)

        def zero_copy(group, rows):
            return pltpu.make_async_copy(
                zero_scr.at[pl.ds(0, rows)], xs_ref.at[pl.ds(pl.multiple_of(group * rows, rows), rows)], sem)

        def fill_groups(lo, hi, rows):
            def fill(g, c):
                zero_copy(g, rows).start()
                return c

            def fill_done(g, c):
                zero_copy(g, rows).wait()
                return c

            lax.fori_loop(lo, hi, fill, 0)
            lax.fori_loop(lo, hi, fill_done, 0)

        def segment(s, carry):
            fill_groups(zlo_ref[s], zhi_ref[s], SEG_ALIGN)
            return carry

        lax.fori_loop(0, N_EXPERTS, segment, 0)
        fill_groups(zlo_ref[N_EXPERTS], zhi_ref[N_EXPERTS], EXPERT_ROWS)


def _dispatch(x1, pos_t, seg_cnt, seg_off, seg_dst, zlo, zhi, n_rows):
    t, dm = x1.shape
    tm = min(MOE_TILE, t)
    return pl.pallas_call(
        functools.partial(_dispatch_kernel, tm=tm, n_steps=t // tm),
        grid_spec=pltpu.PrefetchScalarGridSpec(
            num_scalar_prefetch=5,
            grid=(t // tm,),
            in_specs=[pl.BlockSpec((None, SLOT_PAD, tm), lambda i, *_: (i, 0, 0)),
                      pl.BlockSpec((tm, dm), lambda i, *_: (i, 0))],
            out_specs=pl.BlockSpec(memory_space=pl.ANY),
            scratch_shapes=[pltpu.VMEM((2, _local_rows(tm), dm), MXU_DTYPE),
                            pltpu.VMEM((EXPERT_ROWS, dm), MXU_DTYPE), pltpu.SemaphoreType.DMA((2,))],
        ),
        out_shape=jax.ShapeDtypeStruct((n_rows, dm), MXU_DTYPE),
        compiler_params=_params("arbitrary"),
        name="moe_dispatch",
    )(seg_cnt, seg_off, seg_dst, zlo, zhi, pos_t, x1)


def _expert_kernel(be_ref, nu_ref, xs_ref, wgu_ref, wdn_ref, ys_ref, wgu_scr, wdn_scr):
    i = pl.program_id(0)

    @pl.when(i < nu_ref[0])
    def _():
        @pl.when((i == 0) | (be_ref[i] != be_ref[jnp.maximum(i - 1, 0)]))
        def _():
            wgu_scr[...] = wgu_ref[...].astype(MXU_DTYPE)
            wdn_scr[...] = wdn_ref[...].astype(MXU_DTYPE)

        h = _dot(xs_ref[...], wgu_scr[...])
        f = h.shape[1] // 2
        ys_ref[...] = _dot(jax.nn.silu(h[:, :f]) * h[:, f:], wdn_scr[...]).astype(ys_ref.dtype)

    @pl.when(i >= nu_ref[0])
    def _():
        ys_ref[...] = jnp.zeros_like(ys_ref)


def _experts(xs, block_expert, n_used, w_gu, w_down):
    n_rows, dm = xs.shape
    bm = EXPERT_ROWS
    f2 = w_gu.shape[2]
    used = lambda i, nu: jnp.minimum(i, nu[0] - 1)
    return pl.pallas_call(
        _expert_kernel,
        grid_spec=pltpu.PrefetchScalarGridSpec(
            num_scalar_prefetch=2,
            grid=(n_rows // bm,),
            in_specs=[pl.BlockSpec((bm, dm), lambda i, be, nu: (used(i, nu), 0)),
                      pl.BlockSpec((None, dm, f2), lambda i, be, nu: (be[used(i, nu)], 0, 0)),
                      pl.BlockSpec((None, f2 // 2, dm), lambda i, be, nu: (be[used(i, nu)], 0, 0))],
            out_specs=pl.BlockSpec((bm, dm), lambda i, be, nu: (i, 0)),
            scratch_shapes=[pltpu.VMEM((dm, f2), MXU_DTYPE), pltpu.VMEM((f2 // 2, dm), MXU_DTYPE)],
        ),
        out_shape=jax.ShapeDtypeStruct((n_rows, dm), MXU_DTYPE),
        compiler_params=_params("arbitrary"),
        name="moe_experts",
    )(block_expert, n_used, xs, w_gu, w_down)


def _combine_kernel(cnt_ref, off_ref, dst_ref, x1_ref, gate_ref, pos_ref, wsgu_ref, wsdn_ref, g2_ref, b2_ref,
                    ys_ref, y_ref, loc2, sems, *, tm, alpha, n_steps):
    i = pl.program_id(0)

    def fetch(tile, which):
        _segment_copies(tile, cnt_ref, off_ref, dst_ref, tm, lambda a, b, p: pltpu.make_async_copy(
            ys_ref.at[pl.ds(b, p)], loc2.at[which, pl.ds(a, p)], sems.at[which]))

    def combine_tile(which):
        loc = loc2.at[which]

        @pl.when(i == 0)
        def _():
            loc2[:, SLOT_ROWS:, :] = jnp.zeros((2, MOE_TOPK * tm, loc2.shape[2]), loc2.dtype)
            fetch(i, which)

        @pl.when(i + 1 < n_steps)
        def _():
            fetch(i + 1, 1 - which)

        x1 = x1_ref[...]
        h = _dot(x1, wsgu_ref[...])
        f = h.shape[1] // 2
        shared = _dot(jax.nn.silu(h[:, :f]) * h[:, f:], wsdn_ref[...])

        _wait_slot_copies(loc, sems.at[which], tm)
        n_fixed = SLOT_ROWS // tm

        gate = gate_ref[...]
        pos = pos_ref[...]

        def gather_chunk(sc):
            slot = sc * tm + lax.broadcasted_iota(jnp.int32, (tm, tm), 1)
            w = jnp.zeros((tm, tm), jnp.float32)
            for k in range(MOE_TOPK):
                w = jnp.where(slot == pos[:, k:k + 1], gate[:, k:k + 1], w)
            return _dot(w, loc[sc * tm:(sc + 1) * tm, :])

        routed = shared
        for sc in range(n_fixed):
            routed = routed + gather_chunk(sc)
        y_ref[...] = routed
        overflow = _overflow_rows(i, cnt_ref, off_ref)
        for sc in range(n_fixed, _local_rows(tm) // tm):
            @pl.when((sc - n_fixed) * tm < overflow)
            def _(sc=sc):
                y_ref[...] += gather_chunk(sc)

        y_ref[...] = _layer_norm(alpha * x1 + y_ref[...], g2_ref[...], b2_ref[...])

    for which in range(2):
        pl.when(i % 2 == which)(functools.partial(combine_tile, which))


def _combine(x1, gate, pos, seg_cnt, seg_off, seg_dst, ys, w_sh_gu, w_sh_down, ln_g, ln_b, alpha):
    t, dm = x1.shape
    tm = min(MOE_TILE, t)
    const = lambda a: pl.BlockSpec(a.shape, lambda i, *_: (0,) * a.ndim, pipeline_mode=pl.Buffered(1))
    return pl.pallas_call(
        functools.partial(_combine_kernel, tm=tm, alpha=alpha, n_steps=t // tm),
        grid_spec=pltpu.PrefetchScalarGridSpec(
            num_scalar_prefetch=3,
            grid=(t // tm,),
            in_specs=[pl.BlockSpec((tm, dm), lambda i, *_: (i, 0)),
                      pl.BlockSpec((tm, SLOT_PAD), lambda i, *_: (i, 0)),
                      pl.BlockSpec((tm, SLOT_PAD), lambda i, *_: (i, 0)),
                      const(w_sh_gu), const(w_sh_down), const(ln_g), const(ln_b),
                      pl.BlockSpec(memory_space=pl.ANY)],
            out_specs=pl.BlockSpec((tm, dm), lambda i, *_: (i, 0)),
            scratch_shapes=[pltpu.VMEM((2, _local_rows(tm), dm), MXU_DTYPE), pltpu.SemaphoreType.DMA((2,))],
        ),
        out_shape=jax.ShapeDtypeStruct((t, dm), jnp.float32),
        compiler_params=_params("arbitrary"),
        name="moe_combine",
    )(seg_cnt, seg_off, seg_dst, x1, gate, pos, w_sh_gu, w_sh_down, ln_g, ln_b, ys)


def _moe(x1, gate, pos, stats, counts, w_exp_gu, w_exp_down, w_sh_gu, w_sh_down, ln_g, ln_b, alpha):
    t = x1.shape[0]
    bm = EXPERT_ROWS
    n_tiles = stats.shape[0]
    n_rows = (-(-(t * MOE_TOPK + n_tiles * N_EXPERTS * (SEG_ALIGN - 1) + SLOT_ROWS) // bm) + N_EXPERTS) * bm
    cnt = counts.reshape(N_EXPERTS).astype(jnp.int32)
    padded = (cnt + SEG_SLOT + bm - 1) // bm * bm
    end = jnp.cumsum(padded)
    start = end - padded
    n_used = end[-1:] // bm
    zlo = jnp.concatenate([(start + cnt) // SEG_ALIGN, n_used])
    zhi = jnp.concatenate([end // SEG_ALIGN, jnp.full((1,), n_rows // bm, jnp.int32)])
    block_start = jnp.arange(n_rows // bm, dtype=jnp.int32) * bm
    block_expert = jnp.minimum(jnp.sum((end[None, :] <= block_start[:, None]).astype(jnp.int32), axis=1),
                               N_EXPERTS - 1)
    seg = stats.astype(jnp.int32)
    seg_cnt, seg_off = seg[:, 1, :].reshape(-1), seg[:, 2, :].reshape(-1)
    seg_dst = (start[None, :] + seg[:, 0, :]).reshape(-1)
    pos_t = jnp.transpose(pos.reshape(n_tiles, t // n_tiles, SLOT_PAD), (0, 2, 1))
    xs = _dispatch(x1, pos_t, seg_cnt, seg_off, seg_dst, zlo, zhi, n_rows)
    ys = _experts(xs, block_expert, n_used, w_exp_gu, w_exp_down)
    return _combine(x1, gate, pos, seg_cnt, seg_off, seg_dst, ys, w_sh_gu, w_sh_down, ln_g, ln_b, alpha)


def _layer(x, pos0, past, p, alpha):
    (w_in, a_re, a_im, log_dt, b_re, b_im, c_re, c_im, ssm_d, w_glu, w_attn_proj, w_out, ln1_g, ln1_b,
     w_router, router_bias, w_exp_gu, w_exp_down, w_sh_gu, w_sh_down, ln2_g, ln2_b) = p
    nb, seq, dm = x.shape
    assert pos0 % CHUNK == 0 and seq % Q_BLOCK == 0 and Q_BLOCK == CHUNK and seq % S5_STEPS == 0
    assert nb * seq * MOE_TOPK < 2 ** 24
    t = nb * seq
    f32 = jnp.float32
    x2 = x.reshape(t, dm)
    pos = pos0 + jnp.arange(seq, dtype=jnp.int32)

    u, q, iq, k, v, ik2, ik, iw, k_heads, v_heads = _in_proj(x2, _pack_w_in(w_in), pos, seq)

    a_cat, bcat, ccat = _s5_discretise(a_re, a_im, log_dt, b_re, b_im, c_re, c_im)
    if past is None:
        h0 = jnp.zeros((nb, 2 * NS), f32)
    else:
        h0 = jnp.concatenate([past[3].reshape(nb, NS), past[4].reshape(nb, NS)], axis=1).astype(f32)
    ssm, h_t = _s5(u.reshape(nb, seq, D_SSM), h0, a_cat, bcat, ccat, ssm_d.reshape(1, D_SSM).astype(f32),
                   w_glu.astype(MXU_DTYPE))
    ssm2 = ssm.reshape(t, dm)
    h_re = h_t[:, :NS].reshape(nb, N_GROUPS, N_STATE)
    h_im = h_t[:, NS:].reshape(nb, N_GROUPS, N_STATE)

    k3, v3, ik23 = (a.reshape(nb, seq, LANES) for a in (k, v, ik2))
    if past is not None:
        ck, cv, cik = past[0], past[1], past[2]
        n_past = ck.shape[1]
        k3 = jnp.concatenate([ck.reshape(nb, n_past, LANES), k3], axis=1)
        v3 = jnp.concatenate([cv.reshape(nb, n_past, LANES), v3], axis=1)
        ik23 = jnp.concatenate([jnp.concatenate([cik, cik], axis=-1), ik23], axis=1)
    attn = _dsa(q.reshape(nb, seq, D_ATTN), iq.reshape(nb, seq, IDX_HEADS * IDX_DIM), iw.reshape(nb, seq, LANES),
                k3, v3, ik23, pos0 // CHUNK)

    wap = jnp.transpose(w_attn_proj.reshape(N_KV_HEADS, KV_GROUP, HEAD_DIM, dm), (1, 0, 2, 3)).reshape(D_ATTN, dm)
    wg = w_in[:, w_in.shape[1] - 2 * dm:]
    row = lambda a: a.reshape(1, -1).astype(f32)
    x1, gate, pos_slot, stats, counts = _mix(
        x2, attn.reshape(t, D_ATTN), ssm2, wg.astype(MXU_DTYPE), wap.astype(MXU_DTYPE), w_out.astype(MXU_DTYPE),
        row(ln1_g), row(ln1_b), *_split(w_router.astype(f32)), row(router_bias), alpha)

    y = _moe(x1, gate, pos_slot, stats, counts, w_exp_gu, w_exp_down,
             w_sh_gu.astype(MXU_DTYPE), w_sh_down.astype(MXU_DTYPE), row(ln2_g), row(ln2_b), alpha)
    state = (k_heads.reshape(nb, seq, N_KV_HEADS, HEAD_DIM), v_heads.reshape(nb, seq, N_KV_HEADS, HEAD_DIM),
             ik.reshape(nb, seq, IDX_DIM), h_re, h_im)
    return y.reshape(nb, seq, dm), state


def kernel(x_prompt, x_sample, cache_k, cache_v, cache_idx_k, state_ssm_re, state_ssm_im, w_in, ssm_a_re, ssm_a_im, ssm_log_dt, ssm_b_re, ssm_b_im, ssm_c_re, ssm_c_im, ssm_d, w_glu, w_attn_proj, w_out, ln1_g, ln1_b, w_router, router_bias, w_exp_gu, w_exp_down, w_sh_gu, w_sh_down, ln2_g, ln2_b):
    weights = (w_in, ssm_a_re, ssm_a_im, ssm_log_dt, ssm_b_re, ssm_b_im, ssm_c_re, ssm_c_im, ssm_d,
               w_glu, w_attn_proj, w_out, ln1_g, ln1_b, w_router, router_bias,
               w_exp_gu, w_exp_down, w_sh_gu, w_sh_down, ln2_g, ln2_b)
    depth = w_in.shape[0]
    alpha = (2 * depth) ** 0.25
    past_len = cache_k.shape[2]
    y_p, y_s = x_prompt, x_sample
    new_p, new_s = [], []
    for l in range(depth):
        p_l = tuple(w[l] for w in weights)
        y_p, st_p = _layer(y_p, 0, None, p_l, alpha)
        y_s, st_s = _layer(y_s, past_len,
                           (cache_k[l], cache_v[l], cache_idx_k[l], state_ssm_re[l], state_ssm_im[l]), p_l, alpha)
        new_p.append(st_p)
        new_s.append(st_s)
    k_p, v_p, ik_p, hre_p, him_p = [jnp.stack(a) for a in zip(*new_p)]
    k_s, v_s, ik_s, hre_s, him_s = [jnp.stack(a) for a in zip(*new_s)]
    return (y_p, y_s, k_p, v_p, ik_p, hre_p, him_p, k_s, v_s, ik_s, hre_s, him_s)
```

```python
import functools

import jax
import jax.numpy as jnp
from jax import lax
from jax.experimental import pallas as pl
from jax.experimental.pallas import tpu as pltpu

CHUNK = 64
D_SSM = 512
SSM_GROUP = 16
N_GROUPS = D_SSM // SSM_GROUP
N_STATE = 64
N_HEADS = 8
N_KV_HEADS = 2
HEAD_DIM = 64
D_ATTN = N_HEADS * HEAD_DIM
KV_GROUP = N_HEADS // N_KV_HEADS
ROPE_DIM = HEAD_DIM // 4
ROPE_THETA = 500000.0
IDX_HEADS = 8
IDX_DIM = 64
TOPK_KEYS = 256
Q_BLOCK = 64
N_EXPERTS = 64
MOE_TOPK = 6
D_EXPERT = 256
D_SHARED = 256
ROUTED_SCALE = 2.5
LN_EPS = 1e-5

LANES = 128
MXU_DIM = 256
KEY_CHUNK = 256
DSA_ROWS = 256
EXPERT_ROWS = 512
EXPERT_ROWS_FEW = 128
MOE_TILE = 512
SEG_ALIGN = 16
SEG_SLOT = 64
SLOT_ROWS = N_EXPERTS * SEG_SLOT
SLOT_PAD = 8
NS = N_GROUPS * N_STATE
VMEM_LIMIT = 56 * 1024 * 1024
NEG_BIG = -1e30
INT_MIN = -2147483648

MXU_DTYPE = jnp.bfloat16


def _dot(a, b):
    return jnp.dot(a.astype(MXU_DTYPE), b.astype(MXU_DTYPE), preferred_element_type=jnp.float32)


def _dot_nt(a, b):
    return lax.dot_general(a.astype(MXU_DTYPE), b.astype(MXU_DTYPE), (((1,), (1,)), ((), ())),
                           preferred_element_type=jnp.float32)


def _split(a):
    high = a.astype(MXU_DTYPE)
    return high, (a - high.astype(jnp.float32)).astype(MXU_DTYPE)


def _params(*sem):
    return pltpu.CompilerParams(dimension_semantics=sem, vmem_limit_bytes=VMEM_LIMIT)


def _const_spec(shape):
    return pl.BlockSpec(shape, lambda *_: (0,) * len(shape), pipeline_mode=pl.Buffered(1))


_C_U, _C_Q, _C_IQ, _C_K, _C_V, _C_IK, _C_IW, _C_END = 0, 512, 1024, 1536, 1664, 1792, 1920, 2048


def _inproj_kernel(x_ref, w_ref, cos_ref, sin_ref,
                   u_ref, q_ref, iq_ref, k_ref, v_ref, ik2_ref, ik_ref, iw_ref, kh_ref, vh_ref):
    z = _dot(x_ref[...], w_ref[...])
    cos = cos_ref[...]
    sin = sin_ref[...]
    lane = lax.broadcasted_iota(jnp.int32, cos.shape, 1)
    first_half = (lane % HEAD_DIM) < (ROPE_DIM // 2)

    def rope(zc):
        partner = jnp.where(first_half, pltpu.roll(zc, LANES - ROPE_DIM // 2, 1), pltpu.roll(zc, ROPE_DIM // 2, 1))
        return zc * cos + partner * sin

    u_ref[...] = z[:, _C_U:_C_Q]
    for c in range(4):
        q_ref[:, c * LANES:(c + 1) * LANES] = (
            rope(z[:, _C_Q + c * LANES:_C_Q + (c + 1) * LANES]) * HEAD_DIM ** -0.5).astype(q_ref.dtype)
        iq_ref[:, c * LANES:(c + 1) * LANES] = (
            rope(z[:, _C_IQ + c * LANES:_C_IQ + (c + 1) * LANES]) * IDX_DIM ** -0.5).astype(iq_ref.dtype)
    k = rope(z[:, _C_K:_C_V])
    v = z[:, _C_V:_C_IK]
    k_ref[...] = k
    v_ref[...] = v
    for g in range(N_KV_HEADS):
        kh_ref[:, g, :] = k[:, g * HEAD_DIM:(g + 1) * HEAD_DIM]
        vh_ref[:, g, :] = v[:, g * HEAD_DIM:(g + 1) * HEAD_DIM]
    ik2 = rope(z[:, _C_IK:_C_IW])
    ik2_ref[...] = ik2
    ik_ref[...] = ik2[:, :IDX_DIM]
    iw_ref[...] = z[:, _C_IW:_C_END]


def _rope_tables(pos):
    half = ROPE_DIM // 2
    inv_freq = ROPE_THETA ** (-jnp.arange(half, dtype=jnp.float32) * 2.0 / ROPE_DIM)
    ang = pos.astype(jnp.float32)[:, None] * inv_freq
    cos, sin = jnp.cos(ang), jnp.sin(ang)
    n = pos.shape[0]
    pad = HEAD_DIM - ROPE_DIM
    cos_h = jnp.concatenate([cos, cos, jnp.ones((n, pad), jnp.float32)], axis=-1)
    sin_h = jnp.concatenate([-sin, sin, jnp.zeros((n, pad), jnp.float32)], axis=-1)
    return jnp.tile(cos_h, (1, LANES // HEAD_DIM)), jnp.tile(sin_h, (1, LANES // HEAD_DIM))


def _pack_w_in(w_in):
    s = [0, 512, 1024, 1152, 1280, 1792, 1856, 1864]
    w_u, w_q, w_k, w_v = w_in[:, s[0]:s[1]], w_in[:, s[1]:s[2]], w_in[:, s[2]:s[3]], w_in[:, s[3]:s[4]]
    w_iq, w_ik, w_iw = w_in[:, s[4]:s[5]], w_in[:, s[5]:s[6]], w_in[:, s[6]:s[7]]
    d = w_in.shape[0]
    w_qh = w_q.reshape(d, N_KV_HEADS, KV_GROUP, HEAD_DIM)
    w_qp = jnp.transpose(w_qh, (0, 2, 1, 3)).reshape(d, D_ATTN)
    w_iwp = jnp.concatenate([w_iw, jnp.zeros((d, LANES - IDX_HEADS), w_in.dtype)], axis=1)
    return jnp.concatenate([w_u, w_qp, w_iq, w_k, w_v, w_ik, w_ik, w_iwp], axis=1).astype(MXU_DTYPE)


def _in_proj(x2, w_pack, pos, seq):
    t, d = x2.shape
    tm = min(512, t)
    cos, sin = _rope_tables(pos)
    if seq >= tm:
        per = seq // tm
        tab_map = lambda i: (i % per, 0)
    else:
        cos, sin = jnp.tile(cos, (tm // seq, 1)), jnp.tile(sin, (tm // seq, 1))
        tab_map = lambda i: (0, 0)
    row = lambda w: pl.BlockSpec((tm, w), lambda i: (i, 0))
    widths = (D_SSM, D_ATTN, IDX_HEADS * IDX_DIM, LANES, LANES, LANES, IDX_DIM, LANES)
    heads = pl.BlockSpec((tm, N_KV_HEADS, HEAD_DIM), lambda i: (i, 0, 0))
    return pl.pallas_call(
        _inproj_kernel,
        grid=(t // tm,),
        in_specs=[row(d), _const_spec(w_pack.shape),
                  pl.BlockSpec((tm, LANES), tab_map), pl.BlockSpec((tm, LANES), tab_map)],
        out_specs=[row(w) for w in widths] + [heads, heads],
        out_shape=[jax.ShapeDtypeStruct((t, w), MXU_DTYPE if i in (1, 2) else jnp.float32)
                   for i, w in enumerate(widths)]
        + [jax.ShapeDtypeStruct((t, N_KV_HEADS, HEAD_DIM), jnp.float32)] * 2,
        compiler_params=_params("parallel"),
        name="in_proj",
    )(x2, w_pack, cos, sin)


S5_STEPS = 16
S5_COLS = 512


def _s5_kernel(u_ref, h0_ref, a_ref, bcat_ref, ccat_ref, d_ref, wglu_ref,
               out_ref, ht_ref, h_scr, bu_scr, hall_scr, *, nb):
    @pl.when(pl.program_id(0) == 0)
    def _():
        h_scr[...] = h0_ref[...]

    u = jnp.swapaxes(u_ref[...], 0, 1).reshape(S5_STEPS * nb, D_SSM)
    per_tile = MXU_DIM // N_STATE
    for j in range(2 * NS // MXU_DIM):
        kb = ((j % (NS // MXU_DIM)) * per_tile * SSM_GROUP) // MXU_DIM * MXU_DIM
        cols = slice(j * MXU_DIM, (j + 1) * MXU_DIM)
        bu_scr[:, cols] = _dot(u[:, kb:kb + MXU_DIM], bcat_ref[kb:kb + MXU_DIM, cols])
    for cb in range(NS // S5_COLS):
        re = slice(cb * S5_COLS, (cb + 1) * S5_COLS)
        im = slice(NS + cb * S5_COLS, NS + (cb + 1) * S5_COLS)
        a_re, a_im = a_ref[0:1, re], a_ref[1:2, re]
        h_re, h_im = h_scr[:, re], h_scr[:, im]
        for t in range(S5_STEPS):
            rows = slice(t * nb, (t + 1) * nb)
            h_re, h_im = (a_re * h_re - a_im * h_im + bu_scr[rows, re],
                          a_re * h_im + a_im * h_re + bu_scr[rows, im])
            hall_scr[rows, re] = h_re
            hall_scr[rows, im] = h_im
        h_scr[:, re] = h_re
        h_scr[:, im] = h_im
    y_tiles = []
    for n in range(D_SSM // MXU_DIM):
        cols = slice(n * MXU_DIM, (n + 1) * MXU_DIM)
        k0, kn = n * MXU_DIM // SSM_GROUP * N_STATE, MXU_DIM // SSM_GROUP * N_STATE
        y_tiles.append(_dot(hall_scr[:, k0:k0 + kn], ccat_ref[k0:k0 + kn, cols])
                       + _dot(hall_scr[:, NS + k0:NS + k0 + kn], ccat_ref[NS + k0:NS + k0 + kn, cols]))
    y = jnp.concatenate(y_tiles, axis=1) + d_ref[...] * u
    y = jax.nn.gelu(y)
    g = _dot(y, wglu_ref[...])
    dm = g.shape[1] // 2
    out_ref[...] = jnp.swapaxes((g[:, :dm] * jax.nn.sigmoid(g[:, dm:])).reshape(S5_STEPS, nb, dm), 0, 1)
    ht_ref[...] = h_scr[...]


def _s5_discretise(a_re, a_im, log_dt, b_re, b_im, c_re, c_im):
    f32 = jnp.float32
    ar, ai = a_re.astype(f32), a_im.astype(f32)
    dt = jnp.exp(log_dt.astype(f32))[:, None]
    mag = jnp.exp(dt * ar)
    abar_re, abar_im = mag * jnp.cos(dt * ai), mag * jnp.sin(dt * ai)
    den = ar * ar + ai * ai
    num_re, num_im = abar_re - 1.0, abar_im
    coef_re = (num_re * ar + num_im * ai) / den
    coef_im = (num_im * ar - num_re * ai) / den
    br, bi = b_re.astype(f32), b_im.astype(f32)
    bbar_re = coef_re[..., None] * br - coef_im[..., None] * bi
    bbar_im = coef_re[..., None] * bi + coef_im[..., None] * br
    eye = jnp.eye(N_GROUPS, dtype=f32)
    bd = lambda m: jnp.einsum("gnc,gh->gchn", m, eye).reshape(D_SSM, NS)
    bcat = jnp.concatenate([bd(bbar_re), bd(bbar_im)], axis=1)
    cd = lambda m: jnp.einsum("gcn,gh->gnhc", m, eye).reshape(NS, D_SSM)
    ccat = jnp.concatenate([cd(c_re.astype(f32)), cd(-c_im.astype(f32))], axis=0)
    a_cat = jnp.stack([abar_re.reshape(NS), abar_im.reshape(NS)])
    return a_cat, bcat.astype(MXU_DTYPE), ccat.astype(MXU_DTYPE)


def _s5(u, h0, a_cat, bcat, ccat, d_skip, w_glu):
    nb, seq, _ = u.shape
    dm = w_glu.shape[1] // 2
    rows = S5_STEPS * nb
    return pl.pallas_call(
        functools.partial(_s5_kernel, nb=nb),
        grid=(seq // S5_STEPS,),
        in_specs=[pl.BlockSpec((nb, S5_STEPS, D_SSM), lambda i: (0, i, 0)),
                  _const_spec(h0.shape), _const_spec(a_cat.shape), _const_spec(bcat.shape),
                  _const_spec(ccat.shape), _const_spec(d_skip.shape), _const_spec(w_glu.shape)],
        out_specs=[pl.BlockSpec((nb, S5_STEPS, dm), lambda i: (0, i, 0)),
                   pl.BlockSpec(h0.shape, lambda i: (0, 0))],
        out_shape=[jax.ShapeDtypeStruct((nb, seq, dm), jnp.float32),
                   jax.ShapeDtypeStruct(h0.shape, jnp.float32)],
        scratch_shapes=[pltpu.VMEM(h0.shape, jnp.float32),
                        pltpu.VMEM((rows, 2 * NS), jnp.float32),
                        pltpu.VMEM((rows, 2 * NS), jnp.float32)],
        compiler_params=_params("arbitrary"),
        name="s5_scan_glu",
    )(u, h0, a_cat, bcat, ccat, d_skip, w_glu)


KEY_NEG_INF = -2139095041


def _ordered_f32(key):
    bits = jnp.where(key >= 0, key, key ^ jnp.int32(0x7FFFFFFF))
    return jnp.where(key <= KEY_NEG_INF, -jnp.inf, pltpu.bitcast(bits, jnp.float32))


def _half_mask(shape, upper):
    lane = lax.broadcasted_iota(jnp.int32, shape, 1)
    return (lane >= HEAD_DIM) if upper else (lane < HEAD_DIM)


def _dsa_kernel(q_ref, iq_ref, iw_ref, k_ref, v_ref, ik_ref, o_ref,
                kb, vs0, vs1, ikb, sc_scr, sct_scr, bias_scr, wb_scr, iqlhs_scr, qlhs_scr, s_scr, m_scr, acc_scr,
                *, qt, n_tiles, n_keys, n_sel, first_chunk, idx_bits):
    j = pl.program_id(1)
    kc = KEY_CHUNK
    n_sub = qt // Q_BLOCK
    qtp = max(qt, LANES)

    @pl.when(j == 0)
    def _():
        v = v_ref[...]
        lower = lax.broadcasted_iota(jnp.int32, v.shape, v.ndim - 1) < HEAD_DIM
        kb[...] = k_ref[...].astype(MXU_DTYPE)
        vs0[...] = jnp.where(lower, v, 1.0).astype(MXU_DTYPE)
        vs1[...] = jnp.where(lower, 1.0, v).astype(MXU_DTYPE)
        ikb[...] = ik_ref[...].astype(MXU_DTYPE)

    row_blk = lax.broadcasted_iota(jnp.int32, (qt, 1), 0) // Q_BLOCK
    n_vis = jnp.minimum((first_chunk + j * n_sub + row_blk + 1) * CHUNK, n_keys)
    if n_tiles == 1:
        n_ck = (min((first_chunk + n_sub) * CHUNK, n_keys) + kc - 1) // kc
        chunks = functools.partial(lax.fori_loop, 0, n_ck, unroll=2)
    else:
        n_ck = (jnp.minimum((first_chunk + (j + 1) * n_sub) * CHUNK, n_keys) + kc - 1) // kc
        chunks = functools.partial(lax.fori_loop, 0, n_ck)
    lane_k = lax.broadcasted_iota(jnp.int32, (qt, kc), 1)

    iq = iq_ref[...]
    iw = iw_ref[...] * IDX_HEADS ** -0.5
    for h in range(IDX_HEADS):
        iqlhs_scr[h] = jnp.where(_half_mask((qt, LANES), h % 2 == 1),
                                 iq[:, (h // 2) * LANES:(h // 2 + 1) * LANES], 0.0)
        wb_scr[h] = jnp.broadcast_to(iw[:, h:h + 1], (qt, LANES))

    def score_chunk(c, carry):
        ikc = ikb[c]
        score = None
        for h in range(IDX_HEADS):
            rel = jnp.maximum(_dot_nt(iqlhs_scr[h], ikc), 0.0)
            w = wb_scr[h]
            term = jnp.concatenate([rel[:, s * LANES:(s + 1) * LANES] * w for s in range(kc // LANES)], axis=1)
            score = term if score is None else score + term
        score = jnp.where(c * kc + lane_k < n_vis, score, -jnp.inf)
        sc_scr[c] = score
        if qtp > qt:
            score = jnp.concatenate([score, jnp.full((qtp - qt, kc), -jnp.inf, jnp.float32)], axis=0)
        sct_scr[c] = score.T
        return carry

    chunks(score_chunk, 0)

    sub_k = lax.broadcasted_iota(jnp.int32, (kc // 8, 8, qtp), 0) * 8 + lax.broadcasted_iota(
        jnp.int32, (kc // 8, 8, qtp), 1)

    def count(pred):
        def body(c, acc):
            hit = pred(sct_scr[c].reshape(kc // 8, 8, qtp), c * kc + sub_k)
            return acc + jnp.sum(jnp.where(hit, 1.0, 0.0), axis=0)
        acc = chunks(body, jnp.zeros((8, qtp), jnp.float32))
        for shift in (4, 2, 1):
            acc = acc + pltpu.roll(acc, shift, 0)
        return acc

    def value_bit(i, t_pos):
        cand = t_pos + (jnp.int32(1) << (31 - i))
        cand_value = _ordered_f32(cand)
        return jnp.where(count(lambda sc, idx: sc >= cand_value[None]) >= n_sel, cand, t_pos)

    t = _ordered_f32(lax.fori_loop(0, 32, value_bit, jnp.full((8, qtp), INT_MIN, jnp.int32)))
    def count_above_and_at(t):
        def body(c, accs):
            sc = sct_scr[c].reshape(kc // 8, 8, qtp)
            return (accs[0] + jnp.sum(jnp.where(sc > t[None], 1.0, 0.0), axis=0),
                    accs[1] + jnp.sum(jnp.where(sc == t[None], 1.0, 0.0), axis=0))
        zero = jnp.zeros((8, qtp), jnp.float32)
        above, at = chunks(body, (zero, zero))
        for shift in (4, 2, 1):
            above, at = above + pltpu.roll(above, shift, 0), at + pltpu.roll(at, shift, 0)
        return above, at

    n_above, n_at = count_above_and_at(t)
    need = n_sel - n_above
    n_ge = n_above + n_at

    def last_tied_index():
        def index_bit(i, m):
            cand = m + (jnp.int32(1) << (idx_bits - 1 - i))
            below = count(lambda sc, idx: (sc == t[None]) & (idx < cand[None]))
            return jnp.where(below < need, cand, m)
        return lax.fori_loop(0, idx_bits, index_bit, jnp.zeros((8, qtp), jnp.int32))

    m_idx = lax.cond(jnp.max(n_ge) > n_sel, last_tied_index,
                     lambda: jnp.full((8, qtp), 2 ** 30, jnp.int32))
    t_rep = jnp.broadcast_to(t[0:1], (LANES, qtp)).T[:qt]
    m_rep = jnp.broadcast_to(m_idx[0:1], (LANES, qtp)).T[:qt]
    t_row = jnp.concatenate([t_rep] * (kc // LANES), axis=1)
    m_row = jnp.concatenate([m_rep] * (kc // LANES), axis=1)

    def bias_chunk(c, carry):
        sc = sc_scr[c]
        idx = c * kc + lane_k
        sel = ((sc > t_row) | ((sc == t_row) & (idx <= m_row))) & (idx < n_vis)
        bias_scr[c] = jnp.where(sel, 0.0, NEG_BIG)
        return carry

    chunks(bias_chunk, 0)

    q = q_ref[...]
    rows = KV_GROUP * qt
    for g in range(N_KV_HEADS):
        for r in range(KV_GROUP):
            qlhs_scr[g, r * qt:(r + 1) * qt] = jnp.where(
                _half_mask((qt, LANES), g == 1), q[:, r * LANES:(r + 1) * LANES], 0.0)
    m_scr[...] = jnp.full(m_scr.shape, NEG_BIG, jnp.float32)
    acc_scr[...] = jnp.zeros(acc_scr.shape, jnp.float32)

    def logits_chunk(c, carry):
        bias = bias_scr[c][None]
        for g in range(N_KV_HEADS):
            s = (_dot_nt(qlhs_scr[g], kb[c]).reshape(KV_GROUP, qt, kc) + bias).reshape(rows, kc)
            s_scr[g, c] = s
            m = m_scr[g]
            for u in range(kc // LANES):
                m = jnp.maximum(m, s[:, u * LANES:(u + 1) * LANES])
            m_scr[g] = m
        return carry

    chunks(logits_chunk, 0)
    for g in range(N_KV_HEADS):
        m_scr[g] = jnp.broadcast_to(jnp.max(m_scr[g], axis=-1, keepdims=True), (rows, LANES))

    def pv_chunk(c, carry):
        for g, vs in enumerate((vs0, vs1)):
            m = m_scr[g]
            s = s_scr[g, c]
            p = jnp.concatenate([jnp.exp(s[:, u * LANES:(u + 1) * LANES] - m) for u in range(kc // LANES)], axis=1)
            acc_scr[g] += _dot(p, vs[c])
        return carry

    chunks(pv_chunk, 0)
    outs = [acc_scr[g] / pltpu.roll(acc_scr[g], HEAD_DIM, 1) for g in range(N_KV_HEADS)]
    lower = _half_mask((qt, LANES), False)
    for r in range(KV_GROUP):
        rs = slice(r * qt, (r + 1) * qt)
        o_ref[:, r * LANES:(r + 1) * LANES] = jnp.where(lower, outs[0][rs], outs[1][rs])


def _dsa(q, iq, iw, k_all, v_all, ik2_all, first_chunk):
    nb, seq, _ = q.shape
    n_keys = k_all.shape[1]
    n_sel = min(TOPK_KEYS, n_keys // 4)
    kc = KEY_CHUNK
    qt = min(DSA_ROWS, seq)
    assert seq % qt == 0
    n_ch = -(-n_keys // kc)
    pad = n_ch * kc - n_keys
    chunked = lambda a: jnp.pad(a, ((0, 0), (0, pad), (0, 0))).reshape(nb, n_ch, kc, LANES)
    qspec = lambda w: pl.BlockSpec((None, qt, w), lambda b, j: (b, j, 0))
    kspec = pl.BlockSpec((None, n_ch, kc, LANES), lambda b, j: (b, 0, 0, 0))
    rows = KV_GROUP * qt
    return pl.pallas_call(
        functools.partial(_dsa_kernel, qt=qt, n_tiles=seq // qt, n_keys=n_keys, n_sel=n_sel, first_chunk=first_chunk,
                          idx_bits=max(1, (n_ch * kc - 1).bit_length())),
        grid=(nb, seq // qt),
        in_specs=[qspec(D_ATTN), qspec(IDX_HEADS * IDX_DIM), qspec(LANES), kspec, kspec, kspec],
        out_specs=qspec(D_ATTN),
        out_shape=jax.ShapeDtypeStruct((nb, seq, D_ATTN), jnp.float32),
        scratch_shapes=[pltpu.VMEM((n_ch, kc, LANES), MXU_DTYPE)] * 4
        + [pltpu.VMEM((n_ch, qt, kc), jnp.float32), pltpu.VMEM((n_ch, kc, max(qt, LANES)), jnp.float32),
           pltpu.VMEM((n_ch, qt, kc), jnp.float32),
           pltpu.VMEM((IDX_HEADS, qt, LANES), jnp.float32), pltpu.VMEM((IDX_HEADS, qt, LANES), MXU_DTYPE),
           pltpu.VMEM((N_KV_HEADS, rows, LANES), MXU_DTYPE),
           pltpu.VMEM((N_KV_HEADS, n_ch, rows, kc), jnp.float32),
           pltpu.VMEM((N_KV_HEADS, rows, LANES), jnp.float32), pltpu.VMEM((N_KV_HEADS, rows, LANES), jnp.float32)],
        compiler_params=_params("arbitrary", "arbitrary"),
        name="dsa_attention",
    )(q, iq, iw, chunked(k_all), chunked(v_all), chunked(ik2_all))


def _layer_norm(h, g, b):
    mu = jnp.mean(h, axis=-1, keepdims=True)
    var = jnp.mean(jnp.square(h - mu), axis=-1, keepdims=True)
    return (h - mu) * lax.rsqrt(var + LN_EPS) * g + b


def _mix_kernel(x_ref, attn_ref, ssm_ref, wg_ref, wap_ref, wout_ref, g1_ref, b1_ref, wrh_ref, wrl_ref, rb_ref,
                x1_ref, gate_ref, pos_ref, stats_ref, cnt_ref, cnt_scr, *, alpha):
    i = pl.program_id(0)

    @pl.when(i == 0)
    def _():
        cnt_scr[...] = jnp.zeros_like(cnt_scr)

    tm, dm = x_ref.shape
    n_split = 2 if tm % 16 == 0 else 1
    hm = tm // n_split
    lane = lax.broadcasted_iota(jnp.int32, (hm, N_EXPERTS), 1).astype(jnp.float32)
    slot = lax.broadcasted_iota(jnp.int32, (hm, SLOT_PAD), 1)

    def route_rows(rows):
        x = x_ref[rows, :]
        gates = jax.nn.sigmoid(_dot(x, wg_ref[...]))
        attn_out = _dot(attn_ref[rows, :], wap_ref[...])
        mixed = _dot(gates[:, :dm] * ssm_ref[rows, :] + gates[:, dm:] * attn_out, wout_ref[...])
        x1 = _layer_norm(alpha * x + mixed, g1_ref[...], b1_ref[...])
        x1_ref[rows, :] = x1
        x1_hi, x1_lo = _split(x1)
        logits = _dot(x1_hi, wrh_ref[...]) + (_dot(x1_hi, wrl_ref[...]) + _dot(x1_lo, wrh_ref[...]))
        scores = jax.nn.sigmoid(logits)
        cur = scores + rb_ref[...]
        chosen = jnp.zeros_like(scores)
        picks = []
        for _ in range(MOE_TOPK):
            best = jnp.max(cur, axis=-1, keepdims=True)
            e_k = jnp.min(jnp.where(cur == best, lane, float(N_EXPERTS)), axis=-1, keepdims=True)
            hot = lane == e_k
            picks.append((hot, jnp.sum(jnp.where(hot, scores, 0.0), axis=-1, keepdims=True)))
            chosen = jnp.where(hot, 1.0, chosen)
            cur = jnp.where(hot, -jnp.inf, cur)
        total = picks[0][1]
        for _, s_k in picks[1:]:
            total = total + s_k
        return chosen, picks, total

    halves = [route_rows(slice(h * hm, (h + 1) * hm)) for h in range(n_split)]
    chosen = jnp.concatenate([h[0] for h in halves], axis=0)

    row = lax.broadcasted_iota(jnp.int32, (tm, tm), 0)
    col = lax.broadcasted_iota(jnp.int32, (tm, tm), 1)
    before = _dot(jnp.where(col < row, 1.0, 0.0), chosen)
    tile_cnt = jnp.ceil(jnp.sum(chosen, axis=0, keepdims=True) / SEG_ALIGN) * SEG_ALIGN
    e_row = lax.broadcasted_iota(jnp.int32, (N_EXPERTS, N_EXPERTS), 0)
    e_col = lax.broadcasted_iota(jnp.int32, (N_EXPERTS, N_EXPERTS), 1)
    overflow = jnp.maximum(tile_cnt - SEG_SLOT, 0.0)
    tile_off = jnp.dot(jnp.broadcast_to(overflow, (8, N_EXPERTS)), jnp.where(e_row < e_col, 1.0, 0.0),
                       preferred_element_type=jnp.float32, precision=lax.Precision.HIGHEST)[0:1]
    slot_base = lax.broadcasted_iota(jnp.int32, (1, N_EXPERTS), 1).astype(jnp.float32) * SEG_SLOT
    local = jnp.where(before < SEG_SLOT, slot_base + before, (SLOT_ROWS - SEG_SLOT) + tile_off + before)
    for h, (_, picks, total) in enumerate(halves):
        rows = slice(h * hm, (h + 1) * hm)
        g_out = jnp.zeros((hm, SLOT_PAD), jnp.float32)
        p_out = jnp.zeros((hm, SLOT_PAD), jnp.int32)
        for k, (hot, s_k) in enumerate(picks):
            pos_k = jnp.sum(jnp.where(hot, local[rows], 0.0), axis=-1, keepdims=True)
            g_out = jnp.where(slot == k, s_k / total * ROUTED_SCALE, g_out)
            p_out = jnp.where(slot == k, pos_k.astype(jnp.int32), p_out)
        gate_ref[rows, :] = g_out
        pos_ref[rows, :] = p_out
    srow = lax.broadcasted_iota(jnp.int32, (8, N_EXPERTS), 0)
    stats_ref[...] = jnp.where(srow == 0, cnt_scr[...], jnp.where(srow == 1, tile_cnt, jnp.where(
        srow == 2, tile_off, 0.0)))
    cnt_scr[...] = cnt_scr[...] + tile_cnt
    cnt_ref[...] = cnt_scr[...]


def _mix(x2, attn2, ssm2, wg, wap, wout, ln_g, ln_b, w_router_hi, w_router_lo, router_bias, alpha):
    t, dm = x2.shape
    tm = min(MOE_TILE, t)
    row = lambda w: pl.BlockSpec((tm, w), lambda i: (i, 0))
    consts = (wg, wap, wout, ln_g, ln_b, w_router_hi, w_router_lo, router_bias)
    return pl.pallas_call(
        functools.partial(_mix_kernel, alpha=alpha),
        grid=(t // tm,),
        in_specs=[row(dm), row(D_ATTN), row(dm)] + [_const_spec(c.shape) for c in consts],
        out_specs=[row(dm), row(SLOT_PAD), row(SLOT_PAD),
                   pl.BlockSpec((None, 8, N_EXPERTS), lambda i: (i, 0, 0)),
                   pl.BlockSpec((1, N_EXPERTS), lambda i: (0, 0))],
        out_shape=[jax.ShapeDtypeStruct((t, dm), jnp.float32),
                   jax.ShapeDtypeStruct((t, SLOT_PAD), jnp.float32),
                   jax.ShapeDtypeStruct((t, SLOT_PAD), jnp.int32),
                   jax.ShapeDtypeStruct((t // tm, 8, N_EXPERTS), jnp.float32),
                   jax.ShapeDtypeStruct((1, N_EXPERTS), jnp.float32)],
        scratch_shapes=[pltpu.VMEM((1, N_EXPERTS), jnp.float32)],
        compiler_params=_params("arbitrary"),
        name="mix_ln_router",
    )(x2, attn2, ssm2, *consts)


def _local_rows(tm):
    assert SLOT_ROWS % tm == 0
    return SLOT_ROWS + MOE_TOPK * tm


def _overflow_rows(i, cnt_ref, off_ref):
    last = i * N_EXPERTS + N_EXPERTS - 1
    return off_ref[last] + jnp.maximum(cnt_ref[last] - SEG_SLOT, 0)


def _segment_copies(i, cnt_ref, off_ref, dst_ref, tm, make_copy):
    def segment(e, carry):
        n = cnt_ref[i * N_EXPERTS + e]
        b = pl.multiple_of(dst_ref[i * N_EXPERTS + e], SEG_ALIGN)
        make_copy(pl.multiple_of(e * SEG_SLOT, SEG_SLOT), b, SEG_SLOT).start()

        @pl.when(n > SEG_SLOT)
        def _():
            a = SLOT_ROWS + off_ref[i * N_EXPERTS + e]
            m = n - SEG_SLOT
            for p in [tm >> s for s in range(tm.bit_length()) if (tm >> s) >= SEG_ALIGN]:
                done = (m // (2 * p)) * (2 * p)

                @pl.when((m & p) != 0)
                def _(p=p, done=done):
                    piece = make_copy(pl.multiple_of(a + done, SEG_ALIGN),
                                      pl.multiple_of(b + SEG_SLOT + done, SEG_ALIGN), p)
                    piece.start()
                    piece.wait()

        return carry

    lax.fori_loop(0, N_EXPERTS, segment, 0)


def _wait_slot_copies(buf, sem, tm):
    for _ in range(SLOT_ROWS // tm):
        pltpu.make_async_copy(buf.at[pl.ds(0, tm)], buf.at[pl.ds(0, tm)], sem).wait()


def _dispatch_kernel(cnt_ref, off_ref, dst_ref, zlo_ref, zhi_ref, post_ref, x_ref, xs_ref,
                     loc2, zero_scr, sems, *, tm, n_steps):
    i = pl.program_id(0)

    def sort_and_send(which):
        loc, sem = loc2.at[which], sems.at[which]

        xb = x_ref[...].astype(MXU_DTYPE)
        post = post_ref[...]

        def sort_chunk(sc):
            slot = sc * tm + lax.broadcasted_iota(jnp.int32, (tm, tm), 0)
            onehot = jnp.zeros((tm, tm), jnp.float32)
            for k in range(MOE_TOPK):
                onehot = jnp.where(slot == post[k:k + 1, :], 1.0, onehot)
            loc[sc * tm:(sc + 1) * tm, :] = _dot(onehot, xb).astype(loc.dtype)

        n_fixed = SLOT_ROWS // tm
        for sc in range(n_fixed):
            sort_chunk(sc)
        overflow = _overflow_rows(i, cnt_ref, off_ref)
        for sc in range(n_fixed, _local_rows(tm) // tm):
            pl.when((sc - n_fixed) * tm < overflow)(functools.partial(sort_chunk, sc))

        @pl.when(i >= 1)
        def _():
            _wait_slot_copies(loc2.at[1 - which], sems.at[1 - which], tm)

        _segment_copies(i, cnt_ref, off_ref, dst_ref, tm, lambda a, b, p: pltpu.make_async_copy(
            loc.at[pl.ds(a, p)], xs_ref.at[pl.ds(b, p)], sem))

        @pl.when(i == n_steps - 1)
        def _():
            _wait_slot_copies(loc, sem, tm)

    for which in range(2):
        pl.when(i % 2 == which)(functools.partial(sort_and_send, which))

    @pl.when(i == n_steps - 1)
    def _():
        sem = sems.at[0]
        zero_scr[...] = jnp.zeros_like(zero_scr)

        def zero_copy(group, rows):
            return pltpu.make_async_copy(
                zero_scr.at[pl.ds(0, rows)], xs_ref.at[pl.ds(pl.multiple_of(group * rows, rows), rows)], sem)

        def fill_groups(lo, hi, rows):
            def fill(g, c):
                zero_copy(g, rows).start()
                return c

            def fill_done(g, c):
                zero_copy(g, rows).wait()
                return c

            lax.fori_loop(lo, hi, fill, 0)
            lax.fori_loop(lo, hi, fill_done, 0)

        def segment(s, carry):
            fill_groups(zlo_ref[s], zhi_ref[s], SEG_ALIGN)
            return carry

        lax.fori_loop(0, N_EXPERTS, segment, 0)
        fill_groups(zlo_ref[N_EXPERTS], zhi_ref[N_EXPERTS], zero_scr.shape[0])


def _dispatch(x1, pos_t, seg_cnt, seg_off, seg_dst, zlo, zhi, n_rows, bm):
    t, dm = x1.shape
    tm = min(MOE_TILE, t)
    return pl.pallas_call(
        functools.partial(_dispatch_kernel, tm=tm, n_steps=t // tm),
        grid_spec=pltpu.PrefetchScalarGridSpec(
            num_scalar_prefetch=5,
            grid=(t // tm,),
            in_specs=[pl.BlockSpec((None, SLOT_PAD, tm), lambda i, *_: (i, 0, 0)),
                      pl.BlockSpec((tm, dm), lambda i, *_: (i, 0))],
            out_specs=pl.BlockSpec(memory_space=pl.ANY),
            scratch_shapes=[pltpu.VMEM((2, _local_rows(tm), dm), MXU_DTYPE),
                            pltpu.VMEM((bm, dm), MXU_DTYPE), pltpu.SemaphoreType.DMA((2,))],
        ),
        out_shape=jax.ShapeDtypeStruct((n_rows, dm), MXU_DTYPE),
        compiler_params=_params("arbitrary"),
        name="moe_dispatch",
    )(seg_cnt, seg_off, seg_dst, zlo, zhi, pos_t, x1)


def _expert_kernel(be_ref, nu_ref, xs_ref, wgu_ref, wdn_ref, ys_ref, wgu_scr, wdn_scr):
    i = pl.program_id(0)

    @pl.when(i < nu_ref[0])
    def _():
        @pl.when((i == 0) | (be_ref[i] != be_ref[jnp.maximum(i - 1, 0)]))
        def _():
            wgu_scr[...] = wgu_ref[...].astype(MXU_DTYPE)
            wdn_scr[...] = wdn_ref[...].astype(MXU_DTYPE)

        h = _dot(xs_ref[...], wgu_scr[...])
        f = h.shape[1] // 2
        ys_ref[...] = _dot(jax.nn.silu(h[:, :f]) * h[:, f:], wdn_scr[...]).astype(ys_ref.dtype)

    @pl.when(i >= nu_ref[0])
    def _():
        ys_ref[...] = jnp.zeros_like(ys_ref)


def _experts(xs, block_expert, n_used, w_gu, w_down, bm):
    n_rows, dm = xs.shape
    f2 = w_gu.shape[2]
    used = lambda i, nu: jnp.minimum(i, nu[0] - 1)
    return pl.pallas_call(
        _expert_kernel,
        grid_spec=pltpu.PrefetchScalarGridSpec(
            num_scalar_prefetch=2,
            grid=(n_rows // bm,),
            in_specs=[pl.BlockSpec((bm, dm), lambda i, be, nu: (used(i, nu), 0)),
                      pl.BlockSpec((None, dm, f2), lambda i, be, nu: (be[used(i, nu)], 0, 0)),
                      pl.BlockSpec((None, f2 // 2, dm), lambda i, be, nu: (be[used(i, nu)], 0, 0))],
            out_specs=pl.BlockSpec((bm, dm), lambda i, be, nu: (i, 0)),
            scratch_shapes=[pltpu.VMEM((dm, f2), MXU_DTYPE), pltpu.VMEM((f2 // 2, dm), MXU_DTYPE)],
        ),
        out_shape=jax.ShapeDtypeStruct((n_rows, dm), MXU_DTYPE),
        compiler_params=_params("arbitrary"),
        name="moe_experts",
    )(block_expert, n_used, xs, w_gu, w_down)


def _combine_kernel(cnt_ref, off_ref, dst_ref, x1_ref, gate_ref, pos_ref, wsgu_ref, wsdn_ref, g2_ref, b2_ref,
                    ys_ref, y_ref, loc2, sems, *, tm, alpha, n_steps):
    i = pl.program_id(0)

    def fetch(tile, which):
        _segment_copies(tile, cnt_ref, off_ref, dst_ref, tm, lambda a, b, p: pltpu.make_async_copy(
            ys_ref.at[pl.ds(b, p)], loc2.at[which, pl.ds(a, p)], sems.at[which]))

    def combine_tile(which):
        loc = loc2.at[which]

        @pl.when(i == 0)
        def _():
            loc2[:, SLOT_ROWS:, :] = jnp.zeros((2, MOE_TOPK * tm, loc2.shape[2]), loc2.dtype)
            fetch(i, which)

        @pl.when(i + 1 < n_steps)
        def _():
            fetch(i + 1, 1 - which)

        x1 = x1_ref[...]
        h = _dot(x1, wsgu_ref[...])
        f = h.shape[1] // 2
        shared = _dot(jax.nn.silu(h[:, :f]) * h[:, f:], wsdn_ref[...])

        _wait_slot_copies(loc, sems.at[which], tm)
        n_fixed = SLOT_ROWS // tm

        gate = gate_ref[...]
        pos = pos_ref[...]

        def gather_chunk(sc):
            slot = sc * tm + lax.broadcasted_iota(jnp.int32, (tm, tm), 1)
            w = jnp.zeros((tm, tm), jnp.float32)
            for k in range(MOE_TOPK):
                w = jnp.where(slot == pos[:, k:k + 1], gate[:, k:k + 1], w)
            return _dot(w, loc[sc * tm:(sc + 1) * tm, :])

        routed = shared
        for sc in range(n_fixed):
            routed = routed + gather_chunk(sc)
        y_ref[...] = routed
        overflow = _overflow_rows(i, cnt_ref, off_ref)
        for sc in range(n_fixed, _local_rows(tm) // tm):
            @pl.when((sc - n_fixed) * tm < overflow)
            def _(sc=sc):
                y_ref[...] += gather_chunk(sc)

        y_ref[...] = _layer_norm(alpha * x1 + y_ref[...], g2_ref[...], b2_ref[...])

    for which in range(2):
        pl.when(i % 2 == which)(functools.partial(combine_tile, which))


def _combine(x1, gate, pos, seg_cnt, seg_off, seg_dst, ys, w_sh_gu, w_sh_down, ln_g, ln_b, alpha):
    t, dm = x1.shape
    tm = min(MOE_TILE, t)
    const = lambda a: pl.BlockSpec(a.shape, lambda i, *_: (0,) * a.ndim, pipeline_mode=pl.Buffered(1))
    return pl.pallas_call(
        functools.partial(_combine_kernel, tm=tm, alpha=alpha, n_steps=t // tm),
        grid_spec=pltpu.PrefetchScalarGridSpec(
            num_scalar_prefetch=3,
            grid=(t // tm,),
            in_specs=[pl.BlockSpec((tm, dm), lambda i, *_: (i, 0)),
                      pl.BlockSpec((tm, SLOT_PAD), lambda i, *_: (i, 0)),
                      pl.BlockSpec((tm, SLOT_PAD), lambda i, *_: (i, 0)),
                      const(w_sh_gu), const(w_sh_down), const(ln_g), const(ln_b),
                      pl.BlockSpec(memory_space=pl.ANY)],
            out_specs=pl.BlockSpec((tm, dm), lambda i, *_: (i, 0)),
            scratch_shapes=[pltpu.VMEM((2, _local_rows(tm), dm), MXU_DTYPE), pltpu.SemaphoreType.DMA((2,))],
        ),
        out_shape=jax.ShapeDtypeStruct((t, dm), jnp.float32),
        compiler_params=_params("arbitrary"),
        name="moe_combine",
    )(seg_cnt, seg_off, seg_dst, x1, gate, pos, w_sh_gu, w_sh_down, ln_g, ln_b, ys)


def _moe(x1, gate, pos, stats, counts, w_exp_gu, w_exp_down, w_sh_gu, w_sh_down, ln_g, ln_b, alpha):
    t = x1.shape[0]
    bm = EXPERT_ROWS if t * MOE_TOPK >= N_EXPERTS * EXPERT_ROWS else EXPERT_ROWS_FEW
    n_tiles = stats.shape[0]
    n_rows = (-(-(t * MOE_TOPK + n_tiles * N_EXPERTS * (SEG_ALIGN - 1) + SLOT_ROWS) // bm) + N_EXPERTS) * bm
    cnt = counts.reshape(N_EXPERTS).astype(jnp.int32)
    padded = (cnt + SEG_SLOT + bm - 1) // bm * bm
    end = jnp.cumsum(padded)
    start = end - padded
    n_used = end[-1:] // bm
    zlo = jnp.concatenate([(start + cnt) // SEG_ALIGN, n_used])
    zhi = jnp.concatenate([end // SEG_ALIGN, jnp.full((1,), n_rows // bm, jnp.int32)])
    block_start = jnp.arange(n_rows // bm, dtype=jnp.int32) * bm
    block_expert = jnp.minimum(jnp.sum((end[None, :] <= block_start[:, None]).astype(jnp.int32), axis=1),
                               N_EXPERTS - 1)
    seg = stats.astype(jnp.int32)
    seg_cnt, seg_off = seg[:, 1, :].reshape(-1), seg[:, 2, :].reshape(-1)
    seg_dst = (start[None, :] + seg[:, 0, :]).reshape(-1)
    pos_t = jnp.transpose(pos.reshape(n_tiles, t // n_tiles, SLOT_PAD), (0, 2, 1))
    xs = _dispatch(x1, pos_t, seg_cnt, seg_off, seg_dst, zlo, zhi, n_rows, bm)
    ys = _experts(xs, block_expert, n_used, w_exp_gu, w_exp_down, bm)
    return _combine(x1, gate, pos, seg_cnt, seg_off, seg_dst, ys, w_sh_gu, w_sh_down, ln_g, ln_b, alpha)


def _layer(x, pos0, past, p, alpha):
    (w_in, a_re, a_im, log_dt, b_re, b_im, c_re, c_im, ssm_d, w_glu, w_attn_proj, w_out, ln1_g, ln1_b,
     w_router, router_bias, w_exp_gu, w_exp_down, w_sh_gu, w_sh_down, ln2_g, ln2_b) = p
    nb, seq, dm = x.shape
    assert pos0 % CHUNK == 0 and seq % Q_BLOCK == 0 and Q_BLOCK == CHUNK and seq % S5_STEPS == 0
    assert nb * seq * MOE_TOPK < 2 ** 24
    t = nb * seq
    f32 = jnp.float32
    x2 = x.reshape(t, dm)
    pos = pos0 + jnp.arange(seq, dtype=jnp.int32)

    u, q, iq, k, v, ik2, ik, iw, k_heads, v_heads = _in_proj(x2, _pack_w_in(w_in), pos, seq)

    a_cat, bcat, ccat = _s5_discretise(a_re, a_im, log_dt, b_re, b_im, c_re, c_im)
    if past is None:
        h0 = jnp.zeros((nb, 2 * NS), f32)
    else:
        h0 = jnp.concatenate([past[3].reshape(nb, NS), past[4].reshape(nb, NS)], axis=1).astype(f32)
    ssm, h_t = _s5(u.reshape(nb, seq, D_SSM), h0, a_cat, bcat, ccat, ssm_d.reshape(1, D_SSM).astype(f32),
                   w_glu.astype(MXU_DTYPE))
    ssm2 = ssm.reshape(t, dm)
    h_re = h_t[:, :NS].reshape(nb, N_GROUPS, N_STATE)
    h_im = h_t[:, NS:].reshape(nb, N_GROUPS, N_STATE)

    k3, v3, ik23 = (a.reshape(nb, seq, LANES) for a in (k, v, ik2))
    if past is not None:
        ck, cv, cik = past[0], past[1], past[2]
        n_past = ck.shape[1]
        k3 = jnp.concatenate([ck.reshape(nb, n_past, LANES), k3], axis=1)
        v3 = jnp.concatenate([cv.reshape(nb, n_past, LANES), v3], axis=1)
        ik23 = jnp.concatenate([jnp.concatenate([cik, cik], axis=-1), ik23], axis=1)
    attn = _dsa(q.reshape(nb, seq, D_ATTN), iq.reshape(nb, seq, IDX_HEADS * IDX_DIM), iw.reshape(nb, seq, LANES),
                k3, v3, ik23, pos0 // CHUNK)

    wap = jnp.transpose(w_attn_proj.reshape(N_KV_HEADS, KV_GROUP, HEAD_DIM, dm), (1, 0, 2, 3)).reshape(D_ATTN, dm)
    wg = w_in[:, w_in.shape[1] - 2 * dm:]
    row = lambda a: a.reshape(1, -1).astype(f32)
    x1, gate, pos_slot, stats, counts = _mix(
        x2, attn.reshape(t, D_ATTN), ssm2, wg.astype(MXU_DTYPE), wap.astype(MXU_DTYPE), w_out.astype(MXU_DTYPE),
        row(ln1_g), row(ln1_b), *_split(w_router.astype(f32)), row(router_bias), alpha)

    y = _moe(x1, gate, pos_slot, stats, counts, w_exp_gu, w_exp_down,
             w_sh_gu.astype(MXU_DTYPE), w_sh_down.astype(MXU_DTYPE), row(ln2_g), row(ln2_b), alpha)
    state = (k_heads.reshape(nb, seq, N_KV_HEADS, HEAD_DIM), v_heads.reshape(nb, seq, N_KV_HEADS, HEAD_DIM),
             ik.reshape(nb, seq, IDX_DIM), h_re, h_im)
    return y.reshape(nb, seq, dm), state


def kernel(x_prompt, x_sample, cache_k, cache_v, cache_idx_k, state_ssm_re, state_ssm_im, w_in, ssm_a_re, ssm_a_im, ssm_log_dt, ssm_b_re, ssm_b_im, ssm_c_re, ssm_c_im, ssm_d, w_glu, w_attn_proj, w_out, ln1_g, ln1_b, w_router, router_bias, w_exp_gu, w_exp_down, w_sh_gu, w_sh_down, ln2_g, ln2_b):
    weights = (w_in, ssm_a_re, ssm_a_im, ssm_log_dt, ssm_b_re, ssm_b_im, ssm_c_re, ssm_c_im, ssm_d,
               w_glu, w_attn_proj, w_out, ln1_g, ln1_b, w_router, router_bias,
               w_exp_gu, w_exp_down, w_sh_gu, w_sh_down, ln2_g, ln2_b)
    depth = w_in.shape[0]
    alpha = (2 * depth) ** 0.25
    past_len = cache_k.shape[2]
    y_p, y_s = x_prompt, x_sample
    new_p, new_s = [], []
    for l in range(depth):
        p_l = tuple(w[l] for w in weights)
        y_p, st_p = _layer(y_p, 0, None, p_l, alpha)
        y_s, st_s = _layer(y_s, past_len,
                           (cache_k[l], cache_v[l], cache_idx_k[l], state_ssm_re[l], state_ssm_im[l]), p_l, alpha)
        new_p.append(st_p)
        new_s.append(st_s)
    k_p, v_p, ik_p, hre_p, him_p = [jnp.stack(a) for a in zip(*new_p)]
    k_s, v_s, ik_s, hre_s, him_s = [jnp.stack(a) for a in zip(*new_s)]
    return (y_p, y_s, k_p, v_p, ik_p, hre_p, him_p, k_s, v_s, ik_s, hre_s, him_s)
```

```python
import functools

import jax
import jax.numpy as jnp
from jax import lax
from jax.experimental import pallas as pl
from jax.experimental.pallas import tpu as pltpu

CHUNK = 64
D_SSM = 512
SSM_GROUP = 16
N_GROUPS = D_SSM // SSM_GROUP
N_STATE = 64
N_HEADS = 8
N_KV_HEADS = 2
HEAD_DIM = 64
D_ATTN = N_HEADS * HEAD_DIM
KV_GROUP = N_HEADS // N_KV_HEADS
ROPE_DIM = HEAD_DIM // 4
ROPE_THETA = 500000.0
IDX_HEADS = 8
IDX_DIM = 64
TOPK_KEYS = 256
Q_BLOCK = 64
N_EXPERTS = 64
MOE_TOPK = 6
D_EXPERT = 256
D_SHARED = 256
ROUTED_SCALE = 2.5
LN_EPS = 1e-5

LANES = 128
MXU_DIM = 256
KEY_CHUNK = 256
DSA_ROWS = 256
EXPERT_ROWS = 512
EXPERT_ROWS_FEW = 128
MOE_TILE = 512
SEG_ALIGN = 16
SEG_SLOT = 64
SLOT_ROWS = N_EXPERTS * SEG_SLOT
SLOT_PAD = 8
NS = N_GROUPS * N_STATE
VMEM_LIMIT = 56 * 1024 * 1024
NEG_BIG = -1e30
INT_MIN = -2147483648

MXU_DTYPE = jnp.bfloat16


def _dot(a, b):
    return jnp.dot(a.astype(MXU_DTYPE), b.astype(MXU_DTYPE), preferred_element_type=jnp.float32)


def _dot_nt(a, b):
    return lax.dot_general(a.astype(MXU_DTYPE), b.astype(MXU_DTYPE), (((1,), (1,)), ((), ())),
                           preferred_element_type=jnp.float32)


def _split(a):
    high = a.astype(MXU_DTYPE)
    return high, (a - high.astype(jnp.float32)).astype(MXU_DTYPE)


def _params(*sem):
    return pltpu.CompilerParams(dimension_semantics=sem, vmem_limit_bytes=VMEM_LIMIT)


def _const_spec(shape):
    return pl.BlockSpec(shape, lambda *_: (0,) * len(shape), pipeline_mode=pl.Buffered(1))


_C_U, _C_Q, _C_IQ, _C_K, _C_V, _C_IK, _C_IW, _C_END = 0, 512, 1024, 1536, 1664, 1792, 1920, 2048


def _inproj_kernel(x_ref, w_ref, cos_ref, sin_ref,
                   u_ref, q_ref, iq_ref, k_ref, v_ref, ik2_ref, ik_ref, iw_ref, kh_ref, vh_ref):
    z = _dot(x_ref[...], w_ref[...])
    cos = cos_ref[...]
    sin = sin_ref[...]
    lane = lax.broadcasted_iota(jnp.int32, cos.shape, 1)
    first_half = (lane % HEAD_DIM) < (ROPE_DIM // 2)

    def rope(zc):
        partner = jnp.where(first_half, pltpu.roll(zc, LANES - ROPE_DIM // 2, 1), pltpu.roll(zc, ROPE_DIM // 2, 1))
        return zc * cos + partner * sin

    u_ref[...] = z[:, _C_U:_C_Q]
    for c in range(4):
        q_ref[:, c * LANES:(c + 1) * LANES] = (
            rope(z[:, _C_Q + c * LANES:_C_Q + (c + 1) * LANES]) * HEAD_DIM ** -0.5).astype(q_ref.dtype)
        iq_ref[:, c * LANES:(c + 1) * LANES] = (
            rope(z[:, _C_IQ + c * LANES:_C_IQ + (c + 1) * LANES]) * IDX_DIM ** -0.5).astype(iq_ref.dtype)
    k = rope(z[:, _C_K:_C_V])
    v = z[:, _C_V:_C_IK]
    k_ref[...] = k
    v_ref[...] = v
    for g in range(N_KV_HEADS):
        kh_ref[:, g, :] = k[:, g * HEAD_DIM:(g + 1) * HEAD_DIM]
        vh_ref[:, g, :] = v[:, g * HEAD_DIM:(g + 1) * HEAD_DIM]
    ik2 = rope(z[:, _C_IK:_C_IW])
    ik2_ref[...] = ik2
    ik_ref[...] = ik2[:, :IDX_DIM]
    iw_ref[...] = z[:, _C_IW:_C_END]


def _rope_tables(pos):
    half = ROPE_DIM // 2
    inv_freq = ROPE_THETA ** (-jnp.arange(half, dtype=jnp.float32) * 2.0 / ROPE_DIM)
    ang = pos.astype(jnp.float32)[:, None] * inv_freq
    cos, sin = jnp.cos(ang), jnp.sin(ang)
    n = pos.shape[0]
    pad = HEAD_DIM - ROPE_DIM
    cos_h = jnp.concatenate([cos, cos, jnp.ones((n, pad), jnp.float32)], axis=-1)
    sin_h = jnp.concatenate([-sin, sin, jnp.zeros((n, pad), jnp.float32)], axis=-1)
    return jnp.tile(cos_h, (1, LANES // HEAD_DIM)), jnp.tile(sin_h, (1, LANES // HEAD_DIM))


def _pack_w_in(w_in):
    s = [0, 512, 1024, 1152, 1280, 1792, 1856, 1864]
    w_u, w_q, w_k, w_v = w_in[:, s[0]:s[1]], w_in[:, s[1]:s[2]], w_in[:, s[2]:s[3]], w_in[:, s[3]:s[4]]
    w_iq, w_ik, w_iw = w_in[:, s[4]:s[5]], w_in[:, s[5]:s[6]], w_in[:, s[6]:s[7]]
    d = w_in.shape[0]
    w_qh = w_q.reshape(d, N_KV_HEADS, KV_GROUP, HEAD_DIM)
    w_qp = jnp.transpose(w_qh, (0, 2, 1, 3)).reshape(d, D_ATTN)
    w_iwp = jnp.concatenate([w_iw, jnp.zeros((d, LANES - IDX_HEADS), w_in.dtype)], axis=1)
    return jnp.concatenate([w_u, w_qp, w_iq, w_k, w_v, w_ik, w_ik, w_iwp], axis=1).astype(MXU_DTYPE)


def _in_proj(x2, w_pack, pos, seq):
    t, d = x2.shape
    tm = min(512, t)
    cos, sin = _rope_tables(pos)
    if seq >= tm:
        per = seq // tm
        tab_map = lambda i: (i % per, 0)
    else:
        cos, sin = jnp.tile(cos, (tm // seq, 1)), jnp.tile(sin, (tm // seq, 1))
        tab_map = lambda i: (0, 0)
    row = lambda w: pl.BlockSpec((tm, w), lambda i: (i, 0))
    widths = (D_SSM, D_ATTN, IDX_HEADS * IDX_DIM, LANES, LANES, LANES, IDX_DIM, LANES)
    heads = pl.BlockSpec((tm, N_KV_HEADS, HEAD_DIM), lambda i: (i, 0, 0))
    return pl.pallas_call(
        _inproj_kernel,
        grid=(t // tm,),
        in_specs=[row(d), _const_spec(w_pack.shape),
                  pl.BlockSpec((tm, LANES), tab_map), pl.BlockSpec((tm, LANES), tab_map)],
        out_specs=[row(w) for w in widths] + [heads, heads],
        out_shape=[jax.ShapeDtypeStruct((t, w), MXU_DTYPE if i in (1, 2) else jnp.float32)
                   for i, w in enumerate(widths)]
        + [jax.ShapeDtypeStruct((t, N_KV_HEADS, HEAD_DIM), jnp.float32)] * 2,
        compiler_params=_params("parallel"),
        name="in_proj",
    )(x2, w_pack, cos, sin)


S5_STEPS = 16
S5_COLS = 512


def _s5_kernel(u_ref, h0_ref, a_ref, bcat_ref, ccat_ref, d_ref, wglu_ref,
               out_ref, ht_ref, h_scr, bu_scr, hall_scr, *, nb):
    @pl.when(pl.program_id(0) == 0)
    def _():
        h_scr[...] = h0_ref[...]

    u = jnp.swapaxes(u_ref[...], 0, 1).reshape(S5_STEPS * nb, D_SSM)
    per_tile = MXU_DIM // N_STATE
    for j in range(2 * NS // MXU_DIM):
        kb = ((j % (NS // MXU_DIM)) * per_tile * SSM_GROUP) // MXU_DIM * MXU_DIM
        cols = slice(j * MXU_DIM, (j + 1) * MXU_DIM)
        bu_scr[:, cols] = _dot(u[:, kb:kb + MXU_DIM], bcat_ref[kb:kb + MXU_DIM, cols])
    for cb in range(NS // S5_COLS):
        re = slice(cb * S5_COLS, (cb + 1) * S5_COLS)
        im = slice(NS + cb * S5_COLS, NS + (cb + 1) * S5_COLS)
        a_re, a_im = a_ref[0:1, re], a_ref[1:2, re]
        h_re, h_im = h_scr[:, re], h_scr[:, im]
        for t in range(S5_STEPS):
            rows = slice(t * nb, (t + 1) * nb)
            h_re, h_im = (a_re * h_re - a_im * h_im + bu_scr[rows, re],
                          a_re * h_im + a_im * h_re + bu_scr[rows, im])
            hall_scr[rows, re] = h_re
            hall_scr[rows, im] = h_im
        h_scr[:, re] = h_re
        h_scr[:, im] = h_im
    y_tiles = []
    for n in range(D_SSM // MXU_DIM):
        cols = slice(n * MXU_DIM, (n + 1) * MXU_DIM)
        k0, kn = n * MXU_DIM // SSM_GROUP * N_STATE, MXU_DIM // SSM_GROUP * N_STATE
        y_tiles.append(_dot(hall_scr[:, k0:k0 + kn], ccat_ref[k0:k0 + kn, cols])
                       + _dot(hall_scr[:, NS + k0:NS + k0 + kn], ccat_ref[NS + k0:NS + k0 + kn, cols]))
    y = jnp.concatenate(y_tiles, axis=1) + d_ref[...] * u
    y = jax.nn.gelu(y)
    g = _dot(y, wglu_ref[...])
    dm = g.shape[1] // 2
    out_ref[...] = jnp.swapaxes((g[:, :dm] * jax.nn.sigmoid(g[:, dm:])).reshape(S5_STEPS, nb, dm), 0, 1)
    ht_ref[...] = h_scr[...]


def _s5_discretise(a_re, a_im, log_dt, b_re, b_im, c_re, c_im):
    f32 = jnp.float32
    ar, ai = a_re.astype(f32), a_im.astype(f32)
    dt = jnp.exp(log_dt.astype(f32))[:, None]
    mag = jnp.exp(dt * ar)
    abar_re, abar_im = mag * jnp.cos(dt * ai), mag * jnp.sin(dt * ai)
    den = ar * ar + ai * ai
    num_re, num_im = abar_re - 1.0, abar_im
    coef_re = (num_re * ar + num_im * ai) / den
    coef_im = (num_im * ar - num_re * ai) / den
    br, bi = b_re.astype(f32), b_im.astype(f32)
    bbar_re = coef_re[..., None] * br - coef_im[..., None] * bi
    bbar_im = coef_re[..., None] * bi + coef_im[..., None] * br
    eye = jnp.eye(N_GROUPS, dtype=f32)
    bd = lambda m: jnp.einsum("gnc,gh->gchn", m, eye).reshape(D_SSM, NS)
    bcat = jnp.concatenate([bd(bbar_re), bd(bbar_im)], axis=1)
    cd = lambda m: jnp.einsum("gcn,gh->gnhc", m, eye).reshape(NS, D_SSM)
    ccat = jnp.concatenate([cd(c_re.astype(f32)), cd(-c_im.astype(f32))], axis=0)
    a_cat = jnp.stack([abar_re.reshape(NS), abar_im.reshape(NS)])
    return a_cat, bcat.astype(MXU_DTYPE), ccat.astype(MXU_DTYPE)


def _s5(u, h0, a_cat, bcat, ccat, d_skip, w_glu):
    nb, seq, _ = u.shape
    dm = w_glu.shape[1] // 2
    rows = S5_STEPS * nb
    return pl.pallas_call(
        functools.partial(_s5_kernel, nb=nb),
        grid=(seq // S5_STEPS,),
        in_specs=[pl.BlockSpec((nb, S5_STEPS, D_SSM), lambda i: (0, i, 0)),
                  _const_spec(h0.shape), _const_spec(a_cat.shape), _const_spec(bcat.shape),
                  _const_spec(ccat.shape), _const_spec(d_skip.shape), _const_spec(w_glu.shape)],
        out_specs=[pl.BlockSpec((nb, S5_STEPS, dm), lambda i: (0, i, 0)),
                   pl.BlockSpec(h0.shape, lambda i: (0, 0))],
        out_shape=[jax.ShapeDtypeStruct((nb, seq, dm), jnp.float32),
                   jax.ShapeDtypeStruct(h0.shape, jnp.float32)],
        scratch_shapes=[pltpu.VMEM(h0.shape, jnp.float32),
                        pltpu.VMEM((rows, 2 * NS), jnp.float32),
                        pltpu.VMEM((rows, 2 * NS), jnp.float32)],
        compiler_params=_params("arbitrary"),
        name="s5_scan_glu",
    )(u, h0, a_cat, bcat, ccat, d_skip, w_glu)


KEY_NEG_INF = -2139095041


def _ordered_f32(key):
    bits = jnp.where(key >= 0, key, key ^ jnp.int32(0x7FFFFFFF))
    return jnp.where(key <= KEY_NEG_INF, -jnp.inf, pltpu.bitcast(bits, jnp.float32))


def _half_mask(shape, upper):
    lane = lax.broadcasted_iota(jnp.int32, shape, 1)
    return (lane >= HEAD_DIM) if upper else (lane < HEAD_DIM)


def _dsa_kernel(q_ref, iq_ref, iw_ref, k_ref, v_ref, ik_ref, o_ref,
                kb, vs0, vs1, ikb, sc_scr, sct_scr, wb_scr, iqlhs_scr, qlhs_scr, s_scr, m_scr, acc_scr,
                *, qt, n_tiles, n_keys, n_sel, first_chunk, idx_bits):
    j = pl.program_id(1)
    kc = KEY_CHUNK
    n_sub = qt // Q_BLOCK
    qtp = max(qt, LANES)

    @pl.when(j == 0)
    def _():
        v = v_ref[...]
        lower = lax.broadcasted_iota(jnp.int32, v.shape, v.ndim - 1) < HEAD_DIM
        kb[...] = k_ref[...].astype(MXU_DTYPE)
        vs0[...] = jnp.where(lower, v, 1.0).astype(MXU_DTYPE)
        vs1[...] = jnp.where(lower, 1.0, v).astype(MXU_DTYPE)
        ikb[...] = ik_ref[...].astype(MXU_DTYPE)

    row_blk = lax.broadcasted_iota(jnp.int32, (qt, 1), 0) // Q_BLOCK
    n_vis = jnp.minimum((first_chunk + j * n_sub + row_blk + 1) * CHUNK, n_keys)
    if n_tiles == 1:
        n_ck = (min((first_chunk + n_sub) * CHUNK, n_keys) + kc - 1) // kc
        chunks = functools.partial(lax.fori_loop, 0, n_ck, unroll=2)
    else:
        n_ck = (jnp.minimum((first_chunk + (j + 1) * n_sub) * CHUNK, n_keys) + kc - 1) // kc
        chunks = functools.partial(lax.fori_loop, 0, n_ck)
    lane_k = lax.broadcasted_iota(jnp.int32, (qt, kc), 1)

    iq = iq_ref[...]
    iw = iw_ref[...] * IDX_HEADS ** -0.5
    for h in range(IDX_HEADS):
        iqlhs_scr[h] = jnp.where(_half_mask((qt, LANES), h % 2 == 1),
                                 iq[:, (h // 2) * LANES:(h // 2 + 1) * LANES], 0.0)
        wb_scr[h] = jnp.broadcast_to(iw[:, h:h + 1], (qt, LANES))

    def score_chunk(c, carry):
        ikc = ikb[c]
        score = None
        for h in range(IDX_HEADS):
            rel = jnp.maximum(_dot_nt(iqlhs_scr[h], ikc), 0.0)
            w = wb_scr[h]
            term = jnp.concatenate([rel[:, s * LANES:(s + 1) * LANES] * w for s in range(kc // LANES)], axis=1)
            score = term if score is None else score + term
        score = jnp.where(c * kc + lane_k < n_vis, score, -jnp.inf)
        sc_scr[c] = score
        if qtp > qt:
            score = jnp.concatenate([score, jnp.full((qtp - qt, kc), -jnp.inf, jnp.float32)], axis=0)
        sct_scr[c] = score.T
        return carry

    chunks(score_chunk, 0)

    sub_k = lax.broadcasted_iota(jnp.int32, (kc // 8, 8, qtp), 0) * 8 + lax.broadcasted_iota(
        jnp.int32, (kc // 8, 8, qtp), 1)

    def count(pred):
        def body(c, acc):
            hit = pred(sct_scr[c].reshape(kc // 8, 8, qtp), c * kc + sub_k)
            return acc + jnp.sum(jnp.where(hit, 1.0, 0.0), axis=0)
        acc = chunks(body, jnp.zeros((8, qtp), jnp.float32))
        for shift in (4, 2, 1):
            acc = acc + pltpu.roll(acc, shift, 0)
        return acc

    def value_bit(i, t_pos):
        cand = t_pos + (jnp.int32(1) << (31 - i))
        cand_value = _ordered_f32(cand)
        return jnp.where(count(lambda sc, idx: sc >= cand_value[None]) >= n_sel, cand, t_pos)

    t = _ordered_f32(lax.fori_loop(0, 32, value_bit, jnp.full((8, qtp), INT_MIN, jnp.int32)))
    def count_above_and_at(t):
        def body(c, accs):
            sc = sct_scr[c].reshape(kc // 8, 8, qtp)
            return (accs[0] + jnp.sum(jnp.where(sc > t[None], 1.0, 0.0), axis=0),
                    accs[1] + jnp.sum(jnp.where(sc == t[None], 1.0, 0.0), axis=0))
        zero = jnp.zeros((8, qtp), jnp.float32)
        above, at = chunks(body, (zero, zero))
        for shift in (4, 2, 1):
            above, at = above + pltpu.roll(above, shift, 0), at + pltpu.roll(at, shift, 0)
        return above, at

    n_above, n_at = count_above_and_at(t)
    need = n_sel - n_above
    n_ge = n_above + n_at

    def last_tied_index():
        def index_bit(i, m):
            cand = m + (jnp.int32(1) << (idx_bits - 1 - i))
            below = count(lambda sc, idx: (sc == t[None]) & (idx < cand[None]))
            return jnp.where(below < need, cand, m)
        return lax.fori_loop(0, idx_bits, index_bit, jnp.zeros((8, qtp), jnp.int32))

    m_idx = lax.cond(jnp.max(n_ge) > n_sel, last_tied_index,
                     lambda: jnp.full((8, qtp), 2 ** 30, jnp.int32))
    t_rep = jnp.broadcast_to(t[0:1], (LANES, qtp)).T[:qt]
    m_rep = jnp.broadcast_to(m_idx[0:1], (LANES, qtp)).T[:qt]
    t_row = jnp.concatenate([t_rep] * (kc // LANES), axis=1)
    m_row = jnp.concatenate([m_rep] * (kc // LANES), axis=1)

    def selection_bias(c):
        sc = sc_scr[c]
        idx = c * kc + lane_k
        sel = ((sc > t_row) | ((sc == t_row) & (idx <= m_row))) & (idx < n_vis)
        return jnp.where(sel, 0.0, NEG_BIG)

    q = q_ref[...]
    rows = KV_GROUP * qt
    for g in range(N_KV_HEADS):
        for r in range(KV_GROUP):
            qlhs_scr[g, r * qt:(r + 1) * qt] = jnp.where(
                _half_mask((qt, LANES), g == 1), q[:, r * LANES:(r + 1) * LANES], 0.0)
    m_scr[...] = jnp.full(m_scr.shape, NEG_BIG, jnp.float32)
    acc_scr[...] = jnp.zeros(acc_scr.shape, jnp.float32)

    def logits_chunk(c, carry):
        bias = selection_bias(c)[None]
        for g in range(N_KV_HEADS):
            s = (_dot_nt(qlhs_scr[g], kb[c]).reshape(KV_GROUP, qt, kc) + bias).reshape(rows, kc)
            s_scr[g, c] = s
            m = m_scr[g]
            for u in range(kc // LANES):
                m = jnp.maximum(m, s[:, u * LANES:(u + 1) * LANES])
            m_scr[g] = m
        return carry

    chunks(logits_chunk, 0)
    for g in range(N_KV_HEADS):
        m_scr[g] = jnp.broadcast_to(jnp.max(m_scr[g], axis=-1, keepdims=True), (rows, LANES))

    def pv_chunk(c, carry):
        for g, vs in enumerate((vs0, vs1)):
            m = m_scr[g]
            s = s_scr[g, c]
            p = jnp.concatenate([jnp.exp(s[:, u * LANES:(u + 1) * LANES] - m) for u in range(kc // LANES)], axis=1)
            acc_scr[g] += _dot(p, vs[c])
        return carry

    chunks(pv_chunk, 0)
    outs = [acc_scr[g] / pltpu.roll(acc_scr[g], HEAD_DIM, 1) for g in range(N_KV_HEADS)]
    lower = _half_mask((qt, LANES), False)
    for r in range(KV_GROUP):
        rs = slice(r * qt, (r + 1) * qt)
        o_ref[:, r * LANES:(r + 1) * LANES] = jnp.where(lower, outs[0][rs], outs[1][rs])


def _dsa(q, iq, iw, k_all, v_all, ik2_all, first_chunk):
    nb, seq, _ = q.shape
    n_keys = k_all.shape[1]
    n_sel = min(TOPK_KEYS, n_keys // 4)
    kc = KEY_CHUNK
    qt = min(DSA_ROWS, seq)
    assert seq % qt == 0
    n_ch = -(-n_keys // kc)
    pad = n_ch * kc - n_keys
    chunked = lambda a: jnp.pad(a, ((0, 0), (0, pad), (0, 0))).reshape(nb, n_ch, kc, LANES)
    qspec = lambda w: pl.BlockSpec((None, qt, w), lambda b, j: (b, j, 0))
    kspec = pl.BlockSpec((None, n_ch, kc, LANES), lambda b, j: (b, 0, 0, 0))
    rows = KV_GROUP * qt
    return pl.pallas_call(
        functools.partial(_dsa_kernel, qt=qt, n_tiles=seq // qt, n_keys=n_keys, n_sel=n_sel, first_chunk=first_chunk,
                          idx_bits=max(1, (n_ch * kc - 1).bit_length())),
        grid=(nb, seq // qt),
        in_specs=[qspec(D_ATTN), qspec(IDX_HEADS * IDX_DIM), qspec(LANES), kspec, kspec, kspec],
        out_specs=qspec(D_ATTN),
        out_shape=jax.ShapeDtypeStruct((nb, seq, D_ATTN), jnp.float32),
        scratch_shapes=[pltpu.VMEM((n_ch, kc, LANES), MXU_DTYPE)] * 4
        + [pltpu.VMEM((n_ch, qt, kc), jnp.float32), pltpu.VMEM((n_ch, kc, max(qt, LANES)), jnp.float32),
           pltpu.VMEM((IDX_HEADS, qt, LANES), jnp.float32), pltpu.VMEM((IDX_HEADS, qt, LANES), MXU_DTYPE),
           pltpu.VMEM((N_KV_HEADS, rows, LANES), MXU_DTYPE),
           pltpu.VMEM((N_KV_HEADS, n_ch, rows, kc), jnp.float32),
           pltpu.VMEM((N_KV_HEADS, rows, LANES), jnp.float32), pltpu.VMEM((N_KV_HEADS, rows, LANES), jnp.float32)],
        compiler_params=_params("arbitrary", "arbitrary"),
        name="dsa_attention",
    )(q, iq, iw, chunked(k_all), chunked(v_all), chunked(ik2_all))


def _layer_norm(h, g, b):
    mu = jnp.mean(h, axis=-1, keepdims=True)
    var = jnp.mean(jnp.square(h - mu), axis=-1, keepdims=True)
    return (h - mu) * lax.rsqrt(var + LN_EPS) * g + b


def _mix_kernel(x_ref, attn_ref, ssm_ref, wg_ref, wap_ref, wout_ref, g1_ref, b1_ref, wrh_ref, wrl_ref, rb_ref,
                x1_ref, gate_ref, pos_ref, stats_ref, cnt_ref, cnt_scr, *, alpha):
    i = pl.program_id(0)

    @pl.when(i == 0)
    def _():
        cnt_scr[...] = jnp.zeros_like(cnt_scr)

    tm, dm = x_ref.shape
    n_split = 2 if tm % 16 == 0 else 1
    hm = tm // n_split
    lane = lax.broadcasted_iota(jnp.int32, (hm, N_EXPERTS), 1).astype(jnp.float32)
    slot = lax.broadcasted_iota(jnp.int32, (hm, SLOT_PAD), 1)

    def route_rows(rows):
        x = x_ref[rows, :]
        gates = jax.nn.sigmoid(_dot(x, wg_ref[...]))
        attn_out = _dot(attn_ref[rows, :], wap_ref[...])
        mixed = _dot(gates[:, :dm] * ssm_ref[rows, :] + gates[:, dm:] * attn_out, wout_ref[...])
        x1 = _layer_norm(alpha * x + mixed, g1_ref[...], b1_ref[...])
        x1_ref[rows, :] = x1
        x1_hi, x1_lo = _split(x1)
        logits = _dot(x1_hi, wrh_ref[...]) + (_dot(x1_hi, wrl_ref[...]) + _dot(x1_lo, wrh_ref[...]))
        scores = jax.nn.sigmoid(logits)
        cur = scores + rb_ref[...]
        chosen = jnp.zeros_like(scores)
        picks = []
        for _ in range(MOE_TOPK):
            best = jnp.max(cur, axis=-1, keepdims=True)
            e_k = jnp.min(jnp.where(cur == best, lane, float(N_EXPERTS)), axis=-1, keepdims=True)
            hot = lane == e_k
            picks.append((hot, jnp.sum(jnp.where(hot, scores, 0.0), axis=-1, keepdims=True)))
            chosen = jnp.where(hot, 1.0, chosen)
            cur = jnp.where(hot, -jnp.inf, cur)
        total = picks[0][1]
        for _, s_k in picks[1:]:
            total = total + s_k
        return chosen, picks, total

    halves = [route_rows(slice(h * hm, (h + 1) * hm)) for h in range(n_split)]
    chosen = jnp.concatenate([h[0] for h in halves], axis=0)

    row = lax.broadcasted_iota(jnp.int32, (tm, tm), 0)
    col = lax.broadcasted_iota(jnp.int32, (tm, tm), 1)
    before = _dot(jnp.where(col < row, 1.0, 0.0), chosen)
    tile_cnt = jnp.ceil(jnp.sum(chosen, axis=0, keepdims=True) / SEG_ALIGN) * SEG_ALIGN
    e_row = lax.broadcasted_iota(jnp.int32, (N_EXPERTS, N_EXPERTS), 0)
    e_col = lax.broadcasted_iota(jnp.int32, (N_EXPERTS, N_EXPERTS), 1)
    overflow = jnp.maximum(tile_cnt - SEG_SLOT, 0.0)
    tile_off = jnp.dot(jnp.broadcast_to(overflow, (8, N_EXPERTS)), jnp.where(e_row < e_col, 1.0, 0.0),
                       preferred_element_type=jnp.float32, precision=lax.Precision.HIGHEST)[0:1]
    slot_base = lax.broadcasted_iota(jnp.int32, (1, N_EXPERTS), 1).astype(jnp.float32) * SEG_SLOT
    local = jnp.where(before < SEG_SLOT, slot_base + before, (SLOT_ROWS - SEG_SLOT) + tile_off + before)
    for h, (_, picks, total) in enumerate(halves):
        rows = slice(h * hm, (h + 1) * hm)
        g_out = jnp.zeros((hm, SLOT_PAD), jnp.float32)
        p_out = jnp.zeros((hm, SLOT_PAD), jnp.int32)
        for k, (hot, s_k) in enumerate(picks):
            pos_k = jnp.sum(jnp.where(hot, local[rows], 0.0), axis=-1, keepdims=True)
            g_out = jnp.where(slot == k, s_k / total * ROUTED_SCALE, g_out)
            p_out = jnp.where(slot == k, pos_k.astype(jnp.int32), p_out)
        gate_ref[rows, :] = g_out
        pos_ref[rows, :] = p_out
    srow = lax.broadcasted_iota(jnp.int32, (8, N_EXPERTS), 0)
    stats_ref[...] = jnp.where(srow == 0, cnt_scr[...], jnp.where(srow == 1, tile_cnt, jnp.where(
        srow == 2, tile_off, 0.0)))
    cnt_scr[...] = cnt_scr[...] + tile_cnt
    cnt_ref[...] = cnt_scr[...]


def _mix(x2, attn2, ssm2, wg, wap, wout, ln_g, ln_b, w_router_hi, w_router_lo, router_bias, alpha):
    t, dm = x2.shape
    tm = min(MOE_TILE, t)
    row = lambda w: pl.BlockSpec((tm, w), lambda i: (i, 0))
    consts = (wg, wap, wout, ln_g, ln_b, w_router_hi, w_router_lo, router_bias)
    return pl.pallas_call(
        functools.partial(_mix_kernel, alpha=alpha),
        grid=(t // tm,),
        in_specs=[row(dm), row(D_ATTN), row(dm)] + [_const_spec(c.shape) for c in consts],
        out_specs=[row(dm), row(SLOT_PAD), row(SLOT_PAD),
                   pl.BlockSpec((None, 8, N_EXPERTS), lambda i: (i, 0, 0)),
                   pl.BlockSpec((1, N_EXPERTS), lambda i: (0, 0))],
        out_shape=[jax.ShapeDtypeStruct((t, dm), jnp.float32),
                   jax.ShapeDtypeStruct((t, SLOT_PAD), jnp.float32),
                   jax.ShapeDtypeStruct((t, SLOT_PAD), jnp.int32),
                   jax.ShapeDtypeStruct((t // tm, 8, N_EXPERTS), jnp.float32),
                   jax.ShapeDtypeStruct((1, N_EXPERTS), jnp.float32)],
        scratch_shapes=[pltpu.VMEM((1, N_EXPERTS), jnp.float32)],
        compiler_params=_params("arbitrary"),
        name="mix_ln_router",
    )(x2, attn2, ssm2, *consts)


def _local_rows(tm):
    assert SLOT_ROWS % tm == 0
    return SLOT_ROWS + MOE_TOPK * tm


def _overflow_rows(i, cnt_ref, off_ref):
    last = i * N_EXPERTS + N_EXPERTS - 1
    return off_ref[last] + jnp.maximum(cnt_ref[last] - SEG_SLOT, 0)


def _segment_copies(i, cnt_ref, off_ref, dst_ref, tm, make_copy):
    def segment(e, carry):
        n = cnt_ref[i * N_EXPERTS + e]
        b = pl.multiple_of(dst_ref[i * N_EXPERTS + e], SEG_ALIGN)
        make_copy(pl.multiple_of(e * SEG_SLOT, SEG_SLOT), b, SEG_SLOT).start()

        @pl.when(n > SEG_SLOT)
        def _():
            a = SLOT_ROWS + off_ref[i * N_EXPERTS + e]
            m = n - SEG_SLOT
            for p in [tm >> s for s in range(tm.bit_length()) if (tm >> s) >= SEG_ALIGN]:
                done = (m // (2 * p)) * (2 * p)

                @pl.when((m & p) != 0)
                def _(p=p, done=done):
                    piece = make_copy(pl.multiple_of(a + done, SEG_ALIGN),
                                      pl.multiple_of(b + SEG_SLOT + done, SEG_ALIGN), p)
                    piece.start()
                    piece.wait()

        return carry

    lax.fori_loop(0, N_EXPERTS, segment, 0)


def _wait_slot_copies(buf, sem, tm):
    for _ in range(SLOT_ROWS // tm):
        pltpu.make_async_copy(buf.at[pl.ds(0, tm)], buf.at[pl.ds(0, tm)], sem).wait()


def _dispatch_kernel(cnt_ref, off_ref, dst_ref, zlo_ref, zhi_ref, post_ref, x_ref, xs_ref,
                     loc2, zero_scr, sems, *, tm, n_steps):
    i = pl.program_id(0)

    def sort_and_send(which):
        loc, sem = loc2.at[which], sems.at[which]

        xb = x_ref[...].astype(MXU_DTYPE)
        post = post_ref[...]

        def sort_chunk(sc):
            slot = sc * tm + lax.broadcasted_iota(jnp.int32, (tm, tm), 0)
            onehot = jnp.zeros((tm, tm), jnp.float32)
            for k in range(MOE_TOPK):
                onehot = jnp.where(slot == post[k:k + 1, :], 1.0, onehot)
            loc[sc * tm:(sc + 1) * tm, :] = _dot(onehot, xb).astype(loc.dtype)

        n_fixed = SLOT_ROWS // tm
        for sc in range(n_fixed):
            sort_chunk(sc)
        overflow = _overflow_rows(i, cnt_ref, off_ref)
        for sc in range(n_fixed, _local_rows(tm) // tm):
            pl.when((sc - n_fixed) * tm < overflow)(functools.partial(sort_chunk, sc))

        @pl.when(i >= 1)
        def _():
            _wait_slot_copies(loc2.at[1 - which], sems.at[1 - which], tm)

        _segment_copies(i, cnt_ref, off_ref, dst_ref, tm, lambda a, b, p: pltpu.make_async_copy(
            loc.at[pl.ds(a, p)], xs_ref.at[pl.ds(b, p)], sem))

        @pl.when(i == n_steps - 1)
        def _():
            _wait_slot_copies(loc, sem, tm)

    for which in range(2):
        pl.when(i % 2 == which)(functools.partial(sort_and_send, which))

    @pl.when(i == n_steps - 1)
    def _():
        sem = sems.at[0]
        zero_scr[...] = jnp.zeros_like(zero_scr)

        def zero_copy(group, rows):
            return pltpu.make_async_copy(
                zero_scr.at[pl.ds(0, rows)], xs_ref.at[pl.ds(pl.multiple_of(group * rows, rows), rows)], sem)

        def fill_groups(lo, hi, rows):
            def fill(g, c):
                zero_copy(g, rows).start()
                return c

            def fill_done(g, c):
                zero_copy(g, rows).wait()
                return c

            lax.fori_loop(lo, hi, fill, 0)
            lax.fori_loop(lo, hi, fill_done, 0)

        def segment(s, carry):
            fill_groups(zlo_ref[s], zhi_ref[s], SEG_ALIGN)
            return carry

        lax.fori_loop(0, N_EXPERTS, segment, 0)
        fill_groups(zlo_ref[N_EXPERTS], zhi_ref[N_EXPERTS], zero_scr.shape[0])


def _dispatch(x1, pos_t, seg_cnt, seg_off, seg_dst, zlo, zhi, n_rows, bm):
    t, dm = x1.shape
    tm = min(MOE_TILE, t)
    return pl.pallas_call(
        functools.partial(_dispatch_kernel, tm=tm, n_steps=t // tm),
        grid_spec=pltpu.PrefetchScalarGridSpec(
            num_scalar_prefetch=5,
            grid=(t // tm,),
            in_specs=[pl.BlockSpec((None, SLOT_PAD, tm), lambda i, *_: (i, 0, 0)),
                      pl.BlockSpec((tm, dm), lambda i, *_: (i, 0))],
            out_specs=pl.BlockSpec(memory_space=pl.ANY),
            scratch_shapes=[pltpu.VMEM((2, _local_rows(tm), dm), MXU_DTYPE),
                            pltpu.VMEM((bm, dm), MXU_DTYPE), pltpu.SemaphoreType.DMA((2,))],
        ),
        out_shape=jax.ShapeDtypeStruct((n_rows, dm), MXU_DTYPE),
        compiler_params=_params("arbitrary"),
        name="moe_dispatch",
    )(seg_cnt, seg_off, seg_dst, zlo, zhi, pos_t, x1)


def _expert_kernel(be_ref, nu_ref, xs_ref, wgu_ref, wdn_ref, ys_ref, wgu_scr, wdn_scr):
    i = pl.program_id(0)

    @pl.when(i < nu_ref[0])
    def _():
        @pl.when((i == 0) | (be_ref[i] != be_ref[jnp.maximum(i - 1, 0)]))
        def _():
            wgu_scr[...] = wgu_ref[...].astype(MXU_DTYPE)
            wdn_scr[...] = wdn_ref[...].astype(MXU_DTYPE)

        h = _dot(xs_ref[...], wgu_scr[...])
        f = h.shape[1] // 2
        ys_ref[...] = _dot(jax.nn.silu(h[:, :f]) * h[:, f:], wdn_scr[...]).astype(ys_ref.dtype)

    @pl.when(i >= nu_ref[0])
    def _():
        ys_ref[...] = jnp.zeros_like(ys_ref)


def _experts(xs, block_expert, n_used, w_gu, w_down, bm):
    n_rows, dm = xs.shape
    f2 = w_gu.shape[2]
    used = lambda i, nu: jnp.minimum(i, nu[0] - 1)
    return pl.pallas_call(
        _expert_kernel,
        grid_spec=pltpu.PrefetchScalarGridSpec(
            num_scalar_prefetch=2,
            grid=(n_rows // bm,),
            in_specs=[pl.BlockSpec((bm, dm), lambda i, be, nu: (used(i, nu), 0)),
                      pl.BlockSpec((None, dm, f2), lambda i, be, nu: (be[used(i, nu)], 0, 0)),
                      pl.BlockSpec((None, f2 // 2, dm), lambda i, be, nu: (be[used(i, nu)], 0, 0))],
            out_specs=pl.BlockSpec((bm, dm), lambda i, be, nu: (i, 0)),
            scratch_shapes=[pltpu.VMEM((dm, f2), MXU_DTYPE), pltpu.VMEM((f2 // 2, dm), MXU_DTYPE)],
        ),
        out_shape=jax.ShapeDtypeStruct((n_rows, dm), MXU_DTYPE),
        compiler_params=_params("arbitrary"),
        name="moe_experts",
    )(block_expert, n_used, xs, w_gu, w_down)


def _combine_kernel(cnt_ref, off_ref, dst_ref, x1_ref, gate_ref, pos_ref, wsgu_ref, wsdn_ref, g2_ref, b2_ref,
                    ys_ref, y_ref, loc2, sems, *, tm, alpha, n_steps):
    i = pl.program_id(0)

    def fetch(tile, which):
        _segment_copies(tile, cnt_ref, off_ref, dst_ref, tm, lambda a, b, p: pltpu.make_async_copy(
            ys_ref.at[pl.ds(b, p)], loc2.at[which, pl.ds(a, p)], sems.at[which]))

    def combine_tile(which):
        loc = loc2.at[which]

        @pl.when(i == 0)
        def _():
            loc2[:, SLOT_ROWS:, :] = jnp.zeros((2, MOE_TOPK * tm, loc2.shape[2]), loc2.dtype)
            fetch(i, which)

        @pl.when(i + 1 < n_steps)
        def _():
            fetch(i + 1, 1 - which)

        x1 = x1_ref[...]
        h = _dot(x1, wsgu_ref[...])
        f = h.shape[1] // 2
        shared = _dot(jax.nn.silu(h[:, :f]) * h[:, f:], wsdn_ref[...])

        _wait_slot_copies(loc, sems.at[which], tm)
        n_fixed = SLOT_ROWS // tm

        gate = gate_ref[...]
        pos = pos_ref[...]

        def gather_chunk(sc):
            slot = sc * tm + lax.broadcasted_iota(jnp.int32, (tm, tm), 1)
            w = jnp.zeros((tm, tm), jnp.float32)
            for k in range(MOE_TOPK):
                w = jnp.where(slot == pos[:, k:k + 1], gate[:, k:k + 1], w)
            return _dot(w, loc[sc * tm:(sc + 1) * tm, :])

        routed = shared
        for sc in range(n_fixed):
            routed = routed + gather_chunk(sc)
        y_ref[...] = routed
        overflow = _overflow_rows(i, cnt_ref, off_ref)
        for sc in range(n_fixed, _local_rows(tm) // tm):
            @pl.when((sc - n_fixed) * tm < overflow)
            def _(sc=sc):
                y_ref[...] += gather_chunk(sc)

        y_ref[...] = _layer_norm(alpha * x1 + y_ref[...], g2_ref[...], b2_ref[...])

    for which in range(2):
        pl.when(i % 2 == which)(functools.partial(combine_tile, which))


def _combine(x1, gate, pos, seg_cnt, seg_off, seg_dst, ys, w_sh_gu, w_sh_down, ln_g, ln_b, alpha):
    t, dm = x1.shape
    tm = min(MOE_TILE, t)
    const = lambda a: pl.BlockSpec(a.shape, lambda i, *_: (0,) * a.ndim, pipeline_mode=pl.Buffered(1))
    return pl.pallas_call(
        functools.partial(_combine_kernel, tm=tm, alpha=alpha, n_steps=t // tm),
        grid_spec=pltpu.PrefetchScalarGridSpec(
            num_scalar_prefetch=3,
            grid=(t // tm,),
            in_specs=[pl.BlockSpec((tm, dm), lambda i, *_: (i, 0)),
                      pl.BlockSpec((tm, SLOT_PAD), lambda i, *_: (i, 0)),
                      pl.BlockSpec((tm, SLOT_PAD), lambda i, *_: (i, 0)),
                      const(w_sh_gu), const(w_sh_down), const(ln_g), const(ln_b),
                      pl.BlockSpec(memory_space=pl.ANY)],
            out_specs=pl.BlockSpec((tm, dm), lambda i, *_: (i, 0)),
            scratch_shapes=[pltpu.VMEM((2, _local_rows(tm), dm), MXU_DTYPE), pltpu.SemaphoreType.DMA((2,))],
        ),
        out_shape=jax.ShapeDtypeStruct((t, dm), jnp.float32),
        compiler_params=_params("arbitrary"),
        name="moe_combine",
    )(seg_cnt, seg_off, seg_dst, x1, gate, pos, w_sh_gu, w_sh_down, ln_g, ln_b, ys)


def _moe(x1, gate, pos, stats, counts, w_exp_gu, w_exp_down, w_sh_gu, w_sh_down, ln_g, ln_b, alpha):
    t = x1.shape[0]
    bm = EXPERT_ROWS if t * MOE_TOPK >= N_EXPERTS * EXPERT_ROWS else EXPERT_ROWS_FEW
    n_tiles = stats.shape[0]
    n_rows = (-(-(t * MOE_TOPK + n_tiles * N_EXPERTS * (SEG_ALIGN - 1) + SLOT_ROWS) // bm) + N_EXPERTS) * bm
    cnt = counts.reshape(N_EXPERTS).astype(jnp.int32)
    padded = (cnt + SEG_SLOT + bm - 1) // bm * bm
    end = jnp.cumsum(padded)
    start = end - padded
    n_used = end[-1:] // bm
    zlo = jnp.concatenate([(start + cnt) // SEG_ALIGN, n_used])
    zhi = jnp.concatenate([end // SEG_ALIGN, jnp.full((1,), n_rows // bm, jnp.int32)])
    block_start = jnp.arange(n_rows // bm, dtype=jnp.int32) * bm
    block_expert = jnp.minimum(jnp.sum((end[None, :] <= block_start[:, None]).astype(jnp.int32), axis=1),
                               N_EXPERTS - 1)
    seg = stats.astype(jnp.int32)
    seg_cnt, seg_off = seg[:, 1, :].reshape(-1), seg[:, 2, :].reshape(-1)
    seg_dst = (start[None, :] + seg[:, 0, :]).reshape(-1)
    pos_t = jnp.transpose(pos.reshape(n_tiles, t // n_tiles, SLOT_PAD), (0, 2, 1))
    xs = _dispatch(x1, pos_t, seg_cnt, seg_off, seg_dst, zlo, zhi, n_rows, bm)
    ys = _experts(xs, block_expert, n_used, w_exp_gu, w_exp_down, bm)
    return _combine(x1, gate, pos, seg_cnt, seg_off, seg_dst, ys, w_sh_gu, w_sh_down, ln_g, ln_b, alpha)


def _layer(x, pos0, past, p, alpha):
    (w_in, a_re, a_im, log_dt, b_re, b_im, c_re, c_im, ssm_d, w_glu, w_attn_proj, w_out, ln1_g, ln1_b,
     w_router, router_bias, w_exp_gu, w_exp_down, w_sh_gu, w_sh_down, ln2_g, ln2_b) = p
    nb, seq, dm = x.shape
    assert pos0 % CHUNK == 0 and seq % Q_BLOCK == 0 and Q_BLOCK == CHUNK and seq % S5_STEPS == 0
    assert nb * seq * MOE_TOPK < 2 ** 24
    t = nb * seq
    f32 = jnp.float32
    x2 = x.reshape(t, dm)
    pos = pos0 + jnp.arange(seq, dtype=jnp.int32)

    u, q, iq, k, v, ik2, ik, iw, k_heads, v_heads = _in_proj(x2, _pack_w_in(w_in), pos, seq)

    a_cat, bcat, ccat = _s5_discretise(a_re, a_im, log_dt, b_re, b_im, c_re, c_im)
    if past is None:
        h0 = jnp.zeros((nb, 2 * NS), f32)
    else:
        h0 = jnp.concatenate([past[3].reshape(nb, NS), past[4].reshape(nb, NS)], axis=1).astype(f32)
    ssm, h_t = _s5(u.reshape(nb, seq, D_SSM), h0, a_cat, bcat, ccat, ssm_d.reshape(1, D_SSM).astype(f32),
                   w_glu.astype(MXU_DTYPE))
    ssm2 = ssm.reshape(t, dm)
    h_re = h_t[:, :NS].reshape(nb, N_GROUPS, N_STATE)
    h_im = h_t[:, NS:].reshape(nb, N_GROUPS, N_STATE)

    k3, v3, ik23 = (a.reshape(nb, seq, LANES) for a in (k, v, ik2))
    if past is not None:
        ck, cv, cik = past[0], past[1], past[2]
        n_past = ck.shape[1]
        k3 = jnp.concatenate([ck.reshape(nb, n_past, LANES), k3], axis=1)
        v3 = jnp.concatenate([cv.reshape(nb, n_past, LANES), v3], axis=1)
        ik23 = jnp.concatenate([jnp.concatenate([cik, cik], axis=-1), ik23], axis=1)
    attn = _dsa(q.reshape(nb, seq, D_ATTN), iq.reshape(nb, seq, IDX_HEADS * IDX_DIM), iw.reshape(nb, seq, LANES),
                k3, v3, ik23, pos0 // CHUNK)

    wap = jnp.transpose(w_attn_proj.reshape(N_KV_HEADS, KV_GROUP, HEAD_DIM, dm), (1, 0, 2, 3)).reshape(D_ATTN, dm)
    wg = w_in[:, w_in.shape[1] - 2 * dm:]
    row = lambda a: a.reshape(1, -1).astype(f32)
    x1, gate, pos_slot, stats, counts = _mix(
        x2, attn.reshape(t, D_ATTN), ssm2, wg.astype(MXU_DTYPE), wap.astype(MXU_DTYPE), w_out.astype(MXU_DTYPE),
        row(ln1_g), row(ln1_b), *_split(w_router.astype(f32)), row(router_bias), alpha)

    y = _moe(x1, gate, pos_slot, stats, counts, w_exp_gu, w_exp_down,
             w_sh_gu.astype(MXU_DTYPE), w_sh_down.astype(MXU_DTYPE), row(ln2_g), row(ln2_b), alpha)
    state = (k_heads.reshape(nb, seq, N_KV_HEADS, HEAD_DIM), v_heads.reshape(nb, seq, N_KV_HEADS, HEAD_DIM),
             ik.reshape(nb, seq, IDX_DIM), h_re, h_im)
    return y.reshape(nb, seq, dm), state


def kernel(x_prompt, x_sample, cache_k, cache_v, cache_idx_k, state_ssm_re, state_ssm_im, w_in, ssm_a_re, ssm_a_im, ssm_log_dt, ssm_b_re, ssm_b_im, ssm_c_re, ssm_c_im, ssm_d, w_glu, w_attn_proj, w_out, ln1_g, ln1_b, w_router, router_bias, w_exp_gu, w_exp_down, w_sh_gu, w_sh_down, ln2_g, ln2_b):
    weights = (w_in, ssm_a_re, ssm_a_im, ssm_log_dt, ssm_b_re, ssm_b_im, ssm_c_re, ssm_c_im, ssm_d,
               w_glu, w_attn_proj, w_out, ln1_g, ln1_b, w_router, router_bias,
               w_exp_gu, w_exp_down, w_sh_gu, w_sh_down, ln2_g, ln2_b)
    depth = w_in.shape[0]
    alpha = (2 * depth) ** 0.25
    past_len = cache_k.shape[2]
    y_p, y_s = x_prompt, x_sample
    new_p, new_s = [], []
    for l in range(depth):
        p_l = tuple(w[l] for w in weights)
        y_p, st_p = _layer(y_p, 0, None, p_l, alpha)
        y_s, st_s = _layer(y_s, past_len,
                           (cache_k[l], cache_v[l], cache_idx_k[l], state_ssm_re[l], state_ssm_im[l]), p_l, alpha)
        new_p.append(st_p)
        new_s.append(st_s)
    k_p, v_p, ik_p, hre_p, him_p = [jnp.stack(a) for a in zip(*new_p)]
    k_s, v_s, ik_s, hre_s, him_s = [jnp.stack(a) for a in zip(*new_s)]
    return (y_p, y_s, k_p, v_p, ik_p, hre_p, him_p, k_s, v_s, ik_s, hre_s, him_s)
```
